```python
import math
import jax, jax.numpy as jnp
from jax import lax
import numpy as np

D_MODEL = 1024
BATCH = 32
SEQ = 2048
DEPTH = 1

GRID_W = 64
CTX_LEN = 256
EPS = 1e-6
CONV_W = 1024
CONV_K = 31
GLA_H = 4
GLA_DK = 128
GLA_DV = 256
GLA_KD = GLA_H * GLA_DK
GLA_VD = GLA_H * GLA_DV
GLA_RANK = 16
GLA_TAU = 16.0
GLA_CHUNK = 64
N_EXPERTS = 32
TOP_K = 4
D_FF = 1024
SWIGLU_ALPHA = 1.702
SWIGLU_LIMIT = 7.0
MOE_BLOCK = 128
MIX_SPLITS = (CONV_W, CONV_W, GLA_KD, GLA_KD, GLA_VD, GLA_VD, 2 * GLA_RANK, D_MODEL, D_MODEL)
D_IN = sum(MIX_SPLITS)

kernel_name = "hybrid_conv_gla_moe_dit_block"


def _rmsnorm(x, g):
    xf = x.astype(jnp.float32)
    y = xf * lax.rsqrt(jnp.mean(xf * xf, axis=-1, keepdims=True) + EPS)
    return y.astype(x.dtype) * g


def _layernorm(x, g, b):
    xf = x.astype(jnp.float32)
    mu = jnp.mean(xf, axis=-1, keepdims=True)
    var = jnp.mean(jnp.square(xf - mu), axis=-1, keepdims=True)
    return ((xf - mu) * lax.rsqrt(var + EPS)).astype(x.dtype) * g + b


def _modulate(h, shift, scale):
    return h * (1.0 + scale) + shift


def _conv_module(u, w_dw, b_dw, g_ln, b_ln, w_pw):
    y = lax.conv_general_dilated(
        u, w_dw[:, None, :], window_strides=(1,),
        padding=[(CONV_K // 2, CONV_K // 2)],
        dimension_numbers=("NWC", "WIO", "NWC"),
        feature_group_count=CONV_W) + b_dw
    y = _layernorm(y, g_ln, b_ln)
    return jax.nn.silu(y) @ w_pw


def _flip_segments(z, n_ctx):
    return jnp.concatenate([z[:, :n_ctx][:, ::-1], z[:, n_ctx:][:, ::-1]], axis=1)


def _gla_chunked(q, k, v, log_a):
    B, T, H, DK = q.shape
    DV = v.shape[-1]
    N = T // GLA_CHUNK

    def chunks(z):
        return z.reshape(B, N, GLA_CHUNK, H, z.shape[-1]).transpose(1, 0, 3, 2, 4)

    q, k, v, log_a = chunks(q), chunks(k), chunks(v), chunks(log_a)
    b = jnp.cumsum(log_a, axis=3)
    b_last = b[:, :, :, -1:, :]
    q_dec = q * jnp.exp(b)
    k_inv = k * jnp.exp(-b)
    k_to_end = k * jnp.exp(b_last - b)
    lower = jnp.tril(jnp.ones((GLA_CHUNK, GLA_CHUNK), dtype=bool))
    scores = jnp.einsum("nbhcd,nbhsd->nbhcs", q_dec, k_inv)
    o_intra = jnp.einsum("nbhcs,nbhsv->nbhcv", jnp.where(lower, scores, 0.0), v)

    def step(state, xs):
        q_n, k_n, v_n, bl_n = xs
        o_n = jnp.einsum("bhcd,bhdv->bhcv", q_n, state)
        state = state * jnp.exp(bl_n)[:, :, 0, :, None] + jnp.einsum("bhcd,bhcv->bhdv", k_n, v_n)
        return state, o_n

    s0 = jnp.zeros((B, H, DK, DV), jnp.float32)
    _, o_inter = lax.scan(step, s0, (q_dec, k_to_end, v, b_last))
    o = o_intra + o_inter
    return o.transpose(1, 0, 3, 2, 4).reshape(B, T, H, DV)


def _gla_bidirectional(q, k, v, g, a_lr, w_alpha, b_alpha, g_norm, n_ctx):
    B, T, _ = q.shape
    qf = q.reshape(B, T, GLA_H, GLA_DK).astype(jnp.float32) * (GLA_DK ** -0.5)
    kf = k.reshape(B, T, GLA_H, GLA_DK).astype(jnp.float32)
    vf = v.reshape(B, T, GLA_H, GLA_DV).astype(jnp.float32)
    a_fwd, a_bwd = jnp.split(a_lr, 2, axis=-1)

    def log_decay(a, d):
        z = (a @ w_alpha[d] + b_alpha[d]).astype(jnp.float32)
        return (jax.nn.log_sigmoid(z) / GLA_TAU).reshape(B, T, GLA_H, GLA_DK)

    o_fwd = _gla_chunked(qf, kf, vf, log_decay(a_fwd, 0))
    o_bwd = _flip_segments(
        _gla_chunked(_flip_segments(qf, n_ctx), _flip_segments(kf, n_ctx),
                     _flip_segments(vf, n_ctx), _flip_segments(log_decay(a_bwd, 1), n_ctx)),
        n_ctx)
    o = o_fwd + o_bwd
    o = o * lax.rsqrt(jnp.mean(o * o, axis=-1, keepdims=True) + EPS)
    o = (o.astype(g.dtype) * g_norm).reshape(B, T, GLA_VD)
    return o * jax.nn.silu(g)


def _clamped_swiglu(hid):
    x_glu = jnp.minimum(hid[..., ::2], SWIGLU_LIMIT)
    x_lin = jnp.clip(hid[..., 1::2], -SWIGLU_LIMIT, SWIGLU_LIMIT)
    return x_glu * jax.nn.sigmoid(SWIGLU_ALPHA * x_glu) * (x_lin + 1.0)


def _moe(h, w_router, b_router, w1, b1, w2, b2):
    N, D = h.shape
    logits = (h @ w_router + b_router).astype(jnp.float32)
    top_val, top_idx = lax.top_k(logits, TOP_K)
    probs = jax.nn.softmax(top_val, axis=-1)
    NK = N * TOP_K
    e_flat = top_idx.reshape(NK).astype(jnp.int32)
    tok_flat = jnp.repeat(jnp.arange(N, dtype=jnp.int32), TOP_K)
    p_flat = probs.reshape(NK)
    order = jnp.argsort(e_flat, stable=True)
    e_sorted = e_flat[order]
    counts = jnp.zeros((N_EXPERTS,), jnp.int32).at[e_flat].add(1)
    padded = (counts + MOE_BLOCK - 1) // MOE_BLOCK * MOE_BLOCK
    pad_end = jnp.cumsum(padded)
    pad_start = pad_end - padded
    start = jnp.cumsum(counts) - counts
    dest = pad_start[e_sorted] + (jnp.arange(NK, dtype=jnp.int32) - start[e_sorted])
    n_blocks = (NK + N_EXPERTS * (MOE_BLOCK - 1) + MOE_BLOCK - 1) // MOE_BLOCK
    P = n_blocks * MOE_BLOCK
    row_tok = jnp.full((P,), N, jnp.int32).at[dest].set(tok_flat[order])
    row_p = jnp.zeros((P,), jnp.float32).at[dest].set(p_flat[order])
    blk_expert = jnp.minimum(
        jnp.searchsorted(pad_end, jnp.arange(n_blocks, dtype=jnp.int32) * MOE_BLOCK, side="right"),
        N_EXPERTS - 1)
    h_pad = jnp.concatenate([h, jnp.zeros((1, D), h.dtype)], axis=0)

    def body(acc, blk):
        rows = lax.dynamic_slice(row_tok, (blk * MOE_BLOCK,), (MOE_BLOCK,))
        wts = lax.dynamic_slice(row_p, (blk * MOE_BLOCK,), (MOE_BLOCK,))
        e = blk_expert[blk]
        hid = h_pad[rows] @ w1[e] + b1[e]
        y = _clamped_swiglu(hid) @ w2[e] + b2[e]
        return acc.at[rows].add(y * wts[:, None].astype(y.dtype)), None

    acc, _ = lax.scan(body, jnp.zeros((N + 1, D), h.dtype), jnp.arange(n_blocks, dtype=jnp.int32))
    return acc[:N]


def setup_inputs(seed: int = 0) -> dict:
    key = jax.random.key(seed)
    ks = iter(jax.random.split(key, 32))
    f32 = jnp.float32

    def nrm(shape, scale):
        return jax.random.normal(next(ks), shape, f32) * scale

    L, D = DEPTH, D_MODEL
    return {
        "x": nrm((BATCH, SEQ, D), 1.0),
        "c": nrm((BATCH, D), 1.0),
        "ctx": nrm((BATCH, CTX_LEN, D), 1.0),
        "c_ctx": nrm((D,), 1.0),
        "w_ada": nrm((L, D, 6 * D), D ** -0.5),
        "b_ada": nrm((L, 6 * D), 0.02),
        "g_mix_norm": 1.0 + nrm((L, D), 0.05),
        "w_in": nrm((L, D, D_IN), D ** -0.5),
        "w_dw": nrm((L, CONV_K, CONV_W), CONV_K ** -0.5),
        "b_dw": nrm((L, CONV_W), 0.02),
        "g_conv_ln": 1.0 + nrm((L, CONV_W), 0.05),
        "b_conv_ln": nrm((L, CONV_W), 0.02),
        "w_conv_out": nrm((L, CONV_W, D), CONV_W ** -0.5),
        "w_alpha": nrm((L, 2, GLA_RANK, GLA_KD), GLA_RANK ** -0.5),
        "b_alpha": nrm((L, 2, GLA_KD), 0.5),
        "g_gla_norm": 1.0 + nrm((L, GLA_DV), 0.05),
        "w_gla_out": nrm((L, GLA_VD, D), GLA_VD ** -0.5),
        "w_out": nrm((L, D, D), D ** -0.5),
        "g_ffn_norm": 1.0 + nrm((L, D), 0.05),
        "w_router": nrm((L, D, N_EXPERTS), D ** -0.5),
        "b_router": nrm((L, N_EXPERTS), 0.01),
        "w_exp_in": nrm((L, N_EXPERTS, D, 2 * D_FF), D ** -0.5),
        "b_exp_in": nrm((L, N_EXPERTS, 2 * D_FF), 0.01),
        "w_exp_out": nrm((L, N_EXPERTS, D_FF, D), D_FF ** -0.5),
        "b_exp_out": nrm((L, N_EXPERTS, D), 0.01),
        "g_final": 1.0 + nrm((D,), 0.05),
    }


def reference(x, c, ctx, c_ctx, w_ada, b_ada, g_mix_norm, w_in, w_dw, b_dw, g_conv_ln, b_conv_ln,
              w_conv_out, w_alpha, b_alpha, g_gla_norm, w_gla_out, w_out, g_ffn_norm, w_router,
              b_router, w_exp_in, b_exp_in, w_exp_out, b_exp_out, g_final):
    B, S, D = x.shape
    L = ctx.shape[1]
    ROWS = S // GRID_W
    split_idx = [int(i) for i in np.cumsum(MIX_SPLITS)[:-1]]

    for l in range(DEPTH):
        last = l == DEPTH - 1
        mod_lat = jax.nn.silu(c) @ w_ada[l] + b_ada[l]
        mod_ctx = jax.nn.silu(c_ctx) @ w_ada[l] + b_ada[l]
        sh1, sc1, gt1, sh2, sc2, gt2 = jnp.split(mod_lat, 6, axis=-1)
        csh1, csc1, cgt1, csh2, csc2, cgt2 = jnp.split(mod_ctx, 6, axis=-1)

        h_lat = _modulate(_rmsnorm(x, g_mix_norm[l]), sh1[:, None], sc1[:, None])
        h_ctx = _modulate(_rmsnorm(ctx, g_mix_norm[l]), csh1, csc1)
        h = jnp.concatenate([h_ctx, h_lat], axis=1)
        proj = h @ w_in[l]
        conv_a, conv_b, q, k, v, g, a_lr, gate_conv, gate_gla = jnp.split(proj, split_idx, axis=-1)

        u = conv_a * jax.nn.sigmoid(conv_b)
        y_conv_lat = _conv_module(u[:, L:].reshape(B * ROWS, GRID_W, CONV_W), w_dw[l], b_dw[l],
                                  g_conv_ln[l], b_conv_ln[l], w_conv_out[l]).reshape(B, S, D)
        o_gla = _gla_bidirectional(q, k, v, g, a_lr, w_alpha[l], b_alpha[l], g_gla_norm[l], L)
        y_gla_lat = o_gla[:, L:] @ w_gla_out[l]
        m_lat = jax.nn.sigmoid(gate_conv[:, L:]) * y_conv_lat + jax.nn.sigmoid(gate_gla[:, L:]) * y_gla_lat
        x = x + gt1[:, None] * (m_lat @ w_out[l])

        h2_lat = _modulate(_rmsnorm(x, g_ffn_norm[l]), sh2[:, None], sc2[:, None])
        moe_args = (w_router[l], b_router[l], w_exp_in[l], b_exp_in[l], w_exp_out[l], b_exp_out[l])
        if not last:
            y_conv_ctx = _conv_module(u[:, :L], w_dw[l], b_dw[l], g_conv_ln[l], b_conv_ln[l], w_conv_out[l])
            y_gla_ctx = o_gla[:, :L] @ w_gla_out[l]
            m_ctx = jax.nn.sigmoid(gate_conv[:, :L]) * y_conv_ctx + jax.nn.sigmoid(gate_gla[:, :L]) * y_gla_ctx
            ctx = ctx + cgt1 * (m_ctx @ w_out[l])
            h2_ctx = _modulate(_rmsnorm(ctx, g_ffn_norm[l]), csh2, csc2)
            y_all = _moe(jnp.concatenate([h2_ctx.reshape(B * L, D), h2_lat.reshape(B * S, D)], axis=0), *moe_args)
            ctx = ctx + cgt2 * y_all[:B * L].reshape(B, L, D)
            y_lat = y_all[B * L:].reshape(B, S, D)
        else:
            y_lat = _moe(h2_lat.reshape(B * S, D), *moe_args).reshape(B, S, D)
        x = x + gt2[:, None] * y_lat

    return _rmsnorm(x, g_final)
```

```python
import functools

import jax
import jax.numpy as jnp
from jax import lax
from jax.experimental import pallas as pl
from jax.experimental.pallas import tpu as pltpu

F32 = jnp.float32
BF16 = jnp.bfloat16
I32 = jnp.int32

D_MODEL = 1024
GRID_W = 64
EPS = 1e-6
CONV_W = 1024
CONV_K = 31
GLA_H = 4
GLA_DK = 128
GLA_DV = 256
GLA_KD = GLA_H * GLA_DK
GLA_VD = GLA_H * GLA_DV
GLA_RANK = 16
GLA_TAU = 16.0
GLA_CHUNK = 64
N_EXPERTS = 32
TOP_K = 4
D_FF = 1024
SWIGLU_ALPHA = 1.702
SWIGLU_LIMIT = 7.0

LANES = 128
VMEM_LIMIT = 56 * 1024 * 1024

_C_CONV_A = 0
_C_CONV_B = _C_CONV_A + CONV_W
_C_Q = _C_CONV_B + CONV_W
_C_K = _C_Q + GLA_KD
_C_V = _C_K + GLA_KD
_C_G = _C_V + GLA_VD
_C_GC = _C_G + GLA_VD
_C_GG = _C_GC + D_MODEL
_C_A = _C_GG + D_MODEL
_C_END = _C_A + LANES


def _cparams(sem):
    return pltpu.CompilerParams(dimension_semantics=sem, vmem_limit_bytes=VMEM_LIMIT)


def _dot(a, b):
    return jnp.dot(a, b, preferred_element_type=F32)


def _split_bf16(x):
    hi = x.astype(BF16)
    lo = (x - hi.astype(F32)).astype(BF16)
    return hi, lo


def _dot3(a, b):
    a_hi, a_lo = _split_bf16(a)
    b_hi, b_lo = _split_bf16(b)
    return _dot(a_hi, b_hi) + _dot(a_lo, b_hi) + _dot(a_hi, b_lo)


def _sigmoid(x):
    return 1.0 / (1.0 + jnp.exp(-x))


def _resident(shape):
    nd = len(shape)
    return pl.BlockSpec(shape, lambda *_: (0,) * nd, pipeline_mode=pl.Buffered(1))


def _ada_body(a_ref, w_ref, b_ref, o_ref):
    a = a_ref[...]
    a = a * _sigmoid(a)
    o_ref[...] = _dot3(a, w_ref[...]) + b_ref[...]


def _ada(cc, w, b):
    rows, d = cc.shape
    n = w.shape[1]
    tn = 512
    return pl.pallas_call(
        _ada_body,
        grid=(n // tn,),
        in_specs=[pl.BlockSpec((rows, d), lambda j: (0, 0)),
                  pl.BlockSpec((d, tn), lambda j: (0, j)),
                  pl.BlockSpec((1, tn), lambda j: (0, j))],
        out_specs=pl.BlockSpec((rows, tn), lambda j: (0, j)),
        out_shape=jax.ShapeDtypeStruct((rows, n), F32),
        compiler_params=_cparams(("arbitrary",)),
        name="ada_mod",
    )(cc, w, b.reshape(1, n))


def _norm_mod(xv, gn, sc, sh):
    ms = jnp.mean(xv * xv, axis=-1, keepdims=True)
    y = xv * lax.rsqrt(ms + EPS) * gn
    return (y * (1.0 + sc) + sh).astype(BF16)


def _inproj_lat_body(x_ref, gn_ref, sh_ref, sc_ref, w_ref,
                     u_ref, q_ref, k_ref, v_ref, sg_ref, gc_ref, gg_ref, a_ref):
    h = _norm_mod(x_ref[0], gn_ref[...], sc_ref[0], sh_ref[0])
    ca = _dot(h, w_ref[:, _C_CONV_A:_C_CONV_B])
    cb = _dot(h, w_ref[:, _C_CONV_B:_C_Q])
    u_ref[0] = (ca * _sigmoid(cb)).astype(BF16)
    q_ref[0] = _dot(h, w_ref[:, _C_Q:_C_K]).astype(BF16)
    k_ref[0] = _dot(h, w_ref[:, _C_K:_C_V]).astype(BF16)
    v_ref[0] = _dot(h, w_ref[:, _C_V:_C_G]).astype(BF16)
    g = _dot(h, w_ref[:, _C_G:_C_GC])
    sg_ref[0] = (g * _sigmoid(g)).astype(BF16)
    gc_ref[0] = _sigmoid(_dot(h, w_ref[:, _C_GC:_C_GG])).astype(BF16)
    gg_ref[0] = _sigmoid(_dot(h, w_ref[:, _C_GG:_C_A])).astype(BF16)
    a_ref[0] = _dot(h, w_ref[:, _C_A:_C_END])


def _inproj_ctx_body(x_ref, gn_ref, sh_ref, sc_ref, w_ref, k_ref, v_ref, a_ref):
    h = _norm_mod(x_ref[0], gn_ref[...], sc_ref[...], sh_ref[...])
    k_ref[0] = _dot(h, w_ref[:, _C_K:_C_V]).astype(BF16)
    v_ref[0] = _dot(h, w_ref[:, _C_V:_C_G]).astype(BF16)
    a_ref[0] = _dot(h, w_ref[:, _C_A:_C_END])


def _inproj_lat(x, gn, sh, sc, w, tm):
    b, s, d = x.shape
    row = lambda n: pl.BlockSpec((1, tm, n), lambda bi, i: (bi, i, 0))
    vec = pl.BlockSpec((1, 1, d), lambda bi, i: (bi, 0, 0))
    shp = lambda n, dt: jax.ShapeDtypeStruct((b, s, n), dt)
    return pl.pallas_call(
        _inproj_lat_body,
        grid=(b, s // tm),
        in_specs=[row(d), pl.BlockSpec((1, d), lambda bi, i: (0, 0)), vec, vec, _resident(w.shape)],
        out_specs=[row(CONV_W), row(GLA_KD), row(GLA_KD), row(GLA_VD), row(GLA_VD),
                   row(d), row(d), row(LANES)],
        out_shape=[shp(CONV_W, BF16), shp(GLA_KD, BF16), shp(GLA_KD, BF16), shp(GLA_VD, BF16),
                   shp(GLA_VD, BF16), shp(d, BF16), shp(d, BF16), shp(LANES, F32)],
        compiler_params=_cparams(("arbitrary", "arbitrary")),
        name="inproj_lat",
    )(x, gn, sh, sc, w)


def _inproj_ctx(ctx, gn, sh, sc, w, tm):
    b, l, d = ctx.shape
    row = lambda n: pl.BlockSpec((1, tm, n), lambda bi, i: (bi, i, 0))
    vec = pl.BlockSpec((1, d), lambda bi, i: (0, 0))
    shp = lambda n, dt: jax.ShapeDtypeStruct((b, l, n), dt)
    return pl.pallas_call(
        _inproj_ctx_body,
        grid=(b, l // tm),
        in_specs=[row(d), vec, vec, vec, _resident(w.shape)],
        out_specs=[row(GLA_KD), row(GLA_VD), row(LANES)],
        out_shape=[shp(GLA_KD, BF16), shp(GLA_VD, BF16), shp(LANES, F32)],
        compiler_params=_cparams(("arbitrary", "arbitrary")),
        name="inproj_ctx",
    )(ctx, gn, sh, sc, w)


_CONV_HALO = 16
_CONV_ROWS = GRID_W + 2 * _CONV_HALO


def _conv_body(u_ref, w_ref, bdw_ref, gln_ref, bln_ref, o_ref, pad_ref, y_ref, *, rows_per_step):
    zeros = jnp.zeros((_CONV_HALO, CONV_W), F32)
    for r in range(rows_per_step):
        pad_ref[r, 0:_CONV_HALO, :] = zeros
        pad_ref[r, _CONV_HALO + GRID_W:_CONV_ROWS, :] = zeros
        pad_ref[r, _CONV_HALO:_CONV_HALO + GRID_W, :] = (
            u_ref[r * GRID_W:(r + 1) * GRID_W, :].astype(F32))
    off = _CONV_HALO - CONV_K // 2
    for r in range(rows_per_step):
        for cb in range(CONV_W // LANES):
            ls = slice(cb * LANES, (cb + 1) * LANES)
            acc = pad_ref[r, off:off + GRID_W, ls] * w_ref[0:1, ls]
            for k in range(1, CONV_K):
                acc = acc + pad_ref[r, off + k:off + k + GRID_W, ls] * w_ref[k:k + 1, ls]
            y_ref[r * GRID_W:(r + 1) * GRID_W, ls] = acc + bdw_ref[:, ls]
    y = y_ref[...]
    mu = jnp.mean(y, axis=-1, keepdims=True)
    yc = y - mu
    var = jnp.mean(yc * yc, axis=-1, keepdims=True)
    yn = yc * lax.rsqrt(var + EPS) * gln_ref[...] + bln_ref[...]
    o_ref[...] = (yn * _sigmoid(yn)).astype(BF16)


def _conv(u, w_dw, b_dw, g_ln, b_ln, rows_per_step):
    n, cw = u.shape
    tm = rows_per_step * GRID_W
    vec = pl.BlockSpec((1, cw), lambda i: (0, 0))
    wpad = jnp.zeros((32, cw), F32).at[:CONV_K].set(w_dw)
    return pl.pallas_call(
        functools.partial(_conv_body, rows_per_step=rows_per_step),
        grid=(n // tm,),
        in_specs=[pl.BlockSpec((tm, cw), lambda i: (i, 0)),
                  pl.BlockSpec((32, cw), lambda i: (0, 0)), vec, vec, vec],
        out_specs=pl.BlockSpec((tm, cw), lambda i: (i, 0)),
        out_shape=jax.ShapeDtypeStruct((n, cw), BF16),
        scratch_shapes=[pltpu.VMEM((rows_per_step, _CONV_ROWS, cw), F32),
                        pltpu.VMEM((tm, cw), F32)],
        compiler_params=_cparams(("arbitrary",)),
        name="conv_module",
    )(u, wpad, b_dw.reshape(1, cw), g_ln.reshape(1, cw), b_ln.reshape(1, cw))


def _log_sigmoid(z):
    return jnp.minimum(z, 0.0) - jnp.log(1.0 + jnp.exp(-jnp.abs(z)))


def _gla_chunk(k, v, a, wa, ba, tri, st_ref, *, reverse, q=None):
    c = GLA_CHUNK
    z = _dot3(a, wa) + ba
    loga = _log_sigmoid(z) * (1.0 / GLA_TAU)
    l_hi, l_lo = _split_bf16(loga)
    bcum = _dot(tri, l_hi) + _dot(tri, l_lo)
    tot = bcum[0:1, :] if reverse else bcum[c - 1:c, :]
    eb = jnp.exp(bcum)
    k32 = k.astype(F32)
    k_inv = (k32 * jnp.exp(-bcum)).astype(BF16)
    k_end = (k32 * jnp.exp(tot - bcum)).astype(BF16)
    dec = jnp.exp(tot)
    if q is not None:
        q_dec = (q.astype(F32) * eb * (GLA_DK ** -0.5)).astype(BF16)
        row = lax.broadcasted_iota(I32, (c, c), 0)
        col = lax.broadcasted_iota(I32, (c, c), 1)
        keep = (col >= row) if reverse else (col <= row)
    outs = []
    for h in range(GLA_H):
        ks = slice(h * GLA_DK, (h + 1) * GLA_DK)
        vs = slice(h * GLA_DV, (h + 1) * GLA_DV)
        st = st_ref[h]
        vh = v[:, vs]
        if q is not None:
            scores = lax.dot_general(q_dec[:, ks], k_inv[:, ks], (((1,), (1,)), ((), ())),
                                     preferred_element_type=F32)
            scores = jnp.where(keep, scores, 0.0).astype(BF16)
            outs.append(_dot(scores, vh) + _dot(q_dec[:, ks], st.astype(BF16)))
        kv = lax.dot_general(k_end[:, ks], vh, (((0,), (0,)), ((), ())),
                             preferred_element_type=F32)
        dt = jnp.transpose(jnp.broadcast_to(dec[:, ks], (GLA_DK, GLA_DK)))
        st_ref[h] = st * jnp.concatenate([dt] * (GLA_DV // GLA_DK), axis=1) + kv
    if q is None:
        return None
    return jnp.concatenate(outs, axis=1)


def _tri(reverse):
    c = GLA_CHUNK
    row = lax.broadcasted_iota(I32, (c, c), 0)
    col = lax.broadcasted_iota(I32, (c, c), 1)
    keep = (col >= row) if reverse else (col <= row)
    return jnp.where(keep, 1.0, 0.0).astype(BF16)


def _gla_ctx_body(k_ref, v_ref, a_ref, wa_ref, ba_ref, st_ref, *, reverse, cps):
    j = pl.program_id(1)

    @pl.when(j == 0)
    def _():
        st_ref[...] = jnp.zeros_like(st_ref)

    tri = _tri(reverse)
    a0 = GLA_RANK if reverse else 0
    for s in (range(cps - 1, -1, -1) if reverse else range(cps)):
        rs = slice(s * GLA_CHUNK, (s + 1) * GLA_CHUNK)
        _gla_chunk(k_ref[0, rs, :], v_ref[0, rs, :], a_ref[0, rs, a0:a0 + GLA_RANK],
                   wa_ref[...], ba_ref[...], tri, st_ref.at[0], reverse=reverse)


def _gla_lat_body(*refs, reverse, final, cps):
    if final:
        (k_ref, v_ref, a_ref, q_ref, wa_ref, ba_ref, s0_ref, sg_ref, op_ref, gn_ref,
         o_ref, st_ref) = refs
    else:
        k_ref, v_ref, a_ref, q_ref, wa_ref, ba_ref, s0_ref, o_ref, st_ref = refs
    j = pl.program_id(1)

    @pl.when(j == 0)
    def _():
        st_ref[...] = s0_ref[0]

    tri = _tri(reverse)
    a0 = GLA_RANK if reverse else 0
    for s in (range(cps - 1, -1, -1) if reverse else range(cps)):
        rs = slice(s * GLA_CHUNK, (s + 1) * GLA_CHUNK)
        o = _gla_chunk(k_ref[0, rs, :], v_ref[0, rs, :], a_ref[0, rs, a0:a0 + GLA_RANK],
                       wa_ref[...], ba_ref[...], tri, st_ref, reverse=reverse,
                       q=q_ref[0, rs, :])
        if not final:
            o_ref[0, rs, :] = o.astype(BF16)
            continue
        o = o + op_ref[0, rs, :].astype(F32)
        parts = []
        for h in range(GLA_H):
            oh = o[:, h * GLA_DV:(h + 1) * GLA_DV]
            ms = jnp.mean(oh * oh, axis=-1, keepdims=True)
            parts.append(oh * lax.rsqrt(ms + EPS) * gn_ref[...])
        on = jnp.concatenate(parts, axis=1)
        o_ref[0, rs, :] = (on * sg_ref[0, rs, :].astype(F32)).astype(BF16)


def _gla_ctx(k, v, a, wa, ba, reverse, cps):
    b, l, _ = k.shape
    tm = cps * GLA_CHUNK
    nj = l // tm
    jmap = (lambda j: nj - 1 - j) if reverse else (lambda j: j)
    row = lambda n: pl.BlockSpec((1, tm, n), lambda bi, j: (bi, jmap(j), 0))
    return pl.pallas_call(
        functools.partial(_gla_ctx_body, reverse=reverse, cps=cps),
        grid=(b, nj),
        in_specs=[row(GLA_KD), row(GLA_VD), row(LANES),
                  pl.BlockSpec((GLA_RANK, GLA_KD), lambda bi, j: (0, 0)),
                  pl.BlockSpec((1, GLA_KD), lambda bi, j: (0, 0))],
        out_specs=pl.BlockSpec((1, GLA_H, GLA_DK, GLA_DV), lambda bi, j: (bi, 0, 0, 0)),
        out_shape=jax.ShapeDtypeStruct((b, GLA_H, GLA_DK, GLA_DV), F32),
        compiler_params=_cparams(("arbitrary", "arbitrary")),
        name="gla_ctx_bwd" if reverse else "gla_ctx_fwd",
    )(k, v, a, wa, ba.reshape(1, GLA_KD))


def _gla_lat(k, v, a, q, wa, ba, s0, reverse, cps, sg=None, o_prev=None, g_norm=None):
    b, s, _ = k.shape
    final = sg is not None
    tm = cps * GLA_CHUNK
    nj = s // tm
    jmap = (lambda j: nj - 1 - j) if reverse else (lambda j: j)
    row = lambda n: pl.BlockSpec((1, tm, n), lambda bi, j: (bi, jmap(j), 0))
    in_specs = [row(GLA_KD), row(GLA_VD), row(LANES), row(GLA_KD),
                pl.BlockSpec((GLA_RANK, GLA_KD), lambda bi, j: (0, 0)),
                pl.BlockSpec((1, GLA_KD), lambda bi, j: (0, 0)),
                pl.BlockSpec((1, GLA_H, GLA_DK, GLA_DV), lambda bi, j: (bi, 0, 0, 0))]
    args = [k, v, a, q, wa, ba.reshape(1, GLA_KD), s0]
    if final:
        in_specs += [row(GLA_VD), row(GLA_VD), pl.BlockSpec((1, GLA_DV), lambda bi, j: (0, 0))]
        args += [sg, o_prev, g_norm.reshape(1, GLA_DV)]
    return pl.pallas_call(
        functools.partial(_gla_lat_body, reverse=reverse, final=final, cps=cps),
        grid=(b, nj),
        in_specs=in_specs,
        out_specs=row(GLA_VD),
        out_shape=jax.ShapeDtypeStruct((b, s, GLA_VD), BF16),
        scratch_shapes=[pltpu.VMEM((GLA_H, GLA_DK, GLA_DV), F32)],
        compiler_params=_cparams(("arbitrary", "arbitrary")),
        name="gla_lat_bwd" if reverse else "gla_lat_fwd",
    )(*args)


def _pack_pair(lo, hi):
    lo_b = pltpu.bitcast(lo.astype(BF16).astype(F32), I32)
    hi_b = pltpu.bitcast(hi.astype(BF16).astype(F32), I32)
    return lax.shift_right_logical(lo_b, 16) | (hi_b & jnp.int32(-65536))


def _unpack_pair(w):
    lo = pltpu.bitcast(lax.shift_left(w, 16), F32)
    hi = pltpu.bitcast(w & jnp.int32(-65536), F32)
    return lo, hi


def _merge_body(act_ref, og_ref, gc_ref, gg_ref, x_ref, gt_ref, sh_ref, sc_ref, gn_ref,
                wc_ref, wg_ref, wo_ref, wr_ref, br_ref,
                x1_ref, h2_ref, ti_ref, tp_ref):
    yc = _dot(act_ref[...], wc_ref[...])
    yg = _dot(og_ref[...], wg_ref[...])
    m = gc_ref[...].astype(F32) * yc + gg_ref[...].astype(F32) * yg
    x1 = x_ref[...] + gt_ref[0] * _dot(m.astype(BF16), wo_ref[...])
    x1_ref[...] = x1
    ms = jnp.mean(x1 * x1, axis=-1, keepdims=True)
    h2 = (x1 * lax.rsqrt(ms + EPS) * gn_ref[...]) * (1.0 + sc_ref[0]) + sh_ref[0]
    half = D_MODEL // 2
    h2_ref[...] = _pack_pair(h2[:, :half], h2[:, half:])
    logits = _dot3(h2, wr_ref[...]) + br_ref[...]
    lane = lax.broadcasted_iota(I32, logits.shape, 1).astype(F32)
    neg = jnp.float32(-jnp.inf)
    work = jnp.where(lane < N_EXPERTS, logits, neg)
    vals, idxs = [], []
    for _ in range(TOP_K):
        mx = jnp.max(work, axis=-1, keepdims=True)
        ix = jnp.min(jnp.where(work == mx, lane, float(LANES)), axis=-1, keepdims=True)
        vals.append(mx)
        idxs.append(ix)
        work = jnp.where(lane == ix, neg, work)
    es = [jnp.exp(v - vals[0]) for v in vals]
    den = es[0] + es[1] + es[2] + es[3]
    ti = jnp.zeros(logits.shape, F32)
    tp = jnp.zeros(logits.shape, F32)
    for kk in range(TOP_K):
        ti = jnp.where(lane == kk, idxs[kk], ti)
        tp = jnp.where(lane == kk, es[kk] / den, tp)
    ti_ref[...] = ti.astype(I32)
    tp_ref[...] = tp


def _merge(act, og, gc, gg, x2d, gt1, sh2, sc2, gn, wc, wg, wo, wr, br, tm, rows_per_batch):
    n, d = x2d.shape
    per_b = rows_per_batch // tm
    row = lambda w: pl.BlockSpec((tm, w), lambda i: (i, 0))
    vec = pl.BlockSpec((1, 1, d), lambda i: (i // per_b, 0, 0))
    const = lambda shape: pl.BlockSpec(shape, lambda i: (0,) * len(shape))
    return pl.pallas_call(
        _merge_body,
        grid=(n // tm,),
        in_specs=[row(d), row(d), row(d), row(d), row(d), vec, vec, vec, const((1, d)),
                  _resident(wc.shape), _resident(wg.shape), _resident(wo.shape),
                  const(wr.shape), const((1, LANES))],
        out_specs=[row(d), row(d // 2), row(LANES), row(LANES)],
        out_shape=[jax.ShapeDtypeStruct((n, d), F32), jax.ShapeDtypeStruct((n, d // 2), I32),
                   jax.ShapeDtypeStruct((n, LANES), I32), jax.ShapeDtypeStruct((n, LANES), F32)],
        compiler_params=_cparams(("arbitrary",)),
        name="merge_router",
    )(act, og, gc, gg, x2d, gt1, sh2, sc2, gn, wc, wg, wo, wr, br)


def _dispatch_body(dest_ref, h2_ref, xp_in_ref, xp_ref, sem, *, tokens):
    del xp_in_ref

    def row_copy(t, kk):
        dst = dest_ref[0, 0, t * TOP_K + kk]
        return pltpu.make_async_copy(h2_ref.at[pl.ds(t, 1)], xp_ref.at[pl.ds(dst, 1)], sem)

    def issue(t, carry):
        for kk in range(TOP_K):
            row_copy(t, kk).start()
        return carry

    lax.fori_loop(0, tokens, issue, 0, unroll=8)

    def drain(t, carry):
        for _ in range(TOP_K):
            pltpu.make_async_copy(h2_ref.at[pl.ds(0, 1)], xp_ref.at[pl.ds(0, 1)], sem).wait()
        return carry

    lax.fori_loop(0, tokens, drain, 0, unroll=8)


def _dispatch(dest, h2p, xp_init, tokens):
    n, w = h2p.shape
    steps = n // tokens
    dest3 = dest.reshape(steps, 1, tokens * TOP_K)
    return pl.pallas_call(
        functools.partial(_dispatch_body, tokens=tokens),
        grid=(steps,),
        in_specs=[pl.BlockSpec((1, 1, tokens * TOP_K), lambda i: (i, 0, 0),
                               memory_space=pltpu.SMEM),
                  pl.BlockSpec((tokens, w), lambda i: (i, 0)),
                  pl.BlockSpec(memory_space=pl.ANY)],
        out_specs=pl.BlockSpec(memory_space=pl.ANY),
        out_shape=jax.ShapeDtypeStruct(xp_init.shape, xp_init.dtype),
        scratch_shapes=[pltpu.SemaphoreType.DMA],
        input_output_aliases={2: 0},
        compiler_params=_cparams(("arbitrary",)),
        name="moe_dispatch",
    )(dest3, h2p, xp_init)


def _expert_body(be_ref, nreal_ref, xp_ref, w1_ref, b1_ref, w2_ref, b2_ref, yp_ref):
    i = pl.program_id(0)

    @pl.when(i < nreal_ref[0])
    def _():
        lo, hi = _unpack_pair(xp_ref[...])
        x = jnp.concatenate([lo, hi], axis=1).astype(BF16)
        hg = _dot(x, w1_ref[0, :, 0:D_FF]) + b1_ref[0, :, 0:D_FF]
        hl = _dot(x, w1_ref[0, :, D_FF:2 * D_FF]) + b1_ref[0, :, D_FF:2 * D_FF]
        xg = jnp.minimum(hg, SWIGLU_LIMIT)
        xl = jnp.clip(hl, -SWIGLU_LIMIT, SWIGLU_LIMIT)
        act = xg * _sigmoid(SWIGLU_ALPHA * xg) * (xl + 1.0)
        y = _dot(act.astype(BF16), w2_ref[0]) + b2_ref[0]
        half = D_MODEL // 2
        yp_ref[...] = _pack_pair(y[:, :half], y[:, half:])

    @pl.when(i >= nreal_ref[0])
    def _():
        yp_ref[...] = jnp.zeros_like(yp_ref)


def _experts(blk_expert, n_real, xp, w1, b1, w2, b2, bm):
    p, w = xp.shape
    grid_spec = pltpu.PrefetchScalarGridSpec(
        num_scalar_prefetch=2,
        grid=(p // bm,),
        in_specs=[pl.BlockSpec((bm, w), lambda i, be, nr: (i, 0)),
                  pl.BlockSpec((1, D_MODEL, 2 * D_FF), lambda i, be, nr: (be[i], 0, 0)),
                  pl.BlockSpec((1, 1, 2 * D_FF), lambda i, be, nr: (be[i], 0, 0)),
                  pl.BlockSpec((1, D_FF, D_MODEL), lambda i, be, nr: (be[i], 0, 0)),
                  pl.BlockSpec((1, 1, D_MODEL), lambda i, be, nr: (be[i], 0, 0))],
        out_specs=pl.BlockSpec((bm, w), lambda i, be, nr: (i, 0)),
    )
    return pl.pallas_call(
        _expert_body,
        grid_spec=grid_spec,
        out_shape=jax.ShapeDtypeStruct((p, w), I32),
        compiler_params=_cparams(("arbitrary",)),
        name="moe_experts",
    )(blk_expert, n_real, xp, w1, b1, w2, b2)


def _combine_body(dcur_ref, dnext_ref, yp_ref, tp_ref, x1_ref, gt_ref, gf_ref, o_ref,
                  buf_ref, sem, *, tokens):
    i = pl.program_id(0)
    n = pl.num_programs(0)
    slot = i % 2

    def row_copy(dref, s, t, kk):
        src = dref[0, 0, t * TOP_K + kk]
        return pltpu.make_async_copy(yp_ref.at[pl.ds(src, 1)], buf_ref.at[s, kk, pl.ds(t, 1)],
                                     sem.at[s])

    def issue_all(dref, s):
        def issue(t, carry):
            for kk in range(TOP_K):
                row_copy(dref, s, t, kk).start()
            return carry
        lax.fori_loop(0, tokens, issue, 0, unroll=8)

    @pl.when(i == 0)
    def _():
        issue_all(dcur_ref, 0)

    @pl.when(i + 1 < n)
    def _():
        issue_all(dnext_ref, 1 - slot)

    def drain(t, carry):
        for kk in range(TOP_K):
            pltpu.make_async_copy(yp_ref.at[pl.ds(0, 1)], buf_ref.at[slot, kk, pl.ds(0, 1)],
                                  sem.at[slot]).wait()
        return carry

    lax.fori_loop(0, tokens, drain, 0, unroll=8)

    half = D_MODEL // 2
    tp = tp_ref[...]
    y_lo = jnp.zeros((tokens, half), F32)
    y_hi = jnp.zeros((tokens, half), F32)
    for kk in range(TOP_K):
        lo, hi = _unpack_pair(buf_ref[slot, kk])
        pk = tp[:, kk:kk + 1]
        y_lo = y_lo + pk * lo
        y_hi = y_hi + pk * hi
    y = jnp.concatenate([y_lo, y_hi], axis=1)
    x2 = x1_ref[...] + gt_ref[0] * y
    ms = jnp.mean(x2 * x2, axis=-1, keepdims=True)
    o_ref[...] = x2 * lax.rsqrt(ms + EPS) * gf_ref[...]


def _combine(dest, yp, tp, x1, gt2, gf, tokens, rows_per_batch):
    n, d = x1.shape
    steps = n // tokens
    per_b = rows_per_batch // tokens
    dest3 = dest.reshape(steps, 1, tokens * TOP_K)
    dspec = lambda f: pl.BlockSpec((1, 1, tokens * TOP_K), f, memory_space=pltpu.SMEM)
    return pl.pallas_call(
        functools.partial(_combine_body, tokens=tokens),
        grid=(steps,),
        in_specs=[dspec(lambda i: (i, 0, 0)),
                  dspec(lambda i: (jnp.minimum(i + 1, steps - 1), 0, 0)),
                  pl.BlockSpec(memory_space=pl.ANY),
                  pl.BlockSpec((tokens, LANES), lambda i: (i, 0)),
                  pl.BlockSpec((tokens, d), lambda i: (i, 0)),
                  pl.BlockSpec((1, 1, d), lambda i: (i // per_b, 0, 0)),
                  pl.BlockSpec((1, d), lambda i: (0, 0))],
        out_specs=pl.BlockSpec((tokens, d), lambda i: (i, 0)),
        out_shape=jax.ShapeDtypeStruct((n, d), F32),
        scratch_shapes=[pltpu.VMEM((2, TOP_K, tokens, d // 2), I32),
                        pltpu.SemaphoreType.DMA((2,))],
        compiler_params=_cparams(("arbitrary",)),
        name="moe_combine",
    )(dest3, dest3, yp, tp, x1, gt2, gf)


def _routing_tables(top_idx, bm):
    n = top_idx.shape[0]
    nk = n * TOP_K
    onehot = (top_idx[:, :, None] == jnp.arange(N_EXPERTS, dtype=I32)[None, None, :])
    per_tok = jnp.sum(onehot.astype(I32), axis=1)
    incl = jnp.cumsum(per_tok, axis=0)
    excl = incl - per_tok
    counts = incl[-1]
    padded = (counts + bm - 1) // bm * bm
    pad_end = jnp.cumsum(padded)
    pad_start = pad_end - padded
    rank = jnp.take_along_axis(excl, top_idx, axis=1)
    dest = pad_start[top_idx] + rank
    n_blocks = (nk + N_EXPERTS * (bm - 1) + bm - 1) // bm
    blk_expert = jnp.minimum(
        jnp.searchsorted(pad_end, jnp.arange(n_blocks, dtype=I32) * bm, side="right"),
        N_EXPERTS - 1).astype(I32)
    n_real = (pad_end[-1] // bm).astype(I32).reshape(1)
    return dest.astype(I32), blk_expert, n_real, n_blocks


def _layer(x, c, ctx, c_ctx, w_ada, b_ada, g_mix_norm, w_in, w_dw, b_dw, g_conv_ln, b_conv_ln,
           w_conv_out, w_alpha, b_alpha, g_gla_norm, w_gla_out, w_out, g_ffn_norm, w_router,
           b_router, w_exp_in, b_exp_in, w_exp_out, b_exp_out, g_final, *, cfg):
    b, s, d = x.shape
    n = b * s

    rows = (b + 1 + 7) // 8 * 8
    cc = jnp.zeros((rows, d), F32).at[:b].set(c).at[b].set(c_ctx)
    mod = _ada(cc, w_ada, b_ada)
    sh1, sc1, gt1, sh2, sc2, gt2 = [mod[:b, i * d:(i + 1) * d].reshape(b, 1, d) for i in range(6)]
    csh1 = mod[b:b + 1, 0:d]
    csc1 = mod[b:b + 1, d:2 * d]

    a0 = 2 * CONV_W + 2 * GLA_KD + 2 * GLA_VD
    w_in_r = jnp.concatenate(
        [w_in[:, :a0], w_in[:, a0 + 2 * GLA_RANK:], w_in[:, a0:a0 + 2 * GLA_RANK],
         jnp.zeros((d, LANES - 2 * GLA_RANK), F32)], axis=1).astype(BF16)
    gmn = g_mix_norm.reshape(1, d)

    u, q, k, v, sg, gc, gg, a = _inproj_lat(x, gmn, sh1, sc1, w_in_r, cfg["tm_in"])
    kc, vc, ac = _inproj_ctx(ctx, gmn, csh1, csc1, w_in_r, cfg["tm_ctx"])

    act = _conv(u.reshape(n, CONV_W), w_dw, b_dw, g_conv_ln, b_conv_ln, cfg["conv_rows"])

    cps = cfg["gla_cps"]
    st_b = _gla_ctx(kc, vc, ac, w_alpha[1], b_alpha[1], True, cps)
    o_b = _gla_lat(k, v, a, q, w_alpha[1], b_alpha[1], st_b, True, cps)
    st_f = _gla_ctx(kc, vc, ac, w_alpha[0], b_alpha[0], False, cps)
    og = _gla_lat(k, v, a, q, w_alpha[0], b_alpha[0], st_f, False, cps,
                  sg=sg, o_prev=o_b, g_norm=g_gla_norm)

    wr = jnp.zeros((d, LANES), F32).at[:, :N_EXPERTS].set(w_router)
    br = jnp.zeros((1, LANES), F32).at[0, :N_EXPERTS].set(b_router)
    x1, h2p, ti, tp = _merge(
        act, og.reshape(n, d), gc.reshape(n, d), gg.reshape(n, d), x.reshape(n, d),
        gt1, sh2, sc2, g_ffn_norm.reshape(1, d),
        w_conv_out.astype(BF16), w_gla_out.astype(BF16), w_out.astype(BF16), wr, br,
        cfg["tm_merge"], s)

    bm = cfg["moe_block"]
    dest, blk_expert, n_real, n_blocks = _routing_tables(ti[:, :TOP_K], bm)
    xp = _dispatch(dest, h2p, jnp.zeros((n_blocks * bm, d // 2), I32), cfg["moe_tokens"])

    w1 = jnp.concatenate([w_exp_in[:, :, 0::2], w_exp_in[:, :, 1::2]], axis=2).astype(BF16)
    b1 = jnp.concatenate([b_exp_in[:, 0::2], b_exp_in[:, 1::2]], axis=1).reshape(
        N_EXPERTS, 1, 2 * D_FF)
    yp = _experts(blk_expert, n_real, xp, w1, b1, w_exp_out.astype(BF16),
                  b_exp_out.reshape(N_EXPERTS, 1, d), bm)

    out = _combine(dest, yp, tp, x1, gt2, g_final.reshape(1, d), cfg["moe_tokens"], s)
    return out.reshape(b, s, d)


def _config(s, l):
    return dict(tm_in=min(512, s), tm_ctx=min(256, l), conv_rows=4, gla_cps=4,
                tm_merge=min(512, s), moe_block=512, moe_tokens=128)


def kernel(x, c, ctx, c_ctx, w_ada, b_ada, g_mix_norm, w_in, w_dw, b_dw, g_conv_ln, b_conv_ln,
           w_conv_out, w_alpha, b_alpha, g_gla_norm, w_gla_out, w_out, g_ffn_norm, w_router,
           b_router, w_exp_in, b_exp_in, w_exp_out, b_exp_out, g_final):
    depth = w_ada.shape[0]
    assert depth == 1, "single-layer block: the context stream is only consumed by the GLA scan"
    cfg = _config(x.shape[1], ctx.shape[1])
    return _layer(x, c, ctx, c_ctx, w_ada[0], b_ada[0], g_mix_norm[0], w_in[0], w_dw[0], b_dw[0],
                  g_conv_ln[0], b_conv_ln[0], w_conv_out[0], w_alpha[0], b_alpha[0],
                  g_gla_norm[0], w_gla_out[0], w_out[0], g_ffn_norm[0], w_router[0], b_router[0],
                  w_exp_in[0], b_exp_in[0], w_exp_out[0], b_exp_out[0], g_final, cfg=cfg)
```

```python
import functools

import jax
import jax.numpy as jnp
from jax import lax
from jax.experimental import pallas as pl
from jax.experimental.pallas import tpu as pltpu

F32 = jnp.float32
BF16 = jnp.bfloat16
I32 = jnp.int32

D_MODEL = 1024
GRID_W = 64
EPS = 1e-6
CONV_W = 1024
CONV_K = 31
GLA_H = 4
GLA_DK = 128
GLA_DV = 256
GLA_KD = GLA_H * GLA_DK
GLA_VD = GLA_H * GLA_DV
GLA_RANK = 16
GLA_TAU = 16.0
GLA_CHUNK = 64
N_EXPERTS = 32
TOP_K = 4
D_FF = 1024
SWIGLU_ALPHA = 1.702
SWIGLU_LIMIT = 7.0

LANES = 128
VMEM_LIMIT = 56 * 1024 * 1024

_C_CONV_A = 0
_C_CONV_B = _C_CONV_A + CONV_W
_C_Q = _C_CONV_B + CONV_W
_C_K = _C_Q + GLA_KD
_C_V = _C_K + GLA_KD
_C_G = _C_V + GLA_VD
_C_GC = _C_G + GLA_VD
_C_GG = _C_GC + D_MODEL
_C_A = _C_GG + D_MODEL
_C_END = _C_A + LANES


def _cparams(sem):
    return pltpu.CompilerParams(dimension_semantics=sem, vmem_limit_bytes=VMEM_LIMIT)


def _dot(a, b):
    return jnp.dot(a, b, preferred_element_type=F32)


def _split_bf16(x):
    hi = x.astype(BF16)
    lo = (x - hi.astype(F32)).astype(BF16)
    return hi, lo


def _dot3(a, b):
    a_hi, a_lo = _split_bf16(a)
    b_hi, b_lo = _split_bf16(b)
    return _dot(a_hi, b_hi) + _dot(a_lo, b_hi) + _dot(a_hi, b_lo)


def _sigmoid(x):
    return 1.0 / (1.0 + jnp.exp(-x))


def _resident(shape):
    nd = len(shape)
    return pl.BlockSpec(shape, lambda *_: (0,) * nd, pipeline_mode=pl.Buffered(1))


def _ada_body(a_ref, w_ref, b_ref, o_ref):
    a = a_ref[...]
    a = a * _sigmoid(a)
    o_ref[...] = _dot3(a, w_ref[...]) + b_ref[...]


def _ada(cc, w, b):
    rows, d = cc.shape
    n = w.shape[1]
    tn = 512
    return pl.pallas_call(
        _ada_body,
        grid=(n // tn,),
        in_specs=[pl.BlockSpec((rows, d), lambda j: (0, 0)),
                  pl.BlockSpec((d, tn), lambda j: (0, j)),
                  pl.BlockSpec((1, tn), lambda j: (0, j))],
        out_specs=pl.BlockSpec((rows, tn), lambda j: (0, j)),
        out_shape=jax.ShapeDtypeStruct((rows, n), F32),
        compiler_params=_cparams(("arbitrary",)),
        name="ada_mod",
    )(cc, w, b.reshape(1, n))


def _norm_mod(xv, gn, sc, sh):
    ms = jnp.mean(xv * xv, axis=-1, keepdims=True)
    y = xv * lax.rsqrt(ms + EPS) * gn
    return (y * (1.0 + sc) + sh).astype(BF16)


def _inproj_lat_body(x_ref, gn_ref, sh_ref, sc_ref, w_ref,
                     u_ref, q_ref, k_ref, v_ref, sg_ref, gc_ref, gg_ref, a_ref):
    h = _norm_mod(x_ref[0], gn_ref[...], sc_ref[0], sh_ref[0])
    ca = _dot(h, w_ref[:, _C_CONV_A:_C_CONV_B])
    cb = _dot(h, w_ref[:, _C_CONV_B:_C_Q])
    u_ref[0] = (ca * _sigmoid(cb)).astype(BF16)
    q_ref[0] = _dot(h, w_ref[:, _C_Q:_C_K]).astype(BF16)
    k_ref[0] = _dot(h, w_ref[:, _C_K:_C_V]).astype(BF16)
    v_ref[0] = _dot(h, w_ref[:, _C_V:_C_G]).astype(BF16)
    g = _dot(h, w_ref[:, _C_G:_C_GC])
    sg_ref[0] = (g * _sigmoid(g)).astype(BF16)
    gc_ref[0] = _sigmoid(_dot(h, w_ref[:, _C_GC:_C_GG])).astype(BF16)
    gg_ref[0] = _sigmoid(_dot(h, w_ref[:, _C_GG:_C_A])).astype(BF16)
    a_ref[0] = _dot(h, w_ref[:, _C_A:_C_END])


def _inproj_ctx_body(x_ref, gn_ref, sh_ref, sc_ref, w_ref, k_ref, v_ref, a_ref):
    h = _norm_mod(x_ref[0], gn_ref[...], sc_ref[...], sh_ref[...])
    k_ref[0] = _dot(h, w_ref[:, _C_K:_C_V]).astype(BF16)
    v_ref[0] = _dot(h, w_ref[:, _C_V:_C_G]).astype(BF16)
    a_ref[0] = _dot(h, w_ref[:, _C_A:_C_END])


def _inproj_lat(x, gn, sh, sc, w, tm):
    b, s, d = x.shape
    row = lambda n: pl.BlockSpec((1, tm, n), lambda bi, i: (bi, i, 0))
    vec = pl.BlockSpec((1, 1, d), lambda bi, i: (bi, 0, 0))
    shp = lambda n, dt: jax.ShapeDtypeStruct((b, s, n), dt)
    return pl.pallas_call(
        _inproj_lat_body,
        grid=(b, s // tm),
        in_specs=[row(d), pl.BlockSpec((1, d), lambda bi, i: (0, 0)), vec, vec, _resident(w.shape)],
        out_specs=[row(CONV_W), row(GLA_KD), row(GLA_KD), row(GLA_VD), row(GLA_VD),
                   row(d), row(d), row(LANES)],
        out_shape=[shp(CONV_W, BF16), shp(GLA_KD, BF16), shp(GLA_KD, BF16), shp(GLA_VD, BF16),
                   shp(GLA_VD, BF16), shp(d, BF16), shp(d, BF16), shp(LANES, F32)],
        compiler_params=_cparams(("arbitrary", "arbitrary")),
        name="inproj_lat",
    )(x, gn, sh, sc, w)


def _inproj_ctx(ctx, gn, sh, sc, w, tm):
    b, l, d = ctx.shape
    row = lambda n: pl.BlockSpec((1, tm, n), lambda bi, i: (bi, i, 0))
    vec = pl.BlockSpec((1, d), lambda bi, i: (0, 0))
    shp = lambda n, dt: jax.ShapeDtypeStruct((b, l, n), dt)
    return pl.pallas_call(
        _inproj_ctx_body,
        grid=(b, l // tm),
        in_specs=[row(d), vec, vec, vec, _resident(w.shape)],
        out_specs=[row(GLA_KD), row(GLA_VD), row(LANES)],
        out_shape=[shp(GLA_KD, BF16), shp(GLA_VD, BF16), shp(LANES, F32)],
        compiler_params=_cparams(("arbitrary", "arbitrary")),
        name="inproj_ctx",
    )(ctx, gn, sh, sc, w)


_CONV_SUB = 8
_CONV_SPAN = GRID_W + _CONV_SUB * ((CONV_K - 1) // _CONV_SUB)


def _conv_shift_matrix():
    row = lax.broadcasted_iota(I32, (_CONV_SUB, _CONV_SPAN, GRID_W), 1)
    shift = lax.broadcasted_iota(I32, (_CONV_SUB, _CONV_SPAN, GRID_W), 0)
    col = lax.broadcasted_iota(I32, (_CONV_SUB, _CONV_SPAN, GRID_W), 2)
    m = jnp.where(col == row + shift - CONV_K // 2, 1.0, 0.0)
    return m.reshape(_CONV_SUB * _CONV_SPAN, GRID_W).astype(BF16)


def _conv_body(u_ref, sm_ref, w_ref, bdw_ref, gln_ref, bln_ref, o_ref, sh_ref, y_ref, *,
               rows_per_step):
    for r in range(rows_per_step):
        buf = r % 2
        sh_ref[buf] = _dot(sm_ref[...], u_ref[r * GRID_W:(r + 1) * GRID_W, :])
        for cb in range(CONV_W // LANES):
            ls = slice(cb * LANES, (cb + 1) * LANES)
            acc = None
            for k in range(CONV_K):
                s, a = k % _CONV_SUB, k // _CONV_SUB
                r0 = s * _CONV_SPAN + _CONV_SUB * a
                term = sh_ref[buf, r0:r0 + GRID_W, ls] * w_ref[k:k + 1, ls]
                acc = term if acc is None else acc + term
            y_ref[r * GRID_W:(r + 1) * GRID_W, ls] = acc + bdw_ref[:, ls]
    y = y_ref[...]
    mu = jnp.mean(y, axis=-1, keepdims=True)
    yc = y - mu
    var = jnp.mean(yc * yc, axis=-1, keepdims=True)
    yn = yc * lax.rsqrt(var + EPS) * gln_ref[...] + bln_ref[...]
    o_ref[...] = (yn * _sigmoid(yn)).astype(BF16)


def _conv(u, w_dw, b_dw, g_ln, b_ln, rows_per_step):
    n, cw = u.shape
    tm = rows_per_step * GRID_W
    vec = pl.BlockSpec((1, cw), lambda i: (0, 0))
    wpad = jnp.zeros((32, cw), F32).at[:CONV_K].set(w_dw)
    sm = _conv_shift_matrix()
    return pl.pallas_call(
        functools.partial(_conv_body, rows_per_step=rows_per_step),
        grid=(n // tm,),
        in_specs=[pl.BlockSpec((tm, cw), lambda i: (i, 0)),
                  pl.BlockSpec(sm.shape, lambda i: (0, 0)),
                  pl.BlockSpec((32, cw), lambda i: (0, 0)), vec, vec, vec],
        out_specs=pl.BlockSpec((tm, cw), lambda i: (i, 0)),
        out_shape=jax.ShapeDtypeStruct((n, cw), BF16),
        scratch_shapes=[pltpu.VMEM((2, _CONV_SUB * _CONV_SPAN, cw), F32),
                        pltpu.VMEM((tm, cw), F32)],
        compiler_params=_cparams(("arbitrary",)),
        name="conv_module",
    )(u, sm, wpad, b_dw.reshape(1, cw), g_ln.reshape(1, cw), b_ln.reshape(1, cw))


def _log_sigmoid(z):
    return jnp.minimum(z, 0.0) - jnp.log(1.0 + jnp.exp(-jnp.abs(z)))


def _gla_chunk(k, v, a, wa, ba, tri, st_ref, *, reverse, q=None):
    c = GLA_CHUNK
    z = _dot3(a, wa) + ba
    loga = _log_sigmoid(z) * (1.0 / GLA_TAU)
    l_hi, l_lo = _split_bf16(loga)
    bcum = _dot(tri, l_hi) + _dot(tri, l_lo)
    tot = bcum[0:1, :] if reverse else bcum[c - 1:c, :]
    eb = jnp.exp(bcum)
    k32 = k.astype(F32)
    k_inv = (k32 * jnp.exp(-bcum)).astype(BF16)
    k_end = (k32 * jnp.exp(tot - bcum)).astype(BF16)
    dec = jnp.exp(tot)
    if q is not None:
        q_dec = (q.astype(F32) * eb * (GLA_DK ** -0.5)).astype(BF16)
        row = lax.broadcasted_iota(I32, (c, c), 0)
        col = lax.broadcasted_iota(I32, (c, c), 1)
        keep = (col >= row) if reverse else (col <= row)
    outs = []
    for h in range(GLA_H):
        ks = slice(h * GLA_DK, (h + 1) * GLA_DK)
        vs = slice(h * GLA_DV, (h + 1) * GLA_DV)
        st = st_ref[h]
        vh = v[:, vs]
        if q is not None:
            scores = lax.dot_general(q_dec[:, ks], k_inv[:, ks], (((1,), (1,)), ((), ())),
                                     preferred_element_type=F32)
            scores = jnp.where(keep, scores, 0.0).astype(BF16)
            outs.append(_dot(scores, vh) + _dot(q_dec[:, ks], st.astype(BF16)))
        kv = lax.dot_general(k_end[:, ks], vh, (((0,), (0,)), ((), ())),
                             preferred_element_type=F32)
        dt = jnp.transpose(jnp.broadcast_to(dec[:, ks], (GLA_DK, GLA_DK)))
        st_ref[h] = st * jnp.concatenate([dt] * (GLA_DV // GLA_DK), axis=1) + kv
    if q is None:
        return None
    return jnp.concatenate(outs, axis=1)


def _tri(reverse):
    c = GLA_CHUNK
    row = lax.broadcasted_iota(I32, (c, c), 0)
    col = lax.broadcasted_iota(I32, (c, c), 1)
    keep = (col >= row) if reverse else (col <= row)
    return jnp.where(keep, 1.0, 0.0).astype(BF16)


def _gla_ctx_body(k_ref, v_ref, a_ref, wa_ref, ba_ref, st_ref, *, reverse, cps):
    j = pl.program_id(1)

    @pl.when(j == 0)
    def _():
        st_ref[...] = jnp.zeros_like(st_ref)

    tri = _tri(reverse)
    a0 = GLA_RANK if reverse else 0
    for s in (range(cps - 1, -1, -1) if reverse else range(cps)):
        rs = slice(s * GLA_CHUNK, (s + 1) * GLA_CHUNK)
        _gla_chunk(k_ref[0, rs, :], v_ref[0, rs, :], a_ref[0, rs, a0:a0 + GLA_RANK],
                   wa_ref[...], ba_ref[...], tri, st_ref.at[0], reverse=reverse)


def _gla_lat_body(*refs, reverse, final, cps):
    if final:
        (k_ref, v_ref, a_ref, q_ref, wa_ref, ba_ref, s0_ref, sg_ref, op_ref, gn_ref,
         o_ref, st_ref) = refs
    else:
        k_ref, v_ref, a_ref, q_ref, wa_ref, ba_ref, s0_ref, o_ref, st_ref = refs
    j = pl.program_id(1)

    @pl.when(j == 0)
    def _():
        st_ref[...] = s0_ref[0]

    tri = _tri(reverse)
    a0 = GLA_RANK if reverse else 0
    for s in (range(cps - 1, -1, -1) if reverse else range(cps)):
        rs = slice(s * GLA_CHUNK, (s + 1) * GLA_CHUNK)
        o = _gla_chunk(k_ref[0, rs, :], v_ref[0, rs, :], a_ref[0, rs, a0:a0 + GLA_RANK],
                       wa_ref[...], ba_ref[...], tri, st_ref, reverse=reverse,
                       q=q_ref[0, rs, :])
        if not final:
            o_ref[0, rs, :] = o.astype(BF16)
            continue
        o = o + op_ref[0, rs, :].astype(F32)
        parts = []
        for h in range(GLA_H):
            oh = o[:, h * GLA_DV:(h + 1) * GLA_DV]
            ms = jnp.mean(oh * oh, axis=-1, keepdims=True)
            parts.append(oh * lax.rsqrt(ms + EPS) * gn_ref[...])
        on = jnp.concatenate(parts, axis=1)
        o_ref[0, rs, :] = (on * sg_ref[0, rs, :].astype(F32)).astype(BF16)


def _gla_ctx(k, v, a, wa, ba, reverse, cps):
    b, l, _ = k.shape
    tm = cps * GLA_CHUNK
    nj = l // tm
    jmap = (lambda j: nj - 1 - j) if reverse else (lambda j: j)
    row = lambda n: pl.BlockSpec((1, tm, n), lambda bi, j: (bi, jmap(j), 0))
    return pl.pallas_call(
        functools.partial(_gla_ctx_body, reverse=reverse, cps=cps),
        grid=(b, nj),
        in_specs=[row(GLA_KD), row(GLA_VD), row(LANES),
                  pl.BlockSpec((GLA_RANK, GLA_KD), lambda bi, j: (0, 0)),
                  pl.BlockSpec((1, GLA_KD), lambda bi, j: (0, 0))],
        out_specs=pl.BlockSpec((1, GLA_H, GLA_DK, GLA_DV), lambda bi, j: (bi, 0, 0, 0)),
        out_shape=jax.ShapeDtypeStruct((b, GLA_H, GLA_DK, GLA_DV), F32),
        compiler_params=_cparams(("arbitrary", "arbitrary")),
        name="gla_ctx_bwd" if reverse else "gla_ctx_fwd",
    )(k, v, a, wa, ba.reshape(1, GLA_KD))


def _gla_lat(k, v, a, q, wa, ba, s0, reverse, cps, sg=None, o_prev=None, g_norm=None):
    b, s, _ = k.shape
    final = sg is not None
    tm = cps * GLA_CHUNK
    nj = s // tm
    jmap = (lambda j: nj - 1 - j) if reverse else (lambda j: j)
    row = lambda n: pl.BlockSpec((1, tm, n), lambda bi, j: (bi, jmap(j), 0))
    in_specs = [row(GLA_KD), row(GLA_VD), row(LANES), row(GLA_KD),
                pl.BlockSpec((GLA_RANK, GLA_KD), lambda bi, j: (0, 0)),
                pl.BlockSpec((1, GLA_KD), lambda bi, j: (0, 0)),
                pl.BlockSpec((1, GLA_H, GLA_DK, GLA_DV), lambda bi, j: (bi, 0, 0, 0))]
    args = [k, v, a, q, wa, ba.reshape(1, GLA_KD), s0]
    if final:
        in_specs += [row(GLA_VD), row(GLA_VD), pl.BlockSpec((1, GLA_DV), lambda bi, j: (0, 0))]
        args += [sg, o_prev, g_norm.reshape(1, GLA_DV)]
    return pl.pallas_call(
        functools.partial(_gla_lat_body, reverse=reverse, final=final, cps=cps),
        grid=(b, nj),
        in_specs=in_specs,
        out_specs=row(GLA_VD),
        out_shape=jax.ShapeDtypeStruct((b, s, GLA_VD), BF16),
        scratch_shapes=[pltpu.VMEM((GLA_H, GLA_DK, GLA_DV), F32)],
        compiler_params=_cparams(("arbitrary", "arbitrary")),
        name="gla_lat_bwd" if reverse else "gla_lat_fwd",
    )(*args)


def _pack_pair(lo, hi):
    lo_b = pltpu.bitcast(lo.astype(BF16).astype(F32), I32)
    hi_b = pltpu.bitcast(hi.astype(BF16).astype(F32), I32)
    return lax.shift_right_logical(lo_b, 16) | (hi_b & jnp.int32(-65536))


def _unpack_pair(w):
    lo = pltpu.bitcast(lax.shift_left(w, 16), F32)
    hi = pltpu.bitcast(w & jnp.int32(-65536), F32)
    return lo, hi


def _merge_body(act_ref, og_ref, gc_ref, gg_ref, x_ref, gt_ref, sh_ref, sc_ref, gn_ref,
                wc_ref, wg_ref, wo_ref, wr_ref, br_ref,
                x1_ref, h2_ref, ti_ref, tp_ref):
    yc = _dot(act_ref[...], wc_ref[...])
    yg = _dot(og_ref[...], wg_ref[...])
    m = gc_ref[...].astype(F32) * yc + gg_ref[...].astype(F32) * yg
    x1 = x_ref[...] + gt_ref[0] * _dot(m.astype(BF16), wo_ref[...])
    x1_ref[...] = x1
    ms = jnp.mean(x1 * x1, axis=-1, keepdims=True)
    h2 = (x1 * lax.rsqrt(ms + EPS) * gn_ref[...]) * (1.0 + sc_ref[0]) + sh_ref[0]
    half = D_MODEL // 2
    h2_ref[...] = _pack_pair(h2[:, :half], h2[:, half:])
    logits = _dot3(h2, wr_ref[...]) + br_ref[...]
    lane = lax.broadcasted_iota(I32, logits.shape, 1).astype(F32)
    neg = jnp.float32(-jnp.inf)
    work = jnp.where(lane < N_EXPERTS, logits, neg)
    vals, idxs = [], []
    for _ in range(TOP_K):
        mx = jnp.max(work, axis=-1, keepdims=True)
        ix = jnp.min(jnp.where(work == mx, lane, float(LANES)), axis=-1, keepdims=True)
        vals.append(mx)
        idxs.append(ix)
        work = jnp.where(lane == ix, neg, work)
    es = [jnp.exp(v - vals[0]) for v in vals]
    den = es[0] + es[1] + es[2] + es[3]
    ti = jnp.zeros(logits.shape, F32)
    tp = jnp.zeros(logits.shape, F32)
    for kk in range(TOP_K):
        ti = jnp.where(lane == kk, idxs[kk], ti)
        tp = jnp.where(lane == kk, es[kk] / den, tp)
    ti_ref[...] = ti.astype(I32)
    tp_ref[...] = tp


def _merge(act, og, gc, gg, x2d, gt1, sh2, sc2, gn, wc, wg, wo, wr, br, tm, rows_per_batch):
    n, d = x2d.shape
    per_b = rows_per_batch // tm
    row = lambda w: pl.BlockSpec((tm, w), lambda i: (i, 0))
    vec = pl.BlockSpec((1, 1, d), lambda i: (i // per_b, 0, 0))
    const = lambda shape: pl.BlockSpec(shape, lambda i: (0,) * len(shape))
    return pl.pallas_call(
        _merge_body,
        grid=(n // tm,),
        in_specs=[row(d), row(d), row(d), row(d), row(d), vec, vec, vec, const((1, d)),
                  _resident(wc.shape), _resident(wg.shape), _resident(wo.shape),
                  const(wr.shape), const((1, LANES))],
        out_specs=[row(d), row(d // 2), row(LANES), row(LANES)],
        out_shape=[jax.ShapeDtypeStruct((n, d), F32), jax.ShapeDtypeStruct((n, d // 2), I32),
                   jax.ShapeDtypeStruct((n, LANES), I32), jax.ShapeDtypeStruct((n, LANES), F32)],
        compiler_params=_cparams(("arbitrary",)),
        name="merge_router",
    )(act, og, gc, gg, x2d, gt1, sh2, sc2, gn, wc, wg, wo, wr, br)


def _dispatch_body(dest_ref, h2_ref, xp_in_ref, xp_ref, sem, *, tokens):
    del xp_in_ref

    def row_copy(t, kk):
        dst = dest_ref[0, 0, t * TOP_K + kk]
        return pltpu.make_async_copy(h2_ref.at[pl.ds(t, 1)], xp_ref.at[pl.ds(dst, 1)], sem)

    def issue(t, carry):
        for kk in range(TOP_K):
            row_copy(t, kk).start()
        return carry

    lax.fori_loop(0, tokens, issue, 0, unroll=8)

    def drain(t, carry):
        for _ in range(TOP_K):
            pltpu.make_async_copy(h2_ref.at[pl.ds(0, 1)], xp_ref.at[pl.ds(0, 1)], sem).wait()
        return carry

    lax.fori_loop(0, tokens, drain, 0, unroll=8)


def _dispatch(dest, h2p, xp_init, tokens):
    n, w = h2p.shape
    steps = n // tokens
    dest3 = dest.reshape(steps, 1, tokens * TOP_K)
    return pl.pallas_call(
        functools.partial(_dispatch_body, tokens=tokens),
        grid=(steps,),
        in_specs=[pl.BlockSpec((1, 1, tokens * TOP_K), lambda i: (i, 0, 0),
                               memory_space=pltpu.SMEM),
                  pl.BlockSpec((tokens, w), lambda i: (i, 0)),
                  pl.BlockSpec(memory_space=pl.ANY)],
        out_specs=pl.BlockSpec(memory_space=pl.ANY),
        out_shape=jax.ShapeDtypeStruct(xp_init.shape, xp_init.dtype),
        scratch_shapes=[pltpu.SemaphoreType.DMA],
        input_output_aliases={2: 0},
        compiler_params=_cparams(("arbitrary",)),
        name="moe_dispatch",
    )(dest3, h2p, xp_init)


_GLU_GROUP = 2 * LANES


def _deinterleave_matrix():
    src = lax.broadcasted_iota(I32, (_GLU_GROUP, _GLU_GROUP), 0)
    dst = lax.broadcasted_iota(I32, (_GLU_GROUP, _GLU_GROUP), 1)
    want = jnp.where(dst < LANES, 2 * dst, 2 * (dst - LANES) + 1)
    return jnp.where(src == want, 1.0, 0.0).astype(BF16)


def _expert_body(be_ref, nreal_ref, xp_ref, w1_ref, b1_ref, w2_ref, b2_ref, yp_ref,
                 w1s_ref, w2s_ref):
    i = pl.program_id(0)
    new_expert = jnp.logical_or(i == 0, be_ref[i] != be_ref[jnp.maximum(i - 1, 0)])

    @pl.when(jnp.logical_and(new_expert, i < nreal_ref[0]))
    def _():
        perm = _deinterleave_matrix()
        for g in range(2 * D_FF // _GLU_GROUP):
            cs = slice(g * _GLU_GROUP, (g + 1) * _GLU_GROUP)
            w1s_ref[:, cs] = _dot(w1_ref[0, :, cs].astype(BF16), perm).astype(BF16)
        w2s_ref[...] = w2_ref[0].astype(BF16)

    @pl.when(i < nreal_ref[0])
    def _():
        lo, hi = _unpack_pair(xp_ref[...])
        x = jnp.concatenate([lo, hi], axis=1).astype(BF16)
        hid = _dot(x, w1s_ref[...]) + b1_ref[0]
        ngrp = 2 * D_FF // _GLU_GROUP
        hg = jnp.concatenate(
            [hid[:, g * _GLU_GROUP:g * _GLU_GROUP + LANES] for g in range(ngrp)], axis=1)
        hl = jnp.concatenate(
            [hid[:, g * _GLU_GROUP + LANES:(g + 1) * _GLU_GROUP] for g in range(ngrp)], axis=1)
        xg = jnp.minimum(hg, SWIGLU_LIMIT)
        xl = jnp.clip(hl, -SWIGLU_LIMIT, SWIGLU_LIMIT)
        act = xg * _sigmoid(SWIGLU_ALPHA * xg) * (xl + 1.0)
        y = _dot(act.astype(BF16), w2s_ref[...]) + b2_ref[0]
        half = D_MODEL // 2
        yp_ref[...] = _pack_pair(y[:, :half], y[:, half:])

    @pl.when(i >= nreal_ref[0])
    def _():
        yp_ref[...] = jnp.zeros_like(yp_ref)


def _experts(blk_expert, n_real, xp, w1, b1, w2, b2, bm):
    p, w = xp.shape
    grid_spec = pltpu.PrefetchScalarGridSpec(
        num_scalar_prefetch=2,
        grid=(p // bm,),
        in_specs=[pl.BlockSpec((bm, w), lambda i, be, nr: (i, 0)),
                  pl.BlockSpec((1, D_MODEL, 2 * D_FF), lambda i, be, nr: (be[i], 0, 0)),
                  pl.BlockSpec((1, 1, 2 * D_FF), lambda i, be, nr: (be[i], 0, 0)),
                  pl.BlockSpec((1, D_FF, D_MODEL), lambda i, be, nr: (be[i], 0, 0)),
                  pl.BlockSpec((1, 1, D_MODEL), lambda i, be, nr: (be[i], 0, 0))],
        out_specs=pl.BlockSpec((bm, w), lambda i, be, nr: (i, 0)),
        scratch_shapes=[pltpu.VMEM((D_MODEL, 2 * D_FF), BF16), pltpu.VMEM((D_FF, D_MODEL), BF16)],
    )
    return pl.pallas_call(
        _expert_body,
        grid_spec=grid_spec,
        out_shape=jax.ShapeDtypeStruct((p, w), I32),
        compiler_params=_cparams(("arbitrary",)),
        name="moe_experts",
    )(blk_expert, n_real, xp, w1, b1, w2, b2)


def _combine_body(dcur_ref, dnext_ref, yp_ref, tp_ref, x1_ref, gt_ref, gf_ref, o_ref,
                  buf_ref, sem, *, tokens):
    i = pl.program_id(0)
    n = pl.num_programs(0)
    slot = i % 2

    def row_copy(dref, s, t, kk):
        src = dref[0, 0, t * TOP_K + kk]
        return pltpu.make_async_copy(yp_ref.at[pl.ds(src, 1)], buf_ref.at[s, kk, pl.ds(t, 1)],
                                     sem.at[s])

    def issue_all(dref, s):
        def issue(t, carry):
            for kk in range(TOP_K):
                row_copy(dref, s, t, kk).start()
            return carry
        lax.fori_loop(0, tokens, issue, 0, unroll=8)

    @pl.when(i == 0)
    def _():
        issue_all(dcur_ref, 0)

    @pl.when(i + 1 < n)
    def _():
        issue_all(dnext_ref, 1 - slot)

    def drain(t, carry):
        for kk in range(TOP_K):
            pltpu.make_async_copy(yp_ref.at[pl.ds(0, 1)], buf_ref.at[slot, kk, pl.ds(0, 1)],
                                  sem.at[slot]).wait()
        return carry

    lax.fori_loop(0, tokens, drain, 0, unroll=8)

    half = D_MODEL // 2
    tp = tp_ref[...]
    y_lo = jnp.zeros((tokens, half), F32)
    y_hi = jnp.zeros((tokens, half), F32)
    for kk in range(TOP_K):
        lo, hi = _unpack_pair(buf_ref[slot, kk])
        pk = tp[:, kk:kk + 1]
        y_lo = y_lo + pk * lo
        y_hi = y_hi + pk * hi
    y = jnp.concatenate([y_lo, y_hi], axis=1)
    x2 = x1_ref[...] + gt_ref[0] * y
    ms = jnp.mean(x2 * x2, axis=-1, keepdims=True)
    o_ref[...] = x2 * lax.rsqrt(ms + EPS) * gf_ref[...]


def _combine(dest, yp, tp, x1, gt2, gf, tokens, rows_per_batch):
    n, d = x1.shape
    steps = n // tokens
    per_b = rows_per_batch // tokens
    dest3 = dest.reshape(steps, 1, tokens * TOP_K)
    dspec = lambda f: pl.BlockSpec((1, 1, tokens * TOP_K), f, memory_space=pltpu.SMEM)
    return pl.pallas_call(
        functools.partial(_combine_body, tokens=tokens),
        grid=(steps,),
        in_specs=[dspec(lambda i: (i, 0, 0)),
                  dspec(lambda i: (jnp.minimum(i + 1, steps - 1), 0, 0)),
                  pl.BlockSpec(memory_space=pl.ANY),
                  pl.BlockSpec((tokens, LANES), lambda i: (i, 0)),
                  pl.BlockSpec((tokens, d), lambda i: (i, 0)),
                  pl.BlockSpec((1, 1, d), lambda i: (i // per_b, 0, 0)),
                  pl.BlockSpec((1, d), lambda i: (0, 0))],
        out_specs=pl.BlockSpec((tokens, d), lambda i: (i, 0)),
        out_shape=jax.ShapeDtypeStruct((n, d), F32),
        scratch_shapes=[pltpu.VMEM((2, TOP_K, tokens, d // 2), I32),
                        pltpu.SemaphoreType.DMA((2,))],
        compiler_params=_cparams(("arbitrary",)),
        name="moe_combine",
    )(dest3, dest3, yp, tp, x1, gt2, gf)


def _routing_tables(top_idx, bm):
    n = top_idx.shape[0]
    nk = n * TOP_K
    onehot = (top_idx[:, :, None] == jnp.arange(N_EXPERTS, dtype=I32)[None, None, :])
    per_tok = jnp.sum(onehot.astype(I32), axis=1)
    incl = jnp.cumsum(per_tok, axis=0)
    excl = incl - per_tok
    counts = incl[-1]
    padded = (counts + bm - 1) // bm * bm
    pad_end = jnp.cumsum(padded)
    pad_start = pad_end - padded
    rank = jnp.take_along_axis(excl, top_idx, axis=1)
    dest = pad_start[top_idx] + rank
    n_blocks = (nk + N_EXPERTS * (bm - 1) + bm - 1) // bm
    blk_expert = jnp.minimum(
        jnp.searchsorted(pad_end, jnp.arange(n_blocks, dtype=I32) * bm, side="right"),
        N_EXPERTS - 1).astype(I32)
    n_real = (pad_end[-1] // bm).astype(I32).reshape(1)
    return dest.astype(I32), blk_expert, n_real, n_blocks


def _layer(x, c, ctx, c_ctx, w_ada, b_ada, g_mix_norm, w_in, w_dw, b_dw, g_conv_ln, b_conv_ln,
           w_conv_out, w_alpha, b_alpha, g_gla_norm, w_gla_out, w_out, g_ffn_norm, w_router,
           b_router, w_exp_in, b_exp_in, w_exp_out, b_exp_out, g_final, *, cfg):
    b, s, d = x.shape
    n = b * s

    rows = (b + 1 + 7) // 8 * 8
    cc = jnp.zeros((rows, d), F32).at[:b].set(c).at[b].set(c_ctx)
    mod = _ada(cc, w_ada, b_ada)
    sh1, sc1, gt1, sh2, sc2, gt2 = [mod[:b, i * d:(i + 1) * d].reshape(b, 1, d) for i in range(6)]
    csh1 = mod[b:b + 1, 0:d]
    csc1 = mod[b:b + 1, d:2 * d]

    a0 = 2 * CONV_W + 2 * GLA_KD + 2 * GLA_VD
    w_in_r = jnp.concatenate(
        [w_in[:, :a0], w_in[:, a0 + 2 * GLA_RANK:], w_in[:, a0:a0 + 2 * GLA_RANK],
         jnp.zeros((d, LANES - 2 * GLA_RANK), F32)], axis=1).astype(BF16)
    gmn = g_mix_norm.reshape(1, d)

    u, q, k, v, sg, gc, gg, a = _inproj_lat(x, gmn, sh1, sc1, w_in_r, cfg["tm_in"])
    kc, vc, ac = _inproj_ctx(ctx, gmn, csh1, csc1, w_in_r, cfg["tm_ctx"])

    act = _conv(u.reshape(n, CONV_W), w_dw, b_dw, g_conv_ln, b_conv_ln, cfg["conv_rows"])

    cps = cfg["gla_cps"]
    st_b = _gla_ctx(kc, vc, ac, w_alpha[1], b_alpha[1], True, cps)
    o_b = _gla_lat(k, v, a, q, w_alpha[1], b_alpha[1], st_b, True, cps)
    st_f = _gla_ctx(kc, vc, ac, w_alpha[0], b_alpha[0], False, cps)
    og = _gla_lat(k, v, a, q, w_alpha[0], b_alpha[0], st_f, False, cps,
                  sg=sg, o_prev=o_b, g_norm=g_gla_norm)

    wr = jnp.zeros((d, LANES), F32).at[:, :N_EXPERTS].set(w_router)
    br = jnp.zeros((1, LANES), F32).at[0, :N_EXPERTS].set(b_router)
    x1, h2p, ti, tp = _merge(
        act, og.reshape(n, d), gc.reshape(n, d), gg.reshape(n, d), x.reshape(n, d),
        gt1, sh2, sc2, g_ffn_norm.reshape(1, d),
        w_conv_out.astype(BF16), w_gla_out.astype(BF16), w_out.astype(BF16), wr, br,
        cfg["tm_merge"], s)

    bm = cfg["moe_block"]
    dest, blk_expert, n_real, n_blocks = _routing_tables(ti[:, :TOP_K], bm)
    xp = _dispatch(dest, h2p, jnp.zeros((n_blocks * bm, d // 2), I32), cfg["moe_tokens"])

    b1 = b_exp_in.reshape(N_EXPERTS, 2 * D_FF // _GLU_GROUP, LANES, 2).transpose(0, 1, 3, 2)
    b1 = b1.reshape(N_EXPERTS, 1, 2 * D_FF)
    yp = _experts(blk_expert, n_real, xp, w_exp_in, b1, w_exp_out,
                  b_exp_out.reshape(N_EXPERTS, 1, d), bm)

    out = _combine(dest, yp, tp, x1, gt2, g_final.reshape(1, d), cfg["moe_tokens"], s)
    return out.reshape(b, s, d)


def _config(s, l):
    return dict(tm_in=min(512, s), tm_ctx=min(256, l), conv_rows=4, gla_cps=4,
                tm_merge=min(512, s), moe_block=512, moe_tokens=128)


def kernel(x, c, ctx, c_ctx, w_ada, b_ada, g_mix_norm, w_in, w_dw, b_dw, g_conv_ln, b_conv_ln,
           w_conv_out, w_alpha, b_alpha, g_gla_norm, w_gla_out, w_out, g_ffn_norm, w_router,
           b_router, w_exp_in, b_exp_in, w_exp_out, b_exp_out, g_final):
    depth = w_ada.shape[0]
    assert depth == 1, "single-layer block: the context stream is only consumed by the GLA scan"
    cfg = _config(x.shape[1], ctx.shape[1])
    return _layer(x, c, ctx, c_ctx, w_ada[0], b_ada[0], g_mix_norm[0], w_in[0], w_dw[0], b_dw[0],
                  g_conv_ln[0], b_conv_ln[0], w_conv_out[0], w_alpha[0], b_alpha[0],
                  g_gla_norm[0], w_gla_out[0], w_out[0], g_ffn_norm[0], w_router[0], b_router[0],
                  w_exp_in[0], b_exp_in[0], w_exp_out[0], b_exp_out[0], g_final, cfg=cfg)
```

```python
import functools

import jax
import jax.numpy as jnp
from jax import lax
from jax.experimental import pallas as pl
from jax.experimental.pallas import tpu as pltpu

F32 = jnp.float32
BF16 = jnp.bfloat16
I32 = jnp.int32

D_MODEL = 1024
GRID_W = 64
EPS = 1e-6
CONV_W = 1024
CONV_K = 31
GLA_H = 4
GLA_DK = 128
GLA_DV = 256
GLA_KD = GLA_H * GLA_DK
GLA_VD = GLA_H * GLA_DV
GLA_RANK = 16
GLA_TAU = 16.0
GLA_CHUNK = 64
N_EXPERTS = 32
TOP_K = 4
D_FF = 1024
SWIGLU_ALPHA = 1.702
SWIGLU_LIMIT = 7.0

LANES = 128
VMEM_LIMIT = 56 * 1024 * 1024

_C_CONV_A = 0
_C_CONV_B = _C_CONV_A + CONV_W
_C_Q = _C_CONV_B + CONV_W
_C_K = _C_Q + GLA_KD
_C_V = _C_K + GLA_KD
_C_G = _C_V + GLA_VD
_C_GC = _C_G + GLA_VD
_C_GG = _C_GC + D_MODEL
_C_A = _C_GG + D_MODEL
_C_END = _C_A + LANES


def _cparams(sem):
    return pltpu.CompilerParams(dimension_semantics=sem, vmem_limit_bytes=VMEM_LIMIT)


def _dot(a, b):
    return jnp.dot(a, b, preferred_element_type=F32)


def _split_bf16(x):
    hi = x.astype(BF16)
    lo = (x - hi.astype(F32)).astype(BF16)
    return hi, lo


def _dot3(a, b):
    a_hi, a_lo = _split_bf16(a)
    b_hi, b_lo = _split_bf16(b)
    return _dot(a_hi, b_hi) + _dot(a_lo, b_hi) + _dot(a_hi, b_lo)


def _sigmoid(x):
    return 1.0 / (1.0 + jnp.exp(-x))


def _resident(shape):
    nd = len(shape)
    return pl.BlockSpec(shape, lambda *_: (0,) * nd, pipeline_mode=pl.Buffered(1))


def _ada_body(a_ref, w_ref, b_ref, o_ref):
    a = a_ref[...]
    a = a * _sigmoid(a)
    o_ref[...] = _dot3(a, w_ref[...]) + b_ref[...]


def _ada(cc, w, b):
    rows, d = cc.shape
    n = w.shape[1]
    tn = 512
    return pl.pallas_call(
        _ada_body,
        grid=(n // tn,),
        in_specs=[pl.BlockSpec((rows, d), lambda j: (0, 0)),
                  pl.BlockSpec((d, tn), lambda j: (0, j)),
                  pl.BlockSpec((1, tn), lambda j: (0, j))],
        out_specs=pl.BlockSpec((rows, tn), lambda j: (0, j)),
        out_shape=jax.ShapeDtypeStruct((rows, n), F32),
        compiler_params=_cparams(("arbitrary",)),
        name="ada_mod",
    )(cc, w, b.reshape(1, n))


def _norm_mod(xv, gn, sc, sh):
    ms = jnp.mean(xv * xv, axis=-1, keepdims=True)
    y = xv * lax.rsqrt(ms + EPS) * gn
    return (y * (1.0 + sc) + sh).astype(BF16)


def _inproj_lat_body(x_ref, gn_ref, sh_ref, sc_ref, w_ref,
                     u_ref, q_ref, k_ref, v_ref, sg_ref, gc_ref, gg_ref, a_ref):
    h = _norm_mod(x_ref[0], gn_ref[...], sc_ref[0], sh_ref[0])
    ca = _dot(h, w_ref[:, _C_CONV_A:_C_CONV_B])
    cb = _dot(h, w_ref[:, _C_CONV_B:_C_Q])
    u_ref[0] = (ca * _sigmoid(cb)).astype(BF16)
    q_ref[0] = _dot(h, w_ref[:, _C_Q:_C_K]).astype(BF16)
    k_ref[0] = _dot(h, w_ref[:, _C_K:_C_V]).astype(BF16)
    v_ref[0] = _dot(h, w_ref[:, _C_V:_C_G]).astype(BF16)
    g = _dot(h, w_ref[:, _C_G:_C_GC])
    sg_ref[0] = (g * _sigmoid(g)).astype(BF16)
    gc_ref[0] = _sigmoid(_dot(h, w_ref[:, _C_GC:_C_GG])).astype(BF16)
    gg_ref[0] = _sigmoid(_dot(h, w_ref[:, _C_GG:_C_A])).astype(BF16)
    a_ref[0] = _dot(h, w_ref[:, _C_A:_C_END])


def _inproj_ctx_body(x_ref, gn_ref, sh_ref, sc_ref, w_ref, k_ref, v_ref, a_ref):
    h = _norm_mod(x_ref[0], gn_ref[...], sc_ref[...], sh_ref[...])
    k_ref[0] = _dot(h, w_ref[:, _C_K:_C_V]).astype(BF16)
    v_ref[0] = _dot(h, w_ref[:, _C_V:_C_G]).astype(BF16)
    a_ref[0] = _dot(h, w_ref[:, _C_A:_C_END])


def _inproj_lat(x, gn, sh, sc, w, tm):
    b, s, d = x.shape
    row = lambda n: pl.BlockSpec((1, tm, n), lambda bi, i: (bi, i, 0))
    vec = pl.BlockSpec((1, 1, d), lambda bi, i: (bi, 0, 0))
    shp = lambda n, dt: jax.ShapeDtypeStruct((b, s, n), dt)
    return pl.pallas_call(
        _inproj_lat_body,
        grid=(b, s // tm),
        in_specs=[row(d), pl.BlockSpec((1, d), lambda bi, i: (0, 0)), vec, vec, _resident(w.shape)],
        out_specs=[row(CONV_W), row(GLA_KD), row(GLA_KD), row(GLA_VD), row(GLA_VD),
                   row(d), row(d), row(LANES)],
        out_shape=[shp(CONV_W, BF16), shp(GLA_KD, BF16), shp(GLA_KD, BF16), shp(GLA_VD, BF16),
                   shp(GLA_VD, BF16), shp(d, BF16), shp(d, BF16), shp(LANES, F32)],
        compiler_params=_cparams(("arbitrary", "arbitrary")),
        name="inproj_lat",
    )(x, gn, sh, sc, w)


def _inproj_ctx(ctx, gn, sh, sc, w, tm):
    b, l, d = ctx.shape
    row = lambda n: pl.BlockSpec((1, tm, n), lambda bi, i: (bi, i, 0))
    vec = pl.BlockSpec((1, d), lambda bi, i: (0, 0))
    shp = lambda n, dt: jax.ShapeDtypeStruct((b, l, n), dt)
    return pl.pallas_call(
        _inproj_ctx_body,
        grid=(b, l // tm),
        in_specs=[row(d), vec, vec, vec, _resident(w.shape)],
        out_specs=[row(GLA_KD), row(GLA_VD), row(LANES)],
        out_shape=[shp(GLA_KD, BF16), shp(GLA_VD, BF16), shp(LANES, F32)],
        compiler_params=_cparams(("arbitrary", "arbitrary")),
        name="inproj_ctx",
    )(ctx, gn, sh, sc, w)


_CONV_SUB = 8
_CONV_SPAN = GRID_W + _CONV_SUB * ((CONV_K - 1) // _CONV_SUB)


def _conv_shift_matrix():
    row = lax.broadcasted_iota(I32, (_CONV_SUB, _CONV_SPAN, GRID_W), 1)
    shift = lax.broadcasted_iota(I32, (_CONV_SUB, _CONV_SPAN, GRID_W), 0)
    col = lax.broadcasted_iota(I32, (_CONV_SUB, _CONV_SPAN, GRID_W), 2)
    m = jnp.where(col == row + shift - CONV_K // 2, 1.0, 0.0)
    return m.reshape(_CONV_SUB * _CONV_SPAN, GRID_W).astype(BF16)


def _conv_body(u_ref, sm_ref, w_ref, bdw_ref, gln_ref, bln_ref, o_ref, sh_ref, y_ref, *,
               rows_per_step):
    for r in range(rows_per_step):
        buf = r % 2
        sh_ref[buf] = _dot(sm_ref[...], u_ref[r * GRID_W:(r + 1) * GRID_W, :])
        for cb in range(CONV_W // LANES):
            ls = slice(cb * LANES, (cb + 1) * LANES)
            acc = None
            for k in range(CONV_K):
                s, a = k % _CONV_SUB, k // _CONV_SUB
                r0 = s * _CONV_SPAN + _CONV_SUB * a
                term = sh_ref[buf, r0:r0 + GRID_W, ls] * w_ref[k:k + 1, ls]
                acc = term if acc is None else acc + term
            y_ref[r * GRID_W:(r + 1) * GRID_W, ls] = acc + bdw_ref[:, ls]
    y = y_ref[...]
    mu = jnp.mean(y, axis=-1, keepdims=True)
    yc = y - mu
    var = jnp.mean(yc * yc, axis=-1, keepdims=True)
    yn = yc * lax.rsqrt(var + EPS) * gln_ref[...] + bln_ref[...]
    o_ref[...] = (yn * _sigmoid(yn)).astype(BF16)


def _conv(u, w_dw, b_dw, g_ln, b_ln, rows_per_step):
    n, cw = u.shape
    tm = rows_per_step * GRID_W
    vec = pl.BlockSpec((1, cw), lambda i: (0, 0))
    wpad = jnp.zeros((32, cw), F32).at[:CONV_K].set(w_dw)
    sm = _conv_shift_matrix()
    return pl.pallas_call(
        functools.partial(_conv_body, rows_per_step=rows_per_step),
        grid=(n // tm,),
        in_specs=[pl.BlockSpec((tm, cw), lambda i: (i, 0)),
                  pl.BlockSpec(sm.shape, lambda i: (0, 0)),
                  pl.BlockSpec((32, cw), lambda i: (0, 0)), vec, vec, vec],
        out_specs=pl.BlockSpec((tm, cw), lambda i: (i, 0)),
        out_shape=jax.ShapeDtypeStruct((n, cw), BF16),
        scratch_shapes=[pltpu.VMEM((2, _CONV_SUB * _CONV_SPAN, cw), F32),
                        pltpu.VMEM((tm, cw), F32)],
        compiler_params=_cparams(("arbitrary",)),
        name="conv_module",
    )(u, sm, wpad, b_dw.reshape(1, cw), g_ln.reshape(1, cw), b_ln.reshape(1, cw))


def _log_sigmoid(z):
    return jnp.minimum(z, 0.0) - jnp.log(1.0 + jnp.exp(-jnp.abs(z)))


def _gla_keep_mask(t, reverse):
    row = lax.broadcasted_iota(I32, (t, t), 0)
    col = lax.broadcasted_iota(I32, (t, t), 1)
    same_chunk = lax.shift_right_logical(row, 6) == lax.shift_right_logical(col, 6)
    return jnp.logical_and(same_chunk, (col >= row) if reverse else (col <= row))


def _gla_decay(a, wh_ref, wl_ref, ba, keep):
    a_hi, a_lo = _split_bf16(a)
    z = _dot(jnp.concatenate([a_hi, a_lo], axis=1), wh_ref[...]) + _dot(a_hi, wl_ref[...]) + ba
    loga = _log_sigmoid(z) * (1.0 / GLA_TAU)
    l_hi, l_lo = _split_bf16(loga)
    tri = jnp.where(keep, 1.0, 0.0).astype(BF16)
    return _dot(tri, l_hi) + _dot(tri, l_lo)


def _gla_step(k, v, bcum, keep, states, *, reverse, q=None):
    t = k.shape[0]
    c = GLA_CHUNK
    nch = t // c
    tots = [bcum[n * c:n * c + 1, :] if reverse else bcum[(n + 1) * c - 1:(n + 1) * c, :]
            for n in range(nch)]
    totb = jnp.concatenate([jnp.broadcast_to(tt, (c, GLA_KD)) for tt in tots], axis=0)
    k32 = k.astype(F32)
    k_end = (k32 * jnp.exp(totb - bcum)).astype(BF16)
    decs = [jnp.exp(tt) for tt in tots]
    if q is not None:
        q_dec = (q.astype(F32) * jnp.exp(bcum) * (GLA_DK ** -0.5)).astype(BF16)
        k_inv = (k32 * jnp.exp(-bcum)).astype(BF16)
    order = range(nch - 1, -1, -1) if reverse else range(nch)
    outs, new_states = [], []
    for h in range(GLA_H):
        ks = slice(h * GLA_DK, (h + 1) * GLA_DK)
        vh = v[:, h * GLA_DV:(h + 1) * GLA_DV]
        st = states[h]
        if q is not None:
            scores = lax.dot_general(q_dec[:, ks], k_inv[:, ks], (((1,), (1,)), ((), ())),
                                     preferred_element_type=F32)
            o_h = _dot(jnp.where(keep, scores, 0.0).astype(BF16), vh)
            inter = [None] * nch
        for n in order:
            rs = slice(n * c, (n + 1) * c)
            if q is not None:
                inter[n] = _dot(q_dec[rs, ks], st.astype(BF16))
            kv = lax.dot_general(k_end[rs, ks], vh[rs, :], (((0,), (0,)), ((), ())),
                                 preferred_element_type=F32)
            dt = jnp.transpose(jnp.broadcast_to(decs[n][:, ks], (GLA_DK, GLA_DK)))
            st = st * jnp.concatenate([dt] * (GLA_DV // GLA_DK), axis=1) + kv
        new_states.append(st)
        if q is not None:
            outs.append(o_h + jnp.concatenate(inter, axis=0))
    return new_states, (jnp.concatenate(outs, axis=1) if q is not None else None)


def _gla_ctx_body(k_ref, v_ref, a_ref, wh_ref, wl_ref, ba_ref, st_ref, *, reverse):
    j = pl.program_id(1)

    @pl.when(j == 0)
    def _():
        st_ref[...] = jnp.zeros_like(st_ref)

    keep = _gla_keep_mask(k_ref.shape[1], reverse)
    bcum = _gla_decay(a_ref[0], wh_ref, wl_ref, ba_ref[...], keep)
    states, _ = _gla_step(k_ref[0], v_ref[0], bcum, keep,
                          [st_ref[0, h] for h in range(GLA_H)], reverse=reverse)
    for h in range(GLA_H):
        st_ref[0, h] = states[h]


def _gla_lat_body(*refs, reverse, final):
    if final:
        (k_ref, v_ref, a_ref, an_ref, q_ref, wh_ref, wl_ref, ba_ref, s0_ref, sg_ref, op_ref,
         gn_ref, o_ref, st_ref, bc_ref) = refs
    else:
        (k_ref, v_ref, a_ref, an_ref, q_ref, wh_ref, wl_ref, ba_ref, s0_ref,
         o_ref, st_ref, bc_ref) = refs
    j = pl.program_id(1)
    nb, t = k_ref.shape[0], k_ref.shape[1]
    keep = _gla_keep_mask(t, reverse)

    @pl.when(j == 0)
    def _():
        st_ref[...] = s0_ref[...]
        for bb in range(nb):
            bc_ref[bb] = _gla_decay(a_ref[bb], wh_ref, wl_ref, ba_ref[...], keep)

    for bb in range(nb):
        states, o = _gla_step(k_ref[bb], v_ref[bb], bc_ref[bb], keep,
                              [st_ref[bb, h] for h in range(GLA_H)], reverse=reverse,
                              q=q_ref[bb])
        bc_ref[bb] = _gla_decay(an_ref[bb], wh_ref, wl_ref, ba_ref[...], keep)
        for h in range(GLA_H):
            st_ref[bb, h] = states[h]
        if not final:
            o_ref[bb] = o.astype(BF16)
            continue
        o = o + op_ref[bb].astype(F32)
        parts = []
        for h in range(GLA_H):
            oh = o[:, h * GLA_DV:(h + 1) * GLA_DV]
            ms = jnp.mean(oh * oh, axis=-1, keepdims=True)
            parts.append(oh * lax.rsqrt(ms + EPS) * gn_ref[...])
        o_ref[bb] = (jnp.concatenate(parts, axis=1) * sg_ref[bb].astype(F32)).astype(BF16)


def _gla_decay_weights(w_alpha_d, reverse):
    a0 = GLA_RANK if reverse else 0
    w = jnp.zeros((LANES, GLA_KD), F32).at[a0:a0 + GLA_RANK].set(w_alpha_d)
    hi = w.astype(BF16)
    lo = (w - hi.astype(F32)).astype(BF16)
    return jnp.concatenate([hi, hi], axis=0), lo


def _gla_ctx(k, v, a, w_alpha_d, ba, reverse, cps):
    b, l, _ = k.shape
    tm = cps * GLA_CHUNK
    nj = l // tm
    jmap = (lambda j: nj - 1 - j) if reverse else (lambda j: j)
    row = lambda n: pl.BlockSpec((1, tm, n), lambda bi, j: (bi, jmap(j), 0))
    wh, wl = _gla_decay_weights(w_alpha_d, reverse)
    return pl.pallas_call(
        functools.partial(_gla_ctx_body, reverse=reverse),
        grid=(b, nj),
        in_specs=[row(GLA_KD), row(GLA_VD), row(LANES),
                  pl.BlockSpec(wh.shape, lambda bi, j: (0, 0)),
                  pl.BlockSpec(wl.shape, lambda bi, j: (0, 0)),
                  pl.BlockSpec((1, GLA_KD), lambda bi, j: (0, 0))],
        out_specs=pl.BlockSpec((1, GLA_H, GLA_DK, GLA_DV), lambda bi, j: (bi, 0, 0, 0)),
        out_shape=jax.ShapeDtypeStruct((b, GLA_H, GLA_DK, GLA_DV), F32),
        compiler_params=_cparams(("arbitrary", "arbitrary")),
        name="gla_ctx_bwd" if reverse else "gla_ctx_fwd",
    )(k, v, a, wh, wl, ba.reshape(1, GLA_KD))


def _gla_lat(k, v, a, q, w_alpha_d, ba, s0, reverse, cps, sg=None, o_prev=None, g_norm=None):
    b, s, _ = k.shape
    final = sg is not None
    tm = cps * GLA_CHUNK
    nj = s // tm
    jmap = (lambda j: nj - 1 - j) if reverse else (lambda j: j)
    nb = 2 if b % 2 == 0 else 1
    row = lambda n: pl.BlockSpec((nb, tm, n), lambda bi, j: (bi, jmap(j), 0))
    wh, wl = _gla_decay_weights(w_alpha_d, reverse)
    a_next = pl.BlockSpec((nb, tm, LANES), lambda bi, j: (bi, jmap(jnp.minimum(j + 1, nj - 1)), 0))
    in_specs = [row(GLA_KD), row(GLA_VD), row(LANES), a_next, row(GLA_KD),
                pl.BlockSpec(wh.shape, lambda bi, j: (0, 0)),
                pl.BlockSpec(wl.shape, lambda bi, j: (0, 0)),
                pl.BlockSpec((1, GLA_KD), lambda bi, j: (0, 0)),
                pl.BlockSpec((nb, GLA_H, GLA_DK, GLA_DV), lambda bi, j: (bi, 0, 0, 0))]
    args = [k, v, a, a, q, wh, wl, ba.reshape(1, GLA_KD), s0]
    if final:
        in_specs += [row(GLA_VD), row(GLA_VD), pl.BlockSpec((1, GLA_DV), lambda bi, j: (0, 0))]
        args += [sg, o_prev, g_norm.reshape(1, GLA_DV)]
    return pl.pallas_call(
        functools.partial(_gla_lat_body, reverse=reverse, final=final),
        grid=(b // nb, nj),
        in_specs=in_specs,
        out_specs=row(GLA_VD),
        out_shape=jax.ShapeDtypeStruct((b, s, GLA_VD), BF16),
        scratch_shapes=[pltpu.VMEM((nb, GLA_H, GLA_DK, GLA_DV), F32),
                        pltpu.VMEM((nb, tm, GLA_KD), F32)],
        compiler_params=_cparams(("arbitrary", "arbitrary")),
        name="gla_lat_bwd" if reverse else "gla_lat_fwd",
    )(*args)


def _pack_pair(lo, hi):
    lo_b = pltpu.bitcast(lo.astype(BF16).astype(F32), I32)
    hi_b = pltpu.bitcast(hi.astype(BF16).astype(F32), I32)
    return lax.shift_right_logical(lo_b, 16) | (hi_b & jnp.int32(-65536))


def _unpack_pair(w):
    lo = pltpu.bitcast(lax.shift_left(w, 16), F32)
    hi = pltpu.bitcast(w & jnp.int32(-65536), F32)
    return lo, hi


def _merge_body(act_ref, og_ref, gc_ref, gg_ref, x_ref, gt_ref, sh_ref, sc_ref, gn_ref,
                wc_ref, wg_ref, wo_ref, wr_ref, br_ref,
                x1_ref, h2_ref, ti_ref, tp_ref):
    yc = _dot(act_ref[...], wc_ref[...])
    yg = _dot(og_ref[...], wg_ref[...])
    m = gc_ref[...].astype(F32) * yc + gg_ref[...].astype(F32) * yg
    x1 = x_ref[...] + gt_ref[0] * _dot(m.astype(BF16), wo_ref[...])
    x1_ref[...] = x1
    ms = jnp.mean(x1 * x1, axis=-1, keepdims=True)
    h2 = (x1 * lax.rsqrt(ms + EPS) * gn_ref[...]) * (1.0 + sc_ref[0]) + sh_ref[0]
    half = D_MODEL // 2
    h2_ref[...] = _pack_pair(h2[:, :half], h2[:, half:])
    logits = _dot3(h2, wr_ref[...]) + br_ref[...]
    lane = lax.broadcasted_iota(I32, logits.shape, 1).astype(F32)
    neg = jnp.float32(-jnp.inf)
    work = jnp.where(lane < N_EXPERTS, logits, neg)
    vals, idxs = [], []
    for _ in range(TOP_K):
        mx = jnp.max(work, axis=-1, keepdims=True)
        ix = jnp.min(jnp.where(work == mx, lane, float(LANES)), axis=-1, keepdims=True)
        vals.append(mx)
        idxs.append(ix)
        work = jnp.where(lane == ix, neg, work)
    es = [jnp.exp(v - vals[0]) for v in vals]
    den = es[0] + es[1] + es[2] + es[3]
    ti = jnp.zeros(logits.shape, F32)
    tp = jnp.zeros(logits.shape, F32)
    for kk in range(TOP_K):
        ti = jnp.where(lane == kk, idxs[kk], ti)
        tp = jnp.where(lane == kk, es[kk] / den, tp)
    ti_ref[...] = ti.astype(I32)
    tp_ref[...] = tp


def _merge(act, og, gc, gg, x2d, gt1, sh2, sc2, gn, wc, wg, wo, wr, br, tm, rows_per_batch):
    n, d = x2d.shape
    per_b = rows_per_batch // tm
    row = lambda w: pl.BlockSpec((tm, w), lambda i: (i, 0))
    vec = pl.BlockSpec((1, 1, d), lambda i: (i // per_b, 0, 0))
    const = lambda shape: pl.BlockSpec(shape, lambda i: (0,) * len(shape))
    return pl.pallas_call(
        _merge_body,
        grid=(n // tm,),
        in_specs=[row(d), row(d), row(d), row(d), row(d), vec, vec, vec, const((1, d)),
                  _resident(wc.shape), _resident(wg.shape), _resident(wo.shape),
                  const(wr.shape), const((1, LANES))],
        out_specs=[row(d), row(d // 2), row(LANES), row(LANES)],
        out_shape=[jax.ShapeDtypeStruct((n, d), F32), jax.ShapeDtypeStruct((n, d // 2), I32),
                   jax.ShapeDtypeStruct((n, LANES), I32), jax.ShapeDtypeStruct((n, LANES), F32)],
        compiler_params=_cparams(("arbitrary",)),
        name="merge_router",
    )(act, og, gc, gg, x2d, gt1, sh2, sc2, gn, wc, wg, wo, wr, br)


def _dispatch_body(dest_ref, h2_ref, xp_in_ref, xp_ref, sem, *, tokens):
    del xp_in_ref

    def row_copy(t, kk):
        dst = dest_ref[0, 0, t * TOP_K + kk]
        return pltpu.make_async_copy(h2_ref.at[pl.ds(t, 1)], xp_ref.at[pl.ds(dst, 1)], sem)

    def issue(t, carry):
        for kk in range(TOP_K):
            row_copy(t, kk).start(priority=kk % 2)
        return carry

    lax.fori_loop(0, tokens, issue, 0, unroll=8)

    def drain(t, carry):
        for _ in range(TOP_K):
            pltpu.make_async_copy(h2_ref.at[pl.ds(0, 1)], xp_ref.at[pl.ds(0, 1)], sem).wait()
        return carry

    lax.fori_loop(0, tokens, drain, 0, unroll=8)


def _dispatch(dest, h2p, xp_init, tokens):
    n, w = h2p.shape
    steps = n // tokens
    dest3 = dest.reshape(steps, 1, tokens * TOP_K)
    return pl.pallas_call(
        functools.partial(_dispatch_body, tokens=tokens),
        grid=(steps,),
        in_specs=[pl.BlockSpec((1, 1, tokens * TOP_K), lambda i: (i, 0, 0),
                               memory_space=pltpu.SMEM),
                  pl.BlockSpec((tokens, w), lambda i: (i, 0)),
                  pl.BlockSpec(memory_space=pl.ANY)],
        out_specs=pl.BlockSpec(memory_space=pl.ANY),
        out_shape=jax.ShapeDtypeStruct(xp_init.shape, xp_init.dtype),
        scratch_shapes=[pltpu.SemaphoreType.DMA],
        input_output_aliases={2: 0},
        compiler_params=_cparams(("arbitrary",)),
        name="moe_dispatch",
    )(dest3, h2p, xp_init)


_GLU_GROUP = 2 * LANES


def _deinterleave_matrix():
    src = lax.broadcasted_iota(I32, (_GLU_GROUP, _GLU_GROUP), 0)
    dst = lax.broadcasted_iota(I32, (_GLU_GROUP, _GLU_GROUP), 1)
    want = jnp.where(dst < LANES, 2 * dst, 2 * (dst - LANES) + 1)
    return jnp.where(src == want, 1.0, 0.0).astype(BF16)


def _expert_body(be_ref, nreal_ref, xp_ref, w1_ref, b1_ref, w2_ref, b2_ref, yp_ref,
                 w1s_ref, w2s_ref):
    i = pl.program_id(0)
    new_expert = jnp.logical_or(i == 0, be_ref[i] != be_ref[jnp.maximum(i - 1, 0)])

    @pl.when(jnp.logical_and(new_expert, i < nreal_ref[0]))
    def _():
        perm = _deinterleave_matrix()
        for g in range(2 * D_FF // _GLU_GROUP):
            cs = slice(g * _GLU_GROUP, (g + 1) * _GLU_GROUP)
            w1s_ref[:, cs] = _dot(w1_ref[0, :, cs].astype(BF16), perm).astype(BF16)
        w2s_ref[...] = w2_ref[0].astype(BF16)

    @pl.when(i < nreal_ref[0])
    def _():
        lo, hi = _unpack_pair(xp_ref[...])
        x = jnp.concatenate([lo, hi], axis=1).astype(BF16)
        hid = _dot(x, w1s_ref[...]) + b1_ref[0]
        ngrp = 2 * D_FF // _GLU_GROUP
        hg = jnp.concatenate(
            [hid[:, g * _GLU_GROUP:g * _GLU_GROUP + LANES] for g in range(ngrp)], axis=1)
        hl = jnp.concatenate(
            [hid[:, g * _GLU_GROUP + LANES:(g + 1) * _GLU_GROUP] for g in range(ngrp)], axis=1)
        xg = jnp.minimum(hg, SWIGLU_LIMIT)
        xl = jnp.clip(hl, -SWIGLU_LIMIT, SWIGLU_LIMIT)
        act = xg * _sigmoid(SWIGLU_ALPHA * xg) * (xl + 1.0)
        y = _dot(act.astype(BF16), w2s_ref[...]) + b2_ref[0]
        half = D_MODEL // 2
        yp_ref[...] = _pack_pair(y[:, :half], y[:, half:])

    @pl.when(i >= nreal_ref[0])
    def _():
        yp_ref[...] = jnp.zeros_like(yp_ref)


def _experts(blk_expert, n_real, xp, w1, b1, w2, b2, bm):
    p, w = xp.shape
    grid_spec = pltpu.PrefetchScalarGridSpec(
        num_scalar_prefetch=2,
        grid=(p // bm,),
        in_specs=[pl.BlockSpec((bm, w), lambda i, be, nr: (i, 0)),
                  pl.BlockSpec((1, D_MODEL, 2 * D_FF), lambda i, be, nr: (be[i], 0, 0)),
                  pl.BlockSpec((1, 1, 2 * D_FF), lambda i, be, nr: (be[i], 0, 0)),
                  pl.BlockSpec((1, D_FF, D_MODEL), lambda i, be, nr: (be[i], 0, 0)),
                  pl.BlockSpec((1, 1, D_MODEL), lambda i, be, nr: (be[i], 0, 0))],
        out_specs=pl.BlockSpec((bm, w), lambda i, be, nr: (i, 0)),
        scratch_shapes=[pltpu.VMEM((D_MODEL, 2 * D_FF), BF16), pltpu.VMEM((D_FF, D_MODEL), BF16)],
    )
    return pl.pallas_call(
        _expert_body,
        grid_spec=grid_spec,
        out_shape=jax.ShapeDtypeStruct((p, w), I32),
        compiler_params=_cparams(("arbitrary",)),
        name="moe_experts",
    )(blk_expert, n_real, xp, w1, b1, w2, b2)


def _combine_body(dcur_ref, dnext_ref, yp_ref, tp_ref, x1_ref, gt_ref, gf_ref, o_ref,
                  buf_ref, sem, *, tokens):
    i = pl.program_id(0)
    n = pl.num_programs(0)
    slot = i % 2

    def row_copy(dref, s, t, kk):
        src = dref[0, 0, t * TOP_K + kk]
        return pltpu.make_async_copy(yp_ref.at[pl.ds(src, 1)], buf_ref.at[s, kk, pl.ds(t, 1)],
                                     sem.at[s])

    def issue_all(dref, s):
        def issue(t, carry):
            for kk in range(TOP_K):
                row_copy(dref, s, t, kk).start(priority=kk % 2)
            return carry
        lax.fori_loop(0, tokens, issue, 0, unroll=8)

    @pl.when(i == 0)
    def _():
        issue_all(dcur_ref, 0)

    @pl.when(i + 1 < n)
    def _():
        issue_all(dnext_ref, 1 - slot)

    def drain(t, carry):
        for kk in range(TOP_K):
            pltpu.make_async_copy(yp_ref.at[pl.ds(0, 1)], buf_ref.at[slot, kk, pl.ds(0, 1)],
                                  sem.at[slot]).wait()
        return carry

    lax.fori_loop(0, tokens, drain, 0, unroll=8)

    half = D_MODEL // 2
    tp = tp_ref[...]
    y_lo = jnp.zeros((tokens, half), F32)
    y_hi = jnp.zeros((tokens, half), F32)
    for kk in range(TOP_K):
        lo, hi = _unpack_pair(buf_ref[slot, kk])
        pk = tp[:, kk:kk + 1]
        y_lo = y_lo + pk * lo
        y_hi = y_hi + pk * hi
    y = jnp.concatenate([y_lo, y_hi], axis=1)
    x2 = x1_ref[...] + gt_ref[0] * y
    ms = jnp.mean(x2 * x2, axis=-1, keepdims=True)
    o_ref[...] = x2 * lax.rsqrt(ms + EPS) * gf_ref[...]


def _combine(dest, yp, tp, x1, gt2, gf, tokens, rows_per_batch):
    n, d = x1.shape
    steps = n // tokens
    per_b = rows_per_batch // tokens
    dest3 = dest.reshape(steps, 1, tokens * TOP_K)
    dspec = lambda f: pl.BlockSpec((1, 1, tokens * TOP_K), f, memory_space=pltpu.SMEM)
    return pl.pallas_call(
        functools.partial(_combine_body, tokens=tokens),
        grid=(steps,),
        in_specs=[dspec(lambda i: (i, 0, 0)),
                  dspec(lambda i: (jnp.minimum(i + 1, steps - 1), 0, 0)),
                  pl.BlockSpec(memory_space=pl.ANY),
                  pl.BlockSpec((tokens, LANES), lambda i: (i, 0)),
                  pl.BlockSpec((tokens, d), lambda i: (i, 0)),
                  pl.BlockSpec((1, 1, d), lambda i: (i // per_b, 0, 0)),
                  pl.BlockSpec((1, d), lambda i: (0, 0))],
        out_specs=pl.BlockSpec((tokens, d), lambda i: (i, 0)),
        out_shape=jax.ShapeDtypeStruct((n, d), F32),
        scratch_shapes=[pltpu.VMEM((2, TOP_K, tokens, d // 2), I32),
                        pltpu.SemaphoreType.DMA((2,))],
        compiler_params=_cparams(("arbitrary",)),
        name="moe_combine",
    )(dest3, dest3, yp, tp, x1, gt2, gf)


def _routing_tables(top_idx, bm):
    n = top_idx.shape[0]
    nk = n * TOP_K
    onehot = (top_idx[:, :, None] == jnp.arange(N_EXPERTS, dtype=I32)[None, None, :])
    per_tok = jnp.sum(onehot.astype(I32), axis=1)
    incl = jnp.cumsum(per_tok, axis=0)
    excl = incl - per_tok
    counts = incl[-1]
    padded = (counts + bm - 1) // bm * bm
    pad_end = jnp.cumsum(padded)
    pad_start = pad_end - padded
    rank = jnp.take_along_axis(excl, top_idx, axis=1)
    dest = pad_start[top_idx] + rank
    n_blocks = (nk + N_EXPERTS * (bm - 1) + bm - 1) // bm
    blk_expert = jnp.minimum(
        jnp.searchsorted(pad_end, jnp.arange(n_blocks, dtype=I32) * bm, side="right"),
        N_EXPERTS - 1).astype(I32)
    n_real = (pad_end[-1] // bm).astype(I32).reshape(1)
    return dest.astype(I32), blk_expert, n_real, n_blocks


def _layer(x, c, ctx, c_ctx, w_ada, b_ada, g_mix_norm, w_in, w_dw, b_dw, g_conv_ln, b_conv_ln,
           w_conv_out, w_alpha, b_alpha, g_gla_norm, w_gla_out, w_out, g_ffn_norm, w_router,
           b_router, w_exp_in, b_exp_in, w_exp_out, b_exp_out, g_final, *, cfg):
    b, s, d = x.shape
    n = b * s

    rows = (b + 1 + 7) // 8 * 8
    cc = jnp.zeros((rows, d), F32).at[:b].set(c).at[b].set(c_ctx)
    mod = _ada(cc, w_ada, b_ada)
    sh1, sc1, gt1, sh2, sc2, gt2 = [mod[:b, i * d:(i + 1) * d].reshape(b, 1, d) for i in range(6)]
    csh1 = mod[b:b + 1, 0:d]
    csc1 = mod[b:b + 1, d:2 * d]

    a0 = 2 * CONV_W + 2 * GLA_KD + 2 * GLA_VD
    w_in_r = jnp.concatenate(
        [w_in[:, :a0], w_in[:, a0 + 2 * GLA_RANK:], w_in[:, a0:a0 + 2 * GLA_RANK],
         jnp.zeros((d, LANES - 2 * GLA_RANK), F32)], axis=1).astype(BF16)
    gmn = g_mix_norm.reshape(1, d)

    u, q, k, v, sg, gc, gg, a = _inproj_lat(x, gmn, sh1, sc1, w_in_r, cfg["tm_in"])
    kc, vc, ac = _inproj_ctx(ctx, gmn, csh1, csc1, w_in_r, cfg["tm_ctx"])

    act = _conv(u.reshape(n, CONV_W), w_dw, b_dw, g_conv_ln, b_conv_ln, cfg["conv_rows"])

    cps = cfg["gla_cps"]
    st_b = _gla_ctx(kc, vc, ac, w_alpha[1], b_alpha[1], True, cps)
    o_b = _gla_lat(k, v, a, q, w_alpha[1], b_alpha[1], st_b, True, cps)
    st_f = _gla_ctx(kc, vc, ac, w_alpha[0], b_alpha[0], False, cps)
    og = _gla_lat(k, v, a, q, w_alpha[0], b_alpha[0], st_f, False, cps,
                  sg=sg, o_prev=o_b, g_norm=g_gla_norm)

    wr = jnp.zeros((d, LANES), F32).at[:, :N_EXPERTS].set(w_router)
    br = jnp.zeros((1, LANES), F32).at[0, :N_EXPERTS].set(b_router)
    x1, h2p, ti, tp = _merge(
        act, og.reshape(n, d), gc.reshape(n, d), gg.reshape(n, d), x.reshape(n, d),
        gt1, sh2, sc2, g_ffn_norm.reshape(1, d),
        w_conv_out.astype(BF16), w_gla_out.astype(BF16), w_out.astype(BF16), wr, br,
        cfg["tm_merge"], s)

    bm = cfg["moe_block"]
    dest, blk_expert, n_real, n_blocks = _routing_tables(ti[:, :TOP_K], bm)
    xp = _dispatch(dest, h2p, jnp.zeros((n_blocks * bm, d // 2), I32), cfg["moe_tokens"])

    b1 = b_exp_in.reshape(N_EXPERTS, 2 * D_FF // _GLU_GROUP, LANES, 2).transpose(0, 1, 3, 2)
    b1 = b1.reshape(N_EXPERTS, 1, 2 * D_FF)
    yp = _experts(blk_expert, n_real, xp, w_exp_in, b1, w_exp_out,
                  b_exp_out.reshape(N_EXPERTS, 1, d), bm)

    out = _combine(dest, yp, tp, x1, gt2, g_final.reshape(1, d), cfg["moe_tokens"], s)
    return out.reshape(b, s, d)


def _config(s, l):
    return dict(tm_in=min(512, s), tm_ctx=min(256, l), conv_rows=4, gla_cps=4,
                tm_merge=min(512, s), moe_block=512, moe_tokens=128)


def kernel(x, c, ctx, c_ctx, w_ada, b_ada, g_mix_norm, w_in, w_dw, b_dw, g_conv_ln, b_conv_ln,
           w_conv_out, w_alpha, b_alpha, g_gla_norm, w_gla_out, w_out, g_ffn_norm, w_router,
           b_router, w_exp_in, b_exp_in, w_exp_out, b_exp_out, g_final):
    depth = w_ada.shape[0]
    assert depth == 1, "single-layer block: the context stream is only consumed by the GLA scan"
    cfg = _config(x.shape[1], ctx.shape[1])
    return _layer(x, c, ctx, c_ctx, w_ada[0], b_ada[0], g_mix_norm[0], w_in[0], w_dw[0], b_dw[0],
                  g_conv_ln[0], b_conv_ln[0], w_conv_out[0], w_alpha[0], b_alpha[0],
                  g_gla_norm[0], w_gla_out[0], w_out[0], g_ffn_norm[0], w_router[0], b_router[0],
                  w_exp_in[0], b_exp_in[0], w_exp_out[0], b_exp_out[0], g_final, cfg=cfg)
```

```python
import functools

import jax
import jax.numpy as jnp
from jax import lax
from jax.experimental import pallas as pl
from jax.experimental.pallas import tpu as pltpu
from jax.experimental.pallas import tpu_sc as plsc

F32 = jnp.float32
BF16 = jnp.bfloat16
I32 = jnp.int32

D_MODEL = 1024
GRID_W = 64
EPS = 1e-6
CONV_W = 1024
CONV_K = 31
GLA_H = 4
GLA_DK = 128
GLA_DV = 256
GLA_KD = GLA_H * GLA_DK
GLA_VD = GLA_H * GLA_DV
GLA_RANK = 16
GLA_TAU = 16.0
GLA_CHUNK = 64
N_EXPERTS = 32
TOP_K = 4
D_FF = 1024
SWIGLU_ALPHA = 1.702
SWIGLU_LIMIT = 7.0

LANES = 128
VMEM_LIMIT = 56 * 1024 * 1024

_C_CONV_A = 0
_C_CONV_B = _C_CONV_A + CONV_W
_C_Q = _C_CONV_B + CONV_W
_C_K = _C_Q + GLA_KD
_C_V = _C_K + GLA_KD
_C_G = _C_V + GLA_VD
_C_GC = _C_G + GLA_VD
_C_GG = _C_GC + D_MODEL
_C_A = _C_GG + D_MODEL
_C_END = _C_A + LANES


def _cparams(sem):
    return pltpu.CompilerParams(dimension_semantics=sem, vmem_limit_bytes=VMEM_LIMIT)


def _dot(a, b):
    return jnp.dot(a, b, preferred_element_type=F32)


def _split_bf16(x):
    hi = x.astype(BF16)
    lo = (x - hi.astype(F32)).astype(BF16)
    return hi, lo


def _dot3(a, b):
    a_hi, a_lo = _split_bf16(a)
    b_hi, b_lo = _split_bf16(b)
    return _dot(a_hi, b_hi) + _dot(a_lo, b_hi) + _dot(a_hi, b_lo)


def _sigmoid(x):
    return 1.0 / (1.0 + jnp.exp(-x))


def _resident(shape):
    nd = len(shape)
    return pl.BlockSpec(shape, lambda *_: (0,) * nd, pipeline_mode=pl.Buffered(1))


def _ada_body(a_ref, w_ref, b_ref, o_ref):
    a = a_ref[...]
    a = a * _sigmoid(a)
    o_ref[...] = _dot3(a, w_ref[...]) + b_ref[...]


def _ada(cc, w, b):
    rows, d = cc.shape
    n = w.shape[1]
    tn = 512
    return pl.pallas_call(
        _ada_body,
        grid=(n // tn,),
        in_specs=[pl.BlockSpec((rows, d), lambda j: (0, 0)),
                  pl.BlockSpec((d, tn), lambda j: (0, j)),
                  pl.BlockSpec((1, tn), lambda j: (0, j))],
        out_specs=pl.BlockSpec((rows, tn), lambda j: (0, j)),
        out_shape=jax.ShapeDtypeStruct((rows, n), F32),
        compiler_params=_cparams(("arbitrary",)),
        name="ada_mod",
    )(cc, w, b.reshape(1, n))


def _norm_mod(xv, gn, sc, sh):
    ms = jnp.mean(xv * xv, axis=-1, keepdims=True)
    y = xv * lax.rsqrt(ms + EPS) * gn
    return (y * (1.0 + sc) + sh).astype(BF16)


def _inproj_lat_body(x_ref, gn_ref, sh_ref, sc_ref, w_ref,
                     u_ref, q_ref, k_ref, v_ref, sg_ref, gc_ref, gg_ref, a_ref):
    h = _norm_mod(x_ref[0], gn_ref[...], sc_ref[0], sh_ref[0])
    ca = _dot(h, w_ref[:, _C_CONV_A:_C_CONV_B])
    cb = _dot(h, w_ref[:, _C_CONV_B:_C_Q])
    u_ref[0] = (ca * _sigmoid(cb)).astype(BF16)
    q_ref[0] = _dot(h, w_ref[:, _C_Q:_C_K]).astype(BF16)
    k_ref[0] = _dot(h, w_ref[:, _C_K:_C_V]).astype(BF16)
    v_ref[0] = _dot(h, w_ref[:, _C_V:_C_G]).astype(BF16)
    g = _dot(h, w_ref[:, _C_G:_C_GC])
    sg_ref[0] = (g * _sigmoid(g)).astype(BF16)
    gc_ref[0] = _sigmoid(_dot(h, w_ref[:, _C_GC:_C_GG])).astype(BF16)
    gg_ref[0] = _sigmoid(_dot(h, w_ref[:, _C_GG:_C_A])).astype(BF16)
    a_ref[0] = _dot(h, w_ref[:, _C_A:_C_END])


def _inproj_ctx_body(x_ref, gn_ref, sh_ref, sc_ref, w_ref, k_ref, v_ref, a_ref):
    h = _norm_mod(x_ref[0], gn_ref[...], sc_ref[...], sh_ref[...])
    k_ref[0] = _dot(h, w_ref[:, _C_K:_C_V]).astype(BF16)
    v_ref[0] = _dot(h, w_ref[:, _C_V:_C_G]).astype(BF16)
    a_ref[0] = _dot(h, w_ref[:, _C_A:_C_END])


def _inproj_lat(x, gn, sh, sc, w, tm):
    b, s, d = x.shape
    row = lambda n: pl.BlockSpec((1, tm, n), lambda bi, i: (bi, i, 0))
    vec = pl.BlockSpec((1, 1, d), lambda bi, i: (bi, 0, 0))
    shp = lambda n, dt: jax.ShapeDtypeStruct((b, s, n), dt)
    return pl.pallas_call(
        _inproj_lat_body,
        grid=(b, s // tm),
        in_specs=[row(d), pl.BlockSpec((1, d), lambda bi, i: (0, 0)), vec, vec, _resident(w.shape)],
        out_specs=[row(CONV_W), row(GLA_KD), row(GLA_KD), row(GLA_VD), row(GLA_VD),
                   row(d), row(d), row(LANES)],
        out_shape=[shp(CONV_W, BF16), shp(GLA_KD, BF16), shp(GLA_KD, BF16), shp(GLA_VD, BF16),
                   shp(GLA_VD, BF16), shp(d, BF16), shp(d, BF16), shp(LANES, F32)],
        compiler_params=_cparams(("arbitrary", "arbitrary")),
        name="inproj_lat",
    )(x, gn, sh, sc, w)


def _inproj_ctx(ctx, gn, sh, sc, w, tm):
    b, l, d = ctx.shape
    row = lambda n: pl.BlockSpec((1, tm, n), lambda bi, i: (bi, i, 0))
    vec = pl.BlockSpec((1, d), lambda bi, i: (0, 0))
    shp = lambda n, dt: jax.ShapeDtypeStruct((b, l, n), dt)
    return pl.pallas_call(
        _inproj_ctx_body,
        grid=(b, l // tm),
        in_specs=[row(d), vec, vec, vec, _resident(w.shape)],
        out_specs=[row(GLA_KD), row(GLA_VD), row(LANES)],
        out_shape=[shp(GLA_KD, BF16), shp(GLA_VD, BF16), shp(LANES, F32)],
        compiler_params=_cparams(("arbitrary", "arbitrary")),
        name="inproj_ctx",
    )(ctx, gn, sh, sc, w)


_CONV_SUB = 8
_CONV_SPAN = GRID_W + _CONV_SUB * ((CONV_K - 1) // _CONV_SUB)


def _conv_shift_matrix():
    row = lax.broadcasted_iota(I32, (_CONV_SUB, _CONV_SPAN, GRID_W), 1)
    shift = lax.broadcasted_iota(I32, (_CONV_SUB, _CONV_SPAN, GRID_W), 0)
    col = lax.broadcasted_iota(I32, (_CONV_SUB, _CONV_SPAN, GRID_W), 2)
    m = jnp.where(col == row + shift - CONV_K // 2, 1.0, 0.0)
    return m.reshape(_CONV_SUB * _CONV_SPAN, GRID_W).astype(BF16)


def _conv_body(u_ref, sm_ref, w_ref, bdw_ref, gln_ref, bln_ref, o_ref, sh_ref, y_ref, *,
               rows_per_step):
    for r in range(rows_per_step):
        buf = r % 2
        sh_ref[buf] = _dot(sm_ref[...], u_ref[r * GRID_W:(r + 1) * GRID_W, :])
        for cb in range(CONV_W // LANES):
            ls = slice(cb * LANES, (cb + 1) * LANES)
            acc = None
            for k in range(CONV_K):
                s, a = k % _CONV_SUB, k // _CONV_SUB
                r0 = s * _CONV_SPAN + _CONV_SUB * a
                term = sh_ref[buf, r0:r0 + GRID_W, ls] * w_ref[k:k + 1, ls]
                acc = term if acc is None else acc + term
            y_ref[r * GRID_W:(r + 1) * GRID_W, ls] = acc + bdw_ref[:, ls]
    y = y_ref[...]
    mu = jnp.mean(y, axis=-1, keepdims=True)
    yc = y - mu
    var = jnp.mean(yc * yc, axis=-1, keepdims=True)
    yn = yc * lax.rsqrt(var + EPS) * gln_ref[...] + bln_ref[...]
    o_ref[...] = (yn * _sigmoid(yn)).astype(BF16)


def _conv(u, w_dw, b_dw, g_ln, b_ln, rows_per_step):
    n, cw = u.shape
    tm = rows_per_step * GRID_W
    vec = pl.BlockSpec((1, cw), lambda i: (0, 0))
    wpad = jnp.zeros((32, cw), F32).at[:CONV_K].set(w_dw)
    sm = _conv_shift_matrix()
    return pl.pallas_call(
        functools.partial(_conv_body, rows_per_step=rows_per_step),
        grid=(n // tm,),
        in_specs=[pl.BlockSpec((tm, cw), lambda i: (i, 0)),
                  pl.BlockSpec(sm.shape, lambda i: (0, 0)),
                  pl.BlockSpec((32, cw), lambda i: (0, 0)), vec, vec, vec],
        out_specs=pl.BlockSpec((tm, cw), lambda i: (i, 0)),
        out_shape=jax.ShapeDtypeStruct((n, cw), BF16),
        scratch_shapes=[pltpu.VMEM((2, _CONV_SUB * _CONV_SPAN, cw), F32),
                        pltpu.VMEM((tm, cw), F32)],
        compiler_params=_cparams(("arbitrary",)),
        name="conv_module",
    )(u, sm, wpad, b_dw.reshape(1, cw), g_ln.reshape(1, cw), b_ln.reshape(1, cw))


def _log_sigmoid(z):
    return jnp.minimum(z, 0.0) - jnp.log(1.0 + jnp.exp(-jnp.abs(z)))


def _gla_keep_mask(t, reverse):
    row = lax.broadcasted_iota(I32, (t, t), 0)
    col = lax.broadcasted_iota(I32, (t, t), 1)
    same_chunk = lax.shift_right_logical(row, 6) == lax.shift_right_logical(col, 6)
    return jnp.logical_and(same_chunk, (col >= row) if reverse else (col <= row))


def _gla_decay(a, wh_ref, wl_ref, ba, keep):
    a_hi, a_lo = _split_bf16(a)
    z = _dot(jnp.concatenate([a_hi, a_lo], axis=1), wh_ref[...]) + _dot(a_hi, wl_ref[...]) + ba
    loga = _log_sigmoid(z) * (1.0 / GLA_TAU)
    l_hi, l_lo = _split_bf16(loga)
    tri = jnp.where(keep, 1.0, 0.0).astype(BF16)
    return _dot(tri, l_hi) + _dot(tri, l_lo)


def _gla_step(k, v, bcum, keep, states, *, reverse, q=None):
    t = k.shape[0]
    c = GLA_CHUNK
    nch = t // c
    tots = [bcum[n * c:n * c + 1, :] if reverse else bcum[(n + 1) * c - 1:(n + 1) * c, :]
            for n in range(nch)]
    totb = jnp.concatenate([jnp.broadcast_to(tt, (c, GLA_KD)) for tt in tots], axis=0)
    k32 = k.astype(F32)
    k_end = (k32 * jnp.exp(totb - bcum)).astype(BF16)
    decs = [jnp.exp(tt) for tt in tots]
    if q is not None:
        q_dec = (q.astype(F32) * jnp.exp(bcum) * (GLA_DK ** -0.5)).astype(BF16)
        k_inv = (k32 * jnp.exp(-bcum)).astype(BF16)
    order = range(nch - 1, -1, -1) if reverse else range(nch)
    outs, new_states = [], []
    for h in range(GLA_H):
        ks = slice(h * GLA_DK, (h + 1) * GLA_DK)
        vh = v[:, h * GLA_DV:(h + 1) * GLA_DV]
        st = states[h]
        if q is not None:
            scores = lax.dot_general(q_dec[:, ks], k_inv[:, ks], (((1,), (1,)), ((), ())),
                                     preferred_element_type=F32)
            o_h = _dot(jnp.where(keep, scores, 0.0).astype(BF16), vh)
            inter = [None] * nch
        for n in order:
            rs = slice(n * c, (n + 1) * c)
            if q is not None:
                inter[n] = _dot(q_dec[rs, ks], st.astype(BF16))
            kv = lax.dot_general(k_end[rs, ks], vh[rs, :], (((0,), (0,)), ((), ())),
                                 preferred_element_type=F32)
            dt = jnp.transpose(jnp.broadcast_to(decs[n][:, ks], (GLA_DK, GLA_DK)))
            st = st * jnp.concatenate([dt] * (GLA_DV // GLA_DK), axis=1) + kv
        new_states.append(st)
        if q is not None:
            outs.append(o_h + jnp.concatenate(inter, axis=0))
    return new_states, (jnp.concatenate(outs, axis=1) if q is not None else None)


def _gla_ctx_body(k_ref, v_ref, a_ref, wh_ref, wl_ref, ba_ref, st_ref, *, reverse):
    j = pl.program_id(1)

    @pl.when(j == 0)
    def _():
        st_ref[...] = jnp.zeros_like(st_ref)

    keep = _gla_keep_mask(k_ref.shape[1], reverse)
    bcum = _gla_decay(a_ref[0], wh_ref, wl_ref, ba_ref[...], keep)
    states, _ = _gla_step(k_ref[0], v_ref[0], bcum, keep,
                          [st_ref[0, h] for h in range(GLA_H)], reverse=reverse)
    for h in range(GLA_H):
        st_ref[0, h] = states[h]


def _gla_lat_body(*refs, reverse, final):
    if final:
        (k_ref, v_ref, a_ref, an_ref, q_ref, wh_ref, wl_ref, ba_ref, s0_ref, sg_ref, op_ref,
         gn_ref, o_ref, st_ref, bc_ref) = refs
    else:
        (k_ref, v_ref, a_ref, an_ref, q_ref, wh_ref, wl_ref, ba_ref, s0_ref,
         o_ref, st_ref, bc_ref) = refs
    j = pl.program_id(1)
    nb, t = k_ref.shape[0], k_ref.shape[1]
    keep = _gla_keep_mask(t, reverse)

    @pl.when(j == 0)
    def _():
        st_ref[...] = s0_ref[...]
        for bb in range(nb):
            bc_ref[bb] = _gla_decay(a_ref[bb], wh_ref, wl_ref, ba_ref[...], keep)

    for bb in range(nb):
        states, o = _gla_step(k_ref[bb], v_ref[bb], bc_ref[bb], keep,
                              [st_ref[bb, h] for h in range(GLA_H)], reverse=reverse,
                              q=q_ref[bb])
        bc_ref[bb] = _gla_decay(an_ref[bb], wh_ref, wl_ref, ba_ref[...], keep)
        for h in range(GLA_H):
            st_ref[bb, h] = states[h]
        if not final:
            o_ref[bb] = o.astype(BF16)
            continue
        o = o + op_ref[bb].astype(F32)
        parts = []
        for h in range(GLA_H):
            oh = o[:, h * GLA_DV:(h + 1) * GLA_DV]
            ms = jnp.mean(oh * oh, axis=-1, keepdims=True)
            parts.append(oh * lax.rsqrt(ms + EPS) * gn_ref[...])
        o_ref[bb] = (jnp.concatenate(parts, axis=1) * sg_ref[bb].astype(F32)).astype(BF16)


def _gla_decay_weights(w_alpha_d, reverse):
    a0 = GLA_RANK if reverse else 0
    w = jnp.zeros((LANES, GLA_KD), F32).at[a0:a0 + GLA_RANK].set(w_alpha_d)
    hi = w.astype(BF16)
    lo = (w - hi.astype(F32)).astype(BF16)
    return jnp.concatenate([hi, hi], axis=0), lo


def _gla_ctx(k, v, a, w_alpha_d, ba, reverse, cps):
    b, l, _ = k.shape
    tm = cps * GLA_CHUNK
    nj = l // tm
    jmap = (lambda j: nj - 1 - j) if reverse else (lambda j: j)
    row = lambda n: pl.BlockSpec((1, tm, n), lambda bi, j: (bi, jmap(j), 0))
    wh, wl = _gla_decay_weights(w_alpha_d, reverse)
    return pl.pallas_call(
        functools.partial(_gla_ctx_body, reverse=reverse),
        grid=(b, nj),
        in_specs=[row(GLA_KD), row(GLA_VD), row(LANES),
                  pl.BlockSpec(wh.shape, lambda bi, j: (0, 0)),
                  pl.BlockSpec(wl.shape, lambda bi, j: (0, 0)),
                  pl.BlockSpec((1, GLA_KD), lambda bi, j: (0, 0))],
        out_specs=pl.BlockSpec((1, GLA_H, GLA_DK, GLA_DV), lambda bi, j: (bi, 0, 0, 0)),
        out_shape=jax.ShapeDtypeStruct((b, GLA_H, GLA_DK, GLA_DV), F32),
        compiler_params=_cparams(("arbitrary", "arbitrary")),
        name="gla_ctx_bwd" if reverse else "gla_ctx_fwd",
    )(k, v, a, wh, wl, ba.reshape(1, GLA_KD))


def _gla_lat(k, v, a, q, w_alpha_d, ba, s0, reverse, cps, sg=None, o_prev=None, g_norm=None):
    b, s, _ = k.shape
    final = sg is not None
    tm = cps * GLA_CHUNK
    nj = s // tm
    jmap = (lambda j: nj - 1 - j) if reverse else (lambda j: j)
    nb = 2 if b % 2 == 0 else 1
    row = lambda n: pl.BlockSpec((nb, tm, n), lambda bi, j: (bi, jmap(j), 0))
    wh, wl = _gla_decay_weights(w_alpha_d, reverse)
    a_next = pl.BlockSpec((nb, tm, LANES), lambda bi, j: (bi, jmap(jnp.minimum(j + 1, nj - 1)), 0))
    in_specs = [row(GLA_KD), row(GLA_VD), row(LANES), a_next, row(GLA_KD),
                pl.BlockSpec(wh.shape, lambda bi, j: (0, 0)),
                pl.BlockSpec(wl.shape, lambda bi, j: (0, 0)),
                pl.BlockSpec((1, GLA_KD), lambda bi, j: (0, 0)),
                pl.BlockSpec((nb, GLA_H, GLA_DK, GLA_DV), lambda bi, j: (bi, 0, 0, 0))]
    args = [k, v, a, a, q, wh, wl, ba.reshape(1, GLA_KD), s0]
    if final:
        in_specs += [row(GLA_VD), row(GLA_VD), pl.BlockSpec((1, GLA_DV), lambda bi, j: (0, 0))]
        args += [sg, o_prev, g_norm.reshape(1, GLA_DV)]
    return pl.pallas_call(
        functools.partial(_gla_lat_body, reverse=reverse, final=final),
        grid=(b // nb, nj),
        in_specs=in_specs,
        out_specs=row(GLA_VD),
        out_shape=jax.ShapeDtypeStruct((b, s, GLA_VD), BF16),
        scratch_shapes=[pltpu.VMEM((nb, GLA_H, GLA_DK, GLA_DV), F32),
                        pltpu.VMEM((nb, tm, GLA_KD), F32)],
        compiler_params=_cparams(("arbitrary", "arbitrary")),
        name="gla_lat_bwd" if reverse else "gla_lat_fwd",
    )(*args)


def _pack_pair(lo, hi):
    lo_b = pltpu.bitcast(lo.astype(BF16).astype(F32), I32)
    hi_b = pltpu.bitcast(hi.astype(BF16).astype(F32), I32)
    return lax.shift_right_logical(lo_b, 16) | (hi_b & jnp.int32(-65536))


def _unpack_pair(w):
    lo = pltpu.bitcast(lax.shift_left(w, 16), F32)
    hi = pltpu.bitcast(w & jnp.int32(-65536), F32)
    return lo, hi


def _merge_body(act_ref, og_ref, gc_ref, gg_ref, x_ref, gt_ref, sh_ref, sc_ref, gn_ref,
                wc_ref, wg_ref, wo_ref, wr_ref, br_ref,
                x1_ref, h2_ref, ti_ref, tp_ref, cnt_ref):
    yc = _dot(act_ref[...], wc_ref[...])
    yg = _dot(og_ref[...], wg_ref[...])
    m = gc_ref[...].astype(F32) * yc + gg_ref[...].astype(F32) * yg
    x1 = x_ref[...] + gt_ref[0] * _dot(m.astype(BF16), wo_ref[...])
    x1_ref[...] = x1
    ms = jnp.mean(x1 * x1, axis=-1, keepdims=True)
    h2 = (x1 * lax.rsqrt(ms + EPS) * gn_ref[...]) * (1.0 + sc_ref[0]) + sh_ref[0]
    half = D_MODEL // 2
    h2_ref[...] = _pack_pair(h2[:, :half], h2[:, half:])
    logits = _dot3(h2, wr_ref[...]) + br_ref[...]
    lane = lax.broadcasted_iota(I32, logits.shape, 1).astype(F32)
    neg = jnp.float32(-jnp.inf)
    work = jnp.where(lane < N_EXPERTS, logits, neg)
    vals, idxs = [], []
    for _ in range(TOP_K):
        mx = jnp.max(work, axis=-1, keepdims=True)
        ix = jnp.min(jnp.where(work == mx, lane, float(LANES)), axis=-1, keepdims=True)
        vals.append(mx)
        idxs.append(ix)
        work = jnp.where(lane == ix, neg, work)
    es = [jnp.exp(v - vals[0]) for v in vals]
    den = es[0] + es[1] + es[2] + es[3]
    ti = jnp.zeros(logits.shape, F32)
    tp = jnp.zeros(logits.shape, F32)
    onehot = jnp.zeros(logits.shape, F32)
    for kk in range(TOP_K):
        ti = jnp.where(lane == kk, idxs[kk], ti)
        tp = jnp.where(lane == kk, es[kk] / den, tp)
        onehot = onehot + jnp.where(lane == idxs[kk], 1.0, 0.0)
    @pl.when(pl.program_id(0) == 0)
    def _():
        cnt_ref[...] = jnp.zeros_like(cnt_ref)

    tm = logits.shape[0]
    earlier = (lax.broadcasted_iota(I32, (tm, tm), 1) < lax.broadcasted_iota(I32, (tm, tm), 0))
    before = _dot(jnp.where(earlier, 1.0, 0.0).astype(BF16), onehot.astype(BF16)) + cnt_ref[0:1, :]
    for kk in range(TOP_K):
        rank = jnp.sum(jnp.where(lane == idxs[kk], before, 0.0), axis=-1, keepdims=True)
        ti = jnp.where(lane == TOP_K + kk, rank, ti)
    cnt_ref[...] = cnt_ref[...] + jnp.sum(onehot, axis=0, keepdims=True)
    ti_ref[...] = ti.astype(I32)
    tp_ref[...] = tp


def _merge(act, og, gc, gg, x2d, gt1, sh2, sc2, gn, wc, wg, wo, wr, br, tm, rows_per_batch):
    n, d = x2d.shape
    per_b = rows_per_batch // tm
    row = lambda w: pl.BlockSpec((tm, w), lambda i: (i, 0))
    vec = pl.BlockSpec((1, 1, d), lambda i: (i // per_b, 0, 0))
    const = lambda shape: pl.BlockSpec(shape, lambda i: (0,) * len(shape))
    return pl.pallas_call(
        _merge_body,
        grid=(n // tm,),
        in_specs=[row(d), row(d), row(d), row(d), row(d), vec, vec, vec, const((1, d)),
                  _resident(wc.shape), _resident(wg.shape), _resident(wo.shape),
                  const(wr.shape), const((1, LANES))],
        out_specs=[row(d), row(d // 2), row(LANES), row(LANES), const((8, LANES))],
        out_shape=[jax.ShapeDtypeStruct((n, d), F32), jax.ShapeDtypeStruct((n, d // 2), I32),
                   jax.ShapeDtypeStruct((n, LANES), I32), jax.ShapeDtypeStruct((n, LANES), F32),
                   jax.ShapeDtypeStruct((8, LANES), F32)],
        compiler_params=_cparams(("arbitrary",)),
        name="merge_router",
    )(act, og, gc, gg, x2d, gt1, sh2, sc2, gn, wc, wg, wo, wr, br)


def _dispatch_body(dest_ref, h2_ref, xp_in_ref, xp_ref, sem, *, tokens):
    del xp_in_ref

    def row_copy(t, kk):
        dst = dest_ref[0, 0, t * TOP_K + kk]
        return pltpu.make_async_copy(h2_ref.at[pl.ds(t, 1)], xp_ref.at[pl.ds(dst, 1)], sem)

    def issue(t, carry):
        for kk in range(TOP_K):
            row_copy(t, kk).start(priority=kk % 2)
        return carry

    lax.fori_loop(0, tokens, issue, 0, unroll=8)

    def drain(t, carry):
        for _ in range(TOP_K):
            pltpu.make_async_copy(h2_ref.at[pl.ds(0, 1)], xp_ref.at[pl.ds(0, 1)], sem).wait()
        return carry

    lax.fori_loop(0, tokens, drain, 0, unroll=8)


def _dispatch(dest, h2p, xp_init, tokens):
    n, w = h2p.shape
    steps = n // tokens
    dest3 = dest.reshape(steps, 1, tokens * TOP_K)
    return pl.pallas_call(
        functools.partial(_dispatch_body, tokens=tokens),
        grid=(steps,),
        in_specs=[pl.BlockSpec((1, 1, tokens * TOP_K), lambda i: (i, 0, 0),
                               memory_space=pltpu.SMEM),
                  pl.BlockSpec((tokens, w), lambda i: (i, 0)),
                  pl.BlockSpec(memory_space=pl.ANY)],
        out_specs=pl.BlockSpec(memory_space=pl.ANY),
        out_shape=jax.ShapeDtypeStruct(xp_init.shape, xp_init.dtype),
        scratch_shapes=[pltpu.SemaphoreType.DMA],
        input_output_aliases={2: 0},
        compiler_params=_cparams(("arbitrary",)),
        name="moe_dispatch",
    )(dest3, h2p, xp_init)


_GLU_GROUP = 2 * LANES


def _deinterleave_matrix():
    src = lax.broadcasted_iota(I32, (_GLU_GROUP, _GLU_GROUP), 0)
    dst = lax.broadcasted_iota(I32, (_GLU_GROUP, _GLU_GROUP), 1)
    want = jnp.where(dst < LANES, 2 * dst, 2 * (dst - LANES) + 1)
    return jnp.where(src == want, 1.0, 0.0).astype(BF16)


def _expert_body(be_ref, nreal_ref, xp_ref, w1_ref, b1_ref, w2_ref, b2_ref, yp_ref,
                 w1s_ref, w2s_ref):
    i = pl.program_id(0)
    new_expert = jnp.logical_or(i == 0, be_ref[i] != be_ref[jnp.maximum(i - 1, 0)])

    @pl.when(jnp.logical_and(new_expert, i < nreal_ref[0]))
    def _():
        perm = _deinterleave_matrix()
        for g in range(2 * D_FF // _GLU_GROUP):
            cs = slice(g * _GLU_GROUP, (g + 1) * _GLU_GROUP)
            w1s_ref[:, cs] = _dot(w1_ref[0, :, cs].astype(BF16), perm).astype(BF16)
        w2s_ref[...] = w2_ref[0].astype(BF16)

    @pl.when(i < nreal_ref[0])
    def _():
        lo, hi = _unpack_pair(xp_ref[...])
        x = jnp.concatenate([lo, hi], axis=1).astype(BF16)
        hid = _dot(x, w1s_ref[...]) + b1_ref[0]
        ngrp = 2 * D_FF // _GLU_GROUP
        hg = jnp.concatenate(
            [hid[:, g * _GLU_GROUP:g * _GLU_GROUP + LANES] for g in range(ngrp)], axis=1)
        hl = jnp.concatenate(
            [hid[:, g * _GLU_GROUP + LANES:(g + 1) * _GLU_GROUP] for g in range(ngrp)], axis=1)
        xg = jnp.minimum(hg, SWIGLU_LIMIT)
        xl = jnp.clip(hl, -SWIGLU_LIMIT, SWIGLU_LIMIT)
        act = xg * _sigmoid(SWIGLU_ALPHA * xg) * (xl + 1.0)
        y = _dot(act.astype(BF16), w2s_ref[...]) + b2_ref[0]
        half = D_MODEL // 2
        yp_ref[...] = _pack_pair(y[:, :half], y[:, half:])

    @pl.when(i >= nreal_ref[0])
    def _():
        yp_ref[...] = jnp.zeros_like(yp_ref)


def _experts(blk_expert, n_real, xp, w1, b1, w2, b2, bm):
    p, w = xp.shape
    grid_spec = pltpu.PrefetchScalarGridSpec(
        num_scalar_prefetch=2,
        grid=(p // bm,),
        in_specs=[pl.BlockSpec((bm, w), lambda i, be, nr: (i, 0)),
                  pl.BlockSpec((1, D_MODEL, 2 * D_FF), lambda i, be, nr: (be[i], 0, 0)),
                  pl.BlockSpec((1, 1, 2 * D_FF), lambda i, be, nr: (be[i], 0, 0)),
                  pl.BlockSpec((1, D_FF, D_MODEL), lambda i, be, nr: (be[i], 0, 0)),
                  pl.BlockSpec((1, 1, D_MODEL), lambda i, be, nr: (be[i], 0, 0))],
        out_specs=pl.BlockSpec((bm, w), lambda i, be, nr: (i, 0)),
        scratch_shapes=[pltpu.VMEM((D_MODEL, 2 * D_FF), BF16), pltpu.VMEM((D_FF, D_MODEL), BF16)],
    )
    return pl.pallas_call(
        _expert_body,
        grid_spec=grid_spec,
        out_shape=jax.ShapeDtypeStruct((p, w), I32),
        compiler_params=_cparams(("arbitrary",)),
        name="moe_experts",
    )(blk_expert, n_real, xp, w1, b1, w2, b2)


def _combine_body(dcur_ref, dnext_ref, yp_ref, tp_ref, x1_ref, gt_ref, gf_ref, o_ref,
                  buf_ref, sem, *, tokens):
    i = pl.program_id(0)
    n = pl.num_programs(0)
    slot = i % 2

    def row_copy(dref, s, t, kk):
        src = dref[0, 0, t * TOP_K + kk]
        return pltpu.make_async_copy(yp_ref.at[pl.ds(src, 1)], buf_ref.at[s, kk, pl.ds(t, 1)],
                                     sem.at[s])

    def issue_all(dref, s):
        def issue(t, carry):
            for kk in range(TOP_K):
                row_copy(dref, s, t, kk).start(priority=kk % 2)
            return carry
        lax.fori_loop(0, tokens, issue, 0, unroll=8)

    @pl.when(i == 0)
    def _():
        issue_all(dcur_ref, 0)

    @pl.when(i + 1 < n)
    def _():
        issue_all(dnext_ref, 1 - slot)

    def drain(t, carry):
        for kk in range(TOP_K):
            pltpu.make_async_copy(yp_ref.at[pl.ds(0, 1)], buf_ref.at[slot, kk, pl.ds(0, 1)],
                                  sem.at[slot]).wait()
        return carry

    lax.fori_loop(0, tokens, drain, 0, unroll=8)

    half = D_MODEL // 2
    tp = tp_ref[...]
    y_lo = jnp.zeros((tokens, half), F32)
    y_hi = jnp.zeros((tokens, half), F32)
    for kk in range(TOP_K):
        lo, hi = _unpack_pair(buf_ref[slot, kk])
        pk = tp[:, kk:kk + 1]
        y_lo = y_lo + pk * lo
        y_hi = y_hi + pk * hi
    y = jnp.concatenate([y_lo, y_hi], axis=1)
    x2 = x1_ref[...] + gt_ref[0] * y
    ms = jnp.mean(x2 * x2, axis=-1, keepdims=True)
    o_ref[...] = x2 * lax.rsqrt(ms + EPS) * gf_ref[...]


def _combine(dest, yp, tp, x1, gt2, gf, tokens, rows_per_batch):
    n, d = x1.shape
    steps = n // tokens
    per_b = rows_per_batch // tokens
    dest3 = dest.reshape(steps, 1, tokens * TOP_K)
    dspec = lambda f: pl.BlockSpec((1, 1, tokens * TOP_K), f, memory_space=pltpu.SMEM)
    return pl.pallas_call(
        functools.partial(_combine_body, tokens=tokens),
        grid=(steps,),
        in_specs=[dspec(lambda i: (i, 0, 0)),
                  dspec(lambda i: (jnp.minimum(i + 1, steps - 1), 0, 0)),
                  pl.BlockSpec(memory_space=pl.ANY),
                  pl.BlockSpec((tokens, LANES), lambda i: (i, 0)),
                  pl.BlockSpec((tokens, d), lambda i: (i, 0)),
                  pl.BlockSpec((1, 1, d), lambda i: (i // per_b, 0, 0)),
                  pl.BlockSpec((1, d), lambda i: (0, 0))],
        out_specs=pl.BlockSpec((tokens, d), lambda i: (i, 0)),
        out_shape=jax.ShapeDtypeStruct((n, d), F32),
        scratch_shapes=[pltpu.VMEM((2, TOP_K, tokens, d // 2), I32),
                        pltpu.SemaphoreType.DMA((2,))],
        compiler_params=_cparams(("arbitrary",)),
        name="moe_combine",
    )(dest3, dest3, yp, tp, x1, gt2, gf)


_SC_CORES = 2
_SC_SUBCORES = 16
_SC_CHUNK = 64


def _sc_gather_rows(table, idx):
    rows, width = idx.shape[0], table.shape[1]
    workers = _SC_CORES * _SC_SUBCORES
    per_worker = rows // workers
    assert rows % (workers * _SC_CHUNK) == 0
    mesh = plsc.VectorSubcoreMesh(core_axis_name="c", subcore_axis_name="s")

    def body(table_hbm, idx_hbm, out_hbm, idx_v, rows_v, sem):
        wid = lax.axis_index("s") * _SC_CORES + lax.axis_index("c")
        base = wid * per_worker

        @pl.loop(0, per_worker // _SC_CHUNK)
        def _(i):
            off = base + i * _SC_CHUNK
            pltpu.sync_copy(idx_hbm.at[pl.ds(off, _SC_CHUNK)], idx_v)
            pltpu.async_copy(table_hbm.at[idx_v], rows_v, sem).wait()
            pltpu.sync_copy(rows_v, out_hbm.at[pl.ds(off, _SC_CHUNK)])

    return pl.kernel(
        body,
        out_type=jax.ShapeDtypeStruct((rows, width), table.dtype),
        mesh=mesh,
        scratch_types=[pltpu.VMEM((_SC_CHUNK,), I32), pltpu.VMEM((_SC_CHUNK, width), table.dtype),
                       pltpu.SemaphoreType.DMA],
        name="sc_gather_rows",
    )(table, idx)


def _combine_dense_body(y4_ref, tp_ref, x1_ref, gt_ref, gf_ref, o_ref):
    half = D_MODEL // 2
    tp = tp_ref[...]
    y_lo = y_hi = None
    for kk in range(TOP_K):
        lo, hi = _unpack_pair(y4_ref[:, kk * half:(kk + 1) * half])
        pk = tp[:, kk:kk + 1]
        y_lo = pk * lo if y_lo is None else y_lo + pk * lo
        y_hi = pk * hi if y_hi is None else y_hi + pk * hi
    x2 = x1_ref[...] + gt_ref[0] * jnp.concatenate([y_lo, y_hi], axis=1)
    ms = jnp.mean(x2 * x2, axis=-1, keepdims=True)
    o_ref[...] = x2 * lax.rsqrt(ms + EPS) * gf_ref[...]


def _combine_dense(y4, tp, x1, gt2, gf, tokens, rows_per_batch):
    n, d = x1.shape
    per_b = rows_per_batch // tokens
    return pl.pallas_call(
        _combine_dense_body,
        grid=(n // tokens,),
        in_specs=[pl.BlockSpec((tokens, TOP_K * d // 2), lambda i: (i, 0)),
                  pl.BlockSpec((tokens, LANES), lambda i: (i, 0)),
                  pl.BlockSpec((tokens, d), lambda i: (i, 0)),
                  pl.BlockSpec((1, 1, d), lambda i: (i // per_b, 0, 0)),
                  pl.BlockSpec((1, d), lambda i: (0, 0))],
        out_specs=pl.BlockSpec((tokens, d), lambda i: (i, 0)),
        out_shape=jax.ShapeDtypeStruct((n, d), F32),
        compiler_params=_cparams(("arbitrary",)),
        name="moe_combine_dense",
    )(y4, tp, x1, gt2, gf)


def _routing_tables(top_idx, rank, counts, bm):
    n = top_idx.shape[0]
    nk = n * TOP_K
    padded = (counts + bm - 1) // bm * bm
    pad_end = jnp.cumsum(padded)
    pad_start = pad_end - padded
    dest = pad_start[top_idx] + rank
    n_blocks = (nk + N_EXPERTS * (bm - 1) + bm - 1) // bm
    blk_expert = jnp.minimum(
        jnp.searchsorted(pad_end, jnp.arange(n_blocks, dtype=I32) * bm, side="right"),
        N_EXPERTS - 1).astype(I32)
    n_real = (pad_end[-1] // bm).astype(I32).reshape(1)
    return dest.astype(I32), blk_expert, n_real, n_blocks


def _layer(x, c, ctx, c_ctx, w_ada, b_ada, g_mix_norm, w_in, w_dw, b_dw, g_conv_ln, b_conv_ln,
           w_conv_out, w_alpha, b_alpha, g_gla_norm, w_gla_out, w_out, g_ffn_norm, w_router,
           b_router, w_exp_in, b_exp_in, w_exp_out, b_exp_out, g_final, *, cfg):
    b, s, d = x.shape
    n = b * s

    rows = (b + 1 + 7) // 8 * 8
    cc = jnp.zeros((rows, d), F32).at[:b].set(c).at[b].set(c_ctx)
    mod = _ada(cc, w_ada, b_ada)
    sh1, sc1, gt1, sh2, sc2, gt2 = [mod[:b, i * d:(i + 1) * d].reshape(b, 1, d) for i in range(6)]
    csh1 = mod[b:b + 1, 0:d]
    csc1 = mod[b:b + 1, d:2 * d]

    a0 = 2 * CONV_W + 2 * GLA_KD + 2 * GLA_VD
    w_in_r = jnp.concatenate(
        [w_in[:, :a0], w_in[:, a0 + 2 * GLA_RANK:], w_in[:, a0:a0 + 2 * GLA_RANK],
         jnp.zeros((d, LANES - 2 * GLA_RANK), F32)], axis=1).astype(BF16)
    gmn = g_mix_norm.reshape(1, d)

    u, q, k, v, sg, gc, gg, a = _inproj_lat(x, gmn, sh1, sc1, w_in_r, cfg["tm_in"])
    kc, vc, ac = _inproj_ctx(ctx, gmn, csh1, csc1, w_in_r, cfg["tm_ctx"])

    act = _conv(u.reshape(n, CONV_W), w_dw, b_dw, g_conv_ln, b_conv_ln, cfg["conv_rows"])

    cps = cfg["gla_cps"]
    st_b = _gla_ctx(kc, vc, ac, w_alpha[1], b_alpha[1], True, cps)
    o_b = _gla_lat(k, v, a, q, w_alpha[1], b_alpha[1], st_b, True, cps)
    st_f = _gla_ctx(kc, vc, ac, w_alpha[0], b_alpha[0], False, cps)
    og = _gla_lat(k, v, a, q, w_alpha[0], b_alpha[0], st_f, False, cps,
                  sg=sg, o_prev=o_b, g_norm=g_gla_norm)

    wr = jnp.zeros((d, LANES), F32).at[:, :N_EXPERTS].set(w_router)
    br = jnp.zeros((1, LANES), F32).at[0, :N_EXPERTS].set(b_router)
    x1, h2p, ti, tp, cnt = _merge(
        act, og.reshape(n, d), gc.reshape(n, d), gg.reshape(n, d), x.reshape(n, d),
        gt1, sh2, sc2, g_ffn_norm.reshape(1, d),
        w_conv_out.astype(BF16), w_gla_out.astype(BF16), w_out.astype(BF16), wr, br,
        cfg["tm_merge"], s)

    bm = cfg["moe_block"]
    dest, blk_expert, n_real, n_blocks = _routing_tables(
        ti[:, :TOP_K], ti[:, TOP_K:2 * TOP_K], cnt[0, :N_EXPERTS].astype(I32), bm)
    xp = _dispatch(dest, h2p, jnp.zeros((n_blocks * bm, d // 2), I32), cfg["moe_tokens"])

    b1 = b_exp_in.reshape(N_EXPERTS, 2 * D_FF // _GLU_GROUP, LANES, 2).transpose(0, 1, 3, 2)
    b1 = b1.reshape(N_EXPERTS, 1, 2 * D_FF)
    yp = _experts(blk_expert, n_real, xp, w_exp_in, b1, w_exp_out,
                  b_exp_out.reshape(N_EXPERTS, 1, d), bm)

    y4 = _sc_gather_rows(yp, dest.reshape(n * TOP_K)).reshape(n, TOP_K * d // 2)
    out = _combine_dense(y4, tp, x1, gt2, g_final.reshape(1, d), cfg["tm_combine"], s)
    return out.reshape(b, s, d)


def _config(s, l):
    return dict(tm_in=min(512, s), tm_ctx=min(256, l), conv_rows=4, gla_cps=4,
                tm_merge=min(512, s), moe_block=512, moe_tokens=128, tm_combine=min(256, s))


def kernel(x, c, ctx, c_ctx, w_ada, b_ada, g_mix_norm, w_in, w_dw, b_dw, g_conv_ln, b_conv_ln,
           w_conv_out, w_alpha, b_alpha, g_gla_norm, w_gla_out, w_out, g_ffn_norm, w_router,
           b_router, w_exp_in, b_exp_in, w_exp_out, b_exp_out, g_final):
    depth = w_ada.shape[0]
    assert depth == 1, "single-layer block: the context stream is only consumed by the GLA scan"
    cfg = _config(x.shape[1], ctx.shape[1])
    return _layer(x, c, ctx, c_ctx, w_ada[0], b_ada[0], g_mix_norm[0], w_in[0], w_dw[0], b_dw[0],
                  g_conv_ln[0], b_conv_ln[0], w_conv_out[0], w_alpha[0], b_alpha[0],
                  g_gla_norm[0], w_gla_out[0], w_out[0], g_ffn_norm[0], w_router[0], b_router[0],
                  w_exp_in[0], b_exp_in[0], w_exp_out[0], b_exp_out[0], g_final, cfg=cfg)
```

```python
import functools
import math

import jax
import jax.numpy as jnp
from jax import lax
from jax.experimental import pallas as pl
from jax.experimental.pallas import tpu as pltpu
from jax.experimental.pallas import tpu_sc as plsc

F32 = jnp.float32
BF16 = jnp.bfloat16
I32 = jnp.int32

D_MODEL = 1024
GRID_W = 64
EPS = 1e-6
CONV_W = 1024
CONV_K = 31
GLA_H = 4
GLA_DK = 128
GLA_DV = 256
GLA_KD = GLA_H * GLA_DK
GLA_VD = GLA_H * GLA_DV
GLA_RANK = 16
GLA_TAU = 16.0
GLA_CHUNK = 64
N_EXPERTS = 32
TOP_K = 4
D_FF = 1024
SWIGLU_ALPHA = 1.702
SWIGLU_LIMIT = 7.0

LANES = 128
VMEM_LIMIT = 56 * 1024 * 1024

_C_CONV_A = 0
_C_CONV_B = _C_CONV_A + CONV_W
_C_Q = _C_CONV_B + CONV_W
_C_K = _C_Q + GLA_KD
_C_V = _C_K + GLA_KD
_C_G = _C_V + GLA_VD
_C_GC = _C_G + GLA_VD
_C_GG = _C_GC + D_MODEL
_C_A = _C_GG + D_MODEL
_C_END = _C_A + LANES


def _cparams(sem):
    return pltpu.CompilerParams(dimension_semantics=sem, vmem_limit_bytes=VMEM_LIMIT)


def _dot(a, b):
    return jnp.dot(a, b, preferred_element_type=F32)


def _split_bf16(x):
    hi = x.astype(BF16)
    lo = (x - hi.astype(F32)).astype(BF16)
    return hi, lo


def _dot3(a, b):
    a_hi, a_lo = _split_bf16(a)
    b_hi, b_lo = _split_bf16(b)
    return _dot(a_hi, b_hi) + _dot(a_lo, b_hi) + _dot(a_hi, b_lo)


def _sigmoid(x):
    return 1.0 / (1.0 + jnp.exp(-x))


def _resident(shape):
    nd = len(shape)
    return pl.BlockSpec(shape, lambda *_: (0,) * nd, pipeline_mode=pl.Buffered(1))


def _ada_body(a_ref, w_ref, b_ref, o_ref):
    a = a_ref[...]
    a = a * _sigmoid(a)
    o_ref[...] = _dot3(a, w_ref[...]) + b_ref[...]


def _ada(cc, w, b):
    rows, d = cc.shape
    n = w.shape[1]
    tn = 512
    return pl.pallas_call(
        _ada_body,
        grid=(n // tn,),
        in_specs=[pl.BlockSpec((rows, d), lambda j: (0, 0)),
                  pl.BlockSpec((d, tn), lambda j: (0, j)),
                  pl.BlockSpec((1, tn), lambda j: (0, j))],
        out_specs=pl.BlockSpec((rows, tn), lambda j: (0, j)),
        out_shape=jax.ShapeDtypeStruct((rows, n), F32),
        compiler_params=_cparams(("arbitrary",)),
        name="ada_mod",
    )(cc, w, b.reshape(1, n))


def _norm_mod(xv, gn, sc, sh):
    ms = jnp.mean(xv * xv, axis=-1, keepdims=True)
    y = xv * lax.rsqrt(ms + EPS) * gn
    return (y * (1.0 + sc) + sh).astype(BF16)


def _inproj_lat_body(x_ref, gn_ref, sh_ref, sc_ref, w_ref,
                     u_ref, q_ref, k_ref, v_ref, sg_ref, gc_ref, gg_ref, a_ref):
    h = _norm_mod(x_ref[0], gn_ref[...], sc_ref[0], sh_ref[0])
    ca = _dot(h, w_ref[:, _C_CONV_A:_C_CONV_B])
    cb = _dot(h, w_ref[:, _C_CONV_B:_C_Q])
    u_ref[0] = (ca * _sigmoid(cb)).astype(BF16)
    q_ref[0] = _dot(h, w_ref[:, _C_Q:_C_K]).astype(BF16)
    k_ref[0] = _dot(h, w_ref[:, _C_K:_C_V]).astype(BF16)
    v_ref[0] = _dot(h, w_ref[:, _C_V:_C_G]).astype(BF16)
    g = _dot(h, w_ref[:, _C_G:_C_GC])
    sg_ref[0] = (g * _sigmoid(g)).astype(BF16)
    gc_ref[0] = _sigmoid(_dot(h, w_ref[:, _C_GC:_C_GG])).astype(BF16)
    gg_ref[0] = _sigmoid(_dot(h, w_ref[:, _C_GG:_C_A])).astype(BF16)
    a_ref[0] = _dot(h, w_ref[:, _C_A:_C_END])


def _inproj_ctx_body(x_ref, gn_ref, sh_ref, sc_ref, w_ref, k_ref, v_ref, a_ref):
    h = _norm_mod(x_ref[0], gn_ref[...], sc_ref[...], sh_ref[...])
    k_ref[0] = _dot(h, w_ref[:, _C_K:_C_V]).astype(BF16)
    v_ref[0] = _dot(h, w_ref[:, _C_V:_C_G]).astype(BF16)
    a_ref[0] = _dot(h, w_ref[:, _C_A:_C_END])


def _inproj_lat(x, gn, sh, sc, w, tm):
    b, s, d = x.shape
    row = lambda n: pl.BlockSpec((1, tm, n), lambda bi, i: (bi, i, 0))
    vec = pl.BlockSpec((1, 1, d), lambda bi, i: (bi, 0, 0))
    shp = lambda n, dt: jax.ShapeDtypeStruct((b, s, n), dt)
    return pl.pallas_call(
        _inproj_lat_body,
        grid=(b, s // tm),
        in_specs=[row(d), pl.BlockSpec((1, d), lambda bi, i: (0, 0)), vec, vec, _resident(w.shape)],
        out_specs=[row(CONV_W), row(GLA_KD), row(GLA_KD), row(GLA_VD), row(GLA_VD),
                   row(d), row(d), row(LANES)],
        out_shape=[shp(CONV_W, BF16), shp(GLA_KD, BF16), shp(GLA_KD, BF16), shp(GLA_VD, BF16),
                   shp(GLA_VD, BF16), shp(d, BF16), shp(d, BF16), shp(LANES, F32)],
        compiler_params=_cparams(("arbitrary", "arbitrary")),
        name="inproj_lat",
    )(x, gn, sh, sc, w)


def _inproj_ctx(ctx, gn, sh, sc, w, tm):
    b, l, d = ctx.shape
    row = lambda n: pl.BlockSpec((1, tm, n), lambda bi, i: (bi, i, 0))
    vec = pl.BlockSpec((1, d), lambda bi, i: (0, 0))
    shp = lambda n, dt: jax.ShapeDtypeStruct((b, l, n), dt)
    return pl.pallas_call(
        _inproj_ctx_body,
        grid=(b, l // tm),
        in_specs=[row(d), vec, vec, vec, _resident(w.shape)],
        out_specs=[row(GLA_KD), row(GLA_VD), row(LANES)],
        out_shape=[shp(GLA_KD, BF16), shp(GLA_VD, BF16), shp(LANES, F32)],
        compiler_params=_cparams(("arbitrary", "arbitrary")),
        name="inproj_ctx",
    )(ctx, gn, sh, sc, w)


_CONV_SUB = 8
_CONV_SPAN = GRID_W + _CONV_SUB * ((CONV_K - 1) // _CONV_SUB)


def _conv_shift_matrix():
    row = lax.broadcasted_iota(I32, (_CONV_SUB, _CONV_SPAN, GRID_W), 1)
    shift = lax.broadcasted_iota(I32, (_CONV_SUB, _CONV_SPAN, GRID_W), 0)
    col = lax.broadcasted_iota(I32, (_CONV_SUB, _CONV_SPAN, GRID_W), 2)
    m = jnp.where(col == row + shift - CONV_K // 2, 1.0, 0.0)
    return m.reshape(_CONV_SUB * _CONV_SPAN, GRID_W).astype(BF16)


def _conv_body(u_ref, sm_ref, w_ref, bdw_ref, gln_ref, bln_ref, o_ref, sh_ref, y_ref, *,
               rows_per_step):
    for r in range(rows_per_step):
        buf = r % 2
        sh_ref[buf] = _dot(sm_ref[...], u_ref[r * GRID_W:(r + 1) * GRID_W, :])
        for cb in range(CONV_W // LANES):
            ls = slice(cb * LANES, (cb + 1) * LANES)
            acc = None
            for k in range(CONV_K):
                s, a = k % _CONV_SUB, k // _CONV_SUB
                r0 = s * _CONV_SPAN + _CONV_SUB * a
                term = sh_ref[buf, r0:r0 + GRID_W, ls] * w_ref[k:k + 1, ls]
                acc = term if acc is None else acc + term
            y_ref[r * GRID_W:(r + 1) * GRID_W, ls] = acc + bdw_ref[:, ls]
    y = y_ref[...]
    mu = jnp.mean(y, axis=-1, keepdims=True)
    yc = y - mu
    var = jnp.mean(yc * yc, axis=-1, keepdims=True)
    yn = yc * lax.rsqrt(var + EPS) * gln_ref[...] + bln_ref[...]
    o_ref[...] = (yn * _sigmoid(yn)).astype(BF16)


def _conv(u, w_dw, b_dw, g_ln, b_ln, rows_per_step):
    n, cw = u.shape
    tm = rows_per_step * GRID_W
    vec = pl.BlockSpec((1, cw), lambda i: (0, 0))
    wpad = jnp.zeros((32, cw), F32).at[:CONV_K].set(w_dw)
    sm = _conv_shift_matrix()
    return pl.pallas_call(
        functools.partial(_conv_body, rows_per_step=rows_per_step),
        grid=(n // tm,),
        in_specs=[pl.BlockSpec((tm, cw), lambda i: (i, 0)),
                  pl.BlockSpec(sm.shape, lambda i: (0, 0)),
                  pl.BlockSpec((32, cw), lambda i: (0, 0)), vec, vec, vec],
        out_specs=pl.BlockSpec((tm, cw), lambda i: (i, 0)),
        out_shape=jax.ShapeDtypeStruct((n, cw), BF16),
        scratch_shapes=[pltpu.VMEM((2, _CONV_SUB * _CONV_SPAN, cw), F32),
                        pltpu.VMEM((tm, cw), F32)],
        compiler_params=_cparams(("arbitrary",)),
        name="conv_module",
    )(u, sm, wpad, b_dw.reshape(1, cw), g_ln.reshape(1, cw), b_ln.reshape(1, cw))


def _log_sigmoid(z):
    return jnp.minimum(z, 0.0) - jnp.log(1.0 + jnp.exp(-jnp.abs(z)))


def _gla_keep_mask(t, reverse):
    row = lax.broadcasted_iota(I32, (t, t), 0)
    col = lax.broadcasted_iota(I32, (t, t), 1)
    same_chunk = lax.shift_right_logical(row, 6) == lax.shift_right_logical(col, 6)
    return jnp.logical_and(same_chunk, (col >= row) if reverse else (col <= row))


def _gla_decay(a, wh_ref, wl_ref, ba, keep):
    a_hi, a_lo = _split_bf16(a)
    z = _dot(jnp.concatenate([a_hi, a_lo], axis=1), wh_ref[...]) + _dot(a_hi, wl_ref[...]) + ba
    loga = _log_sigmoid(z) * (1.0 / GLA_TAU)
    l_hi, l_lo = _split_bf16(loga)
    tri = jnp.where(keep, 1.0, 0.0).astype(BF16)
    return _dot(tri, l_hi) + _dot(tri, l_lo)


def _gla_step(k, v, bcum, keep, states, *, reverse, q=None):
    t = k.shape[0]
    c = GLA_CHUNK
    nch = t // c
    tots = [bcum[n * c:n * c + 1, :] if reverse else bcum[(n + 1) * c - 1:(n + 1) * c, :]
            for n in range(nch)]
    totb = jnp.concatenate([jnp.broadcast_to(tt, (c, GLA_KD)) for tt in tots], axis=0)
    k32 = k.astype(F32)
    k_end = (k32 * jnp.exp(totb - bcum)).astype(BF16)
    decs = [jnp.exp(tt) for tt in tots]
    if q is not None:
        q_dec = (q.astype(F32) * jnp.exp(bcum) * (GLA_DK ** -0.5)).astype(BF16)
        k_inv = (k32 * jnp.exp(-bcum)).astype(BF16)
    order = range(nch - 1, -1, -1) if reverse else range(nch)
    outs, new_states = [], []
    for h in range(GLA_H):
        ks = slice(h * GLA_DK, (h + 1) * GLA_DK)
        vh = v[:, h * GLA_DV:(h + 1) * GLA_DV]
        st = states[h]
        if q is not None:
            scores = lax.dot_general(q_dec[:, ks], k_inv[:, ks], (((1,), (1,)), ((), ())),
                                     preferred_element_type=F32)
            o_h = _dot(jnp.where(keep, scores, 0.0).astype(BF16), vh)
            inter = [None] * nch
        for n in order:
            rs = slice(n * c, (n + 1) * c)
            if q is not None:
                inter[n] = _dot(q_dec[rs, ks], st.astype(BF16))
            kv = lax.dot_general(k_end[rs, ks], vh[rs, :], (((0,), (0,)), ((), ())),
                                 preferred_element_type=F32)
            dt = jnp.transpose(jnp.broadcast_to(decs[n][:, ks], (GLA_DK, GLA_DK)))
            st = st * jnp.concatenate([dt] * (GLA_DV // GLA_DK), axis=1) + kv
        new_states.append(st)
        if q is not None:
            outs.append(o_h + jnp.concatenate(inter, axis=0))
    return new_states, (jnp.concatenate(outs, axis=1) if q is not None else None)


def _gla_ctx_body(k_ref, v_ref, a_ref, wh_ref, wl_ref, ba_ref, st_ref, *, reverse):
    j = pl.program_id(1)

    @pl.when(j == 0)
    def _():
        st_ref[...] = jnp.zeros_like(st_ref)

    keep = _gla_keep_mask(k_ref.shape[1], reverse)
    bcum = _gla_decay(a_ref[0], wh_ref, wl_ref, ba_ref[...], keep)
    states, _ = _gla_step(k_ref[0], v_ref[0], bcum, keep,
                          [st_ref[0, h] for h in range(GLA_H)], reverse=reverse)
    for h in range(GLA_H):
        st_ref[0, h] = states[h]


def _gla_lat_body(*refs, reverse, final):
    if final:
        (k_ref, v_ref, a_ref, an_ref, q_ref, wh_ref, wl_ref, ba_ref, s0_ref, sg_ref, op_ref,
         gn_ref, o_ref, st_ref, bc_ref) = refs
    else:
        (k_ref, v_ref, a_ref, an_ref, q_ref, wh_ref, wl_ref, ba_ref, s0_ref,
         o_ref, st_ref, bc_ref) = refs
    j = pl.program_id(1)
    nb, t = k_ref.shape[0], k_ref.shape[1]
    keep = _gla_keep_mask(t, reverse)

    @pl.when(j == 0)
    def _():
        st_ref[...] = s0_ref[...]
        for bb in range(nb):
            bc_ref[bb] = _gla_decay(a_ref[bb], wh_ref, wl_ref, ba_ref[...], keep)

    for bb in range(nb):
        states, o = _gla_step(k_ref[bb], v_ref[bb], bc_ref[bb], keep,
                              [st_ref[bb, h] for h in range(GLA_H)], reverse=reverse,
                              q=q_ref[bb])
        bc_ref[bb] = _gla_decay(an_ref[bb], wh_ref, wl_ref, ba_ref[...], keep)
        for h in range(GLA_H):
            st_ref[bb, h] = states[h]
        if not final:
            o_ref[bb] = o.astype(BF16)
            continue
        o = o + op_ref[bb].astype(F32)
        parts = []
        for h in range(GLA_H):
            oh = o[:, h * GLA_DV:(h + 1) * GLA_DV]
            ms = jnp.mean(oh * oh, axis=-1, keepdims=True)
            parts.append(oh * lax.rsqrt(ms + EPS) * gn_ref[...])
        o_ref[bb] = (jnp.concatenate(parts, axis=1) * sg_ref[bb].astype(F32)).astype(BF16)


def _gla_decay_weights(w_alpha_d, reverse):
    a0 = GLA_RANK if reverse else 0
    w = jnp.zeros((LANES, GLA_KD), F32).at[a0:a0 + GLA_RANK].set(w_alpha_d)
    hi = w.astype(BF16)
    lo = (w - hi.astype(F32)).astype(BF16)
    return jnp.concatenate([hi, hi], axis=0), lo


def _gla_ctx(k, v, a, w_alpha_d, ba, reverse, cps):
    b, l, _ = k.shape
    tm = cps * GLA_CHUNK
    nj = l // tm
    jmap = (lambda j: nj - 1 - j) if reverse else (lambda j: j)
    row = lambda n: pl.BlockSpec((1, tm, n), lambda bi, j: (bi, jmap(j), 0))
    wh, wl = _gla_decay_weights(w_alpha_d, reverse)
    return pl.pallas_call(
        functools.partial(_gla_ctx_body, reverse=reverse),
        grid=(b, nj),
        in_specs=[row(GLA_KD), row(GLA_VD), row(LANES),
                  pl.BlockSpec(wh.shape, lambda bi, j: (0, 0)),
                  pl.BlockSpec(wl.shape, lambda bi, j: (0, 0)),
                  pl.BlockSpec((1, GLA_KD), lambda bi, j: (0, 0))],
        out_specs=pl.BlockSpec((1, GLA_H, GLA_DK, GLA_DV), lambda bi, j: (bi, 0, 0, 0)),
        out_shape=jax.ShapeDtypeStruct((b, GLA_H, GLA_DK, GLA_DV), F32),
        compiler_params=_cparams(("arbitrary", "arbitrary")),
        name="gla_ctx_bwd" if reverse else "gla_ctx_fwd",
    )(k, v, a, wh, wl, ba.reshape(1, GLA_KD))


def _gla_lat(k, v, a, q, w_alpha_d, ba, s0, reverse, cps, sg=None, o_prev=None, g_norm=None):
    b, s, _ = k.shape
    final = sg is not None
    tm = cps * GLA_CHUNK
    nj = s // tm
    jmap = (lambda j: nj - 1 - j) if reverse else (lambda j: j)
    nb = 2 if b % 2 == 0 else 1
    row = lambda n: pl.BlockSpec((nb, tm, n), lambda bi, j: (bi, jmap(j), 0))
    wh, wl = _gla_decay_weights(w_alpha_d, reverse)
    a_next = pl.BlockSpec((nb, tm, LANES), lambda bi, j: (bi, jmap(jnp.minimum(j + 1, nj - 1)), 0))
    in_specs = [row(GLA_KD), row(GLA_VD), row(LANES), a_next, row(GLA_KD),
                pl.BlockSpec(wh.shape, lambda bi, j: (0, 0)),
                pl.BlockSpec(wl.shape, lambda bi, j: (0, 0)),
                pl.BlockSpec((1, GLA_KD), lambda bi, j: (0, 0)),
                pl.BlockSpec((nb, GLA_H, GLA_DK, GLA_DV), lambda bi, j: (bi, 0, 0, 0))]
    args = [k, v, a, a, q, wh, wl, ba.reshape(1, GLA_KD), s0]
    if final:
        in_specs += [row(GLA_VD), row(GLA_VD), pl.BlockSpec((1, GLA_DV), lambda bi, j: (0, 0))]
        args += [sg, o_prev, g_norm.reshape(1, GLA_DV)]
    return pl.pallas_call(
        functools.partial(_gla_lat_body, reverse=reverse, final=final),
        grid=(b // nb, nj),
        in_specs=in_specs,
        out_specs=row(GLA_VD),
        out_shape=jax.ShapeDtypeStruct((b, s, GLA_VD), BF16),
        scratch_shapes=[pltpu.VMEM((nb, GLA_H, GLA_DK, GLA_DV), F32),
                        pltpu.VMEM((nb, tm, GLA_KD), F32)],
        compiler_params=_cparams(("arbitrary", "arbitrary")),
        name="gla_lat_bwd" if reverse else "gla_lat_fwd",
    )(*args)


def _pack_pair(lo, hi):
    lo_b = pltpu.bitcast(lo.astype(BF16).astype(F32), I32)
    hi_b = pltpu.bitcast(hi.astype(BF16).astype(F32), I32)
    return lax.shift_right_logical(lo_b, 16) | (hi_b & jnp.int32(-65536))


def _unpack_pair(w):
    lo = pltpu.bitcast(lax.shift_left(w, 16), F32)
    hi = pltpu.bitcast(w & jnp.int32(-65536), F32)
    return lo, hi


def _merge_body(act_ref, og_ref, gc_ref, gg_ref, x_ref, gt_ref, sh_ref, sc_ref, gn_ref,
                wc_ref, wg_ref, wo_ref, wr_ref, br_ref,
                x1_ref, h2_ref, ti_ref, tp_ref, cnt_ref):
    yc = _dot(act_ref[...], wc_ref[...])
    yg = _dot(og_ref[...], wg_ref[...])
    m = gc_ref[...].astype(F32) * yc + gg_ref[...].astype(F32) * yg
    x1 = x_ref[...] + gt_ref[0] * _dot(m.astype(BF16), wo_ref[...])
    x1_ref[...] = x1
    ms = jnp.mean(x1 * x1, axis=-1, keepdims=True)
    h2 = (x1 * lax.rsqrt(ms + EPS) * gn_ref[...]) * (1.0 + sc_ref[0]) + sh_ref[0]
    half = D_MODEL // 2
    h2_ref[...] = _pack_pair(h2[:, :half], h2[:, half:])
    logits = _dot3(h2, wr_ref[...]) + br_ref[...]
    lane = lax.broadcasted_iota(I32, logits.shape, 1).astype(F32)
    neg = jnp.float32(-jnp.inf)
    work = jnp.where(lane < N_EXPERTS, logits, neg)
    vals, idxs = [], []
    for _ in range(TOP_K):
        mx = jnp.max(work, axis=-1, keepdims=True)
        ix = jnp.min(jnp.where(work == mx, lane, float(LANES)), axis=-1, keepdims=True)
        vals.append(mx)
        idxs.append(ix)
        work = jnp.where(lane == ix, neg, work)
    es = [jnp.exp(v - vals[0]) for v in vals]
    den = es[0] + es[1] + es[2] + es[3]
    ti = jnp.zeros(logits.shape, F32)
    tp = jnp.zeros(logits.shape, F32)
    onehot = jnp.zeros(logits.shape, F32)
    for kk in range(TOP_K):
        ti = jnp.where(lane == kk, idxs[kk], ti)
        tp = jnp.where(lane == kk, es[kk] / den, tp)
        onehot = onehot + jnp.where(lane == idxs[kk], 1.0, 0.0)
    @pl.when(pl.program_id(0) == 0)
    def _():
        cnt_ref[...] = jnp.zeros_like(cnt_ref)

    tm = logits.shape[0]
    earlier = (lax.broadcasted_iota(I32, (tm, tm), 1) < lax.broadcasted_iota(I32, (tm, tm), 0))
    before = _dot(jnp.where(earlier, 1.0, 0.0).astype(BF16), onehot.astype(BF16)) + cnt_ref[0:1, :]
    for kk in range(TOP_K):
        rank = jnp.sum(jnp.where(lane == idxs[kk], before, 0.0), axis=-1, keepdims=True)
        ti = jnp.where(lane == TOP_K + kk, rank, ti)
    cnt_ref[...] = cnt_ref[...] + jnp.sum(onehot, axis=0, keepdims=True)
    ti_ref[...] = ti.astype(I32)
    tp_ref[...] = tp


def _merge(act, og, gc, gg, x2d, gt1, sh2, sc2, gn, wc, wg, wo, wr, br, tm, rows_per_batch):
    n, d = x2d.shape
    per_b = rows_per_batch // tm
    row = lambda w: pl.BlockSpec((tm, w), lambda i: (i, 0))
    vec = pl.BlockSpec((1, 1, d), lambda i: (i // per_b, 0, 0))
    const = lambda shape: pl.BlockSpec(shape, lambda i: (0,) * len(shape))
    return pl.pallas_call(
        _merge_body,
        grid=(n // tm,),
        in_specs=[row(d), row(d), row(d), row(d), row(d), vec, vec, vec, const((1, d)),
                  _resident(wc.shape), _resident(wg.shape), _resident(wo.shape),
                  const(wr.shape), const((1, LANES))],
        out_specs=[row(d), row(d // 2), row(LANES), row(LANES), const((8, LANES))],
        out_shape=[jax.ShapeDtypeStruct((n, d), F32), jax.ShapeDtypeStruct((n, d // 2), I32),
                   jax.ShapeDtypeStruct((n, LANES), I32), jax.ShapeDtypeStruct((n, LANES), F32),
                   jax.ShapeDtypeStruct((8, LANES), F32)],
        compiler_params=_cparams(("arbitrary",)),
        name="merge_router",
    )(act, og, gc, gg, x2d, gt1, sh2, sc2, gn, wc, wg, wo, wr, br)


def _dispatch_body(dest_ref, h2_ref, xp_in_ref, xp_ref, sem, *, tokens):
    del xp_in_ref

    def row_copy(t, kk):
        dst = dest_ref[0, 0, t * TOP_K + kk]
        return pltpu.make_async_copy(h2_ref.at[pl.ds(t, 1)], xp_ref.at[pl.ds(dst, 1)], sem)

    def issue(t, carry):
        for kk in range(TOP_K):
            row_copy(t, kk).start(priority=kk % 2)
        return carry

    lax.fori_loop(0, tokens, issue, 0, unroll=8)

    def drain(t, carry):
        for _ in range(TOP_K):
            pltpu.make_async_copy(h2_ref.at[pl.ds(0, 1)], xp_ref.at[pl.ds(0, 1)], sem).wait()
        return carry

    lax.fori_loop(0, tokens, drain, 0, unroll=8)


def _dispatch(dest, h2p, xp_init, tokens):
    n, w = h2p.shape
    steps = n // tokens
    dest3 = dest.reshape(steps, 1, tokens * TOP_K)
    return pl.pallas_call(
        functools.partial(_dispatch_body, tokens=tokens),
        grid=(steps,),
        in_specs=[pl.BlockSpec((1, 1, tokens * TOP_K), lambda i: (i, 0, 0),
                               memory_space=pltpu.SMEM),
                  pl.BlockSpec((tokens, w), lambda i: (i, 0)),
                  pl.BlockSpec(memory_space=pl.ANY)],
        out_specs=pl.BlockSpec(memory_space=pl.ANY),
        out_shape=jax.ShapeDtypeStruct(xp_init.shape, xp_init.dtype),
        scratch_shapes=[pltpu.SemaphoreType.DMA],
        input_output_aliases={2: 0},
        compiler_params=_cparams(("arbitrary",)),
        name="moe_dispatch",
    )(dest3, h2p, xp_init)


_GLU_GROUP = 2 * LANES


def _deinterleave_matrix():
    src = lax.broadcasted_iota(I32, (_GLU_GROUP, _GLU_GROUP), 0)
    dst = lax.broadcasted_iota(I32, (_GLU_GROUP, _GLU_GROUP), 1)
    want = jnp.where(dst < LANES, 2 * dst, 2 * (dst - LANES) + 1)
    return jnp.where(src == want, 1.0, 0.0).astype(BF16)


def _expert_body(be_ref, nreal_ref, xp_ref, w1_ref, b1_ref, w2_ref, b2_ref, yp_ref,
                 w1s_ref, w2s_ref):
    i = pl.program_id(0)
    new_expert = jnp.logical_or(i == 0, be_ref[i] != be_ref[jnp.maximum(i - 1, 0)])

    @pl.when(jnp.logical_and(new_expert, i < nreal_ref[0]))
    def _():
        perm = _deinterleave_matrix()
        for g in range(2 * D_FF // _GLU_GROUP):
            cs = slice(g * _GLU_GROUP, (g + 1) * _GLU_GROUP)
            w1s_ref[:, cs] = _dot(w1_ref[0, :, cs].astype(BF16), perm).astype(BF16)
        w2s_ref[...] = w2_ref[0].astype(BF16)

    @pl.when(i < nreal_ref[0])
    def _():
        lo, hi = _unpack_pair(xp_ref[...])
        x = jnp.concatenate([lo, hi], axis=1).astype(BF16)
        hid = _dot(x, w1s_ref[...]) + b1_ref[0]
        ngrp = 2 * D_FF // _GLU_GROUP
        hg = jnp.concatenate(
            [hid[:, g * _GLU_GROUP:g * _GLU_GROUP + LANES] for g in range(ngrp)], axis=1)
        hl = jnp.concatenate(
            [hid[:, g * _GLU_GROUP + LANES:(g + 1) * _GLU_GROUP] for g in range(ngrp)], axis=1)
        xg = jnp.minimum(hg, SWIGLU_LIMIT)
        xl = jnp.clip(hl, -SWIGLU_LIMIT, SWIGLU_LIMIT)
        act = xg * _sigmoid(SWIGLU_ALPHA * xg) * (xl + 1.0)
        y = _dot(act.astype(BF16), w2s_ref[...]) + b2_ref[0]
        half = D_MODEL // 2
        yp_ref[...] = _pack_pair(y[:, :half], y[:, half:])

    @pl.when(i >= nreal_ref[0])
    def _():
        yp_ref[...] = jnp.zeros_like(yp_ref)


def _experts(blk_expert, n_real, xp, w1, b1, w2, b2, bm):
    p, w = xp.shape
    grid_spec = pltpu.PrefetchScalarGridSpec(
        num_scalar_prefetch=2,
        grid=(p // bm,),
        in_specs=[pl.BlockSpec((bm, w), lambda i, be, nr: (i, 0)),
                  pl.BlockSpec((1, D_MODEL, 2 * D_FF), lambda i, be, nr: (be[i], 0, 0)),
                  pl.BlockSpec((1, 1, 2 * D_FF), lambda i, be, nr: (be[i], 0, 0)),
                  pl.BlockSpec((1, D_FF, D_MODEL), lambda i, be, nr: (be[i], 0, 0)),
                  pl.BlockSpec((1, 1, D_MODEL), lambda i, be, nr: (be[i], 0, 0))],
        out_specs=pl.BlockSpec((bm, w), lambda i, be, nr: (i, 0)),
        scratch_shapes=[pltpu.VMEM((D_MODEL, 2 * D_FF), BF16), pltpu.VMEM((D_FF, D_MODEL), BF16)],
    )
    return pl.pallas_call(
        _expert_body,
        grid_spec=grid_spec,
        out_shape=jax.ShapeDtypeStruct((p, w), I32),
        compiler_params=_cparams(("arbitrary",)),
        name="moe_experts",
    )(blk_expert, n_real, xp, w1, b1, w2, b2)


def _combine_body(dcur_ref, dnext_ref, yp_ref, tp_ref, x1_ref, gt_ref, gf_ref, o_ref,
                  buf_ref, sem, *, tokens):
    i = pl.program_id(0)
    n = pl.num_programs(0)
    slot = i % 2

    def row_copy(dref, s, t, kk):
        src = dref[0, 0, t * TOP_K + kk]
        return pltpu.make_async_copy(yp_ref.at[pl.ds(src, 1)], buf_ref.at[s, kk, pl.ds(t, 1)],
                                     sem.at[s])

    def issue_all(dref, s):
        def issue(t, carry):
            for kk in range(TOP_K):
                row_copy(dref, s, t, kk).start(priority=kk % 2)
            return carry
        lax.fori_loop(0, tokens, issue, 0, unroll=8)

    @pl.when(i == 0)
    def _():
        issue_all(dcur_ref, 0)

    @pl.when(i + 1 < n)
    def _():
        issue_all(dnext_ref, 1 - slot)

    def drain(t, carry):
        for kk in range(TOP_K):
            pltpu.make_async_copy(yp_ref.at[pl.ds(0, 1)], buf_ref.at[slot, kk, pl.ds(0, 1)],
                                  sem.at[slot]).wait()
        return carry

    lax.fori_loop(0, tokens, drain, 0, unroll=8)

    half = D_MODEL // 2
    tp = tp_ref[...]
    y_lo = jnp.zeros((tokens, half), F32)
    y_hi = jnp.zeros((tokens, half), F32)
    for kk in range(TOP_K):
        lo, hi = _unpack_pair(buf_ref[slot, kk])
        pk = tp[:, kk:kk + 1]
        y_lo = y_lo + pk * lo
        y_hi = y_hi + pk * hi
    y = jnp.concatenate([y_lo, y_hi], axis=1)
    x2 = x1_ref[...] + gt_ref[0] * y
    ms = jnp.mean(x2 * x2, axis=-1, keepdims=True)
    o_ref[...] = x2 * lax.rsqrt(ms + EPS) * gf_ref[...]


def _combine(dest, yp, tp, x1, gt2, gf, tokens, rows_per_batch):
    n, d = x1.shape
    steps = n // tokens
    per_b = rows_per_batch // tokens
    dest3 = dest.reshape(steps, 1, tokens * TOP_K)
    dspec = lambda f: pl.BlockSpec((1, 1, tokens * TOP_K), f, memory_space=pltpu.SMEM)
    return pl.pallas_call(
        functools.partial(_combine_body, tokens=tokens),
        grid=(steps,),
        in_specs=[dspec(lambda i: (i, 0, 0)),
                  dspec(lambda i: (jnp.minimum(i + 1, steps - 1), 0, 0)),
                  pl.BlockSpec(memory_space=pl.ANY),
                  pl.BlockSpec((tokens, LANES), lambda i: (i, 0)),
                  pl.BlockSpec((tokens, d), lambda i: (i, 0)),
                  pl.BlockSpec((1, 1, d), lambda i: (i // per_b, 0, 0)),
                  pl.BlockSpec((1, d), lambda i: (0, 0))],
        out_specs=pl.BlockSpec((tokens, d), lambda i: (i, 0)),
        out_shape=jax.ShapeDtypeStruct((n, d), F32),
        scratch_shapes=[pltpu.VMEM((2, TOP_K, tokens, d // 2), I32),
                        pltpu.SemaphoreType.DMA((2,))],
        compiler_params=_cparams(("arbitrary",)),
        name="moe_combine",
    )(dest3, dest3, yp, tp, x1, gt2, gf)


_SC_CORES = 2
_SC_SUBCORES = 16
_SC_CHUNK = 64


def _sc_gather_rows(table, idx):
    rows, width = idx.shape[0], table.shape[1]
    workers = _SC_CORES * _SC_SUBCORES
    per_worker = rows // workers
    assert rows % (workers * _SC_CHUNK) == 0
    mesh = plsc.VectorSubcoreMesh(core_axis_name="c", subcore_axis_name="s")

    def body(table_hbm, idx_hbm, out_hbm, idx_v, rows_v, sem):
        wid = lax.axis_index("s") * _SC_CORES + lax.axis_index("c")
        base = wid * per_worker

        @pl.loop(0, per_worker // _SC_CHUNK)
        def _(i):
            off = base + i * _SC_CHUNK
            pltpu.sync_copy(idx_hbm.at[pl.ds(off, _SC_CHUNK)], idx_v)
            pltpu.async_copy(table_hbm.at[idx_v], rows_v, sem).wait()
            pltpu.sync_copy(rows_v, out_hbm.at[pl.ds(off, _SC_CHUNK)])

    return pl.kernel(
        body,
        out_type=jax.ShapeDtypeStruct((rows, width), table.dtype),
        mesh=mesh,
        scratch_types=[pltpu.VMEM((_SC_CHUNK,), I32), pltpu.VMEM((_SC_CHUNK, width), table.dtype),
                       pltpu.SemaphoreType.DMA],
        name="sc_gather_rows",
    )(table, idx)


def _combine_dense_body(y4_ref, tp_ref, x1_ref, gt_ref, gf_ref, o_ref):
    half = D_MODEL // 2
    tp = tp_ref[...]
    y_lo = y_hi = None
    for kk in range(TOP_K):
        lo, hi = _unpack_pair(y4_ref[kk])
        pk = tp[:, kk:kk + 1]
        y_lo = pk * lo if y_lo is None else y_lo + pk * lo
        y_hi = pk * hi if y_hi is None else y_hi + pk * hi
    x2 = x1_ref[...] + gt_ref[0] * jnp.concatenate([y_lo, y_hi], axis=1)
    ms = jnp.mean(x2 * x2, axis=-1, keepdims=True)
    o_ref[...] = x2 * lax.rsqrt(ms + EPS) * gf_ref[...]


def _combine_dense(y4, tp, x1, gt2, gf, tokens, rows_per_batch):
    n, d = x1.shape
    per_b = rows_per_batch // tokens
    return pl.pallas_call(
        _combine_dense_body,
        grid=(n // tokens,),
        in_specs=[pl.BlockSpec((TOP_K, tokens, d // 2), lambda i: (0, i, 0)),
                  pl.BlockSpec((tokens, LANES), lambda i: (i, 0)),
                  pl.BlockSpec((tokens, d), lambda i: (i, 0)),
                  pl.BlockSpec((1, 1, d), lambda i: (i // per_b, 0, 0)),
                  pl.BlockSpec((1, d), lambda i: (0, 0))],
        out_specs=pl.BlockSpec((tokens, d), lambda i: (i, 0)),
        out_shape=jax.ShapeDtypeStruct((n, d), F32),
        compiler_params=_cparams(("arbitrary",)),
        name="moe_combine_dense",
    )(y4, tp, x1, gt2, gf)


def _routing_tables(top_idx, rank, counts, bm):
    n = top_idx.shape[0]
    nk = n * TOP_K
    padded = (counts + bm - 1) // bm * bm
    pad_end = jnp.cumsum(padded)
    pad_start = pad_end - padded
    dest = (pad_start[top_idx] + rank).astype(I32)
    quantum = _SC_CORES * _SC_SUBCORES * _SC_CHUNK // math.gcd(bm, _SC_CORES * _SC_SUBCORES * _SC_CHUNK)
    n_blocks = (nk + N_EXPERTS * (bm - 1) + bm - 1) // bm
    n_blocks = (n_blocks + quantum - 1) // quantum * quantum
    starts = jnp.arange(n_blocks, dtype=I32) * bm
    blk_expert = jnp.minimum(jnp.sum((pad_end[None, :] <= starts[:, None]).astype(I32), axis=1),
                             N_EXPERTS - 1).astype(I32)
    n_real = (pad_end[-1] // bm).astype(I32).reshape(1)
    src_tok = jnp.zeros((n_blocks * bm,), I32).at[dest.reshape(nk)].set(
        jnp.arange(nk, dtype=I32) // TOP_K, unique_indices=True)
    return dest, src_tok, blk_expert, n_real


def _layer(x, c, ctx, c_ctx, w_ada, b_ada, g_mix_norm, w_in, w_dw, b_dw, g_conv_ln, b_conv_ln,
           w_conv_out, w_alpha, b_alpha, g_gla_norm, w_gla_out, w_out, g_ffn_norm, w_router,
           b_router, w_exp_in, b_exp_in, w_exp_out, b_exp_out, g_final, *, cfg):
    b, s, d = x.shape
    n = b * s

    rows = (b + 1 + 7) // 8 * 8
    cc = jnp.zeros((rows, d), F32).at[:b].set(c).at[b].set(c_ctx)
    mod = _ada(cc, w_ada, b_ada)
    sh1, sc1, gt1, sh2, sc2, gt2 = [mod[:b, i * d:(i + 1) * d].reshape(b, 1, d) for i in range(6)]
    csh1 = mod[b:b + 1, 0:d]
    csc1 = mod[b:b + 1, d:2 * d]

    a0 = 2 * CONV_W + 2 * GLA_KD + 2 * GLA_VD
    w_in_r = jnp.concatenate(
        [w_in[:, :a0], w_in[:, a0 + 2 * GLA_RANK:], w_in[:, a0:a0 + 2 * GLA_RANK],
         jnp.zeros((d, LANES - 2 * GLA_RANK), F32)], axis=1).astype(BF16)
    gmn = g_mix_norm.reshape(1, d)

    u, q, k, v, sg, gc, gg, a = _inproj_lat(x, gmn, sh1, sc1, w_in_r, cfg["tm_in"])
    kc, vc, ac = _inproj_ctx(ctx, gmn, csh1, csc1, w_in_r, cfg["tm_ctx"])

    act = _conv(u.reshape(n, CONV_W), w_dw, b_dw, g_conv_ln, b_conv_ln, cfg["conv_rows"])

    cps = cfg["gla_cps"]
    st_b = _gla_ctx(kc, vc, ac, w_alpha[1], b_alpha[1], True, cps)
    o_b = _gla_lat(k, v, a, q, w_alpha[1], b_alpha[1], st_b, True, cps)
    st_f = _gla_ctx(kc, vc, ac, w_alpha[0], b_alpha[0], False, cps)
    og = _gla_lat(k, v, a, q, w_alpha[0], b_alpha[0], st_f, False, cps,
                  sg=sg, o_prev=o_b, g_norm=g_gla_norm)

    wr = jnp.zeros((d, LANES), F32).at[:, :N_EXPERTS].set(w_router)
    br = jnp.zeros((1, LANES), F32).at[0, :N_EXPERTS].set(b_router)
    x1, h2p, ti, tp, cnt = _merge(
        act, og.reshape(n, d), gc.reshape(n, d), gg.reshape(n, d), x.reshape(n, d),
        gt1, sh2, sc2, g_ffn_norm.reshape(1, d),
        w_conv_out.astype(BF16), w_gla_out.astype(BF16), w_out.astype(BF16), wr, br,
        cfg["tm_merge"], s)

    bm = cfg["moe_block"]
    dest, src_tok, blk_expert, n_real = _routing_tables(
        ti[:, :TOP_K], ti[:, TOP_K:2 * TOP_K], cnt[0, :N_EXPERTS].astype(I32), bm)
    xp = _sc_gather_rows(h2p, src_tok)

    b1 = b_exp_in.reshape(N_EXPERTS, 2 * D_FF // _GLU_GROUP, LANES, 2).transpose(0, 1, 3, 2)
    b1 = b1.reshape(N_EXPERTS, 1, 2 * D_FF)
    yp = _experts(blk_expert, n_real, xp, w_exp_in, b1, w_exp_out,
                  b_exp_out.reshape(N_EXPERTS, 1, d), bm)

    y4 = _sc_gather_rows(yp, dest.T.reshape(n * TOP_K)).reshape(TOP_K, n, d // 2)
    out = _combine_dense(y4, tp, x1, gt2, g_final.reshape(1, d), cfg["tm_combine"], s)
    return out.reshape(b, s, d)


def _config(s, l):
    return dict(tm_in=min(512, s), tm_ctx=min(256, l), conv_rows=4, gla_cps=4,
                tm_merge=min(512, s), moe_block=512, moe_tokens=128, tm_combine=min(256, s))


def kernel(x, c, ctx, c_ctx, w_ada, b_ada, g_mix_norm, w_in, w_dw, b_dw, g_conv_ln, b_conv_ln,
           w_conv_out, w_alpha, b_alpha, g_gla_norm, w_gla_out, w_out, g_ffn_norm, w_router,
           b_router, w_exp_in, b_exp_in, w_exp_out, b_exp_out, g_final):
    depth = w_ada.shape[0]
    assert depth == 1, "single-layer block: the context stream is only consumed by the GLA scan"
    cfg = _config(x.shape[1], ctx.shape[1])
    return _layer(x, c, ctx, c_ctx, w_ada[0], b_ada[0], g_mix_norm[0], w_in[0], w_dw[0], b_dw[0],
                  g_conv_ln[0], b_conv_ln[0], w_conv_out[0], w_alpha[0], b_alpha[0],
                  g_gla_norm[0], w_gla_out[0], w_out[0], g_ffn_norm[0], w_router[0], b_router[0],
                  w_exp_in[0], b_exp_in[0], w_exp_out[0], b_exp_out[0], g_final, cfg=cfg)
```

```python
import functools
import math

import jax
import jax.numpy as jnp
from jax import lax
from jax.experimental import pallas as pl
from jax.experimental.pallas import tpu as pltpu
from jax.experimental.pallas import tpu_sc as plsc

F32 = jnp.float32
BF16 = jnp.bfloat16
I32 = jnp.int32

D_MODEL = 1024
GRID_W = 64
EPS = 1e-6
CONV_W = 1024
CONV_K = 31
GLA_H = 4
GLA_DK = 128
GLA_DV = 256
GLA_KD = GLA_H * GLA_DK
GLA_VD = GLA_H * GLA_DV
GLA_RANK = 16
GLA_TAU = 16.0
GLA_CHUNK = 64
N_EXPERTS = 32
TOP_K = 4
D_FF = 1024
SWIGLU_ALPHA = 1.702
SWIGLU_LIMIT = 7.0

LANES = 128
VMEM_LIMIT = 56 * 1024 * 1024

_C_CONV_A = 0
_C_CONV_B = _C_CONV_A + CONV_W
_C_Q = _C_CONV_B + CONV_W
_C_K = _C_Q + GLA_KD
_C_V = _C_K + GLA_KD
_C_G = _C_V + GLA_VD
_C_GC = _C_G + GLA_VD
_C_GG = _C_GC + D_MODEL
_C_A = _C_GG + D_MODEL
_C_END = _C_A + LANES


def _cparams(sem):
    return pltpu.CompilerParams(dimension_semantics=sem, vmem_limit_bytes=VMEM_LIMIT)


def _dot(a, b):
    return jnp.dot(a, b, preferred_element_type=F32)


def _split_bf16(x):
    hi = x.astype(BF16)
    lo = (x - hi.astype(F32)).astype(BF16)
    return hi, lo


def _dot3(a, b):
    a_hi, a_lo = _split_bf16(a)
    b_hi, b_lo = _split_bf16(b)
    return _dot(a_hi, b_hi) + _dot(a_lo, b_hi) + _dot(a_hi, b_lo)


def _sigmoid(x):
    return 1.0 / (1.0 + jnp.exp(-x))


def _resident(shape):
    nd = len(shape)
    return pl.BlockSpec(shape, lambda *_: (0,) * nd, pipeline_mode=pl.Buffered(1))


def _ada_body(a_ref, w_ref, b_ref, o_ref):
    a = a_ref[...]
    a = a * _sigmoid(a)
    o_ref[...] = _dot3(a, w_ref[...]) + b_ref[...]


def _ada(cc, w, b):
    rows, d = cc.shape
    n = w.shape[1]
    tn = 512
    return pl.pallas_call(
        _ada_body,
        grid=(n // tn,),
        in_specs=[pl.BlockSpec((rows, d), lambda j: (0, 0)),
                  pl.BlockSpec((d, tn), lambda j: (0, j)),
                  pl.BlockSpec((1, tn), lambda j: (0, j))],
        out_specs=pl.BlockSpec((rows, tn), lambda j: (0, j)),
        out_shape=jax.ShapeDtypeStruct((rows, n), F32),
        compiler_params=_cparams(("arbitrary",)),
        name="ada_mod",
    )(cc, w, b.reshape(1, n))


def _norm_mod(xv, gn, sc, sh):
    ms = jnp.mean(xv * xv, axis=-1, keepdims=True)
    y = xv * lax.rsqrt(ms + EPS) * gn
    return (y * (1.0 + sc) + sh).astype(BF16)


def _inproj_lat_body(x_ref, gn_ref, sh_ref, sc_ref, w_ref,
                     u_ref, q_ref, k_ref, v_ref, sg_ref, gc_ref, gg_ref, a_ref):
    h = _norm_mod(x_ref[0], gn_ref[...], sc_ref[0], sh_ref[0])
    ca = _dot(h, w_ref[:, _C_CONV_A:_C_CONV_B])
    cb = _dot(h, w_ref[:, _C_CONV_B:_C_Q])
    u_ref[0] = (ca * _sigmoid(cb)).astype(BF16)
    q_ref[0] = _dot(h, w_ref[:, _C_Q:_C_K]).astype(BF16)
    k_ref[0] = _dot(h, w_ref[:, _C_K:_C_V]).astype(BF16)
    v_ref[0] = _dot(h, w_ref[:, _C_V:_C_G]).astype(BF16)
    g = _dot(h, w_ref[:, _C_G:_C_GC])
    sg_ref[0] = (g * _sigmoid(g)).astype(BF16)
    gc_ref[0] = _sigmoid(_dot(h, w_ref[:, _C_GC:_C_GG])).astype(BF16)
    gg_ref[0] = _sigmoid(_dot(h, w_ref[:, _C_GG:_C_A])).astype(BF16)
    a_ref[0] = _dot(h, w_ref[:, _C_A:_C_END])


def _inproj_ctx_body(x_ref, gn_ref, sh_ref, sc_ref, w_ref, k_ref, v_ref, a_ref):
    h = _norm_mod(x_ref[0], gn_ref[...], sc_ref[...], sh_ref[...])
    k_ref[0] = _dot(h, w_ref[:, _C_K:_C_V]).astype(BF16)
    v_ref[0] = _dot(h, w_ref[:, _C_V:_C_G]).astype(BF16)
    a_ref[0] = _dot(h, w_ref[:, _C_A:_C_END])


def _inproj_lat(x, gn, sh, sc, w, tm):
    b, s, d = x.shape
    row = lambda n: pl.BlockSpec((1, tm, n), lambda bi, i: (bi, i, 0))
    vec = pl.BlockSpec((1, 1, d), lambda bi, i: (bi, 0, 0))
    shp = lambda n, dt: jax.ShapeDtypeStruct((b, s, n), dt)
    return pl.pallas_call(
        _inproj_lat_body,
        grid=(b, s // tm),
        in_specs=[row(d), pl.BlockSpec((1, d), lambda bi, i: (0, 0)), vec, vec, _resident(w.shape)],
        out_specs=[row(CONV_W), row(GLA_KD), row(GLA_KD), row(GLA_VD), row(GLA_VD),
                   row(d), row(d), row(LANES)],
        out_shape=[shp(CONV_W, BF16), shp(GLA_KD, BF16), shp(GLA_KD, BF16), shp(GLA_VD, BF16),
                   shp(GLA_VD, BF16), shp(d, BF16), shp(d, BF16), shp(LANES, F32)],
        compiler_params=_cparams(("arbitrary", "arbitrary")),
        name="inproj_lat",
    )(x, gn, sh, sc, w)


def _inproj_ctx(ctx, gn, sh, sc, w, tm):
    b, l, d = ctx.shape
    row = lambda n: pl.BlockSpec((1, tm, n), lambda bi, i: (bi, i, 0))
    vec = pl.BlockSpec((1, d), lambda bi, i: (0, 0))
    shp = lambda n, dt: jax.ShapeDtypeStruct((b, l, n), dt)
    return pl.pallas_call(
        _inproj_ctx_body,
        grid=(b, l // tm),
        in_specs=[row(d), vec, vec, vec, _resident(w.shape)],
        out_specs=[row(GLA_KD), row(GLA_VD), row(LANES)],
        out_shape=[shp(GLA_KD, BF16), shp(GLA_VD, BF16), shp(LANES, F32)],
        compiler_params=_cparams(("arbitrary", "arbitrary")),
        name="inproj_ctx",
    )(ctx, gn, sh, sc, w)


_CONV_SUB = 8
_CONV_SPAN = GRID_W + _CONV_SUB * ((CONV_K - 1) // _CONV_SUB)


def _conv_shift_matrix():
    row = lax.broadcasted_iota(I32, (_CONV_SUB, _CONV_SPAN, GRID_W), 1)
    shift = lax.broadcasted_iota(I32, (_CONV_SUB, _CONV_SPAN, GRID_W), 0)
    col = lax.broadcasted_iota(I32, (_CONV_SUB, _CONV_SPAN, GRID_W), 2)
    m = jnp.where(col == row + shift - CONV_K // 2, 1.0, 0.0)
    return m.reshape(_CONV_SUB * _CONV_SPAN, GRID_W).astype(BF16)


def _conv_body(u_ref, sm_ref, w_ref, bdw_ref, gln_ref, bln_ref, o_ref, sh_ref, y_ref, *,
               rows_per_step):
    for r in range(rows_per_step):
        buf = r % 2
        sh_ref[buf] = _dot(sm_ref[...], u_ref[r * GRID_W:(r + 1) * GRID_W, :])
        for cb in range(CONV_W // LANES):
            ls = slice(cb * LANES, (cb + 1) * LANES)
            acc = None
            for k in range(CONV_K):
                s, a = k % _CONV_SUB, k // _CONV_SUB
                r0 = s * _CONV_SPAN + _CONV_SUB * a
                term = sh_ref[buf, r0:r0 + GRID_W, ls] * w_ref[k:k + 1, ls]
                acc = term if acc is None else acc + term
            y_ref[r * GRID_W:(r + 1) * GRID_W, ls] = acc + bdw_ref[:, ls]
    y = y_ref[...]
    mu = jnp.mean(y, axis=-1, keepdims=True)
    yc = y - mu
    var = jnp.mean(yc * yc, axis=-1, keepdims=True)
    yn = yc * lax.rsqrt(var + EPS) * gln_ref[...] + bln_ref[...]
    o_ref[...] = (yn * _sigmoid(yn)).astype(BF16)


def _conv(u, w_dw, b_dw, g_ln, b_ln, rows_per_step):
    n, cw = u.shape
    tm = rows_per_step * GRID_W
    vec = pl.BlockSpec((1, cw), lambda i: (0, 0))
    wpad = jnp.zeros((32, cw), F32).at[:CONV_K].set(w_dw)
    sm = _conv_shift_matrix()
    return pl.pallas_call(
        functools.partial(_conv_body, rows_per_step=rows_per_step),
        grid=(n // tm,),
        in_specs=[pl.BlockSpec((tm, cw), lambda i: (i, 0)),
                  pl.BlockSpec(sm.shape, lambda i: (0, 0)),
                  pl.BlockSpec((32, cw), lambda i: (0, 0)), vec, vec, vec],
        out_specs=pl.BlockSpec((tm, cw), lambda i: (i, 0)),
        out_shape=jax.ShapeDtypeStruct((n, cw), BF16),
        scratch_shapes=[pltpu.VMEM((2, _CONV_SUB * _CONV_SPAN, cw), F32),
                        pltpu.VMEM((tm, cw), F32)],
        compiler_params=_cparams(("arbitrary",)),
        name="conv_module",
    )(u, sm, wpad, b_dw.reshape(1, cw), g_ln.reshape(1, cw), b_ln.reshape(1, cw))


def _log_sigmoid(z):
    return jnp.minimum(z, 0.0) - jnp.log(1.0 + jnp.exp(-jnp.abs(z)))


def _gla_keep_mask(t, reverse):
    row = lax.broadcasted_iota(I32, (t, t), 0)
    col = lax.broadcasted_iota(I32, (t, t), 1)
    same_chunk = lax.shift_right_logical(row, 6) == lax.shift_right_logical(col, 6)
    return jnp.logical_and(same_chunk, (col >= row) if reverse else (col <= row))


def _gla_decay(a, wh_ref, wl_ref, ba, keep):
    a_hi, a_lo = _split_bf16(a)
    z = _dot(jnp.concatenate([a_hi, a_lo], axis=1), wh_ref[...]) + _dot(a_hi, wl_ref[...]) + ba
    loga = _log_sigmoid(z) * (1.0 / GLA_TAU)
    l_hi, l_lo = _split_bf16(loga)
    tri = jnp.where(keep, 1.0, 0.0).astype(BF16)
    return _dot(tri, l_hi) + _dot(tri, l_lo)


def _gla_step(k, v, bcum, keep, states, *, reverse, q=None):
    t = k.shape[0]
    c = GLA_CHUNK
    nch = t // c
    tots = [bcum[n * c:n * c + 1, :] if reverse else bcum[(n + 1) * c - 1:(n + 1) * c, :]
            for n in range(nch)]
    totb = jnp.concatenate([jnp.broadcast_to(tt, (c, GLA_KD)) for tt in tots], axis=0)
    k32 = k.astype(F32)
    k_end = (k32 * jnp.exp(totb - bcum)).astype(BF16)
    decs = [jnp.exp(tt) for tt in tots]
    if q is not None:
        q_dec = (q.astype(F32) * jnp.exp(bcum) * (GLA_DK ** -0.5)).astype(BF16)
        k_inv = (k32 * jnp.exp(-bcum)).astype(BF16)
    order = range(nch - 1, -1, -1) if reverse else range(nch)
    outs, new_states = [], []
    for h in range(GLA_H):
        ks = slice(h * GLA_DK, (h + 1) * GLA_DK)
        vh = v[:, h * GLA_DV:(h + 1) * GLA_DV]
        st = states[h]
        if q is not None:
            scores = lax.dot_general(q_dec[:, ks], k_inv[:, ks], (((1,), (1,)), ((), ())),
                                     preferred_element_type=F32)
            o_h = _dot(jnp.where(keep, scores, 0.0).astype(BF16), vh)
            inter = [None] * nch
        for n in order:
            rs = slice(n * c, (n + 1) * c)
            if q is not None:
                inter[n] = _dot(q_dec[rs, ks], st.astype(BF16))
            kv = lax.dot_general(k_end[rs, ks], vh[rs, :], (((0,), (0,)), ((), ())),
                                 preferred_element_type=F32)
            dt = jnp.transpose(jnp.broadcast_to(decs[n][:, ks], (GLA_DK, GLA_DK)))
            st = st * jnp.concatenate([dt] * (GLA_DV // GLA_DK), axis=1) + kv
        new_states.append(st)
        if q is not None:
            outs.append(o_h + jnp.concatenate(inter, axis=0))
    return new_states, (jnp.concatenate(outs, axis=1) if q is not None else None)


def _gla_ctx_body(k_ref, v_ref, a_ref, wh_ref, wl_ref, ba_ref, st_ref, *, reverse):
    j = pl.program_id(1)

    @pl.when(j == 0)
    def _():
        st_ref[...] = jnp.zeros_like(st_ref)

    keep = _gla_keep_mask(k_ref.shape[1], reverse)
    bcum = _gla_decay(a_ref[0], wh_ref, wl_ref, ba_ref[...], keep)
    states, _ = _gla_step(k_ref[0], v_ref[0], bcum, keep,
                          [st_ref[0, h] for h in range(GLA_H)], reverse=reverse)
    for h in range(GLA_H):
        st_ref[0, h] = states[h]


def _gla_lat_body(*refs, reverse, final):
    if final:
        (k_ref, v_ref, a_ref, an_ref, q_ref, wh_ref, wl_ref, ba_ref, s0_ref, sg_ref, op_ref,
         gn_ref, o_ref, st_ref, bc_ref) = refs
    else:
        (k_ref, v_ref, a_ref, an_ref, q_ref, wh_ref, wl_ref, ba_ref, s0_ref,
         o_ref, st_ref, bc_ref) = refs
    j = pl.program_id(1)
    nb, t = k_ref.shape[0], k_ref.shape[1]
    keep = _gla_keep_mask(t, reverse)

    @pl.when(j == 0)
    def _():
        st_ref[...] = s0_ref[...]
        for bb in range(nb):
            bc_ref[bb] = _gla_decay(a_ref[bb], wh_ref, wl_ref, ba_ref[...], keep)

    for bb in range(nb):
        states, o = _gla_step(k_ref[bb], v_ref[bb], bc_ref[bb], keep,
                              [st_ref[bb, h] for h in range(GLA_H)], reverse=reverse,
                              q=q_ref[bb])
        bc_ref[bb] = _gla_decay(an_ref[bb], wh_ref, wl_ref, ba_ref[...], keep)
        for h in range(GLA_H):
            st_ref[bb, h] = states[h]
        if not final:
            o_ref[bb] = o.astype(BF16)
            continue
        o = o + op_ref[bb].astype(F32)
        parts = []
        for h in range(GLA_H):
            oh = o[:, h * GLA_DV:(h + 1) * GLA_DV]
            ms = jnp.mean(oh * oh, axis=-1, keepdims=True)
            parts.append(oh * lax.rsqrt(ms + EPS) * gn_ref[...])
        o_ref[bb] = (jnp.concatenate(parts, axis=1) * sg_ref[bb].astype(F32)).astype(BF16)


def _gla_decay_weights(w_alpha_d, reverse):
    a0 = GLA_RANK if reverse else 0
    w = jnp.zeros((LANES, GLA_KD), F32).at[a0:a0 + GLA_RANK].set(w_alpha_d)
    hi = w.astype(BF16)
    lo = (w - hi.astype(F32)).astype(BF16)
    return jnp.concatenate([hi, hi], axis=0), lo


def _gla_ctx(k, v, a, w_alpha_d, ba, reverse, cps):
    b, l, _ = k.shape
    tm = cps * GLA_CHUNK
    nj = l // tm
    jmap = (lambda j: nj - 1 - j) if reverse else (lambda j: j)
    row = lambda n: pl.BlockSpec((1, tm, n), lambda bi, j: (bi, jmap(j), 0))
    wh, wl = _gla_decay_weights(w_alpha_d, reverse)
    return pl.pallas_call(
        functools.partial(_gla_ctx_body, reverse=reverse),
        grid=(b, nj),
        in_specs=[row(GLA_KD), row(GLA_VD), row(LANES),
                  pl.BlockSpec(wh.shape, lambda bi, j: (0, 0)),
                  pl.BlockSpec(wl.shape, lambda bi, j: (0, 0)),
                  pl.BlockSpec((1, GLA_KD), lambda bi, j: (0, 0))],
        out_specs=pl.BlockSpec((1, GLA_H, GLA_DK, GLA_DV), lambda bi, j: (bi, 0, 0, 0)),
        out_shape=jax.ShapeDtypeStruct((b, GLA_H, GLA_DK, GLA_DV), F32),
        compiler_params=_cparams(("arbitrary", "arbitrary")),
        name="gla_ctx_bwd" if reverse else "gla_ctx_fwd",
    )(k, v, a, wh, wl, ba.reshape(1, GLA_KD))


def _gla_lat(k, v, a, q, w_alpha_d, ba, s0, reverse, cps, sg=None, o_prev=None, g_norm=None):
    b, s, _ = k.shape
    final = sg is not None
    tm = cps * GLA_CHUNK
    nj = s // tm
    jmap = (lambda j: nj - 1 - j) if reverse else (lambda j: j)
    nb = 2 if b % 2 == 0 else 1
    row = lambda n: pl.BlockSpec((nb, tm, n), lambda bi, j: (bi, jmap(j), 0))
    wh, wl = _gla_decay_weights(w_alpha_d, reverse)
    a_next = pl.BlockSpec((nb, tm, LANES), lambda bi, j: (bi, jmap(jnp.minimum(j + 1, nj - 1)), 0))
    in_specs = [row(GLA_KD), row(GLA_VD), row(LANES), a_next, row(GLA_KD),
                pl.BlockSpec(wh.shape, lambda bi, j: (0, 0)),
                pl.BlockSpec(wl.shape, lambda bi, j: (0, 0)),
                pl.BlockSpec((1, GLA_KD), lambda bi, j: (0, 0)),
                pl.BlockSpec((nb, GLA_H, GLA_DK, GLA_DV), lambda bi, j: (bi, 0, 0, 0))]
    args = [k, v, a, a, q, wh, wl, ba.reshape(1, GLA_KD), s0]
    if final:
        in_specs += [row(GLA_VD), row(GLA_VD), pl.BlockSpec((1, GLA_DV), lambda bi, j: (0, 0))]
        args += [sg, o_prev, g_norm.reshape(1, GLA_DV)]
    return pl.pallas_call(
        functools.partial(_gla_lat_body, reverse=reverse, final=final),
        grid=(b // nb, nj),
        in_specs=in_specs,
        out_specs=row(GLA_VD),
        out_shape=jax.ShapeDtypeStruct((b, s, GLA_VD), BF16),
        scratch_shapes=[pltpu.VMEM((nb, GLA_H, GLA_DK, GLA_DV), F32),
                        pltpu.VMEM((nb, tm, GLA_KD), F32)],
        compiler_params=_cparams(("arbitrary", "arbitrary")),
        name="gla_lat_bwd" if reverse else "gla_lat_fwd",
    )(*args)


def _pack_pair(lo, hi):
    lo_b = pltpu.bitcast(lo.astype(BF16).astype(F32), I32)
    hi_b = pltpu.bitcast(hi.astype(BF16).astype(F32), I32)
    return lax.shift_right_logical(lo_b, 16) | (hi_b & jnp.int32(-65536))


def _unpack_pair(w):
    lo = pltpu.bitcast(lax.shift_left(w, 16), F32)
    hi = pltpu.bitcast(w & jnp.int32(-65536), F32)
    return lo, hi


def _merge_body(act_ref, og_ref, gc_ref, gg_ref, x_ref, gt_ref, sh_ref, sc_ref, gn_ref,
                wc_ref, wg_ref, wo_ref, wr_ref, br_ref,
                x1_ref, h2_ref, ti_ref, tp_ref, cnt_ref):
    yc = _dot(act_ref[...], wc_ref[...])
    yg = _dot(og_ref[...], wg_ref[...])
    m = gc_ref[...].astype(F32) * yc + gg_ref[...].astype(F32) * yg
    x1 = x_ref[...] + gt_ref[0] * _dot(m.astype(BF16), wo_ref[...])
    x1_ref[...] = x1
    ms = jnp.mean(x1 * x1, axis=-1, keepdims=True)
    h2 = (x1 * lax.rsqrt(ms + EPS) * gn_ref[...]) * (1.0 + sc_ref[0]) + sh_ref[0]
    half = D_MODEL // 2
    h2_ref[...] = _pack_pair(h2[:, :half], h2[:, half:])
    logits = _dot3(h2, wr_ref[...]) + br_ref[...]
    lane = lax.broadcasted_iota(I32, logits.shape, 1).astype(F32)
    neg = jnp.float32(-jnp.inf)
    work = jnp.where(lane < N_EXPERTS, logits, neg)
    vals, idxs = [], []
    for _ in range(TOP_K):
        mx = jnp.max(work, axis=-1, keepdims=True)
        ix = jnp.min(jnp.where(work == mx, lane, float(LANES)), axis=-1, keepdims=True)
        vals.append(mx)
        idxs.append(ix)
        work = jnp.where(lane == ix, neg, work)
    es = [jnp.exp(v - vals[0]) for v in vals]
    den = es[0] + es[1] + es[2] + es[3]
    ti = jnp.zeros(logits.shape, F32)
    tp = jnp.zeros(logits.shape, F32)
    onehot = jnp.zeros(logits.shape, F32)
    for kk in range(TOP_K):
        ti = jnp.where(lane == kk, idxs[kk], ti)
        tp = jnp.where(lane == kk, es[kk] / den, tp)
        onehot = onehot + jnp.where(lane == idxs[kk], 1.0, 0.0)
    @pl.when(pl.program_id(0) == 0)
    def _():
        cnt_ref[...] = jnp.zeros_like(cnt_ref)

    tm = logits.shape[0]
    earlier = (lax.broadcasted_iota(I32, (tm, tm), 1) < lax.broadcasted_iota(I32, (tm, tm), 0))
    before = _dot(jnp.where(earlier, 1.0, 0.0).astype(BF16), onehot.astype(BF16)) + cnt_ref[0:1, :]
    for kk in range(TOP_K):
        rank = jnp.sum(jnp.where(lane == idxs[kk], before, 0.0), axis=-1, keepdims=True)
        ti = jnp.where(lane == TOP_K + kk, rank, ti)
    cnt_ref[...] = cnt_ref[...] + jnp.sum(onehot, axis=0, keepdims=True)
    ti_ref[...] = ti.astype(I32)
    tp_ref[...] = tp


def _merge(act, og, gc, gg, x2d, gt1, sh2, sc2, gn, wc, wg, wo, wr, br, tm, rows_per_batch):
    n, d = x2d.shape
    per_b = rows_per_batch // tm
    row = lambda w: pl.BlockSpec((tm, w), lambda i: (i, 0))
    vec = pl.BlockSpec((1, 1, d), lambda i: (i // per_b, 0, 0))
    const = lambda shape: pl.BlockSpec(shape, lambda i: (0,) * len(shape))
    return pl.pallas_call(
        _merge_body,
        grid=(n // tm,),
        in_specs=[row(d), row(d), row(d), row(d), row(d), vec, vec, vec, const((1, d)),
                  _resident(wc.shape), _resident(wg.shape), _resident(wo.shape),
                  const(wr.shape), const((1, LANES))],
        out_specs=[row(d), row(d // 2), row(LANES), row(LANES), const((8, LANES))],
        out_shape=[jax.ShapeDtypeStruct((n, d), F32), jax.ShapeDtypeStruct((n, d // 2), I32),
                   jax.ShapeDtypeStruct((n, LANES), I32), jax.ShapeDtypeStruct((n, LANES), F32),
                   jax.ShapeDtypeStruct((8, LANES), F32)],
        compiler_params=_cparams(("arbitrary",)),
        name="merge_router",
    )(act, og, gc, gg, x2d, gt1, sh2, sc2, gn, wc, wg, wo, wr, br)


def _dispatch_body(dest_ref, h2_ref, xp_in_ref, xp_ref, sem, *, tokens):
    del xp_in_ref

    def row_copy(t, kk):
        dst = dest_ref[0, 0, t * TOP_K + kk]
        return pltpu.make_async_copy(h2_ref.at[pl.ds(t, 1)], xp_ref.at[pl.ds(dst, 1)], sem)

    def issue(t, carry):
        for kk in range(TOP_K):
            row_copy(t, kk).start(priority=kk % 2)
        return carry

    lax.fori_loop(0, tokens, issue, 0, unroll=8)

    def drain(t, carry):
        for _ in range(TOP_K):
            pltpu.make_async_copy(h2_ref.at[pl.ds(0, 1)], xp_ref.at[pl.ds(0, 1)], sem).wait()
        return carry

    lax.fori_loop(0, tokens, drain, 0, unroll=8)


def _dispatch(dest, h2p, xp_init, tokens):
    n, w = h2p.shape
    steps = n // tokens
    dest3 = dest.reshape(steps, 1, tokens * TOP_K)
    return pl.pallas_call(
        functools.partial(_dispatch_body, tokens=tokens),
        grid=(steps,),
        in_specs=[pl.BlockSpec((1, 1, tokens * TOP_K), lambda i: (i, 0, 0),
                               memory_space=pltpu.SMEM),
                  pl.BlockSpec((tokens, w), lambda i: (i, 0)),
                  pl.BlockSpec(memory_space=pl.ANY)],
        out_specs=pl.BlockSpec(memory_space=pl.ANY),
        out_shape=jax.ShapeDtypeStruct(xp_init.shape, xp_init.dtype),
        scratch_shapes=[pltpu.SemaphoreType.DMA],
        input_output_aliases={2: 0},
        compiler_params=_cparams(("arbitrary",)),
        name="moe_dispatch",
    )(dest3, h2p, xp_init)


_GLU_GROUP = 2 * LANES


def _deinterleave_matrix():
    src = lax.broadcasted_iota(I32, (_GLU_GROUP, _GLU_GROUP), 0)
    dst = lax.broadcasted_iota(I32, (_GLU_GROUP, _GLU_GROUP), 1)
    want = jnp.where(dst < LANES, 2 * dst, 2 * (dst - LANES) + 1)
    return jnp.where(src == want, 1.0, 0.0).astype(BF16)


def _expert_body(be_ref, nreal_ref, xp_ref, w1_ref, b1_ref, w2_ref, b2_ref, yp_ref,
                 w1s_ref, w2s_ref):
    i = pl.program_id(0)
    new_expert = jnp.logical_or(i == 0, be_ref[i] != be_ref[jnp.maximum(i - 1, 0)])

    @pl.when(jnp.logical_and(new_expert, i < nreal_ref[0]))
    def _():
        perm = _deinterleave_matrix()
        for g in range(2 * D_FF // _GLU_GROUP):
            cs = slice(g * _GLU_GROUP, (g + 1) * _GLU_GROUP)
            w1s_ref[:, cs] = _dot(w1_ref[0, :, cs].astype(BF16), perm).astype(BF16)
        w2s_ref[...] = w2_ref[0].astype(BF16)

    @pl.when(i < nreal_ref[0])
    def _():
        lo, hi = _unpack_pair(xp_ref[...])
        x = jnp.concatenate([lo, hi], axis=1).astype(BF16)
        hid = _dot(x, w1s_ref[...]) + b1_ref[0]
        ngrp = 2 * D_FF // _GLU_GROUP
        hg = jnp.concatenate(
            [hid[:, g * _GLU_GROUP:g * _GLU_GROUP + LANES] for g in range(ngrp)], axis=1)
        hl = jnp.concatenate(
            [hid[:, g * _GLU_GROUP + LANES:(g + 1) * _GLU_GROUP] for g in range(ngrp)], axis=1)
        xg = jnp.minimum(hg, SWIGLU_LIMIT)
        xl = jnp.clip(hl, -SWIGLU_LIMIT, SWIGLU_LIMIT)
        act = xg * _sigmoid(SWIGLU_ALPHA * xg) * (xl + 1.0)
        y = _dot(act.astype(BF16), w2s_ref[...]) + b2_ref[0]
        half = D_MODEL // 2
        yp_ref[...] = _pack_pair(y[:, :half], y[:, half:])

    @pl.when(i >= nreal_ref[0])
    def _():
        yp_ref[...] = jnp.zeros_like(yp_ref)


def _experts(blk_expert, n_real, xp, w1, b1, w2, b2, bm):
    p, w = xp.shape
    grid_spec = pltpu.PrefetchScalarGridSpec(
        num_scalar_prefetch=2,
        grid=(p // bm,),
        in_specs=[pl.BlockSpec((bm, w), lambda i, be, nr: (i, 0)),
                  pl.BlockSpec((1, D_MODEL, 2 * D_FF), lambda i, be, nr: (be[i], 0, 0)),
                  pl.BlockSpec((1, 1, 2 * D_FF), lambda i, be, nr: (be[i], 0, 0)),
                  pl.BlockSpec((1, D_FF, D_MODEL), lambda i, be, nr: (be[i], 0, 0)),
                  pl.BlockSpec((1, 1, D_MODEL), lambda i, be, nr: (be[i], 0, 0))],
        out_specs=pl.BlockSpec((bm, w), lambda i, be, nr: (i, 0)),
        scratch_shapes=[pltpu.VMEM((D_MODEL, 2 * D_FF), BF16), pltpu.VMEM((D_FF, D_MODEL), BF16)],
    )
    return pl.pallas_call(
        _expert_body,
        grid_spec=grid_spec,
        out_shape=jax.ShapeDtypeStruct((p, w), I32),
        compiler_params=_cparams(("arbitrary",)),
        name="moe_experts",
    )(blk_expert, n_real, xp, w1, b1, w2, b2)


def _combine_body(dcur_ref, dnext_ref, yp_ref, tp_ref, x1_ref, gt_ref, gf_ref, o_ref,
                  buf_ref, sem, *, tokens):
    i = pl.program_id(0)
    n = pl.num_programs(0)
    slot = i % 2

    def row_copy(dref, s, t, kk):
        src = dref[0, 0, t * TOP_K + kk]
        return pltpu.make_async_copy(yp_ref.at[pl.ds(src, 1)], buf_ref.at[s, kk, pl.ds(t, 1)],
                                     sem.at[s])

    def issue_all(dref, s):
        def issue(t, carry):
            for kk in range(TOP_K):
                row_copy(dref, s, t, kk).start(priority=kk % 2)
            return carry
        lax.fori_loop(0, tokens, issue, 0, unroll=8)

    @pl.when(i == 0)
    def _():
        issue_all(dcur_ref, 0)

    @pl.when(i + 1 < n)
    def _():
        issue_all(dnext_ref, 1 - slot)

    def drain(t, carry):
        for kk in range(TOP_K):
            pltpu.make_async_copy(yp_ref.at[pl.ds(0, 1)], buf_ref.at[slot, kk, pl.ds(0, 1)],
                                  sem.at[slot]).wait()
        return carry

    lax.fori_loop(0, tokens, drain, 0, unroll=8)

    half = D_MODEL // 2
    tp = tp_ref[...]
    y_lo = jnp.zeros((tokens, half), F32)
    y_hi = jnp.zeros((tokens, half), F32)
    for kk in range(TOP_K):
        lo, hi = _unpack_pair(buf_ref[slot, kk])
        pk = tp[:, kk:kk + 1]
        y_lo = y_lo + pk * lo
        y_hi = y_hi + pk * hi
    y = jnp.concatenate([y_lo, y_hi], axis=1)
    x2 = x1_ref[...] + gt_ref[0] * y
    ms = jnp.mean(x2 * x2, axis=-1, keepdims=True)
    o_ref[...] = x2 * lax.rsqrt(ms + EPS) * gf_ref[...]


def _combine(dest, yp, tp, x1, gt2, gf, tokens, rows_per_batch):
    n, d = x1.shape
    steps = n // tokens
    per_b = rows_per_batch // tokens
    dest3 = dest.reshape(steps, 1, tokens * TOP_K)
    dspec = lambda f: pl.BlockSpec((1, 1, tokens * TOP_K), f, memory_space=pltpu.SMEM)
    return pl.pallas_call(
        functools.partial(_combine_body, tokens=tokens),
        grid=(steps,),
        in_specs=[dspec(lambda i: (i, 0, 0)),
                  dspec(lambda i: (jnp.minimum(i + 1, steps - 1), 0, 0)),
                  pl.BlockSpec(memory_space=pl.ANY),
                  pl.BlockSpec((tokens, LANES), lambda i: (i, 0)),
                  pl.BlockSpec((tokens, d), lambda i: (i, 0)),
                  pl.BlockSpec((1, 1, d), lambda i: (i // per_b, 0, 0)),
                  pl.BlockSpec((1, d), lambda i: (0, 0))],
        out_specs=pl.BlockSpec((tokens, d), lambda i: (i, 0)),
        out_shape=jax.ShapeDtypeStruct((n, d), F32),
        scratch_shapes=[pltpu.VMEM((2, TOP_K, tokens, d // 2), I32),
                        pltpu.SemaphoreType.DMA((2,))],
        compiler_params=_cparams(("arbitrary",)),
        name="moe_combine",
    )(dest3, dest3, yp, tp, x1, gt2, gf)


_SC_CORES = 2
_SC_SUBCORES = 16
_SC_CHUNK = 64


def _sc_gather_rows(table, idx):
    rows, width = idx.shape[0], table.shape[1]
    workers = _SC_CORES * _SC_SUBCORES
    per_worker = rows // workers
    assert rows % (workers * _SC_CHUNK) == 0
    mesh = plsc.VectorSubcoreMesh(core_axis_name="c", subcore_axis_name="s")

    def body(table_hbm, idx_hbm, out_hbm, idx_v, rows_v, sem):
        wid = lax.axis_index("s") * _SC_CORES + lax.axis_index("c")
        base = wid * per_worker

        @pl.loop(0, per_worker // _SC_CHUNK)
        def _(i):
            off = base + i * _SC_CHUNK
            pltpu.sync_copy(idx_hbm.at[pl.ds(off, _SC_CHUNK)], idx_v)
            pltpu.async_copy(table_hbm.at[idx_v], rows_v, sem).wait()
            pltpu.sync_copy(rows_v, out_hbm.at[pl.ds(off, _SC_CHUNK)])

    return pl.kernel(
        body,
        out_type=jax.ShapeDtypeStruct((rows, width), table.dtype),
        mesh=mesh,
        scratch_types=[pltpu.VMEM((_SC_CHUNK,), I32), pltpu.VMEM((_SC_CHUNK, width), table.dtype),
                       pltpu.SemaphoreType.DMA],
        name="sc_gather_rows",
    )(table, idx)


def _sc_scatter_rows(rows, idx_slots, out_rows):
    n, width = rows.shape
    workers = _SC_CORES * _SC_SUBCORES
    per_worker = n // workers
    assert n % (workers * _SC_CHUNK) == 0
    mesh = plsc.VectorSubcoreMesh(core_axis_name="c", subcore_axis_name="s")

    def body(rows_hbm, idx_hbm, out_hbm, idx_v, rows_v):
        wid = lax.axis_index("s") * _SC_CORES + lax.axis_index("c")
        base = wid * per_worker

        @pl.loop(0, per_worker // _SC_CHUNK)
        def _(i):
            t0 = base + i * _SC_CHUNK
            pltpu.sync_copy(rows_hbm.at[pl.ds(t0, _SC_CHUNK)], rows_v)
            for kk in range(TOP_K):
                pltpu.sync_copy(idx_hbm.at[pl.ds(kk * n + t0, _SC_CHUNK)], idx_v)
                pltpu.sync_copy(rows_v, out_hbm.at[idx_v])

    return pl.kernel(
        body,
        out_type=jax.ShapeDtypeStruct((out_rows, width), rows.dtype),
        mesh=mesh,
        scratch_types=[pltpu.VMEM((_SC_CHUNK,), I32), pltpu.VMEM((_SC_CHUNK, width), rows.dtype)],
        name="sc_scatter_rows",
    )(rows, idx_slots)


def _combine_dense_body(y4_ref, tp_ref, x1_ref, gt_ref, gf_ref, o_ref):
    half = D_MODEL // 2
    tp = tp_ref[...]
    y_lo = y_hi = None
    for kk in range(TOP_K):
        lo, hi = _unpack_pair(y4_ref[kk])
        pk = tp[:, kk:kk + 1]
        y_lo = pk * lo if y_lo is None else y_lo + pk * lo
        y_hi = pk * hi if y_hi is None else y_hi + pk * hi
    x2 = x1_ref[...] + gt_ref[0] * jnp.concatenate([y_lo, y_hi], axis=1)
    ms = jnp.mean(x2 * x2, axis=-1, keepdims=True)
    o_ref[...] = x2 * lax.rsqrt(ms + EPS) * gf_ref[...]


def _combine_dense(y4, tp, x1, gt2, gf, tokens, rows_per_batch):
    n, d = x1.shape
    per_b = rows_per_batch // tokens
    return pl.pallas_call(
        _combine_dense_body,
        grid=(n // tokens,),
        in_specs=[pl.BlockSpec((TOP_K, tokens, d // 2), lambda i: (0, i, 0)),
                  pl.BlockSpec((tokens, LANES), lambda i: (i, 0)),
                  pl.BlockSpec((tokens, d), lambda i: (i, 0)),
                  pl.BlockSpec((1, 1, d), lambda i: (i // per_b, 0, 0)),
                  pl.BlockSpec((1, d), lambda i: (0, 0))],
        out_specs=pl.BlockSpec((tokens, d), lambda i: (i, 0)),
        out_shape=jax.ShapeDtypeStruct((n, d), F32),
        compiler_params=_cparams(("arbitrary",)),
        name="moe_combine_dense",
    )(y4, tp, x1, gt2, gf)


def _routing_tables(top_idx, rank, counts, bm):
    n = top_idx.shape[0]
    nk = n * TOP_K
    padded = (counts + bm - 1) // bm * bm
    pad_end = jnp.cumsum(padded)
    pad_start = pad_end - padded
    dest = (pad_start[top_idx] + rank).astype(I32)
    n_blocks = (nk + N_EXPERTS * (bm - 1) + bm - 1) // bm
    starts = jnp.arange(n_blocks, dtype=I32) * bm
    blk_expert = jnp.minimum(jnp.sum((pad_end[None, :] <= starts[:, None]).astype(I32), axis=1),
                             N_EXPERTS - 1).astype(I32)
    n_real = (pad_end[-1] // bm).astype(I32).reshape(1)
    return dest, blk_expert, n_real, n_blocks


def _layer(x, c, ctx, c_ctx, w_ada, b_ada, g_mix_norm, w_in, w_dw, b_dw, g_conv_ln, b_conv_ln,
           w_conv_out, w_alpha, b_alpha, g_gla_norm, w_gla_out, w_out, g_ffn_norm, w_router,
           b_router, w_exp_in, b_exp_in, w_exp_out, b_exp_out, g_final, *, cfg):
    b, s, d = x.shape
    n = b * s

    rows = (b + 1 + 7) // 8 * 8
    cc = jnp.zeros((rows, d), F32).at[:b].set(c).at[b].set(c_ctx)
    mod = _ada(cc, w_ada, b_ada)
    sh1, sc1, gt1, sh2, sc2, gt2 = [mod[:b, i * d:(i + 1) * d].reshape(b, 1, d) for i in range(6)]
    csh1 = mod[b:b + 1, 0:d]
    csc1 = mod[b:b + 1, d:2 * d]

    a0 = 2 * CONV_W + 2 * GLA_KD + 2 * GLA_VD
    w_in_r = jnp.concatenate(
        [w_in[:, :a0], w_in[:, a0 + 2 * GLA_RANK:], w_in[:, a0:a0 + 2 * GLA_RANK],
         jnp.zeros((d, LANES - 2 * GLA_RANK), F32)], axis=1).astype(BF16)
    gmn = g_mix_norm.reshape(1, d)

    u, q, k, v, sg, gc, gg, a = _inproj_lat(x, gmn, sh1, sc1, w_in_r, cfg["tm_in"])
    kc, vc, ac = _inproj_ctx(ctx, gmn, csh1, csc1, w_in_r, cfg["tm_ctx"])

    act = _conv(u.reshape(n, CONV_W), w_dw, b_dw, g_conv_ln, b_conv_ln, cfg["conv_rows"])

    cps = cfg["gla_cps"]
    st_b = _gla_ctx(kc, vc, ac, w_alpha[1], b_alpha[1], True, cps)
    o_b = _gla_lat(k, v, a, q, w_alpha[1], b_alpha[1], st_b, True, cps)
    st_f = _gla_ctx(kc, vc, ac, w_alpha[0], b_alpha[0], False, cps)
    og = _gla_lat(k, v, a, q, w_alpha[0], b_alpha[0], st_f, False, cps,
                  sg=sg, o_prev=o_b, g_norm=g_gla_norm)

    wr = jnp.zeros((d, LANES), F32).at[:, :N_EXPERTS].set(w_router)
    br = jnp.zeros((1, LANES), F32).at[0, :N_EXPERTS].set(b_router)
    x1, h2p, ti, tp, cnt = _merge(
        act, og.reshape(n, d), gc.reshape(n, d), gg.reshape(n, d), x.reshape(n, d),
        gt1, sh2, sc2, g_ffn_norm.reshape(1, d),
        w_conv_out.astype(BF16), w_gla_out.astype(BF16), w_out.astype(BF16), wr, br,
        cfg["tm_merge"], s)

    bm = cfg["moe_block"]
    dest, blk_expert, n_real, n_blocks = _routing_tables(
        ti[:, :TOP_K], ti[:, TOP_K:2 * TOP_K], cnt[0, :N_EXPERTS].astype(I32), bm)
    dest_slots = dest.T.reshape(n * TOP_K)
    xp = _sc_scatter_rows(h2p, dest_slots, n_blocks * bm)

    b1 = b_exp_in.reshape(N_EXPERTS, 2 * D_FF // _GLU_GROUP, LANES, 2).transpose(0, 1, 3, 2)
    b1 = b1.reshape(N_EXPERTS, 1, 2 * D_FF)
    yp = _experts(blk_expert, n_real, xp, w_exp_in, b1, w_exp_out,
                  b_exp_out.reshape(N_EXPERTS, 1, d), bm)

    y4 = _sc_gather_rows(yp, dest_slots).reshape(TOP_K, n, d // 2)
    out = _combine_dense(y4, tp, x1, gt2, g_final.reshape(1, d), cfg["tm_combine"], s)
    return out.reshape(b, s, d)


def _config(s, l):
    return dict(tm_in=min(512, s), tm_ctx=min(256, l), conv_rows=4, gla_cps=4,
                tm_merge=min(512, s), moe_block=512, moe_tokens=128, tm_combine=min(256, s))


def kernel(x, c, ctx, c_ctx, w_ada, b_ada, g_mix_norm, w_in, w_dw, b_dw, g_conv_ln, b_conv_ln,
           w_conv_out, w_alpha, b_alpha, g_gla_norm, w_gla_out, w_out, g_ffn_norm, w_router,
           b_router, w_exp_in, b_exp_in, w_exp_out, b_exp_out, g_final):
    depth = w_ada.shape[0]
    assert depth == 1, "single-layer block: the context stream is only consumed by the GLA scan"
    cfg = _config(x.shape[1], ctx.shape[1])
    return _layer(x, c, ctx, c_ctx, w_ada[0], b_ada[0], g_mix_norm[0], w_in[0], w_dw[0], b_dw[0],
                  g_conv_ln[0], b_conv_ln[0], w_conv_out[0], w_alpha[0], b_alpha[0],
                  g_gla_norm[0], w_gla_out[0], w_out[0], g_ffn_norm[0], w_router[0], b_router[0],
                  w_exp_in[0], b_exp_in[0], w_exp_out[0], b_exp_out[0], g_final, cfg=cfg)
```

```python
import functools

import jax
import jax.numpy as jnp
from jax import lax
from jax.experimental import pallas as pl
from jax.experimental.pallas import tpu as pltpu
from jax.experimental.pallas import tpu_sc as plsc

F32 = jnp.float32
BF16 = jnp.bfloat16
I32 = jnp.int32

D_MODEL = 1024
GRID_W = 64
EPS = 1e-6
CONV_W = 1024
CONV_K = 31
GLA_H = 4
GLA_DK = 128
GLA_DV = 256
GLA_KD = GLA_H * GLA_DK
GLA_VD = GLA_H * GLA_DV
GLA_RANK = 16
GLA_TAU = 16.0
GLA_CHUNK = 64
N_EXPERTS = 32
TOP_K = 4
D_FF = 1024
SWIGLU_ALPHA = 1.702
SWIGLU_LIMIT = 7.0

LANES = 128
VMEM_LIMIT = 56 * 1024 * 1024

_C_CONV_A = 0
_C_CONV_B = _C_CONV_A + CONV_W
_C_Q = _C_CONV_B + CONV_W
_C_K = _C_Q + GLA_KD
_C_V = _C_K + GLA_KD
_C_G = _C_V + GLA_VD
_C_GC = _C_G + GLA_VD
_C_GG = _C_GC + D_MODEL
_C_A = _C_GG + D_MODEL
_C_END = _C_A + LANES


def _cparams(sem):
    return pltpu.CompilerParams(dimension_semantics=sem, vmem_limit_bytes=VMEM_LIMIT)


def _dot(a, b):
    return jnp.dot(a, b, preferred_element_type=F32)


def _split_bf16(x):
    hi = x.astype(BF16)
    lo = (x - hi.astype(F32)).astype(BF16)
    return hi, lo


def _dot3(a, b):
    a_hi, a_lo = _split_bf16(a)
    b_hi, b_lo = _split_bf16(b)
    return _dot(a_hi, b_hi) + _dot(a_lo, b_hi) + _dot(a_hi, b_lo)


def _sigmoid(x):
    return 1.0 / (1.0 + jnp.exp(-x))


def _resident(shape):
    nd = len(shape)
    return pl.BlockSpec(shape, lambda *_: (0,) * nd, pipeline_mode=pl.Buffered(1))


def _ada_body(a_ref, w_ref, b_ref, o_ref):
    a = a_ref[...]
    a = a * _sigmoid(a)
    o_ref[...] = _dot3(a, w_ref[...]) + b_ref[...]


def _ada(cc, w, b):
    rows, d = cc.shape
    n = w.shape[1]
    tn = 512
    return pl.pallas_call(
        _ada_body,
        grid=(n // tn,),
        in_specs=[pl.BlockSpec((rows, d), lambda j: (0, 0)),
                  pl.BlockSpec((d, tn), lambda j: (0, j)),
                  pl.BlockSpec((1, tn), lambda j: (0, j))],
        out_specs=pl.BlockSpec((rows, tn), lambda j: (0, j)),
        out_shape=jax.ShapeDtypeStruct((rows, n), F32),
        compiler_params=_cparams(("arbitrary",)),
        name="ada_mod",
    )(cc, w, b.reshape(1, n))


def _norm_mod(xv, gn, sc, sh):
    ms = jnp.mean(xv * xv, axis=-1, keepdims=True)
    y = xv * lax.rsqrt(ms + EPS) * gn
    return (y * (1.0 + sc) + sh).astype(BF16)


def _inproj_lat_body(x_ref, gn_ref, sh_ref, sc_ref, w_ref, sm_ref, wdw_ref, bdw_ref, gln_ref, bln_ref,
                     act_ref, q_ref, k_ref, v_ref, sg_ref, gc_ref, gg_ref, a_ref,
                     win_ref, y_ref):
    h = _norm_mod(x_ref[0], gn_ref[...], sc_ref[0], sh_ref[0])
    ca = _dot(h, w_ref[:, _C_CONV_A:_C_CONV_B])
    cb = _dot(h, w_ref[:, _C_CONV_B:_C_Q])
    u = (ca * _sigmoid(cb)).astype(BF16)
    act_ref[0] = _conv_ln_swish(u, sm_ref, wdw_ref, bdw_ref, gln_ref, bln_ref, win_ref, y_ref)
    q_ref[0] = _dot(h, w_ref[:, _C_Q:_C_K]).astype(BF16)
    k_ref[0] = _dot(h, w_ref[:, _C_K:_C_V]).astype(BF16)
    v_ref[0] = _dot(h, w_ref[:, _C_V:_C_G]).astype(BF16)
    g = _dot(h, w_ref[:, _C_G:_C_GC])
    sg_ref[0] = (g * _sigmoid(g)).astype(BF16)
    gc_ref[0] = _sigmoid(_dot(h, w_ref[:, _C_GC:_C_GG])).astype(BF16)
    gg_ref[0] = _sigmoid(_dot(h, w_ref[:, _C_GG:_C_A])).astype(BF16)
    a_ref[0] = _dot(h, w_ref[:, _C_A:_C_END])


def _inproj_ctx_body(x_ref, gn_ref, sh_ref, sc_ref, w_ref, k_ref, v_ref, a_ref):
    h = _norm_mod(x_ref[0], gn_ref[...], sc_ref[...], sh_ref[...])
    k_ref[0] = _dot(h, w_ref[:, _C_K:_C_V]).astype(BF16)
    v_ref[0] = _dot(h, w_ref[:, _C_V:_C_G]).astype(BF16)
    a_ref[0] = _dot(h, w_ref[:, _C_A:_C_END])


def _inproj_lat(x, gn, sh, sc, w, w_dw, b_dw, g_ln, b_ln, tm):
    b, s, d = x.shape
    row = lambda n: pl.BlockSpec((1, tm, n), lambda bi, i: (bi, i, 0))
    vec = pl.BlockSpec((1, 1, d), lambda bi, i: (bi, 0, 0))
    const = lambda shape: pl.BlockSpec(shape, lambda bi, i: (0,) * len(shape))
    shp = lambda n, dt: jax.ShapeDtypeStruct((b, s, n), dt)
    sm = _conv_shift_matrix()
    wpad = jnp.zeros((32, CONV_W), F32).at[:CONV_K].set(w_dw)
    return pl.pallas_call(
        _inproj_lat_body,
        grid=(b, s // tm),
        in_specs=[row(d), const((1, d)), vec, vec, _resident(w.shape),
                  const(sm.shape), const(wpad.shape), const((1, CONV_W)), const((1, CONV_W)),
                  const((1, CONV_W))],
        out_specs=[row(CONV_W), row(GLA_KD), row(GLA_KD), row(GLA_VD), row(GLA_VD),
                   row(d), row(d), row(LANES)],
        out_shape=[shp(CONV_W, BF16), shp(GLA_KD, BF16), shp(GLA_KD, BF16), shp(GLA_VD, BF16),
                   shp(GLA_VD, BF16), shp(d, BF16), shp(d, BF16), shp(LANES, F32)],
        scratch_shapes=[pltpu.VMEM((2, _CONV_SUB * _CONV_SPAN, CONV_W), F32),
                        pltpu.VMEM((tm, CONV_W), F32)],
        compiler_params=_cparams(("arbitrary", "arbitrary")),
        name="inproj_lat",
    )(x, gn, sh, sc, w, sm, wpad, b_dw.reshape(1, CONV_W), g_ln.reshape(1, CONV_W),
      b_ln.reshape(1, CONV_W))


def _inproj_ctx(ctx, gn, sh, sc, w, tm):
    b, l, d = ctx.shape
    row = lambda n: pl.BlockSpec((1, tm, n), lambda bi, i: (bi, i, 0))
    vec = pl.BlockSpec((1, d), lambda bi, i: (0, 0))
    shp = lambda n, dt: jax.ShapeDtypeStruct((b, l, n), dt)
    return pl.pallas_call(
        _inproj_ctx_body,
        grid=(b, l // tm),
        in_specs=[row(d), vec, vec, vec, _resident(w.shape)],
        out_specs=[row(GLA_KD), row(GLA_VD), row(LANES)],
        out_shape=[shp(GLA_KD, BF16), shp(GLA_VD, BF16), shp(LANES, F32)],
        compiler_params=_cparams(("arbitrary", "arbitrary")),
        name="inproj_ctx",
    )(ctx, gn, sh, sc, w)


_CONV_SUB = 8
_CONV_SPAN = GRID_W + _CONV_SUB * ((CONV_K - 1) // _CONV_SUB)


def _conv_shift_matrix():
    row = lax.broadcasted_iota(I32, (_CONV_SUB, _CONV_SPAN, GRID_W), 1)
    shift = lax.broadcasted_iota(I32, (_CONV_SUB, _CONV_SPAN, GRID_W), 0)
    col = lax.broadcasted_iota(I32, (_CONV_SUB, _CONV_SPAN, GRID_W), 2)
    m = jnp.where(col == row + shift - CONV_K // 2, 1.0, 0.0)
    return m.reshape(_CONV_SUB * _CONV_SPAN, GRID_W).astype(BF16)


def _conv_ln_swish(u, sm_ref, w_ref, bdw_ref, gln_ref, bln_ref, win_ref, y_ref):
    for r in range(u.shape[0] // GRID_W):
        buf = r % 2
        win_ref[buf] = _dot(sm_ref[...], u[r * GRID_W:(r + 1) * GRID_W, :])
        for cb in range(CONV_W // LANES):
            ls = slice(cb * LANES, (cb + 1) * LANES)
            acc = None
            for k in range(CONV_K):
                s, a = k % _CONV_SUB, k // _CONV_SUB
                r0 = s * _CONV_SPAN + _CONV_SUB * a
                term = win_ref[buf, r0:r0 + GRID_W, ls] * w_ref[k:k + 1, ls]
                acc = term if acc is None else acc + term
            y_ref[r * GRID_W:(r + 1) * GRID_W, ls] = acc + bdw_ref[:, ls]
    y = y_ref[...]
    mu = jnp.mean(y, axis=-1, keepdims=True)
    yc = y - mu
    var = jnp.mean(yc * yc, axis=-1, keepdims=True)
    yn = yc * lax.rsqrt(var + EPS) * gln_ref[...] + bln_ref[...]
    return (yn * _sigmoid(yn)).astype(BF16)


def _log_sigmoid(z):
    return jnp.minimum(z, 0.0) - jnp.log(1.0 + jnp.exp(-jnp.abs(z)))


def _gla_keep_mask(t, reverse):
    row = lax.broadcasted_iota(I32, (t, t), 0)
    col = lax.broadcasted_iota(I32, (t, t), 1)
    same_chunk = lax.shift_right_logical(row, 6) == lax.shift_right_logical(col, 6)
    return jnp.logical_and(same_chunk, (col >= row) if reverse else (col <= row))


def _gla_decay(a, wh_ref, wl_ref, ba, keep):
    a_hi, a_lo = _split_bf16(a)
    z = _dot(jnp.concatenate([a_hi, a_lo], axis=1), wh_ref[...]) + _dot(a_hi, wl_ref[...]) + ba
    loga = _log_sigmoid(z) * (1.0 / GLA_TAU)
    l_hi, l_lo = _split_bf16(loga)
    tri = jnp.where(keep, 1.0, 0.0).astype(BF16)
    return _dot(tri, l_hi) + _dot(tri, l_lo)


def _gla_step(k, v, bcum, keep, states, *, reverse, q=None):
    t = k.shape[0]
    c = GLA_CHUNK
    nch = t // c
    tots = [bcum[n * c:n * c + 1, :] if reverse else bcum[(n + 1) * c - 1:(n + 1) * c, :]
            for n in range(nch)]
    totb = jnp.concatenate([jnp.broadcast_to(tt, (c, GLA_KD)) for tt in tots], axis=0)
    k32 = k.astype(F32)
    k_end = (k32 * jnp.exp(totb - bcum)).astype(BF16)
    decs = [jnp.exp(tt) for tt in tots]
    if q is not None:
        q_dec = (q.astype(F32) * jnp.exp(bcum) * (GLA_DK ** -0.5)).astype(BF16)
        k_inv = (k32 * jnp.exp(-bcum)).astype(BF16)
    order = range(nch - 1, -1, -1) if reverse else range(nch)
    outs, new_states = [], []
    for h in range(GLA_H):
        ks = slice(h * GLA_DK, (h + 1) * GLA_DK)
        vh = v[:, h * GLA_DV:(h + 1) * GLA_DV]
        st = states[h]
        if q is not None:
            scores = lax.dot_general(q_dec[:, ks], k_inv[:, ks], (((1,), (1,)), ((), ())),
                                     preferred_element_type=F32)
            o_h = _dot(jnp.where(keep, scores, 0.0).astype(BF16), vh)
            inter = [None] * nch
        for n in order:
            rs = slice(n * c, (n + 1) * c)
            if q is not None:
                inter[n] = _dot(q_dec[rs, ks], st.astype(BF16))
            kv = lax.dot_general(k_end[rs, ks], vh[rs, :], (((0,), (0,)), ((), ())),
                                 preferred_element_type=F32)
            dt = jnp.transpose(jnp.broadcast_to(decs[n][:, ks], (GLA_DK, GLA_DK)))
            st = st * jnp.concatenate([dt] * (GLA_DV // GLA_DK), axis=1) + kv
        new_states.append(st)
        if q is not None:
            outs.append(o_h + jnp.concatenate(inter, axis=0))
    return new_states, (jnp.concatenate(outs, axis=1) if q is not None else None)


def _gla_ctx_body(k_ref, v_ref, a_ref, wh_ref, wl_ref, ba_ref, st_ref, *, reverse):
    j = pl.program_id(1)

    @pl.when(j == 0)
    def _():
        st_ref[...] = jnp.zeros_like(st_ref)

    keep = _gla_keep_mask(k_ref.shape[1], reverse)
    bcum = _gla_decay(a_ref[0], wh_ref, wl_ref, ba_ref[...], keep)
    states, _ = _gla_step(k_ref[0], v_ref[0], bcum, keep,
                          [st_ref[0, h] for h in range(GLA_H)], reverse=reverse)
    for h in range(GLA_H):
        st_ref[0, h] = states[h]


def _gla_lat_body(*refs, reverse, final):
    if final:
        (k_ref, v_ref, a_ref, an_ref, q_ref, wh_ref, wl_ref, ba_ref, s0_ref, sg_ref, op_ref,
         gn_ref, o_ref, st_ref, bc_ref) = refs
    else:
        (k_ref, v_ref, a_ref, an_ref, q_ref, wh_ref, wl_ref, ba_ref, s0_ref,
         o_ref, st_ref, bc_ref) = refs
    j = pl.program_id(1)
    nb, t = k_ref.shape[0], k_ref.shape[1]
    keep = _gla_keep_mask(t, reverse)

    @pl.when(j == 0)
    def _():
        st_ref[...] = s0_ref[...]
        for bb in range(nb):
            bc_ref[bb] = _gla_decay(a_ref[bb], wh_ref, wl_ref, ba_ref[...], keep)

    for bb in range(nb):
        states, o = _gla_step(k_ref[bb], v_ref[bb], bc_ref[bb], keep,
                              [st_ref[bb, h] for h in range(GLA_H)], reverse=reverse,
                              q=q_ref[bb])
        bc_ref[bb] = _gla_decay(an_ref[bb], wh_ref, wl_ref, ba_ref[...], keep)
        for h in range(GLA_H):
            st_ref[bb, h] = states[h]
        if not final:
            o_ref[bb] = o.astype(BF16)
            continue
        o = o + op_ref[bb].astype(F32)
        parts = []
        for h in range(GLA_H):
            oh = o[:, h * GLA_DV:(h + 1) * GLA_DV]
            ms = jnp.mean(oh * oh, axis=-1, keepdims=True)
            parts.append(oh * lax.rsqrt(ms + EPS) * gn_ref[...])
        o_ref[bb] = (jnp.concatenate(parts, axis=1) * sg_ref[bb].astype(F32)).astype(BF16)


def _gla_decay_weights(w_alpha_d, reverse):
    a0 = GLA_RANK if reverse else 0
    w = jnp.zeros((LANES, GLA_KD), F32).at[a0:a0 + GLA_RANK].set(w_alpha_d)
    hi = w.astype(BF16)
    lo = (w - hi.astype(F32)).astype(BF16)
    return jnp.concatenate([hi, hi], axis=0), lo


def _gla_ctx(k, v, a, w_alpha_d, ba, reverse, cps):
    b, l, _ = k.shape
    tm = cps * GLA_CHUNK
    nj = l // tm
    jmap = (lambda j: nj - 1 - j) if reverse else (lambda j: j)
    row = lambda n: pl.BlockSpec((1, tm, n), lambda bi, j: (bi, jmap(j), 0))
    wh, wl = _gla_decay_weights(w_alpha_d, reverse)
    return pl.pallas_call(
        functools.partial(_gla_ctx_body, reverse=reverse),
        grid=(b, nj),
        in_specs=[row(GLA_KD), row(GLA_VD), row(LANES),
                  pl.BlockSpec(wh.shape, lambda bi, j: (0, 0)),
                  pl.BlockSpec(wl.shape, lambda bi, j: (0, 0)),
                  pl.BlockSpec((1, GLA_KD), lambda bi, j: (0, 0))],
        out_specs=pl.BlockSpec((1, GLA_H, GLA_DK, GLA_DV), lambda bi, j: (bi, 0, 0, 0)),
        out_shape=jax.ShapeDtypeStruct((b, GLA_H, GLA_DK, GLA_DV), F32),
        compiler_params=_cparams(("arbitrary", "arbitrary")),
        name="gla_ctx_bwd" if reverse else "gla_ctx_fwd",
    )(k, v, a, wh, wl, ba.reshape(1, GLA_KD))


def _gla_lat(k, v, a, q, w_alpha_d, ba, s0, reverse, cps, sg=None, o_prev=None, g_norm=None):
    b, s, _ = k.shape
    final = sg is not None
    tm = cps * GLA_CHUNK
    nj = s // tm
    jmap = (lambda j: nj - 1 - j) if reverse else (lambda j: j)
    nb = 2 if b % 2 == 0 else 1
    row = lambda n: pl.BlockSpec((nb, tm, n), lambda bi, j: (bi, jmap(j), 0))
    wh, wl = _gla_decay_weights(w_alpha_d, reverse)
    a_next = pl.BlockSpec((nb, tm, LANES), lambda bi, j: (bi, jmap(jnp.minimum(j + 1, nj - 1)), 0))
    in_specs = [row(GLA_KD), row(GLA_VD), row(LANES), a_next, row(GLA_KD),
                pl.BlockSpec(wh.shape, lambda bi, j: (0, 0)),
                pl.BlockSpec(wl.shape, lambda bi, j: (0, 0)),
                pl.BlockSpec((1, GLA_KD), lambda bi, j: (0, 0)),
                pl.BlockSpec((nb, GLA_H, GLA_DK, GLA_DV), lambda bi, j: (bi, 0, 0, 0))]
    args = [k, v, a, a, q, wh, wl, ba.reshape(1, GLA_KD), s0]
    if final:
        in_specs += [row(GLA_VD), row(GLA_VD), pl.BlockSpec((1, GLA_DV), lambda bi, j: (0, 0))]
        args += [sg, o_prev, g_norm.reshape(1, GLA_DV)]
    return pl.pallas_call(
        functools.partial(_gla_lat_body, reverse=reverse, final=final),
        grid=(b // nb, nj),
        in_specs=in_specs,
        out_specs=row(GLA_VD),
        out_shape=jax.ShapeDtypeStruct((b, s, GLA_VD), BF16),
        scratch_shapes=[pltpu.VMEM((nb, GLA_H, GLA_DK, GLA_DV), F32),
                        pltpu.VMEM((nb, tm, GLA_KD), F32)],
        compiler_params=_cparams(("arbitrary", "arbitrary")),
        name="gla_lat_bwd" if reverse else "gla_lat_fwd",
    )(*args)


def _pack_pair(lo, hi):
    lo_b = pltpu.bitcast(lo.astype(BF16).astype(F32), I32)
    hi_b = pltpu.bitcast(hi.astype(BF16).astype(F32), I32)
    return lax.shift_right_logical(lo_b, 16) | (hi_b & jnp.int32(-65536))


def _unpack_pair(w):
    lo = pltpu.bitcast(lax.shift_left(w, 16), F32)
    hi = pltpu.bitcast(w & jnp.int32(-65536), F32)
    return lo, hi


def _merge_body(act_ref, og_ref, gc_ref, gg_ref, x_ref, gt_ref, sh_ref, sc_ref, gn_ref,
                wc_ref, wg_ref, wo_ref, wr_ref, br_ref,
                x1_ref, h2_ref, ti_ref, tp_ref, cnt_ref):
    yc = _dot(act_ref[...], wc_ref[...])
    yg = _dot(og_ref[...], wg_ref[...])
    m = gc_ref[...].astype(F32) * yc + gg_ref[...].astype(F32) * yg
    x1 = x_ref[...] + gt_ref[0] * _dot(m.astype(BF16), wo_ref[...])
    x1_ref[...] = x1
    ms = jnp.mean(x1 * x1, axis=-1, keepdims=True)
    h2 = (x1 * lax.rsqrt(ms + EPS) * gn_ref[...]) * (1.0 + sc_ref[0]) + sh_ref[0]
    half = D_MODEL // 2
    h2_ref[...] = _pack_pair(h2[:, :half], h2[:, half:])
    logits = _dot3(h2, wr_ref[...]) + br_ref[...]
    lane = lax.broadcasted_iota(I32, logits.shape, 1).astype(F32)
    neg = jnp.float32(-jnp.inf)
    work = jnp.where(lane < N_EXPERTS, logits, neg)
    vals, idxs = [], []
    for _ in range(TOP_K):
        mx = jnp.max(work, axis=-1, keepdims=True)
        ix = jnp.min(jnp.where(work == mx, lane, float(LANES)), axis=-1, keepdims=True)
        vals.append(mx)
        idxs.append(ix)
        work = jnp.where(lane == ix, neg, work)
    es = [jnp.exp(v - vals[0]) for v in vals]
    den = es[0] + es[1] + es[2] + es[3]
    ti = jnp.zeros(logits.shape, F32)
    tp = jnp.zeros(logits.shape, F32)
    onehot = jnp.zeros(logits.shape, F32)
    for kk in range(TOP_K):
        ti = jnp.where(lane == kk, idxs[kk], ti)
        tp = jnp.where(lane == kk, es[kk] / den, tp)
        onehot = onehot + jnp.where(lane == idxs[kk], 1.0, 0.0)
    @pl.when(pl.program_id(0) == 0)
    def _():
        cnt_ref[...] = jnp.zeros_like(cnt_ref)

    tm = logits.shape[0]
    earlier = (lax.broadcasted_iota(I32, (tm, tm), 1) < lax.broadcasted_iota(I32, (tm, tm), 0))
    before = _dot(jnp.where(earlier, 1.0, 0.0).astype(BF16), onehot.astype(BF16)) + cnt_ref[0:1, :]
    for kk in range(TOP_K):
        rank = jnp.sum(jnp.where(lane == idxs[kk], before, 0.0), axis=-1, keepdims=True)
        ti = jnp.where(lane == TOP_K + kk, rank, ti)
    cnt_ref[...] = cnt_ref[...] + jnp.sum(onehot, axis=0, keepdims=True)
    ti_ref[...] = ti.astype(I32)
    tp_ref[...] = tp


def _merge(act, og, gc, gg, x2d, gt1, sh2, sc2, gn, wc, wg, wo, wr, br, tm, rows_per_batch):
    n, d = x2d.shape
    per_b = rows_per_batch // tm
    row = lambda w: pl.BlockSpec((tm, w), lambda i: (i, 0))
    vec = pl.BlockSpec((1, 1, d), lambda i: (i // per_b, 0, 0))
    const = lambda shape: pl.BlockSpec(shape, lambda i: (0,) * len(shape))
    return pl.pallas_call(
        _merge_body,
        grid=(n // tm,),
        in_specs=[row(d), row(d), row(d), row(d), row(d), vec, vec, vec, const((1, d)),
                  _resident(wc.shape), _resident(wg.shape), _resident(wo.shape),
                  const(wr.shape), const((1, LANES))],
        out_specs=[row(d), row(d // 2), row(LANES), row(LANES), const((8, LANES))],
        out_shape=[jax.ShapeDtypeStruct((n, d), F32), jax.ShapeDtypeStruct((n, d // 2), I32),
                   jax.ShapeDtypeStruct((n, LANES), I32), jax.ShapeDtypeStruct((n, LANES), F32),
                   jax.ShapeDtypeStruct((8, LANES), F32)],
        compiler_params=_cparams(("arbitrary",)),
        name="merge_router",
    )(act, og, gc, gg, x2d, gt1, sh2, sc2, gn, wc, wg, wo, wr, br)


_GLU_GROUP = 2 * LANES


def _deinterleave_matrix():
    src = lax.broadcasted_iota(I32, (_GLU_GROUP, _GLU_GROUP), 0)
    dst = lax.broadcasted_iota(I32, (_GLU_GROUP, _GLU_GROUP), 1)
    want = jnp.where(dst < LANES, 2 * dst, 2 * (dst - LANES) + 1)
    return jnp.where(src == want, 1.0, 0.0).astype(BF16)


def _expert_body(be_ref, nreal_ref, xp_ref, w1_ref, b1_ref, w2_ref, b2_ref, yp_ref,
                 w1s_ref, w2s_ref):
    i = pl.program_id(0)
    new_expert = jnp.logical_or(i == 0, be_ref[i] != be_ref[jnp.maximum(i - 1, 0)])

    @pl.when(jnp.logical_and(new_expert, i < nreal_ref[0]))
    def _():
        perm = _deinterleave_matrix()
        for g in range(2 * D_FF // _GLU_GROUP):
            cs = slice(g * _GLU_GROUP, (g + 1) * _GLU_GROUP)
            w1s_ref[:, cs] = _dot(w1_ref[0, :, cs].astype(BF16), perm).astype(BF16)
        w2s_ref[...] = w2_ref[0].astype(BF16)

    @pl.when(i < nreal_ref[0])
    def _():
        lo, hi = _unpack_pair(xp_ref[...])
        x = jnp.concatenate([lo, hi], axis=1).astype(BF16)
        hid = _dot(x, w1s_ref[...]) + b1_ref[0]
        ngrp = 2 * D_FF // _GLU_GROUP
        hg = jnp.concatenate(
            [hid[:, g * _GLU_GROUP:g * _GLU_GROUP + LANES] for g in range(ngrp)], axis=1)
        hl = jnp.concatenate(
            [hid[:, g * _GLU_GROUP + LANES:(g + 1) * _GLU_GROUP] for g in range(ngrp)], axis=1)
        xg = jnp.minimum(hg, SWIGLU_LIMIT)
        xl = jnp.clip(hl, -SWIGLU_LIMIT, SWIGLU_LIMIT)
        act = xg * _sigmoid(SWIGLU_ALPHA * xg) * (xl + 1.0)
        y = _dot(act.astype(BF16), w2s_ref[...]) + b2_ref[0]
        half = D_MODEL // 2
        yp_ref[...] = _pack_pair(y[:, :half], y[:, half:])

    @pl.when(i >= nreal_ref[0])
    def _():
        yp_ref[...] = jnp.zeros_like(yp_ref)


def _experts(blk_expert, n_real, xp, w1, b1, w2, b2, bm):
    p, w = xp.shape
    grid_spec = pltpu.PrefetchScalarGridSpec(
        num_scalar_prefetch=2,
        grid=(p // bm,),
        in_specs=[pl.BlockSpec((bm, w), lambda i, be, nr: (i, 0)),
                  pl.BlockSpec((1, D_MODEL, 2 * D_FF), lambda i, be, nr: (be[i], 0, 0)),
                  pl.BlockSpec((1, 1, 2 * D_FF), lambda i, be, nr: (be[i], 0, 0)),
                  pl.BlockSpec((1, D_FF, D_MODEL), lambda i, be, nr: (be[i], 0, 0)),
                  pl.BlockSpec((1, 1, D_MODEL), lambda i, be, nr: (be[i], 0, 0))],
        out_specs=pl.BlockSpec((bm, w), lambda i, be, nr: (i, 0)),
        scratch_shapes=[pltpu.VMEM((D_MODEL, 2 * D_FF), BF16), pltpu.VMEM((D_FF, D_MODEL), BF16)],
    )
    return pl.pallas_call(
        _expert_body,
        grid_spec=grid_spec,
        out_shape=jax.ShapeDtypeStruct((p, w), I32),
        compiler_params=_cparams(("arbitrary",)),
        name="moe_experts",
    )(blk_expert, n_real, xp, w1, b1, w2, b2)


_SC_CORES = 2
_SC_SUBCORES = 16
_SC_CHUNK = 64


def _sc_gather_rows(table, idx):
    rows, width = idx.shape[0], table.shape[1]
    workers = _SC_CORES * _SC_SUBCORES
    per_worker = rows // workers
    assert rows % (workers * _SC_CHUNK) == 0
    mesh = plsc.VectorSubcoreMesh(core_axis_name="c", subcore_axis_name="s")

    n_chunks = per_worker // _SC_CHUNK
    assert n_chunks % 2 == 0 and n_chunks >= 4

    def body(table_hbm, idx_hbm, out_hbm, idx0, idx1, rows0, rows1, gsem0, gsem1, wsem0, wsem1):
        wid = lax.axis_index("s") * _SC_CORES + lax.axis_index("c")
        base = wid * per_worker
        bufs = ((idx0, rows0, gsem0, wsem0), (idx1, rows1, gsem1, wsem1))

        def out_rows(c):
            return out_hbm.at[pl.ds(base + c * _SC_CHUNK, _SC_CHUNK)]

        def gather_start(c, b):
            idx_v, rows_v, gsem, _ = bufs[b]
            pltpu.sync_copy(idx_hbm.at[pl.ds(base + c * _SC_CHUNK, _SC_CHUNK)], idx_v)
            pltpu.make_async_copy(table_hbm.at[idx_v], rows_v, gsem).start()

        def gather_wait(b):
            idx_v, rows_v, gsem, _ = bufs[b]
            pltpu.make_async_copy(table_hbm.at[idx_v], rows_v, gsem).wait()

        def write_start(c, b):
            _, rows_v, _, wsem = bufs[b]
            pltpu.make_async_copy(rows_v, out_rows(c), wsem).start()

        def write_wait(c, b):
            _, rows_v, _, wsem = bufs[b]
            pltpu.make_async_copy(rows_v, out_rows(c), wsem).wait()

        gather_start(0, 0)
        gather_wait(0)
        write_start(0, 0)
        gather_start(1, 1)

        @pl.loop(1, n_chunks - 1, step=2)
        def _(c):
            gather_wait(1)
            write_start(c, 1)
            write_wait(c - 1, 0)
            gather_start(c + 1, 0)
            gather_wait(0)
            write_start(c + 1, 0)
            write_wait(c, 1)
            gather_start(c + 2, 1)

        gather_wait(1)
        write_start(n_chunks - 1, 1)
        write_wait(n_chunks - 2, 0)
        write_wait(n_chunks - 1, 1)

    chunk = lambda: pltpu.VMEM((_SC_CHUNK, width), table.dtype)
    return pl.kernel(
        body,
        out_type=jax.ShapeDtypeStruct((rows, width), table.dtype),
        mesh=mesh,
        scratch_types=[pltpu.VMEM((_SC_CHUNK,), I32), pltpu.VMEM((_SC_CHUNK,), I32), chunk(), chunk(),
                       pltpu.SemaphoreType.DMA, pltpu.SemaphoreType.DMA,
                       pltpu.SemaphoreType.DMA, pltpu.SemaphoreType.DMA],
        name="sc_gather_rows",
    )(table, idx)


def _sc_scatter_rows(rows, idx_slots, out_rows):
    n, width = rows.shape
    workers = _SC_CORES * _SC_SUBCORES
    per_worker = n // workers
    assert n % (workers * _SC_CHUNK) == 0
    mesh = plsc.VectorSubcoreMesh(core_axis_name="c", subcore_axis_name="s")

    def body(rows_hbm, idx_hbm, out_hbm, idx_v, rows_v):
        wid = lax.axis_index("s") * _SC_CORES + lax.axis_index("c")
        base = wid * per_worker

        @pl.loop(0, per_worker // _SC_CHUNK)
        def _(i):
            t0 = base + i * _SC_CHUNK
            pltpu.sync_copy(rows_hbm.at[pl.ds(t0, _SC_CHUNK)], rows_v)
            for kk in range(TOP_K):
                pltpu.sync_copy(idx_hbm.at[pl.ds(kk * n + t0, _SC_CHUNK)], idx_v)
                pltpu.sync_copy(rows_v, out_hbm.at[idx_v])

    return pl.kernel(
        body,
        out_type=jax.ShapeDtypeStruct((out_rows, width), rows.dtype),
        mesh=mesh,
        scratch_types=[pltpu.VMEM((_SC_CHUNK,), I32), pltpu.VMEM((_SC_CHUNK, width), rows.dtype)],
        name="sc_scatter_rows",
    )(rows, idx_slots)


def _combine_dense_body(y4_ref, tp_ref, x1_ref, gt_ref, gf_ref, o_ref):
    half = D_MODEL // 2
    tp = tp_ref[...]
    y_lo = y_hi = None
    for kk in range(TOP_K):
        lo, hi = _unpack_pair(y4_ref[kk])
        pk = tp[:, kk:kk + 1]
        y_lo = pk * lo if y_lo is None else y_lo + pk * lo
        y_hi = pk * hi if y_hi is None else y_hi + pk * hi
    x2 = x1_ref[...] + gt_ref[0] * jnp.concatenate([y_lo, y_hi], axis=1)
    ms = jnp.mean(x2 * x2, axis=-1, keepdims=True)
    o_ref[...] = x2 * lax.rsqrt(ms + EPS) * gf_ref[...]


def _combine_dense(y4, tp, x1, gt2, gf, tokens, rows_per_batch):
    n, d = x1.shape
    per_b = rows_per_batch // tokens
    return pl.pallas_call(
        _combine_dense_body,
        grid=(n // tokens,),
        in_specs=[pl.BlockSpec((TOP_K, tokens, d // 2), lambda i: (0, i, 0)),
                  pl.BlockSpec((tokens, LANES), lambda i: (i, 0)),
                  pl.BlockSpec((tokens, d), lambda i: (i, 0)),
                  pl.BlockSpec((1, 1, d), lambda i: (i // per_b, 0, 0)),
                  pl.BlockSpec((1, d), lambda i: (0, 0))],
        out_specs=pl.BlockSpec((tokens, d), lambda i: (i, 0)),
        out_shape=jax.ShapeDtypeStruct((n, d), F32),
        compiler_params=_cparams(("arbitrary",)),
        name="moe_combine_dense",
    )(y4, tp, x1, gt2, gf)


def _routing_tables(top_idx, rank, counts, bm):
    n = top_idx.shape[0]
    nk = n * TOP_K
    padded = (counts + bm - 1) // bm * bm
    pad_end = jnp.cumsum(padded)
    pad_start = pad_end - padded
    dest = (pad_start[top_idx] + rank).astype(I32)
    n_blocks = (nk + N_EXPERTS * (bm - 1) + bm - 1) // bm
    starts = jnp.arange(n_blocks, dtype=I32) * bm
    blk_expert = jnp.minimum(jnp.sum((pad_end[None, :] <= starts[:, None]).astype(I32), axis=1),
                             N_EXPERTS - 1).astype(I32)
    n_real = (pad_end[-1] // bm).astype(I32).reshape(1)
    return dest, blk_expert, n_real, n_blocks


def _layer(x, c, ctx, c_ctx, w_ada, b_ada, g_mix_norm, w_in, w_dw, b_dw, g_conv_ln, b_conv_ln,
           w_conv_out, w_alpha, b_alpha, g_gla_norm, w_gla_out, w_out, g_ffn_norm, w_router,
           b_router, w_exp_in, b_exp_in, w_exp_out, b_exp_out, g_final, *, cfg):
    b, s, d = x.shape
    n = b * s

    rows = (b + 1 + 7) // 8 * 8
    cc = jnp.zeros((rows, d), F32).at[:b].set(c).at[b].set(c_ctx)
    mod = _ada(cc, w_ada, b_ada)
    sh1, sc1, gt1, sh2, sc2, gt2 = [mod[:b, i * d:(i + 1) * d].reshape(b, 1, d) for i in range(6)]
    csh1 = mod[b:b + 1, 0:d]
    csc1 = mod[b:b + 1, d:2 * d]

    a0 = 2 * CONV_W + 2 * GLA_KD + 2 * GLA_VD
    w_in_r = jnp.concatenate(
        [w_in[:, :a0], w_in[:, a0 + 2 * GLA_RANK:], w_in[:, a0:a0 + 2 * GLA_RANK],
         jnp.zeros((d, LANES - 2 * GLA_RANK), F32)], axis=1).astype(BF16)
    gmn = g_mix_norm.reshape(1, d)

    act, q, k, v, sg, gc, gg, a = _inproj_lat(x, gmn, sh1, sc1, w_in_r, w_dw, b_dw, g_conv_ln,
                                              b_conv_ln, cfg["tm_in"])
    act = act.reshape(n, CONV_W)
    kc, vc, ac = _inproj_ctx(ctx, gmn, csh1, csc1, w_in_r, cfg["tm_ctx"])

    cps = cfg["gla_cps"]
    st_b = _gla_ctx(kc, vc, ac, w_alpha[1], b_alpha[1], True, cps)
    o_b = _gla_lat(k, v, a, q, w_alpha[1], b_alpha[1], st_b, True, cps)
    st_f = _gla_ctx(kc, vc, ac, w_alpha[0], b_alpha[0], False, cps)
    og = _gla_lat(k, v, a, q, w_alpha[0], b_alpha[0], st_f, False, cps,
                  sg=sg, o_prev=o_b, g_norm=g_gla_norm)

    wr = jnp.zeros((d, LANES), F32).at[:, :N_EXPERTS].set(w_router)
    br = jnp.zeros((1, LANES), F32).at[0, :N_EXPERTS].set(b_router)
    x1, h2p, ti, tp, cnt = _merge(
        act, og.reshape(n, d), gc.reshape(n, d), gg.reshape(n, d), x.reshape(n, d),
        gt1, sh2, sc2, g_ffn_norm.reshape(1, d),
        w_conv_out.astype(BF16), w_gla_out.astype(BF16), w_out.astype(BF16), wr, br,
        cfg["tm_merge"], s)

    bm = cfg["moe_block"]
    dest, blk_expert, n_real, n_blocks = _routing_tables(
        ti[:, :TOP_K], ti[:, TOP_K:2 * TOP_K], cnt[0, :N_EXPERTS].astype(I32), bm)
    dest_slots = dest.T.reshape(n * TOP_K)
    xp = _sc_scatter_rows(h2p, dest_slots, n_blocks * bm)

    b1 = b_exp_in.reshape(N_EXPERTS, 2 * D_FF // _GLU_GROUP, LANES, 2).transpose(0, 1, 3, 2)
    b1 = b1.reshape(N_EXPERTS, 1, 2 * D_FF)
    yp = _experts(blk_expert, n_real, xp, w_exp_in, b1, w_exp_out,
                  b_exp_out.reshape(N_EXPERTS, 1, d), bm)

    y4 = _sc_gather_rows(yp, dest_slots).reshape(TOP_K, n, d // 2)
    out = _combine_dense(y4, tp, x1, gt2, g_final.reshape(1, d), cfg["tm_combine"], s)
    return out.reshape(b, s, d)


def _config(s, l):
    return dict(tm_in=min(512, s), tm_ctx=min(256, l), gla_cps=4,
                tm_merge=min(512, s), moe_block=512, tm_combine=min(256, s))


def kernel(x, c, ctx, c_ctx, w_ada, b_ada, g_mix_norm, w_in, w_dw, b_dw, g_conv_ln, b_conv_ln,
           w_conv_out, w_alpha, b_alpha, g_gla_norm, w_gla_out, w_out, g_ffn_norm, w_router,
           b_router, w_exp_in, b_exp_in, w_exp_out, b_exp_out, g_final):
    depth = w_ada.shape[0]
    assert depth == 1, "single-layer block: the context stream is only consumed by the GLA scan"
    cfg = _config(x.shape[1], ctx.shape[1])
    return _layer(x, c, ctx, c_ctx, w_ada[0], b_ada[0], g_mix_norm[0], w_in[0], w_dw[0], b_dw[0],
                  g_conv_ln[0], b_conv_ln[0], w_conv_out[0], w_alpha[0], b_alpha[0],
                  g_gla_norm[0], w_gla_out[0], w_out[0], g_ffn_norm[0], w_router[0], b_router[0],
                  w_exp_in[0], b_exp_in[0], w_exp_out[0], b_exp_out[0], g_final, cfg=cfg)
```

```python
import functools

import jax
import jax.numpy as jnp
from jax import lax
from jax.experimental import pallas as pl
from jax.experimental.pallas import tpu as pltpu
from jax.experimental.pallas import tpu_sc as plsc

F32 = jnp.float32
BF16 = jnp.bfloat16
I32 = jnp.int32

D_MODEL = 1024
GRID_W = 64
EPS = 1e-6
CONV_W = 1024
CONV_K = 31
GLA_H = 4
GLA_DK = 128
GLA_DV = 256
GLA_KD = GLA_H * GLA_DK
GLA_VD = GLA_H * GLA_DV
GLA_RANK = 16
GLA_TAU = 16.0
GLA_CHUNK = 64
N_EXPERTS = 32
TOP_K = 4
D_FF = 1024
SWIGLU_ALPHA = 1.702
SWIGLU_LIMIT = 7.0

LANES = 128
VMEM_LIMIT = 56 * 1024 * 1024

_C_CONV_A = 0
_C_CONV_B = _C_CONV_A + CONV_W
_C_Q = _C_CONV_B + CONV_W
_C_K = _C_Q + GLA_KD
_C_V = _C_K + GLA_KD
_C_G = _C_V + GLA_VD
_C_GC = _C_G + GLA_VD
_C_GG = _C_GC + D_MODEL
_C_A = _C_GG + D_MODEL
_C_END = _C_A + LANES


def _cparams(sem):
    return pltpu.CompilerParams(dimension_semantics=sem, vmem_limit_bytes=VMEM_LIMIT)


def _dot(a, b):
    return jnp.dot(a, b, preferred_element_type=F32)


def _split_bf16(x):
    hi = x.astype(BF16)
    lo = (x - hi.astype(F32)).astype(BF16)
    return hi, lo


def _dot3(a, b):
    a_hi, a_lo = _split_bf16(a)
    b_hi, b_lo = _split_bf16(b)
    return _dot(a_hi, b_hi) + _dot(a_lo, b_hi) + _dot(a_hi, b_lo)


def _sigmoid(x):
    return 1.0 / (1.0 + jnp.exp(-x))


def _resident(shape):
    nd = len(shape)
    return pl.BlockSpec(shape, lambda *_: (0,) * nd, pipeline_mode=pl.Buffered(1))


def _ada_body(a_ref, w_ref, b_ref, o_ref):
    a = a_ref[...]
    a = a * _sigmoid(a)
    o_ref[...] = _dot3(a, w_ref[...]) + b_ref[...]


def _ada(cc, w, b):
    rows, d = cc.shape
    n = w.shape[1]
    tn = 512
    return pl.pallas_call(
        _ada_body,
        grid=(n // tn,),
        in_specs=[pl.BlockSpec((rows, d), lambda j: (0, 0)),
                  pl.BlockSpec((d, tn), lambda j: (0, j)),
                  pl.BlockSpec((1, tn), lambda j: (0, j))],
        out_specs=pl.BlockSpec((rows, tn), lambda j: (0, j)),
        out_shape=jax.ShapeDtypeStruct((rows, n), F32),
        compiler_params=_cparams(("arbitrary",)),
        name="ada_mod",
    )(cc, w, b.reshape(1, n))


def _norm_mod(xv, gn, sc, sh):
    ms = jnp.mean(xv * xv, axis=-1, keepdims=True)
    y = xv * lax.rsqrt(ms + EPS) * gn
    return (y * (1.0 + sc) + sh).astype(BF16)


def _inproj_lat_body(x_ref, gn_ref, sh_ref, sc_ref, w_ref, sm_ref, wdw_ref, bdw_ref, gln_ref, bln_ref,
                     act_ref, q_ref, k_ref, v_ref, sg_ref, gc_ref, gg_ref, a_ref,
                     win_ref, y_ref):
    h = _norm_mod(x_ref[0], gn_ref[...], sc_ref[0], sh_ref[0])
    ca = _dot(h, w_ref[:, _C_CONV_A:_C_CONV_B])
    cb = _dot(h, w_ref[:, _C_CONV_B:_C_Q])
    u = (ca * _sigmoid(cb)).astype(BF16)
    act_ref[0] = _conv_ln_swish(u, sm_ref, wdw_ref, bdw_ref, gln_ref, bln_ref, win_ref, y_ref)
    q_ref[0] = _dot(h, w_ref[:, _C_Q:_C_K]).astype(BF16)
    k_ref[0] = _dot(h, w_ref[:, _C_K:_C_V]).astype(BF16)
    v_ref[0] = _dot(h, w_ref[:, _C_V:_C_G]).astype(BF16)
    g = _dot(h, w_ref[:, _C_G:_C_GC])
    sg_ref[0] = (g * _sigmoid(g)).astype(BF16)
    gc_ref[0] = _sigmoid(_dot(h, w_ref[:, _C_GC:_C_GG])).astype(BF16)
    gg_ref[0] = _sigmoid(_dot(h, w_ref[:, _C_GG:_C_A])).astype(BF16)
    a_ref[0] = _dot(h, w_ref[:, _C_A:_C_END])


def _inproj_ctx_body(x_ref, gn_ref, sh_ref, sc_ref, w_ref, k_ref, v_ref, a_ref):
    h = _norm_mod(x_ref[0], gn_ref[...], sc_ref[...], sh_ref[...])
    k_ref[0] = _dot(h, w_ref[:, _C_K:_C_V]).astype(BF16)
    v_ref[0] = _dot(h, w_ref[:, _C_V:_C_G]).astype(BF16)
    a_ref[0] = _dot(h, w_ref[:, _C_A:_C_END])


def _inproj_lat(x, gn, sh, sc, w, w_dw, b_dw, g_ln, b_ln, tm):
    b, s, d = x.shape
    row = lambda n: pl.BlockSpec((1, tm, n), lambda bi, i: (bi, i, 0))
    vec = pl.BlockSpec((1, 1, d), lambda bi, i: (bi, 0, 0))
    const = lambda shape: pl.BlockSpec(shape, lambda bi, i: (0,) * len(shape))
    shp = lambda n, dt: jax.ShapeDtypeStruct((b, s, n), dt)
    sm = _conv_shift_matrix()
    wpad = jnp.zeros((32, CONV_W), F32).at[:CONV_K].set(w_dw)
    return pl.pallas_call(
        _inproj_lat_body,
        grid=(b, s // tm),
        in_specs=[row(d), const((1, d)), vec, vec, _resident(w.shape),
                  const(sm.shape), const(wpad.shape), const((1, CONV_W)), const((1, CONV_W)),
                  const((1, CONV_W))],
        out_specs=[row(CONV_W), row(GLA_KD), row(GLA_KD), row(GLA_VD), row(GLA_VD),
                   row(d), row(d), row(LANES)],
        out_shape=[shp(CONV_W, BF16), shp(GLA_KD, BF16), shp(GLA_KD, BF16), shp(GLA_VD, BF16),
                   shp(GLA_VD, BF16), shp(d, BF16), shp(d, BF16), shp(LANES, F32)],
        scratch_shapes=[pltpu.VMEM((2, _CONV_SUB * _CONV_SPAN, CONV_W), F32),
                        pltpu.VMEM((tm, CONV_W), F32)],
        compiler_params=_cparams(("arbitrary", "arbitrary")),
        name="inproj_lat",
    )(x, gn, sh, sc, w, sm, wpad, b_dw.reshape(1, CONV_W), g_ln.reshape(1, CONV_W),
      b_ln.reshape(1, CONV_W))


def _inproj_ctx(ctx, gn, sh, sc, w, tm):
    b, l, d = ctx.shape
    row = lambda n: pl.BlockSpec((1, tm, n), lambda bi, i: (bi, i, 0))
    vec = pl.BlockSpec((1, d), lambda bi, i: (0, 0))
    shp = lambda n, dt: jax.ShapeDtypeStruct((b, l, n), dt)
    return pl.pallas_call(
        _inproj_ctx_body,
        grid=(b, l // tm),
        in_specs=[row(d), vec, vec, vec, _resident(w.shape)],
        out_specs=[row(GLA_KD), row(GLA_VD), row(LANES)],
        out_shape=[shp(GLA_KD, BF16), shp(GLA_VD, BF16), shp(LANES, F32)],
        compiler_params=_cparams(("arbitrary", "arbitrary")),
        name="inproj_ctx",
    )(ctx, gn, sh, sc, w)


_CONV_SUB = 8
_CONV_SPAN = GRID_W + _CONV_SUB * ((CONV_K - 1) // _CONV_SUB)


def _conv_shift_matrix():
    row = lax.broadcasted_iota(I32, (_CONV_SUB, _CONV_SPAN, GRID_W), 1)
    shift = lax.broadcasted_iota(I32, (_CONV_SUB, _CONV_SPAN, GRID_W), 0)
    col = lax.broadcasted_iota(I32, (_CONV_SUB, _CONV_SPAN, GRID_W), 2)
    m = jnp.where(col == row + shift - CONV_K // 2, 1.0, 0.0)
    return m.reshape(_CONV_SUB * _CONV_SPAN, GRID_W).astype(BF16)


def _conv_ln_swish(u, sm_ref, w_ref, bdw_ref, gln_ref, bln_ref, win_ref, y_ref):
    for r in range(u.shape[0] // GRID_W):
        buf = r % 2
        win_ref[buf] = _dot(sm_ref[...], u[r * GRID_W:(r + 1) * GRID_W, :])
        for cb in range(CONV_W // LANES):
            ls = slice(cb * LANES, (cb + 1) * LANES)
            acc = None
            for k in range(CONV_K):
                s, a = k % _CONV_SUB, k // _CONV_SUB
                r0 = s * _CONV_SPAN + _CONV_SUB * a
                term = win_ref[buf, r0:r0 + GRID_W, ls] * w_ref[k:k + 1, ls]
                acc = term if acc is None else acc + term
            y_ref[r * GRID_W:(r + 1) * GRID_W, ls] = acc + bdw_ref[:, ls]
    y = y_ref[...]
    mu = jnp.mean(y, axis=-1, keepdims=True)
    yc = y - mu
    var = jnp.mean(yc * yc, axis=-1, keepdims=True)
    yn = yc * lax.rsqrt(var + EPS) * gln_ref[...] + bln_ref[...]
    return (yn * _sigmoid(yn)).astype(BF16)


def _log_sigmoid(z):
    return jnp.minimum(z, 0.0) - jnp.log(1.0 + jnp.exp(-jnp.abs(z)))


def _gla_keep_mask(t, reverse):
    row = lax.broadcasted_iota(I32, (t, t), 0)
    col = lax.broadcasted_iota(I32, (t, t), 1)
    same_chunk = lax.shift_right_logical(row, 6) == lax.shift_right_logical(col, 6)
    return jnp.logical_and(same_chunk, (col >= row) if reverse else (col <= row))


def _gla_decay(a, wh_ref, wl_ref, ba, keep):
    a_hi, a_lo = _split_bf16(a)
    z = _dot(jnp.concatenate([a_hi, a_lo], axis=1), wh_ref[...]) + _dot(a_hi, wl_ref[...]) + ba
    loga = _log_sigmoid(z) * (1.0 / GLA_TAU)
    l_hi, l_lo = _split_bf16(loga)
    tri = jnp.where(keep, 1.0, 0.0).astype(BF16)
    return _dot(tri, l_hi) + _dot(tri, l_lo)


def _gla_step(k, v, bcum, keep, states, *, reverse, q=None):
    t = k.shape[0]
    c = GLA_CHUNK
    nch = t // c
    tots = [bcum[n * c:n * c + 1, :] if reverse else bcum[(n + 1) * c - 1:(n + 1) * c, :]
            for n in range(nch)]
    totb = jnp.concatenate([jnp.broadcast_to(tt, (c, GLA_KD)) for tt in tots], axis=0)
    k32 = k.astype(F32)
    k_end = (k32 * jnp.exp(totb - bcum)).astype(BF16)
    decs = [jnp.exp(tt) for tt in tots]
    if q is not None:
        q_dec = (q.astype(F32) * jnp.exp(bcum) * (GLA_DK ** -0.5)).astype(BF16)
        k_inv = (k32 * jnp.exp(-bcum)).astype(BF16)
    order = range(nch - 1, -1, -1) if reverse else range(nch)
    outs, new_states = [], []
    for h in range(GLA_H):
        ks = slice(h * GLA_DK, (h + 1) * GLA_DK)
        vh = v[:, h * GLA_DV:(h + 1) * GLA_DV]
        st = states[h]
        if q is not None:
            scores = lax.dot_general(q_dec[:, ks], k_inv[:, ks], (((1,), (1,)), ((), ())),
                                     preferred_element_type=F32)
            o_h = _dot(jnp.where(keep, scores, 0.0).astype(BF16), vh)
            inter = [None] * nch
        for n in order:
            rs = slice(n * c, (n + 1) * c)
            if q is not None:
                inter[n] = _dot(q_dec[rs, ks], st.astype(BF16))
            kv = lax.dot_general(k_end[rs, ks], vh[rs, :], (((0,), (0,)), ((), ())),
                                 preferred_element_type=F32)
            dt = jnp.transpose(jnp.broadcast_to(decs[n][:, ks], (GLA_DK, GLA_DK)))
            st = st * jnp.concatenate([dt] * (GLA_DV // GLA_DK), axis=1) + kv
        new_states.append(st)
        if q is not None:
            outs.append(o_h + jnp.concatenate(inter, axis=0))
    return new_states, (jnp.concatenate(outs, axis=1) if q is not None else None)


def _gla_ctx_body(k_ref, v_ref, a_ref, wh_ref, wl_ref, ba_ref, st_ref, *, reverse):
    j = pl.program_id(1)

    @pl.when(j == 0)
    def _():
        st_ref[...] = jnp.zeros_like(st_ref)

    keep = _gla_keep_mask(k_ref.shape[1], reverse)
    bcum = _gla_decay(a_ref[0], wh_ref, wl_ref, ba_ref[...], keep)
    states, _ = _gla_step(k_ref[0], v_ref[0], bcum, keep,
                          [st_ref[0, h] for h in range(GLA_H)], reverse=reverse)
    for h in range(GLA_H):
        st_ref[0, h] = states[h]


def _gla_lat_body(*refs, reverse, final):
    if final:
        (k_ref, v_ref, a_ref, an_ref, q_ref, wh_ref, wl_ref, ba_ref, s0_ref, sg_ref, op_ref,
         gn_ref, o_ref, st_ref, bc_ref) = refs
    else:
        (k_ref, v_ref, a_ref, an_ref, q_ref, wh_ref, wl_ref, ba_ref, s0_ref,
         o_ref, st_ref, bc_ref) = refs
    j = pl.program_id(1)
    nb, t = k_ref.shape[0], k_ref.shape[1]
    keep = _gla_keep_mask(t, reverse)

    @pl.when(j == 0)
    def _():
        st_ref[...] = s0_ref[...]
        for bb in range(nb):
            bc_ref[bb] = _gla_decay(a_ref[bb], wh_ref, wl_ref, ba_ref[...], keep)

    for bb in range(nb):
        states, o = _gla_step(k_ref[bb], v_ref[bb], bc_ref[bb], keep,
                              [st_ref[bb, h] for h in range(GLA_H)], reverse=reverse,
                              q=q_ref[bb])
        bc_ref[bb] = _gla_decay(an_ref[bb], wh_ref, wl_ref, ba_ref[...], keep)
        for h in range(GLA_H):
            st_ref[bb, h] = states[h]
        if not final:
            o_ref[bb] = o.astype(BF16)
            continue
        o = o + op_ref[bb].astype(F32)
        parts = []
        for h in range(GLA_H):
            oh = o[:, h * GLA_DV:(h + 1) * GLA_DV]
            ms = jnp.mean(oh * oh, axis=-1, keepdims=True)
            parts.append(oh * lax.rsqrt(ms + EPS) * gn_ref[...])
        o_ref[bb] = (jnp.concatenate(parts, axis=1) * sg_ref[bb].astype(F32)).astype(BF16)


def _gla_decay_weights(w_alpha_d, reverse):
    a0 = GLA_RANK if reverse else 0
    w = jnp.zeros((LANES, GLA_KD), F32).at[a0:a0 + GLA_RANK].set(w_alpha_d)
    hi = w.astype(BF16)
    lo = (w - hi.astype(F32)).astype(BF16)
    return jnp.concatenate([hi, hi], axis=0), lo


def _gla_ctx(k, v, a, w_alpha_d, ba, reverse, cps):
    b, l, _ = k.shape
    tm = cps * GLA_CHUNK
    nj = l // tm
    jmap = (lambda j: nj - 1 - j) if reverse else (lambda j: j)
    row = lambda n: pl.BlockSpec((1, tm, n), lambda bi, j: (bi, jmap(j), 0))
    wh, wl = _gla_decay_weights(w_alpha_d, reverse)
    return pl.pallas_call(
        functools.partial(_gla_ctx_body, reverse=reverse),
        grid=(b, nj),
        in_specs=[row(GLA_KD), row(GLA_VD), row(LANES),
                  pl.BlockSpec(wh.shape, lambda bi, j: (0, 0)),
                  pl.BlockSpec(wl.shape, lambda bi, j: (0, 0)),
                  pl.BlockSpec((1, GLA_KD), lambda bi, j: (0, 0))],
        out_specs=pl.BlockSpec((1, GLA_H, GLA_DK, GLA_DV), lambda bi, j: (bi, 0, 0, 0)),
        out_shape=jax.ShapeDtypeStruct((b, GLA_H, GLA_DK, GLA_DV), F32),
        compiler_params=_cparams(("arbitrary", "arbitrary")),
        name="gla_ctx_bwd" if reverse else "gla_ctx_fwd",
    )(k, v, a, wh, wl, ba.reshape(1, GLA_KD))


def _gla_lat(k, v, a, q, w_alpha_d, ba, s0, reverse, cps, sg=None, o_prev=None, g_norm=None):
    b, s, _ = k.shape
    final = sg is not None
    tm = cps * GLA_CHUNK
    nj = s // tm
    jmap = (lambda j: nj - 1 - j) if reverse else (lambda j: j)
    nb = 4 if b % 4 == 0 else 2 if b % 2 == 0 else 1
    row = lambda n: pl.BlockSpec((nb, tm, n), lambda bi, j: (bi, jmap(j), 0))
    wh, wl = _gla_decay_weights(w_alpha_d, reverse)
    a_next = pl.BlockSpec((nb, tm, LANES), lambda bi, j: (bi, jmap(jnp.minimum(j + 1, nj - 1)), 0))
    in_specs = [row(GLA_KD), row(GLA_VD), row(LANES), a_next, row(GLA_KD),
                pl.BlockSpec(wh.shape, lambda bi, j: (0, 0)),
                pl.BlockSpec(wl.shape, lambda bi, j: (0, 0)),
                pl.BlockSpec((1, GLA_KD), lambda bi, j: (0, 0)),
                pl.BlockSpec((nb, GLA_H, GLA_DK, GLA_DV), lambda bi, j: (bi, 0, 0, 0))]
    args = [k, v, a, a, q, wh, wl, ba.reshape(1, GLA_KD), s0]
    if final:
        in_specs += [row(GLA_VD), row(GLA_VD), pl.BlockSpec((1, GLA_DV), lambda bi, j: (0, 0))]
        args += [sg, o_prev, g_norm.reshape(1, GLA_DV)]
    return pl.pallas_call(
        functools.partial(_gla_lat_body, reverse=reverse, final=final),
        grid=(b // nb, nj),
        in_specs=in_specs,
        out_specs=row(GLA_VD),
        out_shape=jax.ShapeDtypeStruct((b, s, GLA_VD), BF16),
        scratch_shapes=[pltpu.VMEM((nb, GLA_H, GLA_DK, GLA_DV), F32),
                        pltpu.VMEM((nb, tm, GLA_KD), F32)],
        compiler_params=_cparams(("arbitrary", "arbitrary")),
        name="gla_lat_bwd" if reverse else "gla_lat_fwd",
    )(*args)


def _pack_pair(lo, hi):
    lo_b = pltpu.bitcast(lo.astype(BF16).astype(F32), I32)
    hi_b = pltpu.bitcast(hi.astype(BF16).astype(F32), I32)
    return lax.shift_right_logical(lo_b, 16) | (hi_b & jnp.int32(-65536))


def _unpack_pair(w):
    lo = pltpu.bitcast(lax.shift_left(w, 16), F32)
    hi = pltpu.bitcast(w & jnp.int32(-65536), F32)
    return lo, hi


def _merge_body(act_ref, og_ref, gc_ref, gg_ref, x_ref, gt_ref, sh_ref, sc_ref, gn_ref,
                wc_ref, wg_ref, wo_ref, wr_ref, br_ref,
                x1_ref, h2_ref, ti_ref, tp_ref, cnt_ref):
    yc = _dot(act_ref[...], wc_ref[...])
    yg = _dot(og_ref[...], wg_ref[...])
    m = gc_ref[...].astype(F32) * yc + gg_ref[...].astype(F32) * yg
    x1 = x_ref[...] + gt_ref[0] * _dot(m.astype(BF16), wo_ref[...])
    x1_ref[...] = x1
    ms = jnp.mean(x1 * x1, axis=-1, keepdims=True)
    h2 = (x1 * lax.rsqrt(ms + EPS) * gn_ref[...]) * (1.0 + sc_ref[0]) + sh_ref[0]
    half = D_MODEL // 2
    h2_ref[...] = _pack_pair(h2[:, :half], h2[:, half:])
    logits = _dot3(h2, wr_ref[...]) + br_ref[...]
    lane = lax.broadcasted_iota(I32, logits.shape, 1).astype(F32)
    neg = jnp.float32(-jnp.inf)
    work = jnp.where(lane < N_EXPERTS, logits, neg)
    vals, idxs = [], []
    for _ in range(TOP_K):
        mx = jnp.max(work, axis=-1, keepdims=True)
        ix = jnp.min(jnp.where(work == mx, lane, float(LANES)), axis=-1, keepdims=True)
        vals.append(mx)
        idxs.append(ix)
        work = jnp.where(lane == ix, neg, work)
    es = [jnp.exp(v - vals[0]) for v in vals]
    den = es[0] + es[1] + es[2] + es[3]
    ti = jnp.zeros(logits.shape, F32)
    tp = jnp.zeros(logits.shape, F32)
    onehot = jnp.zeros(logits.shape, F32)
    for kk in range(TOP_K):
        ti = jnp.where(lane == kk, idxs[kk], ti)
        tp = jnp.where(lane == kk, es[kk] / den, tp)
        onehot = onehot + jnp.where(lane == idxs[kk], 1.0, 0.0)
    @pl.when(pl.program_id(0) == 0)
    def _():
        cnt_ref[...] = jnp.zeros_like(cnt_ref)

    tm = logits.shape[0]
    earlier = (lax.broadcasted_iota(I32, (tm, tm), 1) < lax.broadcasted_iota(I32, (tm, tm), 0))
    before = _dot(jnp.where(earlier, 1.0, 0.0).astype(BF16), onehot.astype(BF16)) + cnt_ref[0:1, :]
    for kk in range(TOP_K):
        rank = jnp.sum(jnp.where(lane == idxs[kk], before, 0.0), axis=-1, keepdims=True)
        ti = jnp.where(lane == TOP_K + kk, rank, ti)
    cnt_ref[...] = cnt_ref[...] + jnp.sum(onehot, axis=0, keepdims=True)
    ti_ref[...] = ti.astype(I32)
    tp_ref[...] = tp


def _merge(act, og, gc, gg, x2d, gt1, sh2, sc2, gn, wc, wg, wo, wr, br, tm, rows_per_batch):
    n, d = x2d.shape
    per_b = rows_per_batch // tm
    row = lambda w: pl.BlockSpec((tm, w), lambda i: (i, 0))
    vec = pl.BlockSpec((1, 1, d), lambda i: (i // per_b, 0, 0))
    const = lambda shape: pl.BlockSpec(shape, lambda i: (0,) * len(shape))
    return pl.pallas_call(
        _merge_body,
        grid=(n // tm,),
        in_specs=[row(d), row(d), row(d), row(d), row(d), vec, vec, vec, const((1, d)),
                  _resident(wc.shape), _resident(wg.shape), _resident(wo.shape),
                  const(wr.shape), const((1, LANES))],
        out_specs=[row(d), row(d // 2), row(LANES), row(LANES), const((8, LANES))],
        out_shape=[jax.ShapeDtypeStruct((n, d), F32), jax.ShapeDtypeStruct((n, d // 2), I32),
                   jax.ShapeDtypeStruct((n, LANES), I32), jax.ShapeDtypeStruct((n, LANES), F32),
                   jax.ShapeDtypeStruct((8, LANES), F32)],
        compiler_params=_cparams(("arbitrary",)),
        name="merge_router",
    )(act, og, gc, gg, x2d, gt1, sh2, sc2, gn, wc, wg, wo, wr, br)


_GLU_GROUP = 2 * LANES


def _deinterleave_matrix():
    src = lax.broadcasted_iota(I32, (_GLU_GROUP, _GLU_GROUP), 0)
    dst = lax.broadcasted_iota(I32, (_GLU_GROUP, _GLU_GROUP), 1)
    want = jnp.where(dst < LANES, 2 * dst, 2 * (dst - LANES) + 1)
    return jnp.where(src == want, 1.0, 0.0).astype(BF16)


def _expert_body(be_ref, nreal_ref, xp_ref, w1_ref, b1_ref, w2_ref, b2_ref, yp_ref,
                 w1s_ref, w2s_ref):
    i = pl.program_id(0)
    new_expert = jnp.logical_or(i == 0, be_ref[i] != be_ref[jnp.maximum(i - 1, 0)])

    @pl.when(jnp.logical_and(new_expert, i < nreal_ref[0]))
    def _():
        perm = _deinterleave_matrix()
        for g in range(2 * D_FF // _GLU_GROUP):
            cs = slice(g * _GLU_GROUP, (g + 1) * _GLU_GROUP)
            w1s_ref[:, cs] = _dot(w1_ref[0, :, cs].astype(BF16), perm).astype(BF16)
        w2s_ref[...] = w2_ref[0].astype(BF16)

    @pl.when(i < nreal_ref[0])
    def _():
        lo, hi = _unpack_pair(xp_ref[...])
        x = jnp.concatenate([lo, hi], axis=1).astype(BF16)
        hid = _dot(x, w1s_ref[...]) + b1_ref[0]
        ngrp = 2 * D_FF // _GLU_GROUP
        hg = jnp.concatenate(
            [hid[:, g * _GLU_GROUP:g * _GLU_GROUP + LANES] for g in range(ngrp)], axis=1)
        hl = jnp.concatenate(
            [hid[:, g * _GLU_GROUP + LANES:(g + 1) * _GLU_GROUP] for g in range(ngrp)], axis=1)
        xg = jnp.minimum(hg, SWIGLU_LIMIT)
        xl = jnp.clip(hl, -SWIGLU_LIMIT, SWIGLU_LIMIT)
        act = xg * _sigmoid(SWIGLU_ALPHA * xg) * (xl + 1.0)
        y = _dot(act.astype(BF16), w2s_ref[...]) + b2_ref[0]
        half = D_MODEL // 2
        yp_ref[...] = _pack_pair(y[:, :half], y[:, half:])

    @pl.when(i >= nreal_ref[0])
    def _():
        yp_ref[...] = jnp.zeros_like(yp_ref)


def _experts(blk_expert, n_real, xp, w1, b1, w2, b2, bm):
    p, w = xp.shape
    grid_spec = pltpu.PrefetchScalarGridSpec(
        num_scalar_prefetch=2,
        grid=(p // bm,),
        in_specs=[pl.BlockSpec((bm, w), lambda i, be, nr: (i, 0)),
                  pl.BlockSpec((1, D_MODEL, 2 * D_FF), lambda i, be, nr: (be[i], 0, 0)),
                  pl.BlockSpec((1, 1, 2 * D_FF), lambda i, be, nr: (be[i], 0, 0)),
                  pl.BlockSpec((1, D_FF, D_MODEL), lambda i, be, nr: (be[i], 0, 0)),
                  pl.BlockSpec((1, 1, D_MODEL), lambda i, be, nr: (be[i], 0, 0))],
        out_specs=pl.BlockSpec((bm, w), lambda i, be, nr: (i, 0)),
        scratch_shapes=[pltpu.VMEM((D_MODEL, 2 * D_FF), BF16), pltpu.VMEM((D_FF, D_MODEL), BF16)],
    )
    return pl.pallas_call(
        _expert_body,
        grid_spec=grid_spec,
        out_shape=jax.ShapeDtypeStruct((p, w), I32),
        compiler_params=_cparams(("arbitrary",)),
        name="moe_experts",
    )(blk_expert, n_real, xp, w1, b1, w2, b2)


_SC_CORES = 2
_SC_SUBCORES = 16
_SC_CHUNK = 64


def _sc_gather_rows(table, idx):
    rows, width = idx.shape[0], table.shape[1]
    workers = _SC_CORES * _SC_SUBCORES
    per_worker = rows // workers
    assert rows % (workers * _SC_CHUNK) == 0
    mesh = plsc.VectorSubcoreMesh(core_axis_name="c", subcore_axis_name="s")

    n_chunks = per_worker // _SC_CHUNK
    assert n_chunks % 2 == 0 and n_chunks >= 4

    def body(table_hbm, idx_hbm, out_hbm, idx0, idx1, rows0, rows1, gsem0, gsem1, wsem0, wsem1):
        wid = lax.axis_index("s") * _SC_CORES + lax.axis_index("c")
        base = wid * per_worker
        bufs = ((idx0, rows0, gsem0, wsem0), (idx1, rows1, gsem1, wsem1))

        def out_rows(c):
            return out_hbm.at[pl.ds(base + c * _SC_CHUNK, _SC_CHUNK)]

        def gather_start(c, b):
            idx_v, rows_v, gsem, _ = bufs[b]
            pltpu.sync_copy(idx_hbm.at[pl.ds(base + c * _SC_CHUNK, _SC_CHUNK)], idx_v)
            pltpu.make_async_copy(table_hbm.at[idx_v], rows_v, gsem).start()

        def gather_wait(b):
            idx_v, rows_v, gsem, _ = bufs[b]
            pltpu.make_async_copy(table_hbm.at[idx_v], rows_v, gsem).wait()

        def write_start(c, b):
            _, rows_v, _, wsem = bufs[b]
            pltpu.make_async_copy(rows_v, out_rows(c), wsem).start()

        def write_wait(c, b):
            _, rows_v, _, wsem = bufs[b]
            pltpu.make_async_copy(rows_v, out_rows(c), wsem).wait()

        gather_start(0, 0)
        gather_wait(0)
        write_start(0, 0)
        gather_start(1, 1)

        @pl.loop(1, n_chunks - 1, step=2)
        def _(c):
            gather_wait(1)
            write_start(c, 1)
            write_wait(c - 1, 0)
            gather_start(c + 1, 0)
            gather_wait(0)
            write_start(c + 1, 0)
            write_wait(c, 1)
            gather_start(c + 2, 1)

        gather_wait(1)
        write_start(n_chunks - 1, 1)
        write_wait(n_chunks - 2, 0)
        write_wait(n_chunks - 1, 1)

    chunk = lambda: pltpu.VMEM((_SC_CHUNK, width), table.dtype)
    return pl.kernel(
        body,
        out_type=jax.ShapeDtypeStruct((rows, width), table.dtype),
        mesh=mesh,
        scratch_types=[pltpu.VMEM((_SC_CHUNK,), I32), pltpu.VMEM((_SC_CHUNK,), I32), chunk(), chunk(),
                       pltpu.SemaphoreType.DMA, pltpu.SemaphoreType.DMA,
                       pltpu.SemaphoreType.DMA, pltpu.SemaphoreType.DMA],
        name="sc_gather_rows",
    )(table, idx)


def _sc_scatter_rows(rows, idx_slots, out_rows):
    n, width = rows.shape
    workers = _SC_CORES * _SC_SUBCORES
    per_worker = n // workers
    assert n % (workers * _SC_CHUNK) == 0
    mesh = plsc.VectorSubcoreMesh(core_axis_name="c", subcore_axis_name="s")

    def body(rows_hbm, idx_hbm, out_hbm, idx_v, rows_v):
        wid = lax.axis_index("s") * _SC_CORES + lax.axis_index("c")
        base = wid * per_worker

        @pl.loop(0, per_worker // _SC_CHUNK)
        def _(i):
            t0 = base + i * _SC_CHUNK
            pltpu.sync_copy(rows_hbm.at[pl.ds(t0, _SC_CHUNK)], rows_v)
            for kk in range(TOP_K):
                pltpu.sync_copy(idx_hbm.at[pl.ds(kk * n + t0, _SC_CHUNK)], idx_v)
                pltpu.sync_copy(rows_v, out_hbm.at[idx_v])

    return pl.kernel(
        body,
        out_type=jax.ShapeDtypeStruct((out_rows, width), rows.dtype),
        mesh=mesh,
        scratch_types=[pltpu.VMEM((_SC_CHUNK,), I32), pltpu.VMEM((_SC_CHUNK, width), rows.dtype)],
        name="sc_scatter_rows",
    )(rows, idx_slots)


def _combine_dense_body(y4_ref, tp_ref, x1_ref, gt_ref, gf_ref, o_ref):
    half = D_MODEL // 2
    tp = tp_ref[...]
    y_lo = y_hi = None
    for kk in range(TOP_K):
        lo, hi = _unpack_pair(y4_ref[kk])
        pk = tp[:, kk:kk + 1]
        y_lo = pk * lo if y_lo is None else y_lo + pk * lo
        y_hi = pk * hi if y_hi is None else y_hi + pk * hi
    x2 = x1_ref[...] + gt_ref[0] * jnp.concatenate([y_lo, y_hi], axis=1)
    ms = jnp.mean(x2 * x2, axis=-1, keepdims=True)
    o_ref[...] = x2 * lax.rsqrt(ms + EPS) * gf_ref[...]


def _combine_dense(y4, tp, x1, gt2, gf, tokens, rows_per_batch):
    n, d = x1.shape
    per_b = rows_per_batch // tokens
    return pl.pallas_call(
        _combine_dense_body,
        grid=(n // tokens,),
        in_specs=[pl.BlockSpec((TOP_K, tokens, d // 2), lambda i: (0, i, 0)),
                  pl.BlockSpec((tokens, LANES), lambda i: (i, 0)),
                  pl.BlockSpec((tokens, d), lambda i: (i, 0)),
                  pl.BlockSpec((1, 1, d), lambda i: (i // per_b, 0, 0)),
                  pl.BlockSpec((1, d), lambda i: (0, 0))],
        out_specs=pl.BlockSpec((tokens, d), lambda i: (i, 0)),
        out_shape=jax.ShapeDtypeStruct((n, d), F32),
        compiler_params=_cparams(("arbitrary",)),
        name="moe_combine_dense",
    )(y4, tp, x1, gt2, gf)


def _routing_tables(top_idx, rank, counts, bm):
    n = top_idx.shape[0]
    nk = n * TOP_K
    padded = (counts + bm - 1) // bm * bm
    pad_end = jnp.cumsum(padded)
    pad_start = pad_end - padded
    dest = (pad_start[top_idx] + rank).astype(I32)
    n_blocks = (nk + N_EXPERTS * (bm - 1) + bm - 1) // bm
    starts = jnp.arange(n_blocks, dtype=I32) * bm
    blk_expert = jnp.minimum(jnp.sum((pad_end[None, :] <= starts[:, None]).astype(I32), axis=1),
                             N_EXPERTS - 1).astype(I32)
    n_real = (pad_end[-1] // bm).astype(I32).reshape(1)
    return dest, blk_expert, n_real, n_blocks


def _layer(x, c, ctx, c_ctx, w_ada, b_ada, g_mix_norm, w_in, w_dw, b_dw, g_conv_ln, b_conv_ln,
           w_conv_out, w_alpha, b_alpha, g_gla_norm, w_gla_out, w_out, g_ffn_norm, w_router,
           b_router, w_exp_in, b_exp_in, w_exp_out, b_exp_out, g_final, *, cfg):
    b, s, d = x.shape
    n = b * s

    rows = (b + 1 + 7) // 8 * 8
    cc = jnp.zeros((rows, d), F32).at[:b].set(c).at[b].set(c_ctx)
    mod = _ada(cc, w_ada, b_ada)
    sh1, sc1, gt1, sh2, sc2, gt2 = [mod[:b, i * d:(i + 1) * d].reshape(b, 1, d) for i in range(6)]
    csh1 = mod[b:b + 1, 0:d]
    csc1 = mod[b:b + 1, d:2 * d]

    a0 = 2 * CONV_W + 2 * GLA_KD + 2 * GLA_VD
    w_in_r = jnp.concatenate(
        [w_in[:, :a0], w_in[:, a0 + 2 * GLA_RANK:], w_in[:, a0:a0 + 2 * GLA_RANK],
         jnp.zeros((d, LANES - 2 * GLA_RANK), F32)], axis=1).astype(BF16)
    gmn = g_mix_norm.reshape(1, d)

    act, q, k, v, sg, gc, gg, a = _inproj_lat(x, gmn, sh1, sc1, w_in_r, w_dw, b_dw, g_conv_ln,
                                              b_conv_ln, cfg["tm_in"])
    act = act.reshape(n, CONV_W)
    kc, vc, ac = _inproj_ctx(ctx, gmn, csh1, csc1, w_in_r, cfg["tm_ctx"])

    cps = cfg["gla_cps"]
    st_b = _gla_ctx(kc, vc, ac, w_alpha[1], b_alpha[1], True, cps)
    o_b = _gla_lat(k, v, a, q, w_alpha[1], b_alpha[1], st_b, True, cps)
    st_f = _gla_ctx(kc, vc, ac, w_alpha[0], b_alpha[0], False, cps)
    og = _gla_lat(k, v, a, q, w_alpha[0], b_alpha[0], st_f, False, cps,
                  sg=sg, o_prev=o_b, g_norm=g_gla_norm)

    wr = jnp.zeros((d, LANES), F32).at[:, :N_EXPERTS].set(w_router)
    br = jnp.zeros((1, LANES), F32).at[0, :N_EXPERTS].set(b_router)
    x1, h2p, ti, tp, cnt = _merge(
        act, og.reshape(n, d), gc.reshape(n, d), gg.reshape(n, d), x.reshape(n, d),
        gt1, sh2, sc2, g_ffn_norm.reshape(1, d),
        w_conv_out.astype(BF16), w_gla_out.astype(BF16), w_out.astype(BF16), wr, br,
        cfg["tm_merge"], s)

    bm = cfg["moe_block"]
    dest, blk_expert, n_real, n_blocks = _routing_tables(
        ti[:, :TOP_K], ti[:, TOP_K:2 * TOP_K], cnt[0, :N_EXPERTS].astype(I32), bm)
    dest_slots = dest.T.reshape(n * TOP_K)
    xp = _sc_scatter_rows(h2p, dest_slots, n_blocks * bm)

    b1 = b_exp_in.reshape(N_EXPERTS, 2 * D_FF // _GLU_GROUP, LANES, 2).transpose(0, 1, 3, 2)
    b1 = b1.reshape(N_EXPERTS, 1, 2 * D_FF)
    yp = _experts(blk_expert, n_real, xp, w_exp_in, b1, w_exp_out,
                  b_exp_out.reshape(N_EXPERTS, 1, d), bm)

    y4 = _sc_gather_rows(yp, dest_slots).reshape(TOP_K, n, d // 2)
    out = _combine_dense(y4, tp, x1, gt2, g_final.reshape(1, d), cfg["tm_combine"], s)
    return out.reshape(b, s, d)


def _config(s, l):
    return dict(tm_in=min(512, s), tm_ctx=min(256, l), gla_cps=2,
                tm_merge=min(512, s), moe_block=512, tm_combine=min(256, s))


def kernel(x, c, ctx, c_ctx, w_ada, b_ada, g_mix_norm, w_in, w_dw, b_dw, g_conv_ln, b_conv_ln,
           w_conv_out, w_alpha, b_alpha, g_gla_norm, w_gla_out, w_out, g_ffn_norm, w_router,
           b_router, w_exp_in, b_exp_in, w_exp_out, b_exp_out, g_final):
    depth = w_ada.shape[0]
    assert depth == 1, "single-layer block: the context stream is only consumed by the GLA scan"
    cfg = _config(x.shape[1], ctx.shape[1])
    return _layer(x, c, ctx, c_ctx, w_ada[0], b_ada[0], g_mix_norm[0], w_in[0], w_dw[0], b_dw[0],
                  g_conv_ln[0], b_conv_ln[0], w_conv_out[0], w_alpha[0], b_alpha[0],
                  g_gla_norm[0], w_gla_out[0], w_out[0], g_ffn_norm[0], w_router[0], b_router[0],
                  w_exp_in[0], b_exp_in[0], w_exp_out[0], b_exp_out[0], g_final, cfg=cfg)
```

```python
import functools

import jax
import jax.numpy as jnp
from jax import lax
from jax.experimental import pallas as pl
from jax.experimental.pallas import tpu as pltpu
from jax.experimental.pallas import tpu_sc as plsc

F32 = jnp.float32
BF16 = jnp.bfloat16
I32 = jnp.int32

D_MODEL = 1024
GRID_W = 64
EPS = 1e-6
CONV_W = 1024
CONV_K = 31
GLA_H = 4
GLA_DK = 128
GLA_DV = 256
GLA_KD = GLA_H * GLA_DK
GLA_VD = GLA_H * GLA_DV
GLA_RANK = 16
GLA_TAU = 16.0
GLA_CHUNK = 64
N_EXPERTS = 32
TOP_K = 4
D_FF = 1024
SWIGLU_ALPHA = 1.702
SWIGLU_LIMIT = 7.0

LANES = 128
VMEM_LIMIT = 56 * 1024 * 1024

_C_CONV_A = 0
_C_CONV_B = _C_CONV_A + CONV_W
_C_Q = _C_CONV_B + CONV_W
_C_K = _C_Q + GLA_KD
_C_V = _C_K + GLA_KD
_C_G = _C_V + GLA_VD
_C_GC = _C_G + GLA_VD
_C_GG = _C_GC + D_MODEL
_C_A = _C_GG + D_MODEL
_C_END = _C_A + LANES


def _cparams(sem):
    return pltpu.CompilerParams(dimension_semantics=sem, vmem_limit_bytes=VMEM_LIMIT)


def _dot(a, b):
    return jnp.dot(a, b, preferred_element_type=F32)


def _split_bf16(x):
    hi = x.astype(BF16)
    lo = (x - hi.astype(F32)).astype(BF16)
    return hi, lo


def _dot3(a, b):
    a_hi, a_lo = _split_bf16(a)
    b_hi, b_lo = _split_bf16(b)
    return _dot(a_hi, b_hi) + _dot(a_lo, b_hi) + _dot(a_hi, b_lo)


def _sigmoid(x):
    return 1.0 / (1.0 + jnp.exp(-x))


def _resident(shape):
    nd = len(shape)
    return pl.BlockSpec(shape, lambda *_: (0,) * nd, pipeline_mode=pl.Buffered(1))


def _ada_body(a_ref, w_ref, b_ref, o_ref):
    a = a_ref[...]
    a = a * _sigmoid(a)
    o_ref[...] = _dot3(a, w_ref[...]) + b_ref[...]


def _ada(cc, w, b):
    rows, d = cc.shape
    n = w.shape[1]
    tn = 512
    return pl.pallas_call(
        _ada_body,
        grid=(n // tn,),
        in_specs=[pl.BlockSpec((rows, d), lambda j: (0, 0)),
                  pl.BlockSpec((d, tn), lambda j: (0, j)),
                  pl.BlockSpec((1, tn), lambda j: (0, j))],
        out_specs=pl.BlockSpec((rows, tn), lambda j: (0, j)),
        out_shape=jax.ShapeDtypeStruct((rows, n), F32),
        compiler_params=_cparams(("arbitrary",)),
        name="ada_mod",
    )(cc, w, b.reshape(1, n))


def _norm_mod(xv, gn, sc, sh):
    ms = jnp.mean(xv * xv, axis=-1, keepdims=True)
    y = xv * lax.rsqrt(ms + EPS) * gn
    return (y * (1.0 + sc) + sh).astype(BF16)


def _inproj_lat_body(x_ref, gn_ref, sh_ref, sc_ref, w_ref, sm_ref, wdw_ref, bdw_ref, gln_ref, bln_ref,
                     act_ref, q_ref, k_ref, v_ref, sg_ref, gc_ref, gg_ref, a_ref,
                     win_ref, y_ref):
    h = _norm_mod(x_ref[0], gn_ref[...], sc_ref[0], sh_ref[0])
    ca = _dot(h, w_ref[:, _C_CONV_A:_C_CONV_B])
    cb = _dot(h, w_ref[:, _C_CONV_B:_C_Q])
    u = (ca * _sigmoid(cb)).astype(BF16)
    q_ref[0] = _dot(h, w_ref[:, _C_Q:_C_K]).astype(BF16)
    k_ref[0] = _dot(h, w_ref[:, _C_K:_C_V]).astype(BF16)
    v_ref[0] = _dot(h, w_ref[:, _C_V:_C_G]).astype(BF16)
    g = _dot(h, w_ref[:, _C_G:_C_GC])
    sg_ref[0] = (g * _sigmoid(g)).astype(BF16)
    gc_ref[0] = _sigmoid(_dot(h, w_ref[:, _C_GC:_C_GG])).astype(BF16)
    gg_ref[0] = _sigmoid(_dot(h, w_ref[:, _C_GG:_C_A])).astype(BF16)
    a_ref[0] = _dot(h, w_ref[:, _C_A:_C_END])
    act_ref[0] = _conv_ln_swish(u, sm_ref, wdw_ref, bdw_ref, gln_ref, bln_ref, win_ref, y_ref)


def _inproj_ctx_body(x_ref, gn_ref, sh_ref, sc_ref, w_ref, k_ref, v_ref, a_ref):
    h = _norm_mod(x_ref[0], gn_ref[...], sc_ref[...], sh_ref[...])
    k_ref[0] = _dot(h, w_ref[:, _C_K:_C_V]).astype(BF16)
    v_ref[0] = _dot(h, w_ref[:, _C_V:_C_G]).astype(BF16)
    a_ref[0] = _dot(h, w_ref[:, _C_A:_C_END])


def _inproj_lat(x, gn, sh, sc, w, w_dw, b_dw, g_ln, b_ln, tm):
    b, s, d = x.shape
    row = lambda n: pl.BlockSpec((1, tm, n), lambda bi, i: (bi, i, 0))
    vec = pl.BlockSpec((1, 1, d), lambda bi, i: (bi, 0, 0))
    const = lambda shape: pl.BlockSpec(shape, lambda bi, i: (0,) * len(shape))
    shp = lambda n, dt: jax.ShapeDtypeStruct((b, s, n), dt)
    sm = _conv_shift_matrix()
    wpad = jnp.zeros((32, CONV_W), F32).at[:CONV_K].set(w_dw)
    return pl.pallas_call(
        _inproj_lat_body,
        grid=(b, s // tm),
        in_specs=[row(d), const((1, d)), vec, vec, _resident(w.shape),
                  const(sm.shape), const(wpad.shape), const((1, CONV_W)), const((1, CONV_W)),
                  const((1, CONV_W))],
        out_specs=[row(CONV_W), row(GLA_KD), row(GLA_KD), row(GLA_VD), row(GLA_VD),
                   row(d), row(d), row(LANES)],
        out_shape=[shp(CONV_W, BF16), shp(GLA_KD, BF16), shp(GLA_KD, BF16), shp(GLA_VD, BF16),
                   shp(GLA_VD, BF16), shp(d, BF16), shp(d, BF16), shp(LANES, F32)],
        scratch_shapes=[pltpu.VMEM((2, _CONV_SUB * _CONV_SPAN, CONV_W), F32),
                        pltpu.VMEM((tm, CONV_W), F32)],
        compiler_params=_cparams(("arbitrary", "arbitrary")),
        name="inproj_lat",
    )(x, gn, sh, sc, w, sm, wpad, b_dw.reshape(1, CONV_W), g_ln.reshape(1, CONV_W),
      b_ln.reshape(1, CONV_W))


def _inproj_ctx(ctx, gn, sh, sc, w, tm):
    b, l, d = ctx.shape
    row = lambda n: pl.BlockSpec((1, tm, n), lambda bi, i: (bi, i, 0))
    vec = pl.BlockSpec((1, d), lambda bi, i: (0, 0))
    shp = lambda n, dt: jax.ShapeDtypeStruct((b, l, n), dt)
    return pl.pallas_call(
        _inproj_ctx_body,
        grid=(b, l // tm),
        in_specs=[row(d), vec, vec, vec, _resident(w.shape)],
        out_specs=[row(GLA_KD), row(GLA_VD), row(LANES)],
        out_shape=[shp(GLA_KD, BF16), shp(GLA_VD, BF16), shp(LANES, F32)],
        compiler_params=_cparams(("arbitrary", "arbitrary")),
        name="inproj_ctx",
    )(ctx, gn, sh, sc, w)


_CONV_SUB = 8
_CONV_SPAN = GRID_W + _CONV_SUB * ((CONV_K - 1) // _CONV_SUB)


def _conv_shift_matrix():
    row = lax.broadcasted_iota(I32, (_CONV_SUB, _CONV_SPAN, GRID_W), 1)
    shift = lax.broadcasted_iota(I32, (_CONV_SUB, _CONV_SPAN, GRID_W), 0)
    col = lax.broadcasted_iota(I32, (_CONV_SUB, _CONV_SPAN, GRID_W), 2)
    m = jnp.where(col == row + shift - CONV_K // 2, 1.0, 0.0)
    return m.reshape(_CONV_SUB * _CONV_SPAN, GRID_W).astype(BF16)


def _conv_ln_swish(u, sm_ref, w_ref, bdw_ref, gln_ref, bln_ref, win_ref, y_ref):
    for r in range(u.shape[0] // GRID_W):
        buf = r % 2
        win_ref[buf] = _dot(sm_ref[...], u[r * GRID_W:(r + 1) * GRID_W, :])
        for cb in range(CONV_W // LANES):
            ls = slice(cb * LANES, (cb + 1) * LANES)
            acc = None
            for k in range(CONV_K):
                s, a = k % _CONV_SUB, k // _CONV_SUB
                r0 = s * _CONV_SPAN + _CONV_SUB * a
                term = win_ref[buf, r0:r0 + GRID_W, ls] * w_ref[k:k + 1, ls]
                acc = term if acc is None else acc + term
            y_ref[r * GRID_W:(r + 1) * GRID_W, ls] = acc + bdw_ref[:, ls]
    y = y_ref[...]
    mu = jnp.mean(y, axis=-1, keepdims=True)
    yc = y - mu
    var = jnp.mean(yc * yc, axis=-1, keepdims=True)
    yn = yc * lax.rsqrt(var + EPS) * gln_ref[...] + bln_ref[...]
    return (yn * _sigmoid(yn)).astype(BF16)


def _log_sigmoid(z):
    return jnp.minimum(z, 0.0) - jnp.log(1.0 + jnp.exp(-jnp.abs(z)))


def _gla_keep_mask(t, reverse):
    row = lax.broadcasted_iota(I32, (t, t), 0)
    col = lax.broadcasted_iota(I32, (t, t), 1)
    same_chunk = lax.shift_right_logical(row, 6) == lax.shift_right_logical(col, 6)
    return jnp.logical_and(same_chunk, (col >= row) if reverse else (col <= row))


def _gla_decay(a, wh_ref, wl_ref, ba, keep):
    a_hi, a_lo = _split_bf16(a)
    z = _dot(jnp.concatenate([a_hi, a_lo], axis=1), wh_ref[...]) + _dot(a_hi, wl_ref[...]) + ba
    loga = _log_sigmoid(z) * (1.0 / GLA_TAU)
    l_hi, l_lo = _split_bf16(loga)
    tri = jnp.where(keep, 1.0, 0.0).astype(BF16)
    return _dot(tri, l_hi) + _dot(tri, l_lo)


def _gla_step(k, v, bcum, keep, states, *, reverse, q=None):
    t = k.shape[0]
    c = GLA_CHUNK
    nch = t // c
    tots = [bcum[n * c:n * c + 1, :] if reverse else bcum[(n + 1) * c - 1:(n + 1) * c, :]
            for n in range(nch)]
    totb = jnp.concatenate([jnp.broadcast_to(tt, (c, GLA_KD)) for tt in tots], axis=0)
    k32 = k.astype(F32)
    k_end = (k32 * jnp.exp(totb - bcum)).astype(BF16)
    decs = [jnp.exp(tt) for tt in tots]
    if q is not None:
        q_dec = (q.astype(F32) * jnp.exp(bcum) * (GLA_DK ** -0.5)).astype(BF16)
        k_inv = (k32 * jnp.exp(-bcum)).astype(BF16)
    order = range(nch - 1, -1, -1) if reverse else range(nch)
    outs, new_states = [], []
    for h in range(GLA_H):
        ks = slice(h * GLA_DK, (h + 1) * GLA_DK)
        vh = v[:, h * GLA_DV:(h + 1) * GLA_DV]
        st = states[h]
        if q is not None:
            scores = lax.dot_general(q_dec[:, ks], k_inv[:, ks], (((1,), (1,)), ((), ())),
                                     preferred_element_type=F32)
            o_h = _dot(jnp.where(keep, scores, 0.0).astype(BF16), vh)
            inter = [None] * nch
        for n in order:
            rs = slice(n * c, (n + 1) * c)
            if q is not None:
                inter[n] = _dot(q_dec[rs, ks], st.astype(BF16))
            kv = lax.dot_general(k_end[rs, ks], vh[rs, :], (((0,), (0,)), ((), ())),
                                 preferred_element_type=F32)
            dt = jnp.transpose(jnp.broadcast_to(decs[n][:, ks], (GLA_DK, GLA_DK)))
            st = st * jnp.concatenate([dt] * (GLA_DV // GLA_DK), axis=1) + kv
        new_states.append(st)
        if q is not None:
            outs.append(o_h + jnp.concatenate(inter, axis=0))
    return new_states, (jnp.concatenate(outs, axis=1) if q is not None else None)


def _gla_ctx_body(k_ref, v_ref, a_ref, wh_ref, wl_ref, ba_ref, st_ref, *, reverse):
    j = pl.program_id(1)

    @pl.when(j == 0)
    def _():
        st_ref[...] = jnp.zeros_like(st_ref)

    keep = _gla_keep_mask(k_ref.shape[1], reverse)
    bcum = _gla_decay(a_ref[0], wh_ref, wl_ref, ba_ref[...], keep)
    states, _ = _gla_step(k_ref[0], v_ref[0], bcum, keep,
                          [st_ref[0, h] for h in range(GLA_H)], reverse=reverse)
    for h in range(GLA_H):
        st_ref[0, h] = states[h]


def _gla_lat_body(*refs, reverse, final):
    if final:
        (k_ref, v_ref, a_ref, an_ref, q_ref, wh_ref, wl_ref, ba_ref, s0_ref, sg_ref, op_ref,
         gn_ref, o_ref, st_ref, bc_ref) = refs
    else:
        (k_ref, v_ref, a_ref, an_ref, q_ref, wh_ref, wl_ref, ba_ref, s0_ref,
         o_ref, st_ref, bc_ref) = refs
    j = pl.program_id(1)
    nb, t = k_ref.shape[0], k_ref.shape[1]
    keep = _gla_keep_mask(t, reverse)

    @pl.when(j == 0)
    def _():
        st_ref[...] = s0_ref[...]
        for bb in range(nb):
            bc_ref[bb] = _gla_decay(a_ref[bb], wh_ref, wl_ref, ba_ref[...], keep)

    for bb in range(nb):
        states, o = _gla_step(k_ref[bb], v_ref[bb], bc_ref[bb], keep,
                              [st_ref[bb, h] for h in range(GLA_H)], reverse=reverse,
                              q=q_ref[bb])
        bc_ref[bb] = _gla_decay(an_ref[bb], wh_ref, wl_ref, ba_ref[...], keep)
        for h in range(GLA_H):
            st_ref[bb, h] = states[h]
        if not final:
            o_ref[bb] = o.astype(BF16)
            continue
        o = o + op_ref[bb].astype(F32)
        parts = []
        for h in range(GLA_H):
            oh = o[:, h * GLA_DV:(h + 1) * GLA_DV]
            ms = jnp.mean(oh * oh, axis=-1, keepdims=True)
            parts.append(oh * lax.rsqrt(ms + EPS) * gn_ref[...])
        o_ref[bb] = (jnp.concatenate(parts, axis=1) * sg_ref[bb].astype(F32)).astype(BF16)


def _gla_decay_weights(w_alpha_d, reverse):
    a0 = GLA_RANK if reverse else 0
    w = jnp.zeros((LANES, GLA_KD), F32).at[a0:a0 + GLA_RANK].set(w_alpha_d)
    hi = w.astype(BF16)
    lo = (w - hi.astype(F32)).astype(BF16)
    return jnp.concatenate([hi, hi], axis=0), lo


def _gla_ctx(k, v, a, w_alpha_d, ba, reverse, cps):
    b, l, _ = k.shape
    tm = cps * GLA_CHUNK
    nj = l // tm
    jmap = (lambda j: nj - 1 - j) if reverse else (lambda j: j)
    row = lambda n: pl.BlockSpec((1, tm, n), lambda bi, j: (bi, jmap(j), 0))
    wh, wl = _gla_decay_weights(w_alpha_d, reverse)
    return pl.pallas_call(
        functools.partial(_gla_ctx_body, reverse=reverse),
        grid=(b, nj),
        in_specs=[row(GLA_KD), row(GLA_VD), row(LANES),
                  pl.BlockSpec(wh.shape, lambda bi, j: (0, 0)),
                  pl.BlockSpec(wl.shape, lambda bi, j: (0, 0)),
                  pl.BlockSpec((1, GLA_KD), lambda bi, j: (0, 0))],
        out_specs=pl.BlockSpec((1, GLA_H, GLA_DK, GLA_DV), lambda bi, j: (bi, 0, 0, 0)),
        out_shape=jax.ShapeDtypeStruct((b, GLA_H, GLA_DK, GLA_DV), F32),
        compiler_params=_cparams(("arbitrary", "arbitrary")),
        name="gla_ctx_bwd" if reverse else "gla_ctx_fwd",
    )(k, v, a, wh, wl, ba.reshape(1, GLA_KD))


def _gla_lat(k, v, a, q, w_alpha_d, ba, s0, reverse, cps, sg=None, o_prev=None, g_norm=None):
    b, s, _ = k.shape
    final = sg is not None
    tm = cps * GLA_CHUNK
    nj = s // tm
    jmap = (lambda j: nj - 1 - j) if reverse else (lambda j: j)
    nb = 4 if b % 4 == 0 else 2 if b % 2 == 0 else 1
    row = lambda n: pl.BlockSpec((nb, tm, n), lambda bi, j: (bi, jmap(j), 0))
    wh, wl = _gla_decay_weights(w_alpha_d, reverse)
    a_next = pl.BlockSpec((nb, tm, LANES), lambda bi, j: (bi, jmap(jnp.minimum(j + 1, nj - 1)), 0))
    in_specs = [row(GLA_KD), row(GLA_VD), row(LANES), a_next, row(GLA_KD),
                pl.BlockSpec(wh.shape, lambda bi, j: (0, 0)),
                pl.BlockSpec(wl.shape, lambda bi, j: (0, 0)),
                pl.BlockSpec((1, GLA_KD), lambda bi, j: (0, 0)),
                pl.BlockSpec((nb, GLA_H, GLA_DK, GLA_DV), lambda bi, j: (bi, 0, 0, 0))]
    args = [k, v, a, a, q, wh, wl, ba.reshape(1, GLA_KD), s0]
    if final:
        in_specs += [row(GLA_VD), row(GLA_VD), pl.BlockSpec((1, GLA_DV), lambda bi, j: (0, 0))]
        args += [sg, o_prev, g_norm.reshape(1, GLA_DV)]
    return pl.pallas_call(
        functools.partial(_gla_lat_body, reverse=reverse, final=final),
        grid=(b // nb, nj),
        in_specs=in_specs,
        out_specs=row(GLA_VD),
        out_shape=jax.ShapeDtypeStruct((b, s, GLA_VD), BF16),
        scratch_shapes=[pltpu.VMEM((nb, GLA_H, GLA_DK, GLA_DV), F32),
                        pltpu.VMEM((nb, tm, GLA_KD), F32)],
        compiler_params=_cparams(("arbitrary", "arbitrary")),
        name="gla_lat_bwd" if reverse else "gla_lat_fwd",
    )(*args)


def _pack_pair(lo, hi):
    lo_b = pltpu.bitcast(lo.astype(BF16).astype(F32), I32)
    hi_b = pltpu.bitcast(hi.astype(BF16).astype(F32), I32)
    return lax.shift_right_logical(lo_b, 16) | (hi_b & jnp.int32(-65536))


def _unpack_pair(w):
    lo = pltpu.bitcast(lax.shift_left(w, 16), F32)
    hi = pltpu.bitcast(w & jnp.int32(-65536), F32)
    return lo, hi


def _merge_body(act_ref, og_ref, gc_ref, gg_ref, x_ref, gt_ref, sh_ref, sc_ref, gn_ref,
                wc_ref, wg_ref, wo_ref, wr_ref, br_ref,
                x1_ref, h2_ref, ti_ref, tp_ref, cnt_ref):
    yc = _dot(act_ref[...], wc_ref[...])
    yg = _dot(og_ref[...], wg_ref[...])
    m = gc_ref[...].astype(F32) * yc + gg_ref[...].astype(F32) * yg
    x1 = x_ref[...] + gt_ref[0] * _dot(m.astype(BF16), wo_ref[...])
    x1_ref[...] = x1
    ms = jnp.mean(x1 * x1, axis=-1, keepdims=True)
    h2 = (x1 * lax.rsqrt(ms + EPS) * gn_ref[...]) * (1.0 + sc_ref[0]) + sh_ref[0]
    half = D_MODEL // 2
    h2_ref[...] = _pack_pair(h2[:, :half], h2[:, half:])
    logits = _dot3(h2, wr_ref[...]) + br_ref[...]
    lane = lax.broadcasted_iota(I32, logits.shape, 1).astype(F32)
    neg = jnp.float32(-jnp.inf)
    work = jnp.where(lane < N_EXPERTS, logits, neg)
    vals, idxs = [], []
    for _ in range(TOP_K):
        mx = jnp.max(work, axis=-1, keepdims=True)
        ix = jnp.min(jnp.where(work == mx, lane, float(LANES)), axis=-1, keepdims=True)
        vals.append(mx)
        idxs.append(ix)
        work = jnp.where(lane == ix, neg, work)
    es = [jnp.exp(v - vals[0]) for v in vals]
    den = es[0] + es[1] + es[2] + es[3]
    ti = jnp.zeros(logits.shape, F32)
    tp = jnp.zeros(logits.shape, F32)
    onehot = jnp.zeros(logits.shape, F32)
    for kk in range(TOP_K):
        ti = jnp.where(lane == kk, idxs[kk], ti)
        tp = jnp.where(lane == kk, es[kk] / den, tp)
        onehot = onehot + jnp.where(lane == idxs[kk], 1.0, 0.0)
    @pl.when(pl.program_id(0) == 0)
    def _():
        cnt_ref[...] = jnp.zeros_like(cnt_ref)

    tm = logits.shape[0]
    earlier = (lax.broadcasted_iota(I32, (tm, tm), 1) < lax.broadcasted_iota(I32, (tm, tm), 0))
    before = _dot(jnp.where(earlier, 1.0, 0.0).astype(BF16), onehot.astype(BF16)) + cnt_ref[0:1, :]
    for kk in range(TOP_K):
        rank = jnp.sum(jnp.where(lane == idxs[kk], before, 0.0), axis=-1, keepdims=True)
        ti = jnp.where(lane == TOP_K + kk, rank, ti)
    cnt_ref[...] = cnt_ref[...] + jnp.sum(onehot, axis=0, keepdims=True)
    ti_ref[...] = ti.astype(I32)
    tp_ref[...] = tp


def _merge(act, og, gc, gg, x2d, gt1, sh2, sc2, gn, wc, wg, wo, wr, br, tm, rows_per_batch,
           first_row, n):
    d = x2d.shape[1]
    per_b = rows_per_batch // tm
    off = first_row // tm
    row_in = lambda w: pl.BlockSpec((tm, w), lambda i: (i + off, 0))
    row = lambda w: pl.BlockSpec((tm, w), lambda i: (i, 0))
    vec = pl.BlockSpec((1, 1, d), lambda i: ((i + off) // per_b, 0, 0))
    const = lambda shape: pl.BlockSpec(shape, lambda i: (0,) * len(shape))
    return pl.pallas_call(
        _merge_body,
        grid=(n // tm,),
        in_specs=[row_in(d), row_in(d), row_in(d), row_in(d), row_in(d), vec, vec, vec,
                  const((1, d)),
                  _resident(wc.shape), _resident(wg.shape), _resident(wo.shape),
                  const(wr.shape), const((1, LANES))],
        out_specs=[row(d), row(d // 2), row(LANES), row(LANES), const((8, LANES))],
        out_shape=[jax.ShapeDtypeStruct((n, d), F32), jax.ShapeDtypeStruct((n, d // 2), I32),
                   jax.ShapeDtypeStruct((n, LANES), I32), jax.ShapeDtypeStruct((n, LANES), F32),
                   jax.ShapeDtypeStruct((8, LANES), F32)],
        compiler_params=_cparams(("arbitrary",)),
        name="merge_router",
    )(act, og, gc, gg, x2d, gt1, sh2, sc2, gn, wc, wg, wo, wr, br)


_GLU_GROUP = 2 * LANES


def _deinterleave_matrix():
    src = lax.broadcasted_iota(I32, (_GLU_GROUP, _GLU_GROUP), 0)
    dst = lax.broadcasted_iota(I32, (_GLU_GROUP, _GLU_GROUP), 1)
    want = jnp.where(dst < LANES, 2 * dst, 2 * (dst - LANES) + 1)
    return jnp.where(src == want, 1.0, 0.0).astype(BF16)


def _expert_body(be_ref, nreal_ref, xp_ref, w1_ref, b1_ref, w2_ref, b2_ref, yp_ref,
                 w1s_ref, w2s_ref):
    i = pl.program_id(0)
    new_expert = jnp.logical_or(i == 0, be_ref[i] != be_ref[jnp.maximum(i - 1, 0)])

    @pl.when(jnp.logical_and(new_expert, i < nreal_ref[0]))
    def _():
        perm = _deinterleave_matrix()
        for g in range(2 * D_FF // _GLU_GROUP):
            cs = slice(g * _GLU_GROUP, (g + 1) * _GLU_GROUP)
            w1s_ref[:, cs] = _dot(w1_ref[0, :, cs].astype(BF16), perm).astype(BF16)
        w2s_ref[...] = w2_ref[0].astype(BF16)

    @pl.when(i < nreal_ref[0])
    def _():
        lo, hi = _unpack_pair(xp_ref[...])
        x = jnp.concatenate([lo, hi], axis=1).astype(BF16)
        hid = _dot(x, w1s_ref[...]) + b1_ref[0]
        ngrp = 2 * D_FF // _GLU_GROUP
        hg = jnp.concatenate(
            [hid[:, g * _GLU_GROUP:g * _GLU_GROUP + LANES] for g in range(ngrp)], axis=1)
        hl = jnp.concatenate(
            [hid[:, g * _GLU_GROUP + LANES:(g + 1) * _GLU_GROUP] for g in range(ngrp)], axis=1)
        xg = jnp.minimum(hg, SWIGLU_LIMIT)
        xl = jnp.clip(hl, -SWIGLU_LIMIT, SWIGLU_LIMIT)
        act = xg * _sigmoid(SWIGLU_ALPHA * xg) * (xl + 1.0)
        y = _dot(act.astype(BF16), w2s_ref[...]) + b2_ref[0]
        half = D_MODEL // 2
        yp_ref[...] = _pack_pair(y[:, :half], y[:, half:])

    @pl.when(i >= nreal_ref[0])
    def _():
        yp_ref[...] = jnp.zeros_like(yp_ref)


def _experts(blk_expert, n_real, xp, w1, b1, w2, b2, bm):
    p, w = xp.shape
    grid_spec = pltpu.PrefetchScalarGridSpec(
        num_scalar_prefetch=2,
        grid=(p // bm,),
        in_specs=[pl.BlockSpec((bm, w), lambda i, be, nr: (i, 0)),
                  pl.BlockSpec((1, D_MODEL, 2 * D_FF), lambda i, be, nr: (be[i], 0, 0)),
                  pl.BlockSpec((1, 1, 2 * D_FF), lambda i, be, nr: (be[i], 0, 0)),
                  pl.BlockSpec((1, D_FF, D_MODEL), lambda i, be, nr: (be[i], 0, 0)),
                  pl.BlockSpec((1, 1, D_MODEL), lambda i, be, nr: (be[i], 0, 0))],
        out_specs=pl.BlockSpec((bm, w), lambda i, be, nr: (i, 0)),
        scratch_shapes=[pltpu.VMEM((D_MODEL, 2 * D_FF), BF16), pltpu.VMEM((D_FF, D_MODEL), BF16)],
    )
    return pl.pallas_call(
        _expert_body,
        grid_spec=grid_spec,
        out_shape=jax.ShapeDtypeStruct((p, w), I32),
        compiler_params=_cparams(("arbitrary",)),
        name="moe_experts",
    )(blk_expert, n_real, xp, w1, b1, w2, b2)


_SC_CORES = 2
_SC_SUBCORES = 16
_SC_CHUNK = 64


def _sc_gather_rows(table, idx):
    rows, width = idx.shape[0], table.shape[1]
    workers = _SC_CORES * _SC_SUBCORES
    per_worker = rows // workers
    assert rows % (workers * _SC_CHUNK) == 0
    mesh = plsc.VectorSubcoreMesh(core_axis_name="c", subcore_axis_name="s")

    n_chunks = per_worker // _SC_CHUNK
    assert n_chunks % 2 == 0 and n_chunks >= 4

    def body(table_hbm, idx_hbm, out_hbm, idx0, idx1, rows0, rows1, gsem0, gsem1, wsem0, wsem1):
        wid = lax.axis_index("s") * _SC_CORES + lax.axis_index("c")
        base = wid * per_worker
        bufs = ((idx0, rows0, gsem0, wsem0), (idx1, rows1, gsem1, wsem1))

        def out_rows(c):
            return out_hbm.at[pl.ds(base + c * _SC_CHUNK, _SC_CHUNK)]

        def gather_start(c, b):
            idx_v, rows_v, gsem, _ = bufs[b]
            pltpu.sync_copy(idx_hbm.at[pl.ds(base + c * _SC_CHUNK, _SC_CHUNK)], idx_v)
            pltpu.make_async_copy(table_hbm.at[idx_v], rows_v, gsem).start()

        def gather_wait(b):
            idx_v, rows_v, gsem, _ = bufs[b]
            pltpu.make_async_copy(table_hbm.at[idx_v], rows_v, gsem).wait()

        def write_start(c, b):
            _, rows_v, _, wsem = bufs[b]
            pltpu.make_async_copy(rows_v, out_rows(c), wsem).start()

        def write_wait(c, b):
            _, rows_v, _, wsem = bufs[b]
            pltpu.make_async_copy(rows_v, out_rows(c), wsem).wait()

        gather_start(0, 0)
        gather_wait(0)
        write_start(0, 0)
        gather_start(1, 1)

        @pl.loop(1, n_chunks - 1, step=2)
        def _(c):
            gather_wait(1)
            write_start(c, 1)
            write_wait(c - 1, 0)
            gather_start(c + 1, 0)
            gather_wait(0)
            write_start(c + 1, 0)
            write_wait(c, 1)
            gather_start(c + 2, 1)

        gather_wait(1)
        write_start(n_chunks - 1, 1)
        write_wait(n_chunks - 2, 0)
        write_wait(n_chunks - 1, 1)

    chunk = lambda: pltpu.VMEM((_SC_CHUNK, width), table.dtype)
    return pl.kernel(
        body,
        out_type=jax.ShapeDtypeStruct((rows, width), table.dtype),
        mesh=mesh,
        scratch_types=[pltpu.VMEM((_SC_CHUNK,), I32), pltpu.VMEM((_SC_CHUNK,), I32), chunk(), chunk(),
                       pltpu.SemaphoreType.DMA, pltpu.SemaphoreType.DMA,
                       pltpu.SemaphoreType.DMA, pltpu.SemaphoreType.DMA],
        name="sc_gather_rows",
    )(table, idx)


def _sc_scatter_rows(rows, idx_slots, out_rows):
    n, width = rows.shape
    workers = _SC_CORES * _SC_SUBCORES
    per_worker = n // workers
    assert n % (workers * _SC_CHUNK) == 0
    mesh = plsc.VectorSubcoreMesh(core_axis_name="c", subcore_axis_name="s")

    def body(rows_hbm, idx_hbm, out_hbm, idx_v, rows_v):
        wid = lax.axis_index("s") * _SC_CORES + lax.axis_index("c")
        base = wid * per_worker

        @pl.loop(0, per_worker // _SC_CHUNK)
        def _(i):
            t0 = base + i * _SC_CHUNK
            pltpu.sync_copy(rows_hbm.at[pl.ds(t0, _SC_CHUNK)], rows_v)
            for kk in range(TOP_K):
                pltpu.sync_copy(idx_hbm.at[pl.ds(kk * n + t0, _SC_CHUNK)], idx_v)
                pltpu.sync_copy(rows_v, out_hbm.at[idx_v])

    return pl.kernel(
        body,
        out_type=jax.ShapeDtypeStruct((out_rows, width), rows.dtype),
        mesh=mesh,
        scratch_types=[pltpu.VMEM((_SC_CHUNK,), I32), pltpu.VMEM((_SC_CHUNK, width), rows.dtype)],
        name="sc_scatter_rows",
    )(rows, idx_slots)


def _combine_dense_body(y4_ref, tp_ref, x1_ref, gt_ref, gf_ref, *out_refs):
    o_ref = out_refs[-1]
    half = D_MODEL // 2
    tp = tp_ref[...]
    y_lo = y_hi = None
    for kk in range(TOP_K):
        lo, hi = _unpack_pair(y4_ref[kk])
        pk = tp[:, kk:kk + 1]
        y_lo = pk * lo if y_lo is None else y_lo + pk * lo
        y_hi = pk * hi if y_hi is None else y_hi + pk * hi
    x2 = x1_ref[...] + gt_ref[0] * jnp.concatenate([y_lo, y_hi], axis=1)
    ms = jnp.mean(x2 * x2, axis=-1, keepdims=True)
    o_ref[...] = x2 * lax.rsqrt(ms + EPS) * gf_ref[...]


def _combine_dense(y4, tp, x1, gt2, gf, tokens, rows_per_batch, first_row, n_total, out_prev):
    n, d = x1.shape
    per_b = rows_per_batch // tokens
    off = first_row // tokens
    in_specs = [pl.BlockSpec((TOP_K, tokens, d // 2), lambda i: (0, i, 0)),
                pl.BlockSpec((tokens, LANES), lambda i: (i, 0)),
                pl.BlockSpec((tokens, d), lambda i: (i, 0)),
                pl.BlockSpec((1, 1, d), lambda i: ((i + off) // per_b, 0, 0)),
                pl.BlockSpec((1, d), lambda i: (0, 0))]
    args = [y4, tp, x1, gt2, gf]
    aliases = {}
    if out_prev is not None:
        in_specs.append(pl.BlockSpec(memory_space=pl.ANY))
        args.append(out_prev)
        aliases = {len(args) - 1: 0}
    return pl.pallas_call(
        _combine_dense_body,
        grid=(n // tokens,),
        in_specs=in_specs,
        out_specs=pl.BlockSpec((tokens, d), lambda i: (i + off, 0)),
        out_shape=jax.ShapeDtypeStruct((n_total, d), F32),
        input_output_aliases=aliases,
        compiler_params=_cparams(("arbitrary",)),
        name="moe_combine_dense",
    )(*args)


def _routing_tables(top_idx, rank, counts, bm):
    n = top_idx.shape[0]
    nk = n * TOP_K
    padded = (counts + bm - 1) // bm * bm
    pad_end = jnp.cumsum(padded)
    pad_start = pad_end - padded
    dest = (pad_start[top_idx] + rank).astype(I32)
    n_blocks = (nk + N_EXPERTS * (bm - 1) + bm - 1) // bm
    starts = jnp.arange(n_blocks, dtype=I32) * bm
    blk_expert = jnp.minimum(jnp.sum((pad_end[None, :] <= starts[:, None]).astype(I32), axis=1),
                             N_EXPERTS - 1).astype(I32)
    n_real = (pad_end[-1] // bm).astype(I32).reshape(1)
    return dest, blk_expert, n_real, n_blocks


def _layer(x, c, ctx, c_ctx, w_ada, b_ada, g_mix_norm, w_in, w_dw, b_dw, g_conv_ln, b_conv_ln,
           w_conv_out, w_alpha, b_alpha, g_gla_norm, w_gla_out, w_out, g_ffn_norm, w_router,
           b_router, w_exp_in, b_exp_in, w_exp_out, b_exp_out, g_final, *, cfg):
    b, s, d = x.shape
    n = b * s

    rows = (b + 1 + 7) // 8 * 8
    cc = jnp.zeros((rows, d), F32).at[:b].set(c).at[b].set(c_ctx)
    mod = _ada(cc, w_ada, b_ada)
    sh1, sc1, gt1, sh2, sc2, gt2 = [mod[:b, i * d:(i + 1) * d].reshape(b, 1, d) for i in range(6)]
    csh1 = mod[b:b + 1, 0:d]
    csc1 = mod[b:b + 1, d:2 * d]

    a0 = 2 * CONV_W + 2 * GLA_KD + 2 * GLA_VD
    w_in_r = jnp.concatenate(
        [w_in[:, :a0], w_in[:, a0 + 2 * GLA_RANK:], w_in[:, a0:a0 + 2 * GLA_RANK],
         jnp.zeros((d, LANES - 2 * GLA_RANK), F32)], axis=1).astype(BF16)
    gmn = g_mix_norm.reshape(1, d)

    act, q, k, v, sg, gc, gg, a = _inproj_lat(x, gmn, sh1, sc1, w_in_r, w_dw, b_dw, g_conv_ln,
                                              b_conv_ln, cfg["tm_in"])
    act = act.reshape(n, CONV_W)
    kc, vc, ac = _inproj_ctx(ctx, gmn, csh1, csc1, w_in_r, cfg["tm_ctx"])

    cps = cfg["gla_cps"]
    st_f = _gla_ctx(kc, vc, ac, w_alpha[0], b_alpha[0], False, cps)
    o_f = _gla_lat(k, v, a, q, w_alpha[0], b_alpha[0], st_f, False, cps)
    st_b = _gla_ctx(kc, vc, ac, w_alpha[1], b_alpha[1], True, cps)
    og = _gla_lat(k, v, a, q, w_alpha[1], b_alpha[1], st_b, True, cps,
                  sg=sg, o_prev=o_f, g_norm=g_gla_norm)

    wr = jnp.zeros((d, LANES), F32).at[:, :N_EXPERTS].set(w_router)
    br = jnp.zeros((1, LANES), F32).at[0, :N_EXPERTS].set(b_router)
    b1 = b_exp_in.reshape(N_EXPERTS, 2 * D_FF // _GLU_GROUP, LANES, 2).transpose(0, 1, 3, 2)
    b1 = b1.reshape(N_EXPERTS, 1, 2 * D_FF)
    bm = cfg["moe_block"]

    groups = cfg["moe_groups"]
    ng = n // groups
    staged = []
    for g in range(groups):
        x1, h2p, ti, tp, cnt = _merge(
            act, og.reshape(n, d), gc.reshape(n, d), gg.reshape(n, d), x.reshape(n, d),
            gt1, sh2, sc2, g_ffn_norm.reshape(1, d),
            w_conv_out.astype(BF16), w_gla_out.astype(BF16), w_out.astype(BF16), wr, br,
            cfg["tm_merge"], s, g * ng, ng)
        dest, blk_expert, n_real, n_blocks = _routing_tables(
            ti[:, :TOP_K], ti[:, TOP_K:2 * TOP_K], cnt[0, :N_EXPERTS].astype(I32), bm)
        dest_slots = dest.T.reshape(ng * TOP_K)
        xp = _sc_scatter_rows(h2p, dest_slots, n_blocks * bm)
        staged.append((x1, tp, dest_slots, blk_expert, n_real, xp))
    gathered = []
    for x1, tp, dest_slots, blk_expert, n_real, xp in staged:
        yp = _experts(blk_expert, n_real, xp, w_exp_in, b1, w_exp_out,
                      b_exp_out.reshape(N_EXPERTS, 1, d), bm)
        gathered.append(_sc_gather_rows(yp, dest_slots).reshape(TOP_K, ng, d // 2))
    out = None
    for g, (x1, tp, *_) in enumerate(staged):
        out = _combine_dense(gathered[g], tp, x1, gt2, g_final.reshape(1, d), cfg["tm_combine"], s,
                             g * ng, n, out)
    return out.reshape(b, s, d)


def _config(s, l):
    return dict(tm_in=min(512, s), tm_ctx=min(256, l), gla_cps=2,
                tm_merge=min(512, s), moe_block=512, tm_combine=min(256, s), moe_groups=2)


def kernel(x, c, ctx, c_ctx, w_ada, b_ada, g_mix_norm, w_in, w_dw, b_dw, g_conv_ln, b_conv_ln,
           w_conv_out, w_alpha, b_alpha, g_gla_norm, w_gla_out, w_out, g_ffn_norm, w_router,
           b_router, w_exp_in, b_exp_in, w_exp_out, b_exp_out, g_final):
    depth = w_ada.shape[0]
    assert depth == 1, "single-layer block: the context stream is only consumed by the GLA scan"
    cfg = _config(x.shape[1], ctx.shape[1])
    return _layer(x, c, ctx, c_ctx, w_ada[0], b_ada[0], g_mix_norm[0], w_in[0], w_dw[0], b_dw[0],
                  g_conv_ln[0], b_conv_ln[0], w_conv_out[0], w_alpha[0], b_alpha[0],
                  g_gla_norm[0], w_gla_out[0], w_out[0], g_ffn_norm[0], w_router[0], b_router[0],
                  w_exp_in[0], b_exp_in[0], w_exp_out[0], b_exp_out[0], g_final, cfg=cfg)
```

```python
import functools

import jax
import jax.numpy as jnp
from jax import lax
from jax.experimental import pallas as pl
from jax.experimental.pallas import tpu as pltpu
from jax.experimental.pallas import tpu_sc as plsc

F32 = jnp.float32
BF16 = jnp.bfloat16
I32 = jnp.int32

D_MODEL = 1024
GRID_W = 64
EPS = 1e-6
CONV_W = 1024
CONV_K = 31
GLA_H = 4
GLA_DK = 128
GLA_DV = 256
GLA_KD = GLA_H * GLA_DK
GLA_VD = GLA_H * GLA_DV
GLA_RANK = 16
GLA_TAU = 16.0
GLA_CHUNK = 64
N_EXPERTS = 32
TOP_K = 4
D_FF = 1024
SWIGLU_ALPHA = 1.702
SWIGLU_LIMIT = 7.0

LANES = 128
VMEM_LIMIT = 56 * 1024 * 1024

_C_CONV_A = 0
_C_CONV_B = _C_CONV_A + CONV_W
_C_Q = _C_CONV_B + CONV_W
_C_K = _C_Q + GLA_KD
_C_V = _C_K + GLA_KD
_C_G = _C_V + GLA_VD
_C_GC = _C_G + GLA_VD
_C_GG = _C_GC + D_MODEL
_C_A = _C_GG + D_MODEL
_C_END = _C_A + LANES


def _cparams(sem):
    return pltpu.CompilerParams(dimension_semantics=sem, vmem_limit_bytes=VMEM_LIMIT)


def _dot(a, b):
    return jnp.dot(a, b, preferred_element_type=F32)


def _split_bf16(x):
    hi = x.astype(BF16)
    lo = (x - hi.astype(F32)).astype(BF16)
    return hi, lo


def _dot3(a, b):
    a_hi, a_lo = _split_bf16(a)
    b_hi, b_lo = _split_bf16(b)
    return _dot(a_hi, b_hi) + _dot(a_lo, b_hi) + _dot(a_hi, b_lo)


def _sigmoid(x):
    return 1.0 / (1.0 + jnp.exp(-x))


def _resident(shape):
    nd = len(shape)
    return pl.BlockSpec(shape, lambda *_: (0,) * nd, pipeline_mode=pl.Buffered(1))


def _ada_body(a_ref, w_ref, b_ref, o_ref):
    a = a_ref[...]
    a = a * _sigmoid(a)
    o_ref[...] = _dot3(a, w_ref[...]) + b_ref[...]


def _ada(cc, w, b):
    rows, d = cc.shape
    n = w.shape[1]
    tn = 512
    return pl.pallas_call(
        _ada_body,
        grid=(n // tn,),
        in_specs=[pl.BlockSpec((rows, d), lambda j: (0, 0)),
                  pl.BlockSpec((d, tn), lambda j: (0, j)),
                  pl.BlockSpec((1, tn), lambda j: (0, j))],
        out_specs=pl.BlockSpec((rows, tn), lambda j: (0, j)),
        out_shape=jax.ShapeDtypeStruct((rows, n), F32),
        compiler_params=_cparams(("arbitrary",)),
        name="ada_mod",
    )(cc, w, b.reshape(1, n))


def _norm_mod(xv, gn, sc, sh):
    ms = jnp.mean(xv * xv, axis=-1, keepdims=True)
    y = xv * lax.rsqrt(ms + EPS) * gn
    return (y * (1.0 + sc) + sh).astype(BF16)


def _inproj_lat_body(x_ref, gn_ref, sh_ref, sc_ref, w_ref, sm_ref, wdw_ref, bdw_ref, gln_ref, bln_ref,
                     act_ref, q_ref, k_ref, v_ref, sg_ref, gc_ref, gg_ref, a_ref,
                     win_ref, y_ref):
    h = _norm_mod(x_ref[0], gn_ref[...], sc_ref[0], sh_ref[0])
    ca = _dot(h, w_ref[:, _C_CONV_A:_C_CONV_B])
    cb = _dot(h, w_ref[:, _C_CONV_B:_C_Q])
    u = (ca * _sigmoid(cb)).astype(BF16)
    q_ref[0] = _dot(h, w_ref[:, _C_Q:_C_K]).astype(BF16)
    k_ref[0] = _dot(h, w_ref[:, _C_K:_C_V]).astype(BF16)
    v_ref[0] = _dot(h, w_ref[:, _C_V:_C_G]).astype(BF16)
    g = _dot(h, w_ref[:, _C_G:_C_GC])
    sg_ref[0] = (g * _sigmoid(g)).astype(BF16)
    gc_ref[0] = _sigmoid(_dot(h, w_ref[:, _C_GC:_C_GG])).astype(BF16)
    gg_ref[0] = _sigmoid(_dot(h, w_ref[:, _C_GG:_C_A])).astype(BF16)
    a_ref[0] = _dot(h, w_ref[:, _C_A:_C_END])
    act_ref[0] = _conv_ln_swish(u, sm_ref, wdw_ref, bdw_ref, gln_ref, bln_ref, win_ref, y_ref)


def _inproj_ctx_body(x_ref, gn_ref, sh_ref, sc_ref, w_ref, k_ref, v_ref, a_ref):
    h = _norm_mod(x_ref[0], gn_ref[...], sc_ref[...], sh_ref[...])
    k_ref[0] = _dot(h, w_ref[:, _C_K:_C_V]).astype(BF16)
    v_ref[0] = _dot(h, w_ref[:, _C_V:_C_G]).astype(BF16)
    a_ref[0] = _dot(h, w_ref[:, _C_A:_C_END])


def _inproj_lat(x, gn, sh, sc, w, w_dw, b_dw, g_ln, b_ln, tm):
    b, s, d = x.shape
    row = lambda n: pl.BlockSpec((1, tm, n), lambda bi, i: (bi, i, 0))
    vec = pl.BlockSpec((1, 1, d), lambda bi, i: (bi, 0, 0))
    const = lambda shape: pl.BlockSpec(shape, lambda bi, i: (0,) * len(shape))
    shp = lambda n, dt: jax.ShapeDtypeStruct((b, s, n), dt)
    sm = _conv_shift_matrix()
    wpad = jnp.zeros((32, CONV_W), F32).at[:CONV_K].set(w_dw)
    return pl.pallas_call(
        _inproj_lat_body,
        grid=(b, s // tm),
        in_specs=[row(d), const((1, d)), vec, vec, _resident(w.shape),
                  const(sm.shape), const(wpad.shape), const((1, CONV_W)), const((1, CONV_W)),
                  const((1, CONV_W))],
        out_specs=[row(CONV_W), row(GLA_KD), row(GLA_KD), row(GLA_VD), row(GLA_VD),
                   row(d), row(d), row(LANES)],
        out_shape=[shp(CONV_W, BF16), shp(GLA_KD, BF16), shp(GLA_KD, BF16), shp(GLA_VD, BF16),
                   shp(GLA_VD, BF16), shp(d, BF16), shp(d, BF16), shp(LANES, F32)],
        scratch_shapes=[pltpu.VMEM((2, _CONV_SUB * _CONV_SPAN, CONV_W), F32),
                        pltpu.VMEM((tm, CONV_W), F32)],
        compiler_params=_cparams(("arbitrary", "arbitrary")),
        name="inproj_lat",
    )(x, gn, sh, sc, w, sm, wpad, b_dw.reshape(1, CONV_W), g_ln.reshape(1, CONV_W),
      b_ln.reshape(1, CONV_W))


def _inproj_ctx(ctx, gn, sh, sc, w, tm):
    b, l, d = ctx.shape
    row = lambda n: pl.BlockSpec((1, tm, n), lambda bi, i: (bi, i, 0))
    vec = pl.BlockSpec((1, d), lambda bi, i: (0, 0))
    shp = lambda n, dt: jax.ShapeDtypeStruct((b, l, n), dt)
    return pl.pallas_call(
        _inproj_ctx_body,
        grid=(b, l // tm),
        in_specs=[row(d), vec, vec, vec, _resident(w.shape)],
        out_specs=[row(GLA_KD), row(GLA_VD), row(LANES)],
        out_shape=[shp(GLA_KD, BF16), shp(GLA_VD, BF16), shp(LANES, F32)],
        compiler_params=_cparams(("arbitrary", "arbitrary")),
        name="inproj_ctx",
    )(ctx, gn, sh, sc, w)


_CONV_SUB = 8
_CONV_SPAN = GRID_W + _CONV_SUB * ((CONV_K - 1) // _CONV_SUB)


def _conv_shift_matrix():
    row = lax.broadcasted_iota(I32, (_CONV_SUB, _CONV_SPAN, GRID_W), 1)
    shift = lax.broadcasted_iota(I32, (_CONV_SUB, _CONV_SPAN, GRID_W), 0)
    col = lax.broadcasted_iota(I32, (_CONV_SUB, _CONV_SPAN, GRID_W), 2)
    m = jnp.where(col == row + shift - CONV_K // 2, 1.0, 0.0)
    return m.reshape(_CONV_SUB * _CONV_SPAN, GRID_W).astype(BF16)


def _conv_ln_swish(u, sm_ref, w_ref, bdw_ref, gln_ref, bln_ref, win_ref, y_ref):
    for r in range(u.shape[0] // GRID_W):
        buf = r % 2
        win_ref[buf] = _dot(sm_ref[...], u[r * GRID_W:(r + 1) * GRID_W, :])
        for cb in range(CONV_W // LANES):
            ls = slice(cb * LANES, (cb + 1) * LANES)
            acc = None
            for k in range(CONV_K):
                s, a = k % _CONV_SUB, k // _CONV_SUB
                r0 = s * _CONV_SPAN + _CONV_SUB * a
                term = win_ref[buf, r0:r0 + GRID_W, ls] * w_ref[k:k + 1, ls]
                acc = term if acc is None else acc + term
            y_ref[r * GRID_W:(r + 1) * GRID_W, ls] = acc + bdw_ref[:, ls]
    y = y_ref[...]
    mu = jnp.mean(y, axis=-1, keepdims=True)
    yc = y - mu
    var = jnp.mean(yc * yc, axis=-1, keepdims=True)
    yn = yc * lax.rsqrt(var + EPS) * gln_ref[...] + bln_ref[...]
    return (yn * _sigmoid(yn)).astype(BF16)


def _log_sigmoid(z):
    return jnp.minimum(z, 0.0) - jnp.log(1.0 + jnp.exp(-jnp.abs(z)))


def _gla_keep_mask(t, reverse):
    row = lax.broadcasted_iota(I32, (t, t), 0)
    col = lax.broadcasted_iota(I32, (t, t), 1)
    same_chunk = lax.shift_right_logical(row, 6) == lax.shift_right_logical(col, 6)
    return jnp.logical_and(same_chunk, (col >= row) if reverse else (col <= row))


def _gla_decay(a, wh_ref, wl_ref, ba, keep):
    a_hi, a_lo = _split_bf16(a)
    z = _dot(jnp.concatenate([a_hi, a_lo], axis=1), wh_ref[...]) + _dot(a_hi, wl_ref[...]) + ba
    loga = _log_sigmoid(z) * (1.0 / GLA_TAU)
    l_hi, l_lo = _split_bf16(loga)
    tri = jnp.where(keep, 1.0, 0.0).astype(BF16)
    return _dot(tri, l_hi) + _dot(tri, l_lo)


def _gla_step(k, v, bcum, keep, states, *, reverse, q=None):
    t = k.shape[0]
    c = GLA_CHUNK
    nch = t // c
    tots = [bcum[n * c:n * c + 1, :] if reverse else bcum[(n + 1) * c - 1:(n + 1) * c, :]
            for n in range(nch)]
    totb = jnp.concatenate([jnp.broadcast_to(tt, (c, GLA_KD)) for tt in tots], axis=0)
    k32 = k.astype(F32)
    k_end = (k32 * jnp.exp(totb - bcum)).astype(BF16)
    decs = [jnp.exp(tt) for tt in tots]
    if q is not None:
        q_dec = (q.astype(F32) * jnp.exp(bcum) * (GLA_DK ** -0.5)).astype(BF16)
        k_inv = (k32 * jnp.exp(-bcum)).astype(BF16)
    outs, new_states = [], []
    for h in range(GLA_H):
        ks = slice(h * GLA_DK, (h + 1) * GLA_DK)
        vh = v[:, h * GLA_DV:(h + 1) * GLA_DV]
        st = states[h]
        if q is not None:
            scores = lax.dot_general(q_dec[:, ks], k_inv[:, ks], (((1,), (1,)), ((), ())),
                                     preferred_element_type=F32)
            o_h = _dot(jnp.where(keep, scores, 0.0).astype(BF16), vh)
            inter = []
        for n in range(nch):
            rs = slice(n * c, (n + 1) * c)
            if q is not None:
                inter.append(_dot(q_dec[rs, ks], st.astype(BF16)))
            kv = lax.dot_general(k_end[rs, ks], vh[rs, :], (((0,), (0,)), ((), ())),
                                 preferred_element_type=F32)
            dt = jnp.transpose(jnp.broadcast_to(decs[n][:, ks], (GLA_DK, GLA_DK)))
            st = st * jnp.concatenate([dt] * (GLA_DV // GLA_DK), axis=1) + kv
        new_states.append(st)
        if q is not None:
            outs.append(o_h + jnp.concatenate(inter, axis=0))
    return new_states, (jnp.concatenate(outs, axis=1) if q is not None else None)


def _scan_order(x, reverse):
    if not reverse:
        return x
    nch = x.shape[0] // GLA_CHUNK
    return jnp.concatenate(
        [x[n * GLA_CHUNK:(n + 1) * GLA_CHUNK] for n in range(nch - 1, -1, -1)], axis=0)


def _gla_ctx_body(k_ref, v_ref, a_ref, wh_ref, wl_ref, ba_ref, st_ref, *, reverse):
    j = pl.program_id(1)

    @pl.when(j == 0)
    def _():
        st_ref[...] = jnp.zeros_like(st_ref)

    keep = _gla_keep_mask(k_ref.shape[1], reverse)
    scan = functools.partial(_scan_order, reverse=reverse)
    bcum = _gla_decay(scan(a_ref[0]), wh_ref, wl_ref, ba_ref[...], keep)
    states, _ = _gla_step(scan(k_ref[0]), scan(v_ref[0]), bcum, keep,
                          [st_ref[0, h] for h in range(GLA_H)], reverse=reverse)
    for h in range(GLA_H):
        st_ref[0, h] = states[h]


def _gla_lat_body(*refs, reverse, final):
    if final:
        (k_ref, v_ref, a_ref, an_ref, q_ref, wh_ref, wl_ref, ba_ref, s0_ref, sg_ref, op_ref,
         gn_ref, o_ref, st_ref, bc_ref) = refs
    else:
        (k_ref, v_ref, a_ref, an_ref, q_ref, wh_ref, wl_ref, ba_ref, s0_ref,
         o_ref, st_ref, bc_ref) = refs
    j = pl.program_id(1)
    nb, t = k_ref.shape[0], k_ref.shape[1]
    keep = _gla_keep_mask(t, reverse)
    scan = functools.partial(_scan_order, reverse=reverse)

    @pl.when(j == 0)
    def _():
        st_ref[...] = s0_ref[...]
        for bb in range(nb):
            bc_ref[bb] = _gla_decay(scan(a_ref[bb]), wh_ref, wl_ref, ba_ref[...], keep)

    for bb in range(nb):
        states, o = _gla_step(scan(k_ref[bb]), scan(v_ref[bb]), bc_ref[bb], keep,
                              [st_ref[bb, h] for h in range(GLA_H)], reverse=reverse,
                              q=scan(q_ref[bb]))
        bc_ref[bb] = _gla_decay(scan(an_ref[bb]), wh_ref, wl_ref, ba_ref[...], keep)
        for h in range(GLA_H):
            st_ref[bb, h] = states[h]
        if not final:
            o_ref[bb] = scan(o.astype(BF16))
            continue
        o = o + scan(op_ref[bb]).astype(F32)
        parts = []
        for h in range(GLA_H):
            oh = o[:, h * GLA_DV:(h + 1) * GLA_DV]
            ms = jnp.mean(oh * oh, axis=-1, keepdims=True)
            parts.append(oh * lax.rsqrt(ms + EPS) * gn_ref[...])
        o_ref[bb] = scan((jnp.concatenate(parts, axis=1)
                          * scan(sg_ref[bb]).astype(F32)).astype(BF16))


def _gla_decay_weights(w_alpha_d, reverse):
    a0 = GLA_RANK if reverse else 0
    w = jnp.zeros((LANES, GLA_KD), F32).at[a0:a0 + GLA_RANK].set(w_alpha_d)
    hi = w.astype(BF16)
    lo = (w - hi.astype(F32)).astype(BF16)
    return jnp.concatenate([hi, hi], axis=0), lo


def _gla_ctx(k, v, a, w_alpha_d, ba, reverse, cps):
    b, l, _ = k.shape
    tm = cps * GLA_CHUNK
    nj = l // tm
    jmap = (lambda j: nj - 1 - j) if reverse else (lambda j: j)
    row = lambda n: pl.BlockSpec((1, tm, n), lambda bi, j: (bi, jmap(j), 0))
    wh, wl = _gla_decay_weights(w_alpha_d, reverse)
    return pl.pallas_call(
        functools.partial(_gla_ctx_body, reverse=reverse),
        grid=(b, nj),
        in_specs=[row(GLA_KD), row(GLA_VD), row(LANES),
                  pl.BlockSpec(wh.shape, lambda bi, j: (0, 0)),
                  pl.BlockSpec(wl.shape, lambda bi, j: (0, 0)),
                  pl.BlockSpec((1, GLA_KD), lambda bi, j: (0, 0))],
        out_specs=pl.BlockSpec((1, GLA_H, GLA_DK, GLA_DV), lambda bi, j: (bi, 0, 0, 0)),
        out_shape=jax.ShapeDtypeStruct((b, GLA_H, GLA_DK, GLA_DV), F32),
        compiler_params=_cparams(("arbitrary", "arbitrary")),
        name="gla_ctx_bwd" if reverse else "gla_ctx_fwd",
    )(k, v, a, wh, wl, ba.reshape(1, GLA_KD))


def _gla_lat(k, v, a, q, w_alpha_d, ba, s0, reverse, cps, sg=None, o_prev=None, g_norm=None):
    b, s, _ = k.shape
    final = sg is not None
    tm = cps * GLA_CHUNK
    nj = s // tm
    jmap = (lambda j: nj - 1 - j) if reverse else (lambda j: j)
    nb = 4 if b % 4 == 0 else 2 if b % 2 == 0 else 1
    row = lambda n: pl.BlockSpec((nb, tm, n), lambda bi, j: (bi, jmap(j), 0))
    wh, wl = _gla_decay_weights(w_alpha_d, reverse)
    a_next = pl.BlockSpec((nb, tm, LANES), lambda bi, j: (bi, jmap(jnp.minimum(j + 1, nj - 1)), 0))
    in_specs = [row(GLA_KD), row(GLA_VD), row(LANES), a_next, row(GLA_KD),
                pl.BlockSpec(wh.shape, lambda bi, j: (0, 0)),
                pl.BlockSpec(wl.shape, lambda bi, j: (0, 0)),
                pl.BlockSpec((1, GLA_KD), lambda bi, j: (0, 0)),
                pl.BlockSpec((nb, GLA_H, GLA_DK, GLA_DV), lambda bi, j: (bi, 0, 0, 0))]
    args = [k, v, a, a, q, wh, wl, ba.reshape(1, GLA_KD), s0]
    if final:
        in_specs += [row(GLA_VD), row(GLA_VD), pl.BlockSpec((1, GLA_DV), lambda bi, j: (0, 0))]
        args += [sg, o_prev, g_norm.reshape(1, GLA_DV)]
    return pl.pallas_call(
        functools.partial(_gla_lat_body, reverse=reverse, final=final),
        grid=(b // nb, nj),
        in_specs=in_specs,
        out_specs=row(GLA_VD),
        out_shape=jax.ShapeDtypeStruct((b, s, GLA_VD), BF16),
        scratch_shapes=[pltpu.VMEM((nb, GLA_H, GLA_DK, GLA_DV), F32),
                        pltpu.VMEM((nb, tm, GLA_KD), F32)],
        compiler_params=_cparams(("arbitrary", "arbitrary")),
        name="gla_lat_bwd" if reverse else "gla_lat_fwd",
    )(*args)


def _pack_pair(lo, hi):
    lo_b = pltpu.bitcast(lo.astype(BF16).astype(F32), I32)
    hi_b = pltpu.bitcast(hi.astype(BF16).astype(F32), I32)
    return lax.shift_right_logical(lo_b, 16) | (hi_b & jnp.int32(-65536))


def _unpack_pair(w):
    lo = pltpu.bitcast(lax.shift_left(w, 16), F32)
    hi = pltpu.bitcast(w & jnp.int32(-65536), F32)
    return lo, hi


def _merge_body(act_ref, og_ref, gc_ref, gg_ref, x_ref, gt_ref, sh_ref, sc_ref, gn_ref,
                wc_ref, wg_ref, wo_ref, wr_ref, br_ref,
                x1_ref, h2_ref, ti_ref, tp_ref, cnt_ref):
    yc = _dot(act_ref[...], wc_ref[...])
    yg = _dot(og_ref[...], wg_ref[...])
    m = gc_ref[...].astype(F32) * yc + gg_ref[...].astype(F32) * yg
    x1 = x_ref[...] + gt_ref[0] * _dot(m.astype(BF16), wo_ref[...])
    x1_ref[...] = x1
    ms = jnp.mean(x1 * x1, axis=-1, keepdims=True)
    h2 = (x1 * lax.rsqrt(ms + EPS) * gn_ref[...]) * (1.0 + sc_ref[0]) + sh_ref[0]
    half = D_MODEL // 2
    h2_ref[...] = _pack_pair(h2[:, :half], h2[:, half:])
    logits = _dot3(h2, wr_ref[...]) + br_ref[...]
    lane = lax.broadcasted_iota(I32, logits.shape, 1).astype(F32)
    neg = jnp.float32(-jnp.inf)
    work = jnp.where(lane < N_EXPERTS, logits, neg)
    vals, idxs = [], []
    for _ in range(TOP_K):
        mx = jnp.max(work, axis=-1, keepdims=True)
        ix = jnp.min(jnp.where(work == mx, lane, float(LANES)), axis=-1, keepdims=True)
        vals.append(mx)
        idxs.append(ix)
        work = jnp.where(lane == ix, neg, work)
    es = [jnp.exp(v - vals[0]) for v in vals]
    den = es[0] + es[1] + es[2] + es[3]
    ti = jnp.zeros(logits.shape, F32)
    tp = jnp.zeros(logits.shape, F32)
    onehot = jnp.zeros(logits.shape, F32)
    for kk in range(TOP_K):
        ti = jnp.where(lane == kk, idxs[kk], ti)
        tp = jnp.where(lane == kk, es[kk] / den, tp)
        onehot = onehot + jnp.where(lane == idxs[kk], 1.0, 0.0)
    @pl.when(pl.program_id(0) == 0)
    def _():
        cnt_ref[...] = jnp.zeros_like(cnt_ref)

    tm = logits.shape[0]
    earlier = (lax.broadcasted_iota(I32, (tm, tm), 1) < lax.broadcasted_iota(I32, (tm, tm), 0))
    before = _dot(jnp.where(earlier, 1.0, 0.0).astype(BF16), onehot.astype(BF16)) + cnt_ref[0:1, :]
    for kk in range(TOP_K):
        rank = jnp.sum(jnp.where(lane == idxs[kk], before, 0.0), axis=-1, keepdims=True)
        ti = jnp.where(lane == TOP_K + kk, rank, ti)
    cnt_ref[...] = cnt_ref[...] + jnp.sum(onehot, axis=0, keepdims=True)
    ti_ref[...] = ti.astype(I32)
    tp_ref[...] = tp


def _merge(act, og, gc, gg, x2d, gt1, sh2, sc2, gn, wc, wg, wo, wr, br, tm, rows_per_batch,
           first_row, n):
    d = x2d.shape[1]
    per_b = rows_per_batch // tm
    off = first_row // tm
    row_in = lambda w: pl.BlockSpec((tm, w), lambda i: (i + off, 0))
    row = lambda w: pl.BlockSpec((tm, w), lambda i: (i, 0))
    vec = pl.BlockSpec((1, 1, d), lambda i: ((i + off) // per_b, 0, 0))
    const = lambda shape: pl.BlockSpec(shape, lambda i: (0,) * len(shape))
    return pl.pallas_call(
        _merge_body,
        grid=(n // tm,),
        in_specs=[row_in(d), row_in(d), row_in(d), row_in(d), row_in(d), vec, vec, vec,
                  const((1, d)),
                  _resident(wc.shape), _resident(wg.shape), _resident(wo.shape),
                  const(wr.shape), const((1, LANES))],
        out_specs=[row(d), row(d // 2), row(LANES), row(LANES), const((8, LANES))],
        out_shape=[jax.ShapeDtypeStruct((n, d), F32), jax.ShapeDtypeStruct((n, d // 2), I32),
                   jax.ShapeDtypeStruct((n, LANES), I32), jax.ShapeDtypeStruct((n, LANES), F32),
                   jax.ShapeDtypeStruct((8, LANES), F32)],
        compiler_params=_cparams(("arbitrary",)),
        name="merge_router",
    )(act, og, gc, gg, x2d, gt1, sh2, sc2, gn, wc, wg, wo, wr, br)


_GLU_GROUP = 2 * LANES


def _deinterleave_matrix():
    src = lax.broadcasted_iota(I32, (_GLU_GROUP, _GLU_GROUP), 0)
    dst = lax.broadcasted_iota(I32, (_GLU_GROUP, _GLU_GROUP), 1)
    want = jnp.where(dst < LANES, 2 * dst, 2 * (dst - LANES) + 1)
    return jnp.where(src == want, 1.0, 0.0).astype(BF16)


def _expert_body(be_ref, nreal_ref, xp_ref, w1_ref, b1_ref, w2_ref, b2_ref, yp_ref,
                 w1s_ref, w2s_ref):
    i = pl.program_id(0)
    new_expert = jnp.logical_or(i == 0, be_ref[i] != be_ref[jnp.maximum(i - 1, 0)])

    @pl.when(jnp.logical_and(new_expert, i < nreal_ref[0]))
    def _():
        perm = _deinterleave_matrix()
        for g in range(2 * D_FF // _GLU_GROUP):
            cs = slice(g * _GLU_GROUP, (g + 1) * _GLU_GROUP)
            w1s_ref[:, cs] = _dot(w1_ref[0, :, cs].astype(BF16), perm).astype(BF16)
        w2s_ref[...] = w2_ref[0].astype(BF16)

    @pl.when(i < nreal_ref[0])
    def _():
        lo, hi = _unpack_pair(xp_ref[...])
        x = jnp.concatenate([lo, hi], axis=1).astype(BF16)
        hid = _dot(x, w1s_ref[...]) + b1_ref[0]
        ngrp = 2 * D_FF // _GLU_GROUP
        hg = jnp.concatenate(
            [hid[:, g * _GLU_GROUP:g * _GLU_GROUP + LANES] for g in range(ngrp)], axis=1)
        hl = jnp.concatenate(
            [hid[:, g * _GLU_GROUP + LANES:(g + 1) * _GLU_GROUP] for g in range(ngrp)], axis=1)
        xg = jnp.minimum(hg, SWIGLU_LIMIT)
        xl = jnp.clip(hl, -SWIGLU_LIMIT, SWIGLU_LIMIT)
        act = xg * _sigmoid(SWIGLU_ALPHA * xg) * (xl + 1.0)
        y = _dot(act.astype(BF16), w2s_ref[...]) + b2_ref[0]
        half = D_MODEL // 2
        yp_ref[...] = _pack_pair(y[:, :half], y[:, half:])

    @pl.when(i >= nreal_ref[0])
    def _():
        yp_ref[...] = jnp.zeros_like(yp_ref)


def _experts(blk_expert, n_real, xp, w1, b1, w2, b2, bm):
    p, w = xp.shape
    grid_spec = pltpu.PrefetchScalarGridSpec(
        num_scalar_prefetch=2,
        grid=(p // bm,),
        in_specs=[pl.BlockSpec((bm, w), lambda i, be, nr: (i, 0)),
                  pl.BlockSpec((1, D_MODEL, 2 * D_FF), lambda i, be, nr: (be[i], 0, 0)),
                  pl.BlockSpec((1, 1, 2 * D_FF), lambda i, be, nr: (be[i], 0, 0)),
                  pl.BlockSpec((1, D_FF, D_MODEL), lambda i, be, nr: (be[i], 0, 0)),
                  pl.BlockSpec((1, 1, D_MODEL), lambda i, be, nr: (be[i], 0, 0))],
        out_specs=pl.BlockSpec((bm, w), lambda i, be, nr: (i, 0)),
        scratch_shapes=[pltpu.VMEM((D_MODEL, 2 * D_FF), BF16), pltpu.VMEM((D_FF, D_MODEL), BF16)],
    )
    return pl.pallas_call(
        _expert_body,
        grid_spec=grid_spec,
        out_shape=jax.ShapeDtypeStruct((p, w), I32),
        compiler_params=_cparams(("arbitrary",)),
        name="moe_experts",
    )(blk_expert, n_real, xp, w1, b1, w2, b2)


_SC_CORES = 2
_SC_SUBCORES = 16
_SC_CHUNK = 64


def _sc_gather_rows(table, idx):
    rows, width = idx.shape[0], table.shape[1]
    workers = _SC_CORES * _SC_SUBCORES
    per_worker = rows // workers
    assert rows % (workers * _SC_CHUNK) == 0
    mesh = plsc.VectorSubcoreMesh(core_axis_name="c", subcore_axis_name="s")

    n_chunks = per_worker // _SC_CHUNK
    assert n_chunks % 2 == 0 and n_chunks >= 4

    def body(table_hbm, idx_hbm, out_hbm, idx0, idx1, rows0, rows1, gsem0, gsem1, wsem0, wsem1):
        wid = lax.axis_index("s") * _SC_CORES + lax.axis_index("c")
        base = wid * per_worker
        bufs = ((idx0, rows0, gsem0, wsem0), (idx1, rows1, gsem1, wsem1))

        def out_rows(c):
            return out_hbm.at[pl.ds(base + c * _SC_CHUNK, _SC_CHUNK)]

        def gather_start(c, b):
            idx_v, rows_v, gsem, _ = bufs[b]
            pltpu.sync_copy(idx_hbm.at[pl.ds(base + c * _SC_CHUNK, _SC_CHUNK)], idx_v)
            pltpu.make_async_copy(table_hbm.at[idx_v], rows_v, gsem).start()

        def gather_wait(b):
            idx_v, rows_v, gsem, _ = bufs[b]
            pltpu.make_async_copy(table_hbm.at[idx_v], rows_v, gsem).wait()

        def write_start(c, b):
            _, rows_v, _, wsem = bufs[b]
            pltpu.make_async_copy(rows_v, out_rows(c), wsem).start()

        def write_wait(c, b):
            _, rows_v, _, wsem = bufs[b]
            pltpu.make_async_copy(rows_v, out_rows(c), wsem).wait()

        gather_start(0, 0)
        gather_wait(0)
        write_start(0, 0)
        gather_start(1, 1)

        @pl.loop(1, n_chunks - 1, step=2)
        def _(c):
            gather_wait(1)
            write_start(c, 1)
            write_wait(c - 1, 0)
            gather_start(c + 1, 0)
            gather_wait(0)
            write_start(c + 1, 0)
            write_wait(c, 1)
            gather_start(c + 2, 1)

        gather_wait(1)
        write_start(n_chunks - 1, 1)
        write_wait(n_chunks - 2, 0)
        write_wait(n_chunks - 1, 1)

    chunk = lambda: pltpu.VMEM((_SC_CHUNK, width), table.dtype)
    return pl.kernel(
        body,
        out_type=jax.ShapeDtypeStruct((rows, width), table.dtype),
        mesh=mesh,
        scratch_types=[pltpu.VMEM((_SC_CHUNK,), I32), pltpu.VMEM((_SC_CHUNK,), I32), chunk(), chunk(),
                       pltpu.SemaphoreType.DMA, pltpu.SemaphoreType.DMA,
                       pltpu.SemaphoreType.DMA, pltpu.SemaphoreType.DMA],
        name="sc_gather_rows",
    )(table, idx)


def _sc_scatter_rows(rows, idx_slots, out_rows):
    n, width = rows.shape
    workers = _SC_CORES * _SC_SUBCORES
    per_worker = n // workers
    assert n % (workers * _SC_CHUNK) == 0
    mesh = plsc.VectorSubcoreMesh(core_axis_name="c", subcore_axis_name="s")

    def body(rows_hbm, idx_hbm, out_hbm, idx_v, rows_v):
        wid = lax.axis_index("s") * _SC_CORES + lax.axis_index("c")
        base = wid * per_worker

        @pl.loop(0, per_worker // _SC_CHUNK)
        def _(i):
            t0 = base + i * _SC_CHUNK
            pltpu.sync_copy(rows_hbm.at[pl.ds(t0, _SC_CHUNK)], rows_v)
            for kk in range(TOP_K):
                pltpu.sync_copy(idx_hbm.at[pl.ds(kk * n + t0, _SC_CHUNK)], idx_v)
                pltpu.sync_copy(rows_v, out_hbm.at[idx_v])

    return pl.kernel(
        body,
        out_type=jax.ShapeDtypeStruct((out_rows, width), rows.dtype),
        mesh=mesh,
        scratch_types=[pltpu.VMEM((_SC_CHUNK,), I32), pltpu.VMEM((_SC_CHUNK, width), rows.dtype)],
        name="sc_scatter_rows",
    )(rows, idx_slots)


def _combine_dense_body(y4_ref, tp_ref, x1_ref, gt_ref, gf_ref, *out_refs):
    o_ref = out_refs[-1]
    half = D_MODEL // 2
    tp = tp_ref[...]
    y_lo = y_hi = None
    for kk in range(TOP_K):
        lo, hi = _unpack_pair(y4_ref[kk])
        pk = tp[:, kk:kk + 1]
        y_lo = pk * lo if y_lo is None else y_lo + pk * lo
        y_hi = pk * hi if y_hi is None else y_hi + pk * hi
    x2 = x1_ref[...] + gt_ref[0] * jnp.concatenate([y_lo, y_hi], axis=1)
    ms = jnp.mean(x2 * x2, axis=-1, keepdims=True)
    o_ref[...] = x2 * lax.rsqrt(ms + EPS) * gf_ref[...]


def _combine_dense(y4, tp, x1, gt2, gf, tokens, rows_per_batch, first_row, n_total, out_prev):
    n, d = x1.shape
    per_b = rows_per_batch // tokens
    off = first_row // tokens
    in_specs = [pl.BlockSpec((TOP_K, tokens, d // 2), lambda i: (0, i, 0)),
                pl.BlockSpec((tokens, LANES), lambda i: (i, 0)),
                pl.BlockSpec((tokens, d), lambda i: (i, 0)),
                pl.BlockSpec((1, 1, d), lambda i: ((i + off) // per_b, 0, 0)),
                pl.BlockSpec((1, d), lambda i: (0, 0))]
    args = [y4, tp, x1, gt2, gf]
    aliases = {}
    if out_prev is not None:
        in_specs.append(pl.BlockSpec(memory_space=pl.ANY))
        args.append(out_prev)
        aliases = {len(args) - 1: 0}
    return pl.pallas_call(
        _combine_dense_body,
        grid=(n // tokens,),
        in_specs=in_specs,
        out_specs=pl.BlockSpec((tokens, d), lambda i: (i + off, 0)),
        out_shape=jax.ShapeDtypeStruct((n_total, d), F32),
        input_output_aliases=aliases,
        compiler_params=_cparams(("arbitrary",)),
        name="moe_combine_dense",
    )(*args)


def _routing_tables(top_idx, rank, counts, bm):
    n = top_idx.shape[0]
    nk = n * TOP_K
    padded = (counts + bm - 1) // bm * bm
    pad_end = jnp.cumsum(padded)
    pad_start = pad_end - padded
    dest = (pad_start[top_idx] + rank).astype(I32)
    n_blocks = (nk + N_EXPERTS * (bm - 1) + bm - 1) // bm
    starts = jnp.arange(n_blocks, dtype=I32) * bm
    blk_expert = jnp.minimum(jnp.sum((pad_end[None, :] <= starts[:, None]).astype(I32), axis=1),
                             N_EXPERTS - 1).astype(I32)
    n_real = (pad_end[-1] // bm).astype(I32).reshape(1)
    return dest, blk_expert, n_real, n_blocks


def _layer(x, c, ctx, c_ctx, w_ada, b_ada, g_mix_norm, w_in, w_dw, b_dw, g_conv_ln, b_conv_ln,
           w_conv_out, w_alpha, b_alpha, g_gla_norm, w_gla_out, w_out, g_ffn_norm, w_router,
           b_router, w_exp_in, b_exp_in, w_exp_out, b_exp_out, g_final, *, cfg):
    b, s, d = x.shape
    n = b * s

    rows = (b + 1 + 7) // 8 * 8
    cc = jnp.zeros((rows, d), F32).at[:b].set(c).at[b].set(c_ctx)
    mod = _ada(cc, w_ada, b_ada)
    sh1, sc1, gt1, sh2, sc2, gt2 = [mod[:b, i * d:(i + 1) * d].reshape(b, 1, d) for i in range(6)]
    csh1 = mod[b:b + 1, 0:d]
    csc1 = mod[b:b + 1, d:2 * d]

    a0 = 2 * CONV_W + 2 * GLA_KD + 2 * GLA_VD
    w_in_r = jnp.concatenate(
        [w_in[:, :a0], w_in[:, a0 + 2 * GLA_RANK:], w_in[:, a0:a0 + 2 * GLA_RANK],
         jnp.zeros((d, LANES - 2 * GLA_RANK), F32)], axis=1).astype(BF16)
    gmn = g_mix_norm.reshape(1, d)

    act, q, k, v, sg, gc, gg, a = _inproj_lat(x, gmn, sh1, sc1, w_in_r, w_dw, b_dw, g_conv_ln,
                                              b_conv_ln, cfg["tm_in"])
    act = act.reshape(n, CONV_W)
    kc, vc, ac = _inproj_ctx(ctx, gmn, csh1, csc1, w_in_r, cfg["tm_ctx"])

    cps = cfg["gla_cps"]
    st_f = _gla_ctx(kc, vc, ac, w_alpha[0], b_alpha[0], False, cps)
    o_f = _gla_lat(k, v, a, q, w_alpha[0], b_alpha[0], st_f, False, cps)
    st_b = _gla_ctx(kc, vc, ac, w_alpha[1], b_alpha[1], True, cps)
    og = _gla_lat(k, v, a, q, w_alpha[1], b_alpha[1], st_b, True, cps,
                  sg=sg, o_prev=o_f, g_norm=g_gla_norm)

    wr = jnp.zeros((d, LANES), F32).at[:, :N_EXPERTS].set(w_router)
    br = jnp.zeros((1, LANES), F32).at[0, :N_EXPERTS].set(b_router)
    b1 = b_exp_in.reshape(N_EXPERTS, 2 * D_FF // _GLU_GROUP, LANES, 2).transpose(0, 1, 3, 2)
    b1 = b1.reshape(N_EXPERTS, 1, 2 * D_FF)
    bm = cfg["moe_block"]

    groups = cfg["moe_groups"]
    ng = n // groups
    staged = []
    for g in range(groups):
        x1, h2p, ti, tp, cnt = _merge(
            act, og.reshape(n, d), gc.reshape(n, d), gg.reshape(n, d), x.reshape(n, d),
            gt1, sh2, sc2, g_ffn_norm.reshape(1, d),
            w_conv_out.astype(BF16), w_gla_out.astype(BF16), w_out.astype(BF16), wr, br,
            cfg["tm_merge"], s, g * ng, ng)
        dest, blk_expert, n_real, n_blocks = _routing_tables(
            ti[:, :TOP_K], ti[:, TOP_K:2 * TOP_K], cnt[0, :N_EXPERTS].astype(I32), bm)
        dest_slots = dest.T.reshape(ng * TOP_K)
        xp = _sc_scatter_rows(h2p, dest_slots, n_blocks * bm)
        staged.append((x1, tp, dest_slots, blk_expert, n_real, xp))
    gathered = []
    for x1, tp, dest_slots, blk_expert, n_real, xp in staged:
        yp = _experts(blk_expert, n_real, xp, w_exp_in, b1, w_exp_out,
                      b_exp_out.reshape(N_EXPERTS, 1, d), bm)
        gathered.append(_sc_gather_rows(yp, dest_slots).reshape(TOP_K, ng, d // 2))
    out = None
    for g, (x1, tp, *_) in enumerate(staged):
        out = _combine_dense(gathered[g], tp, x1, gt2, g_final.reshape(1, d), cfg["tm_combine"], s,
                             g * ng, n, out)
    return out.reshape(b, s, d)


def _config(s, l):
    return dict(tm_in=min(512, s), tm_ctx=min(256, l), gla_cps=2,
                tm_merge=min(512, s), moe_block=512, tm_combine=min(256, s), moe_groups=2)


def kernel(x, c, ctx, c_ctx, w_ada, b_ada, g_mix_norm, w_in, w_dw, b_dw, g_conv_ln, b_conv_ln,
           w_conv_out, w_alpha, b_alpha, g_gla_norm, w_gla_out, w_out, g_ffn_norm, w_router,
           b_router, w_exp_in, b_exp_in, w_exp_out, b_exp_out, g_final):
    depth = w_ada.shape[0]
    assert depth == 1, "single-layer block: the context stream is only consumed by the GLA scan"
    cfg = _config(x.shape[1], ctx.shape[1])
    return _layer(x, c, ctx, c_ctx, w_ada[0], b_ada[0], g_mix_norm[0], w_in[0], w_dw[0], b_dw[0],
                  g_conv_ln[0], b_conv_ln[0], w_conv_out[0], w_alpha[0], b_alpha[0],
                  g_gla_norm[0], w_gla_out[0], w_out[0], g_ffn_norm[0], w_router[0], b_router[0],
                  w_exp_in[0], b_exp_in[0], w_exp_out[0], b_exp_out[0], g_final, cfg=cfg)
```

```python
import functools

import jax
import jax.numpy as jnp
from jax import lax
from jax.experimental import pallas as pl
from jax.experimental.pallas import tpu as pltpu
from jax.experimental.pallas import tpu_sc as plsc

F32 = jnp.float32
BF16 = jnp.bfloat16
I32 = jnp.int32

D_MODEL = 1024
GRID_W = 64
EPS = 1e-6
CONV_W = 1024
CONV_K = 31
GLA_H = 4
GLA_DK = 128
GLA_DV = 256
GLA_KD = GLA_H * GLA_DK
GLA_VD = GLA_H * GLA_DV
GLA_RANK = 16
GLA_TAU = 16.0
GLA_CHUNK = 64
N_EXPERTS = 32
TOP_K = 4
D_FF = 1024
SWIGLU_ALPHA = 1.702
SWIGLU_LIMIT = 7.0

LANES = 128
VMEM_LIMIT = 56 * 1024 * 1024

_C_CONV_A = 0
_C_CONV_B = _C_CONV_A + CONV_W
_C_Q = _C_CONV_B + CONV_W
_C_K = _C_Q + GLA_KD
_C_V = _C_K + GLA_KD
_C_G = _C_V + GLA_VD
_C_GC = _C_G + GLA_VD
_C_GG = _C_GC + D_MODEL
_C_A = _C_GG + D_MODEL
_C_END = _C_A + LANES


def _cparams(sem):
    return pltpu.CompilerParams(dimension_semantics=sem, vmem_limit_bytes=VMEM_LIMIT)


def _dot(a, b):
    return jnp.dot(a, b, preferred_element_type=F32)


def _split_bf16(x):
    hi = x.astype(BF16)
    lo = (x - hi.astype(F32)).astype(BF16)
    return hi, lo


def _dot3(a, b):
    a_hi, a_lo = _split_bf16(a)
    b_hi, b_lo = _split_bf16(b)
    return _dot(a_hi, b_hi) + _dot(a_lo, b_hi) + _dot(a_hi, b_lo)


def _sigmoid(x):
    return 1.0 / (1.0 + jnp.exp(-x))


def _resident(shape):
    nd = len(shape)
    return pl.BlockSpec(shape, lambda *_: (0,) * nd, pipeline_mode=pl.Buffered(1))


def _ada_body(a_ref, w_ref, b_ref, o_ref):
    a = a_ref[...]
    a = a * _sigmoid(a)
    o_ref[...] = _dot3(a, w_ref[...]) + b_ref[...]


def _ada(cc, w, b):
    rows, d = cc.shape
    n = w.shape[1]
    tn = 512
    return pl.pallas_call(
        _ada_body,
        grid=(n // tn,),
        in_specs=[pl.BlockSpec((rows, d), lambda j: (0, 0)),
                  pl.BlockSpec((d, tn), lambda j: (0, j)),
                  pl.BlockSpec((1, tn), lambda j: (0, j))],
        out_specs=pl.BlockSpec((rows, tn), lambda j: (0, j)),
        out_shape=jax.ShapeDtypeStruct((rows, n), F32),
        compiler_params=_cparams(("arbitrary",)),
        name="ada_mod",
    )(cc, w, b.reshape(1, n))


def _norm_mod(xv, gn, sc, sh):
    ms = jnp.mean(xv * xv, axis=-1, keepdims=True)
    y = xv * lax.rsqrt(ms + EPS) * gn
    return (y * (1.0 + sc) + sh).astype(BF16)


def _inproj_lat_body(x_ref, gn_ref, sh_ref, sc_ref, w_ref, sm_ref, wdw_ref, bdw_ref, gln_ref, bln_ref,
                     act_ref, q_ref, k_ref, v_ref, sg_ref, gc_ref, gg_ref, a_ref,
                     win_ref, y_ref):
    h = _norm_mod(x_ref[0], gn_ref[...], sc_ref[0], sh_ref[0])
    ca = _dot(h, w_ref[:, _C_CONV_A:_C_CONV_B])
    cb = _dot(h, w_ref[:, _C_CONV_B:_C_Q])
    u = (ca * _sigmoid(cb)).astype(BF16)
    q_ref[0] = _dot(h, w_ref[:, _C_Q:_C_K]).astype(BF16)
    k_ref[0] = _dot(h, w_ref[:, _C_K:_C_V]).astype(BF16)
    v_ref[0] = _dot(h, w_ref[:, _C_V:_C_G]).astype(BF16)
    g = _dot(h, w_ref[:, _C_G:_C_GC])
    sg_ref[0] = (g * _sigmoid(g)).astype(BF16)
    gc_ref[0] = _sigmoid(_dot(h, w_ref[:, _C_GC:_C_GG])).astype(BF16)
    gg_ref[0] = _sigmoid(_dot(h, w_ref[:, _C_GG:_C_A])).astype(BF16)
    a_ref[0] = _dot(h, w_ref[:, _C_A:_C_END])
    act_ref[0] = _conv_ln_swish(u, sm_ref, wdw_ref, bdw_ref, gln_ref, bln_ref, win_ref, y_ref)


def _inproj_ctx_body(x_ref, gn_ref, sh_ref, sc_ref, w_ref, k_ref, v_ref, a_ref):
    h = _norm_mod(x_ref[0], gn_ref[...], sc_ref[...], sh_ref[...])
    k_ref[0] = _dot(h, w_ref[:, _C_K:_C_V]).astype(BF16)
    v_ref[0] = _dot(h, w_ref[:, _C_V:_C_G]).astype(BF16)
    a_ref[0] = _dot(h, w_ref[:, _C_A:_C_END])


def _inproj_lat(x, gn, sh, sc, w, w_dw, b_dw, g_ln, b_ln, tm):
    b, s, d = x.shape
    row = lambda n: pl.BlockSpec((1, tm, n), lambda bi, i: (bi, i, 0))
    vec = pl.BlockSpec((1, 1, d), lambda bi, i: (bi, 0, 0))
    const = lambda shape: pl.BlockSpec(shape, lambda bi, i: (0,) * len(shape))
    shp = lambda n, dt: jax.ShapeDtypeStruct((b, s, n), dt)
    sm = _conv_shift_matrix()
    wpad = jnp.zeros((32, CONV_W), F32).at[:CONV_K].set(w_dw)
    return pl.pallas_call(
        _inproj_lat_body,
        grid=(b, s // tm),
        in_specs=[row(d), const((1, d)), vec, vec, _resident(w.shape),
                  const(sm.shape), const(wpad.shape), const((1, CONV_W)), const((1, CONV_W)),
                  const((1, CONV_W))],
        out_specs=[row(CONV_W), row(GLA_KD), row(GLA_KD), row(GLA_VD), row(GLA_VD),
                   row(d), row(d), row(LANES)],
        out_shape=[shp(CONV_W, BF16), shp(GLA_KD, BF16), shp(GLA_KD, BF16), shp(GLA_VD, BF16),
                   shp(GLA_VD, BF16), shp(d, BF16), shp(d, BF16), shp(LANES, F32)],
        scratch_shapes=[pltpu.VMEM((2, _CONV_SUB * _CONV_SPAN, CONV_W), F32),
                        pltpu.VMEM((tm, CONV_W), F32)],
        compiler_params=_cparams(("arbitrary", "arbitrary")),
        name="inproj_lat",
    )(x, gn, sh, sc, w, sm, wpad, b_dw.reshape(1, CONV_W), g_ln.reshape(1, CONV_W),
      b_ln.reshape(1, CONV_W))


def _inproj_ctx(ctx, gn, sh, sc, w, tm):
    b, l, d = ctx.shape
    row = lambda n: pl.BlockSpec((1, tm, n), lambda bi, i: (bi, i, 0))
    vec = pl.BlockSpec((1, d), lambda bi, i: (0, 0))
    shp = lambda n, dt: jax.ShapeDtypeStruct((b, l, n), dt)
    return pl.pallas_call(
        _inproj_ctx_body,
        grid=(b, l // tm),
        in_specs=[row(d), vec, vec, vec, _resident(w.shape)],
        out_specs=[row(GLA_KD), row(GLA_VD), row(LANES)],
        out_shape=[shp(GLA_KD, BF16), shp(GLA_VD, BF16), shp(LANES, F32)],
        compiler_params=_cparams(("arbitrary", "arbitrary")),
        name="inproj_ctx",
    )(ctx, gn, sh, sc, w)


_CONV_SUB = 8
_CONV_SPAN = GRID_W + _CONV_SUB * ((CONV_K - 1) // _CONV_SUB)


def _conv_shift_matrix():
    row = lax.broadcasted_iota(I32, (_CONV_SUB, _CONV_SPAN, GRID_W), 1)
    shift = lax.broadcasted_iota(I32, (_CONV_SUB, _CONV_SPAN, GRID_W), 0)
    col = lax.broadcasted_iota(I32, (_CONV_SUB, _CONV_SPAN, GRID_W), 2)
    m = jnp.where(col == row + shift - CONV_K // 2, 1.0, 0.0)
    return m.reshape(_CONV_SUB * _CONV_SPAN, GRID_W).astype(BF16)


def _conv_ln_swish(u, sm_ref, w_ref, bdw_ref, gln_ref, bln_ref, win_ref, y_ref):
    for r in range(u.shape[0] // GRID_W):
        buf = r % 2
        win_ref[buf] = _dot(sm_ref[...], u[r * GRID_W:(r + 1) * GRID_W, :])
        for cb in range(CONV_W // LANES):
            ls = slice(cb * LANES, (cb + 1) * LANES)
            acc = None
            for k in range(CONV_K):
                s, a = k % _CONV_SUB, k // _CONV_SUB
                r0 = s * _CONV_SPAN + _CONV_SUB * a
                term = win_ref[buf, r0:r0 + GRID_W, ls] * w_ref[k:k + 1, ls]
                acc = term if acc is None else acc + term
            y_ref[r * GRID_W:(r + 1) * GRID_W, ls] = acc + bdw_ref[:, ls]
    y = y_ref[...]
    mu = jnp.mean(y, axis=-1, keepdims=True)
    yc = y - mu
    var = jnp.mean(yc * yc, axis=-1, keepdims=True)
    yn = yc * lax.rsqrt(var + EPS) * gln_ref[...] + bln_ref[...]
    return (yn * _sigmoid(yn)).astype(BF16)


def _log_sigmoid(z):
    return jnp.minimum(z, 0.0) - jnp.log(1.0 + jnp.exp(-jnp.abs(z)))


def _gla_keep_mask(t, reverse):
    row = lax.broadcasted_iota(I32, (t, t), 0)
    col = lax.broadcasted_iota(I32, (t, t), 1)
    same_chunk = lax.shift_right_logical(row, 6) == lax.shift_right_logical(col, 6)
    return jnp.logical_and(same_chunk, (col >= row) if reverse else (col <= row))


def _gla_decay(a, wh_ref, wl_ref, ba, keep):
    a_hi, a_lo = _split_bf16(a)
    z = _dot(jnp.concatenate([a_hi, a_lo], axis=1), wh_ref[...]) + _dot(a_hi, wl_ref[...]) + ba
    loga = _log_sigmoid(z) * (1.0 / GLA_TAU)
    l_hi, l_lo = _split_bf16(loga)
    tri = jnp.where(keep, 1.0, 0.0).astype(BF16)
    return _dot(tri, l_hi) + _dot(tri, l_lo)


def _gla_step(k, v, bcum, keep, states, *, reverse, q=None):
    t = k.shape[0]
    c = GLA_CHUNK
    nch = t // c
    tots = [bcum[n * c:n * c + 1, :] if reverse else bcum[(n + 1) * c - 1:(n + 1) * c, :]
            for n in range(nch)]
    totb = jnp.concatenate([jnp.broadcast_to(tt, (c, GLA_KD)) for tt in tots], axis=0)
    k32 = k.astype(F32)
    k_end = (k32 * jnp.exp(totb - bcum)).astype(BF16)
    decs = [jnp.exp(tt) for tt in tots]
    if q is not None:
        q_dec = (q.astype(F32) * jnp.exp(bcum) * (GLA_DK ** -0.5)).astype(BF16)
        k_inv = (k32 * jnp.exp(-bcum)).astype(BF16)
    outs, new_states = [], []
    for h in range(GLA_H):
        ks = slice(h * GLA_DK, (h + 1) * GLA_DK)
        vh = v[:, h * GLA_DV:(h + 1) * GLA_DV]
        st = states[h]
        if q is not None:
            scores = lax.dot_general(q_dec[:, ks], k_inv[:, ks], (((1,), (1,)), ((), ())),
                                     preferred_element_type=F32)
            o_h = _dot(jnp.where(keep, scores, 0.0).astype(BF16), vh)
            inter = []
        for n in range(nch):
            rs = slice(n * c, (n + 1) * c)
            if q is not None:
                inter.append(_dot(q_dec[rs, ks], st.astype(BF16)))
            kv = lax.dot_general(k_end[rs, ks], vh[rs, :], (((0,), (0,)), ((), ())),
                                 preferred_element_type=F32)
            dt = jnp.transpose(jnp.broadcast_to(decs[n][:, ks], (GLA_DK, GLA_DK)))
            st = st * jnp.concatenate([dt] * (GLA_DV // GLA_DK), axis=1) + kv
        new_states.append(st)
        if q is not None:
            outs.append(o_h + jnp.concatenate(inter, axis=0))
    return new_states, (jnp.concatenate(outs, axis=1) if q is not None else None)


def _scan_order(x, reverse):
    if not reverse:
        return x
    nch = x.shape[0] // GLA_CHUNK
    return jnp.concatenate(
        [x[n * GLA_CHUNK:(n + 1) * GLA_CHUNK] for n in range(nch - 1, -1, -1)], axis=0)


def _gla_ctx_body(k_ref, v_ref, a_ref, wh_ref, wl_ref, ba_ref, st_ref, *, reverse):
    j = pl.program_id(1)

    @pl.when(j == 0)
    def _():
        st_ref[...] = jnp.zeros_like(st_ref)

    keep = _gla_keep_mask(k_ref.shape[1], reverse)
    scan = functools.partial(_scan_order, reverse=reverse)
    bcum = _gla_decay(scan(a_ref[0]), wh_ref, wl_ref, ba_ref[...], keep)
    states, _ = _gla_step(scan(k_ref[0]), scan(v_ref[0]), bcum, keep,
                          [st_ref[0, h] for h in range(GLA_H)], reverse=reverse)
    for h in range(GLA_H):
        st_ref[0, h] = states[h]


def _gla_lat_body(*refs, reverse, final):
    if final:
        (k_ref, v_ref, a_ref, an_ref, q_ref, wh_ref, wl_ref, ba_ref, s0_ref, sg_ref, op_ref,
         gn_ref, o_ref, st_ref, bc_ref) = refs
    else:
        (k_ref, v_ref, a_ref, an_ref, q_ref, wh_ref, wl_ref, ba_ref, s0_ref,
         o_ref, st_ref, bc_ref) = refs
    j = pl.program_id(1)
    nb, t = k_ref.shape[0], k_ref.shape[1]
    keep = _gla_keep_mask(t, reverse)
    scan = functools.partial(_scan_order, reverse=reverse)

    @pl.when(j == 0)
    def _():
        st_ref[...] = s0_ref[...]
        for bb in range(nb):
            bc_ref[bb] = _gla_decay(scan(a_ref[bb]), wh_ref, wl_ref, ba_ref[...], keep)

    for bb in range(nb):
        states, o = _gla_step(scan(k_ref[bb]), scan(v_ref[bb]), bc_ref[bb], keep,
                              [st_ref[bb, h] for h in range(GLA_H)], reverse=reverse,
                              q=scan(q_ref[bb]))
        bc_ref[bb] = _gla_decay(scan(an_ref[bb]), wh_ref, wl_ref, ba_ref[...], keep)
        for h in range(GLA_H):
            st_ref[bb, h] = states[h]
        if not final:
            o_ref[bb] = scan(o.astype(BF16))
            continue
        o = o + scan(op_ref[bb]).astype(F32)
        parts = []
        for h in range(GLA_H):
            oh = o[:, h * GLA_DV:(h + 1) * GLA_DV]
            ms = jnp.mean(oh * oh, axis=-1, keepdims=True)
            parts.append(oh * lax.rsqrt(ms + EPS) * gn_ref[...])
        o_ref[bb] = scan((jnp.concatenate(parts, axis=1)
                          * scan(sg_ref[bb]).astype(F32)).astype(BF16))


def _gla_decay_weights(w_alpha_d, reverse):
    a0 = GLA_RANK if reverse else 0
    w = jnp.zeros((LANES, GLA_KD), F32).at[a0:a0 + GLA_RANK].set(w_alpha_d)
    hi = w.astype(BF16)
    lo = (w - hi.astype(F32)).astype(BF16)
    return jnp.concatenate([hi, hi], axis=0), lo


def _gla_ctx(k, v, a, w_alpha_d, ba, reverse, cps):
    b, l, _ = k.shape
    tm = cps * GLA_CHUNK
    nj = l // tm
    jmap = (lambda j: nj - 1 - j) if reverse else (lambda j: j)
    row = lambda n: pl.BlockSpec((1, tm, n), lambda bi, j: (bi, jmap(j), 0))
    wh, wl = _gla_decay_weights(w_alpha_d, reverse)
    return pl.pallas_call(
        functools.partial(_gla_ctx_body, reverse=reverse),
        grid=(b, nj),
        in_specs=[row(GLA_KD), row(GLA_VD), row(LANES),
                  pl.BlockSpec(wh.shape, lambda bi, j: (0, 0)),
                  pl.BlockSpec(wl.shape, lambda bi, j: (0, 0)),
                  pl.BlockSpec((1, GLA_KD), lambda bi, j: (0, 0))],
        out_specs=pl.BlockSpec((1, GLA_H, GLA_DK, GLA_DV), lambda bi, j: (bi, 0, 0, 0)),
        out_shape=jax.ShapeDtypeStruct((b, GLA_H, GLA_DK, GLA_DV), F32),
        compiler_params=_cparams(("arbitrary", "arbitrary")),
        name="gla_ctx_bwd" if reverse else "gla_ctx_fwd",
    )(k, v, a, wh, wl, ba.reshape(1, GLA_KD))


def _gla_lat(k, v, a, q, w_alpha_d, ba, s0, reverse, cps, sg=None, o_prev=None, g_norm=None):
    b, s, _ = k.shape
    final = sg is not None
    tm = cps * GLA_CHUNK
    nj = s // tm
    jmap = (lambda j: nj - 1 - j) if reverse else (lambda j: j)
    nb = 8 if b % 8 == 0 else 2 if b % 2 == 0 else 1
    row = lambda n: pl.BlockSpec((nb, tm, n), lambda bi, j: (bi, jmap(j), 0))
    wh, wl = _gla_decay_weights(w_alpha_d, reverse)
    a_next = pl.BlockSpec((nb, tm, LANES), lambda bi, j: (bi, jmap(jnp.minimum(j + 1, nj - 1)), 0))
    in_specs = [row(GLA_KD), row(GLA_VD), row(LANES), a_next, row(GLA_KD),
                pl.BlockSpec(wh.shape, lambda bi, j: (0, 0)),
                pl.BlockSpec(wl.shape, lambda bi, j: (0, 0)),
                pl.BlockSpec((1, GLA_KD), lambda bi, j: (0, 0)),
                pl.BlockSpec((nb, GLA_H, GLA_DK, GLA_DV), lambda bi, j: (bi, 0, 0, 0))]
    args = [k, v, a, a, q, wh, wl, ba.reshape(1, GLA_KD), s0]
    if final:
        in_specs += [row(GLA_VD), row(GLA_VD), pl.BlockSpec((1, GLA_DV), lambda bi, j: (0, 0))]
        args += [sg, o_prev, g_norm.reshape(1, GLA_DV)]
    return pl.pallas_call(
        functools.partial(_gla_lat_body, reverse=reverse, final=final),
        grid=(b // nb, nj),
        in_specs=in_specs,
        out_specs=row(GLA_VD),
        out_shape=jax.ShapeDtypeStruct((b, s, GLA_VD), BF16),
        scratch_shapes=[pltpu.VMEM((nb, GLA_H, GLA_DK, GLA_DV), F32),
                        pltpu.VMEM((nb, tm, GLA_KD), F32)],
        compiler_params=_cparams(("arbitrary", "arbitrary")),
        name="gla_lat_bwd" if reverse else "gla_lat_fwd",
    )(*args)


def _pack_pair(lo, hi):
    lo_b = pltpu.bitcast(lo.astype(BF16).astype(F32), I32)
    hi_b = pltpu.bitcast(hi.astype(BF16).astype(F32), I32)
    return lax.shift_right_logical(lo_b, 16) | (hi_b & jnp.int32(-65536))


def _unpack_pair(w):
    lo = pltpu.bitcast(lax.shift_left(w, 16), F32)
    hi = pltpu.bitcast(w & jnp.int32(-65536), F32)
    return lo, hi


def _merge_body(act_ref, og_ref, gc_ref, gg_ref, x_ref, gt_ref, sh_ref, sc_ref, gn_ref,
                wc_ref, wg_ref, wo_ref, wr_ref, br_ref,
                x1_ref, h2_ref, ti_ref, tp_ref, cnt_ref):
    yc = _dot(act_ref[...], wc_ref[...])
    yg = _dot(og_ref[...], wg_ref[...])
    m = gc_ref[...].astype(F32) * yc + gg_ref[...].astype(F32) * yg
    x1 = x_ref[...] + gt_ref[0] * _dot(m.astype(BF16), wo_ref[...])
    x1_ref[...] = x1
    ms = jnp.mean(x1 * x1, axis=-1, keepdims=True)
    h2 = (x1 * lax.rsqrt(ms + EPS) * gn_ref[...]) * (1.0 + sc_ref[0]) + sh_ref[0]
    half = D_MODEL // 2
    h2_ref[...] = _pack_pair(h2[:, :half], h2[:, half:])
    logits = _dot3(h2, wr_ref[...]) + br_ref[...]
    lane = lax.broadcasted_iota(I32, logits.shape, 1).astype(F32)
    neg = jnp.float32(-jnp.inf)
    work = jnp.where(lane < N_EXPERTS, logits, neg)
    vals, idxs = [], []
    for _ in range(TOP_K):
        mx = jnp.max(work, axis=-1, keepdims=True)
        ix = jnp.min(jnp.where(work == mx, lane, float(LANES)), axis=-1, keepdims=True)
        vals.append(mx)
        idxs.append(ix)
        work = jnp.where(lane == ix, neg, work)
    es = [jnp.exp(v - vals[0]) for v in vals]
    den = es[0] + es[1] + es[2] + es[3]
    ti = jnp.zeros(logits.shape, F32)
    tp = jnp.zeros(logits.shape, F32)
    onehot = jnp.zeros(logits.shape, F32)
    for kk in range(TOP_K):
        ti = jnp.where(lane == kk, idxs[kk], ti)
        tp = jnp.where(lane == kk, es[kk] / den, tp)
        onehot = onehot + jnp.where(lane == idxs[kk], 1.0, 0.0)
    @pl.when(pl.program_id(0) == 0)
    def _():
        cnt_ref[...] = jnp.zeros_like(cnt_ref)

    tm = logits.shape[0]
    earlier = (lax.broadcasted_iota(I32, (tm, tm), 1) < lax.broadcasted_iota(I32, (tm, tm), 0))
    before = _dot(jnp.where(earlier, 1.0, 0.0).astype(BF16), onehot.astype(BF16)) + cnt_ref[0:1, :]
    for kk in range(TOP_K):
        rank = jnp.sum(jnp.where(lane == idxs[kk], before, 0.0), axis=-1, keepdims=True)
        ti = jnp.where(lane == TOP_K + kk, rank, ti)
    cnt_ref[...] = cnt_ref[...] + jnp.sum(onehot, axis=0, keepdims=True)
    ti_ref[...] = ti.astype(I32)
    tp_ref[...] = tp


def _merge(act, og, gc, gg, x2d, gt1, sh2, sc2, gn, wc, wg, wo, wr, br, tm, rows_per_batch,
           first_row, n):
    d = x2d.shape[1]
    per_b = rows_per_batch // tm
    off = first_row // tm
    row_in = lambda w: pl.BlockSpec((tm, w), lambda i: (i + off, 0))
    row = lambda w: pl.BlockSpec((tm, w), lambda i: (i, 0))
    vec = pl.BlockSpec((1, 1, d), lambda i: ((i + off) // per_b, 0, 0))
    const = lambda shape: pl.BlockSpec(shape, lambda i: (0,) * len(shape))
    return pl.pallas_call(
        _merge_body,
        grid=(n // tm,),
        in_specs=[row_in(d), row_in(d), row_in(d), row_in(d), row_in(d), vec, vec, vec,
                  const((1, d)),
                  _resident(wc.shape), _resident(wg.shape), _resident(wo.shape),
                  const(wr.shape), const((1, LANES))],
        out_specs=[row(d), row(d // 2), row(LANES), row(LANES), const((8, LANES))],
        out_shape=[jax.ShapeDtypeStruct((n, d), F32), jax.ShapeDtypeStruct((n, d // 2), I32),
                   jax.ShapeDtypeStruct((n, LANES), I32), jax.ShapeDtypeStruct((n, LANES), F32),
                   jax.ShapeDtypeStruct((8, LANES), F32)],
        compiler_params=_cparams(("arbitrary",)),
        name="merge_router",
    )(act, og, gc, gg, x2d, gt1, sh2, sc2, gn, wc, wg, wo, wr, br)


_GLU_GROUP = 2 * LANES


def _deinterleave_matrix():
    src = lax.broadcasted_iota(I32, (_GLU_GROUP, _GLU_GROUP), 0)
    dst = lax.broadcasted_iota(I32, (_GLU_GROUP, _GLU_GROUP), 1)
    want = jnp.where(dst < LANES, 2 * dst, 2 * (dst - LANES) + 1)
    return jnp.where(src == want, 1.0, 0.0).astype(BF16)


def _expert_body(be_ref, nreal_ref, xp_ref, w1_ref, b1_ref, w2_ref, b2_ref, yp_ref,
                 w1s_ref, w2s_ref):
    i = pl.program_id(0)
    new_expert = jnp.logical_or(i == 0, be_ref[i] != be_ref[jnp.maximum(i - 1, 0)])

    @pl.when(jnp.logical_and(new_expert, i < nreal_ref[0]))
    def _():
        perm = _deinterleave_matrix()
        for g in range(2 * D_FF // _GLU_GROUP):
            cs = slice(g * _GLU_GROUP, (g + 1) * _GLU_GROUP)
            w1s_ref[:, cs] = _dot(w1_ref[0, :, cs].astype(BF16), perm).astype(BF16)
        w2s_ref[...] = w2_ref[0].astype(BF16)

    @pl.when(i < nreal_ref[0])
    def _():
        lo, hi = _unpack_pair(xp_ref[...])
        x = jnp.concatenate([lo, hi], axis=1).astype(BF16)
        hid = _dot(x, w1s_ref[...]) + b1_ref[0]
        ngrp = 2 * D_FF // _GLU_GROUP
        hg = jnp.concatenate(
            [hid[:, g * _GLU_GROUP:g * _GLU_GROUP + LANES] for g in range(ngrp)], axis=1)
        hl = jnp.concatenate(
            [hid[:, g * _GLU_GROUP + LANES:(g + 1) * _GLU_GROUP] for g in range(ngrp)], axis=1)
        xg = jnp.minimum(hg, SWIGLU_LIMIT)
        xl = jnp.clip(hl, -SWIGLU_LIMIT, SWIGLU_LIMIT)
        act = xg * _sigmoid(SWIGLU_ALPHA * xg) * (xl + 1.0)
        y = _dot(act.astype(BF16), w2s_ref[...]) + b2_ref[0]
        half = D_MODEL // 2
        yp_ref[...] = _pack_pair(y[:, :half], y[:, half:])

    @pl.when(i >= nreal_ref[0])
    def _():
        yp_ref[...] = jnp.zeros_like(yp_ref)


def _experts(blk_expert, n_real, xp, w1, b1, w2, b2, bm):
    p, w = xp.shape
    grid_spec = pltpu.PrefetchScalarGridSpec(
        num_scalar_prefetch=2,
        grid=(p // bm,),
        in_specs=[pl.BlockSpec((bm, w), lambda i, be, nr: (i, 0)),
                  pl.BlockSpec((1, D_MODEL, 2 * D_FF), lambda i, be, nr: (be[i], 0, 0)),
                  pl.BlockSpec((1, 1, 2 * D_FF), lambda i, be, nr: (be[i], 0, 0)),
                  pl.BlockSpec((1, D_FF, D_MODEL), lambda i, be, nr: (be[i], 0, 0)),
                  pl.BlockSpec((1, 1, D_MODEL), lambda i, be, nr: (be[i], 0, 0))],
        out_specs=pl.BlockSpec((bm, w), lambda i, be, nr: (i, 0)),
        scratch_shapes=[pltpu.VMEM((D_MODEL, 2 * D_FF), BF16), pltpu.VMEM((D_FF, D_MODEL), BF16)],
    )
    return pl.pallas_call(
        _expert_body,
        grid_spec=grid_spec,
        out_shape=jax.ShapeDtypeStruct((p, w), I32),
        compiler_params=_cparams(("arbitrary",)),
        name="moe_experts",
    )(blk_expert, n_real, xp, w1, b1, w2, b2)


_SC_CORES = 2
_SC_SUBCORES = 16
_SC_CHUNK = 64


def _sc_gather_rows(table, idx):
    rows, width = idx.shape[0], table.shape[1]
    workers = _SC_CORES * _SC_SUBCORES
    per_worker = rows // workers
    assert rows % (workers * _SC_CHUNK) == 0
    mesh = plsc.VectorSubcoreMesh(core_axis_name="c", subcore_axis_name="s")

    n_chunks = per_worker // _SC_CHUNK
    assert n_chunks % 2 == 0 and n_chunks >= 4

    def body(table_hbm, idx_hbm, out_hbm, idx0, idx1, rows0, rows1, gsem0, gsem1, wsem0, wsem1):
        wid = lax.axis_index("s") * _SC_CORES + lax.axis_index("c")
        base = wid * per_worker
        bufs = ((idx0, rows0, gsem0, wsem0), (idx1, rows1, gsem1, wsem1))

        def out_rows(c):
            return out_hbm.at[pl.ds(base + c * _SC_CHUNK, _SC_CHUNK)]

        def gather_start(c, b):
            idx_v, rows_v, gsem, _ = bufs[b]
            pltpu.sync_copy(idx_hbm.at[pl.ds(base + c * _SC_CHUNK, _SC_CHUNK)], idx_v)
            pltpu.make_async_copy(table_hbm.at[idx_v], rows_v, gsem).start()

        def gather_wait(b):
            idx_v, rows_v, gsem, _ = bufs[b]
            pltpu.make_async_copy(table_hbm.at[idx_v], rows_v, gsem).wait()

        def write_start(c, b):
            _, rows_v, _, wsem = bufs[b]
            pltpu.make_async_copy(rows_v, out_rows(c), wsem).start()

        def write_wait(c, b):
            _, rows_v, _, wsem = bufs[b]
            pltpu.make_async_copy(rows_v, out_rows(c), wsem).wait()

        gather_start(0, 0)
        gather_wait(0)
        write_start(0, 0)
        gather_start(1, 1)

        @pl.loop(1, n_chunks - 1, step=2)
        def _(c):
            gather_wait(1)
            write_start(c, 1)
            write_wait(c - 1, 0)
            gather_start(c + 1, 0)
            gather_wait(0)
            write_start(c + 1, 0)
            write_wait(c, 1)
            gather_start(c + 2, 1)

        gather_wait(1)
        write_start(n_chunks - 1, 1)
        write_wait(n_chunks - 2, 0)
        write_wait(n_chunks - 1, 1)

    chunk = lambda: pltpu.VMEM((_SC_CHUNK, width), table.dtype)
    return pl.kernel(
        body,
        out_type=jax.ShapeDtypeStruct((rows, width), table.dtype),
        mesh=mesh,
        scratch_types=[pltpu.VMEM((_SC_CHUNK,), I32), pltpu.VMEM((_SC_CHUNK,), I32), chunk(), chunk(),
                       pltpu.SemaphoreType.DMA, pltpu.SemaphoreType.DMA,
                       pltpu.SemaphoreType.DMA, pltpu.SemaphoreType.DMA],
        name="sc_gather_rows",
    )(table, idx)


def _sc_scatter_rows(rows, idx_slots, out_rows):
    n, width = rows.shape
    workers = _SC_CORES * _SC_SUBCORES
    per_worker = n // workers
    assert n % (workers * _SC_CHUNK) == 0
    mesh = plsc.VectorSubcoreMesh(core_axis_name="c", subcore_axis_name="s")

    def body(rows_hbm, idx_hbm, out_hbm, idx_v, rows_v):
        wid = lax.axis_index("s") * _SC_CORES + lax.axis_index("c")
        base = wid * per_worker

        @pl.loop(0, per_worker // _SC_CHUNK)
        def _(i):
            t0 = base + i * _SC_CHUNK
            pltpu.sync_copy(rows_hbm.at[pl.ds(t0, _SC_CHUNK)], rows_v)
            for kk in range(TOP_K):
                pltpu.sync_copy(idx_hbm.at[pl.ds(kk * n + t0, _SC_CHUNK)], idx_v)
                pltpu.sync_copy(rows_v, out_hbm.at[idx_v])

    return pl.kernel(
        body,
        out_type=jax.ShapeDtypeStruct((out_rows, width), rows.dtype),
        mesh=mesh,
        scratch_types=[pltpu.VMEM((_SC_CHUNK,), I32), pltpu.VMEM((_SC_CHUNK, width), rows.dtype)],
        name="sc_scatter_rows",
    )(rows, idx_slots)


def _combine_dense_body(y4_ref, tp_ref, x1_ref, gt_ref, gf_ref, *out_refs):
    o_ref = out_refs[-1]
    half = D_MODEL // 2
    tp = tp_ref[...]
    y_lo = y_hi = None
    for kk in range(TOP_K):
        lo, hi = _unpack_pair(y4_ref[kk])
        pk = tp[:, kk:kk + 1]
        y_lo = pk * lo if y_lo is None else y_lo + pk * lo
        y_hi = pk * hi if y_hi is None else y_hi + pk * hi
    x2 = x1_ref[...] + gt_ref[0] * jnp.concatenate([y_lo, y_hi], axis=1)
    ms = jnp.mean(x2 * x2, axis=-1, keepdims=True)
    o_ref[...] = x2 * lax.rsqrt(ms + EPS) * gf_ref[...]


def _combine_dense(y4, tp, x1, gt2, gf, tokens, rows_per_batch, first_row, n_total, out_prev):
    n, d = x1.shape
    per_b = rows_per_batch // tokens
    off = first_row // tokens
    in_specs = [pl.BlockSpec((TOP_K, tokens, d // 2), lambda i: (0, i, 0)),
                pl.BlockSpec((tokens, LANES), lambda i: (i, 0)),
                pl.BlockSpec((tokens, d), lambda i: (i, 0)),
                pl.BlockSpec((1, 1, d), lambda i: ((i + off) // per_b, 0, 0)),
                pl.BlockSpec((1, d), lambda i: (0, 0))]
    args = [y4, tp, x1, gt2, gf]
    aliases = {}
    if out_prev is not None:
        in_specs.append(pl.BlockSpec(memory_space=pl.ANY))
        args.append(out_prev)
        aliases = {len(args) - 1: 0}
    return pl.pallas_call(
        _combine_dense_body,
        grid=(n // tokens,),
        in_specs=in_specs,
        out_specs=pl.BlockSpec((tokens, d), lambda i: (i + off, 0)),
        out_shape=jax.ShapeDtypeStruct((n_total, d), F32),
        input_output_aliases=aliases,
        compiler_params=_cparams(("arbitrary",)),
        name="moe_combine_dense",
    )(*args)


def _routing_tables(top_idx, rank, counts, bm):
    n = top_idx.shape[0]
    nk = n * TOP_K
    padded = (counts + bm - 1) // bm * bm
    pad_end = jnp.cumsum(padded)
    pad_start = pad_end - padded
    dest = (pad_start[top_idx] + rank).astype(I32)
    n_blocks = (nk + N_EXPERTS * (bm - 1) + bm - 1) // bm
    starts = jnp.arange(n_blocks, dtype=I32) * bm
    blk_expert = jnp.minimum(jnp.sum((pad_end[None, :] <= starts[:, None]).astype(I32), axis=1),
                             N_EXPERTS - 1).astype(I32)
    n_real = (pad_end[-1] // bm).astype(I32).reshape(1)
    return dest, blk_expert, n_real, n_blocks


def _layer(x, c, ctx, c_ctx, w_ada, b_ada, g_mix_norm, w_in, w_dw, b_dw, g_conv_ln, b_conv_ln,
           w_conv_out, w_alpha, b_alpha, g_gla_norm, w_gla_out, w_out, g_ffn_norm, w_router,
           b_router, w_exp_in, b_exp_in, w_exp_out, b_exp_out, g_final, *, cfg):
    b, s, d = x.shape
    n = b * s

    rows = (b + 1 + 7) // 8 * 8
    cc = jnp.zeros((rows, d), F32).at[:b].set(c).at[b].set(c_ctx)
    mod = _ada(cc, w_ada, b_ada)
    sh1, sc1, gt1, sh2, sc2, gt2 = [mod[:b, i * d:(i + 1) * d].reshape(b, 1, d) for i in range(6)]
    csh1 = mod[b:b + 1, 0:d]
    csc1 = mod[b:b + 1, d:2 * d]

    a0 = 2 * CONV_W + 2 * GLA_KD + 2 * GLA_VD
    w_in_r = jnp.concatenate(
        [w_in[:, :a0], w_in[:, a0 + 2 * GLA_RANK:], w_in[:, a0:a0 + 2 * GLA_RANK],
         jnp.zeros((d, LANES - 2 * GLA_RANK), F32)], axis=1).astype(BF16)
    gmn = g_mix_norm.reshape(1, d)

    act, q, k, v, sg, gc, gg, a = _inproj_lat(x, gmn, sh1, sc1, w_in_r, w_dw, b_dw, g_conv_ln,
                                              b_conv_ln, cfg["tm_in"])
    act = act.reshape(n, CONV_W)
    kc, vc, ac = _inproj_ctx(ctx, gmn, csh1, csc1, w_in_r, cfg["tm_ctx"])

    cps, cps_ctx = cfg["gla_cps"], cfg["gla_cps_ctx"]
    st_f = _gla_ctx(kc, vc, ac, w_alpha[0], b_alpha[0], False, cps_ctx)
    o_f = _gla_lat(k, v, a, q, w_alpha[0], b_alpha[0], st_f, False, cps)
    st_b = _gla_ctx(kc, vc, ac, w_alpha[1], b_alpha[1], True, cps_ctx)
    og = _gla_lat(k, v, a, q, w_alpha[1], b_alpha[1], st_b, True, cps,
                  sg=sg, o_prev=o_f, g_norm=g_gla_norm)

    wr = jnp.zeros((d, LANES), F32).at[:, :N_EXPERTS].set(w_router)
    br = jnp.zeros((1, LANES), F32).at[0, :N_EXPERTS].set(b_router)
    b1 = b_exp_in.reshape(N_EXPERTS, 2 * D_FF // _GLU_GROUP, LANES, 2).transpose(0, 1, 3, 2)
    b1 = b1.reshape(N_EXPERTS, 1, 2 * D_FF)
    bm = cfg["moe_block"]

    groups = cfg["moe_groups"]
    ng = n // groups
    staged = []
    for g in range(groups):
        x1, h2p, ti, tp, cnt = _merge(
            act, og.reshape(n, d), gc.reshape(n, d), gg.reshape(n, d), x.reshape(n, d),
            gt1, sh2, sc2, g_ffn_norm.reshape(1, d),
            w_conv_out.astype(BF16), w_gla_out.astype(BF16), w_out.astype(BF16), wr, br,
            cfg["tm_merge"], s, g * ng, ng)
        dest, blk_expert, n_real, n_blocks = _routing_tables(
            ti[:, :TOP_K], ti[:, TOP_K:2 * TOP_K], cnt[0, :N_EXPERTS].astype(I32), bm)
        dest_slots = dest.T.reshape(ng * TOP_K)
        xp = _sc_scatter_rows(h2p, dest_slots, n_blocks * bm)
        staged.append((x1, tp, dest_slots, blk_expert, n_real, xp))
    gathered = []
    for x1, tp, dest_slots, blk_expert, n_real, xp in staged:
        yp = _experts(blk_expert, n_real, xp, w_exp_in, b1, w_exp_out,
                      b_exp_out.reshape(N_EXPERTS, 1, d), bm)
        gathered.append(_sc_gather_rows(yp, dest_slots).reshape(TOP_K, ng, d // 2))
    out = None
    for g, (x1, tp, *_) in enumerate(staged):
        out = _combine_dense(gathered[g], tp, x1, gt2, g_final.reshape(1, d), cfg["tm_combine"], s,
                             g * ng, n, out)
    return out.reshape(b, s, d)


def _config(s, l):
    return dict(tm_in=min(512, s), tm_ctx=min(256, l), gla_cps=2, gla_cps_ctx=4,
                tm_merge=min(512, s), moe_block=512, tm_combine=min(256, s), moe_groups=2)


def kernel(x, c, ctx, c_ctx, w_ada, b_ada, g_mix_norm, w_in, w_dw, b_dw, g_conv_ln, b_conv_ln,
           w_conv_out, w_alpha, b_alpha, g_gla_norm, w_gla_out, w_out, g_ffn_norm, w_router,
           b_router, w_exp_in, b_exp_in, w_exp_out, b_exp_out, g_final):
    depth = w_ada.shape[0]
    assert depth == 1, "single-layer block: the context stream is only consumed by the GLA scan"
    cfg = _config(x.shape[1], ctx.shape[1])
    return _layer(x, c, ctx, c_ctx, w_ada[0], b_ada[0], g_mix_norm[0], w_in[0], w_dw[0], b_dw[0],
                  g_conv_ln[0], b_conv_ln[0], w_conv_out[0], w_alpha[0], b_alpha[0],
                  g_gla_norm[0], w_gla_out[0], w_out[0], g_ffn_norm[0], w_router[0], b_router[0],
                  w_exp_in[0], b_exp_in[0], w_exp_out[0], b_exp_out[0], g_final, cfg=cfg)
```

```python
import functools

import jax
import jax.numpy as jnp
from jax import lax
from jax.experimental import pallas as pl
from jax.experimental.pallas import tpu as pltpu
from jax.experimental.pallas import tpu_sc as plsc

F32 = jnp.float32
BF16 = jnp.bfloat16
I32 = jnp.int32

D_MODEL = 1024
GRID_W = 64
EPS = 1e-6
CONV_W = 1024
CONV_K = 31
GLA_H = 4
GLA_DK = 128
GLA_DV = 256
GLA_KD = GLA_H * GLA_DK
GLA_VD = GLA_H * GLA_DV
GLA_RANK = 16
GLA_TAU = 16.0
GLA_CHUNK = 64
N_EXPERTS = 32
TOP_K = 4
D_FF = 1024
SWIGLU_ALPHA = 1.702
SWIGLU_LIMIT = 7.0

LANES = 128
SUBLANES = 8
VMEM_LIMIT = 56 * 1024 * 1024

_C_CONV_A = 0
_C_CONV_B = _C_CONV_A + CONV_W
_C_Q = _C_CONV_B + CONV_W
_C_K = _C_Q + GLA_KD
_C_V = _C_K + GLA_KD
_C_G = _C_V + GLA_VD
_C_GC = _C_G + GLA_VD
_C_GG = _C_GC + D_MODEL
_C_A = _C_GG + D_MODEL
_C_END = _C_A + LANES


def _cparams(sem):
    return pltpu.CompilerParams(dimension_semantics=sem, vmem_limit_bytes=VMEM_LIMIT)


def _dot(a, b):
    return jnp.dot(a, b, preferred_element_type=F32)


def _split_bf16(x):
    hi = x.astype(BF16)
    lo = (x - hi.astype(F32)).astype(BF16)
    return hi, lo


def _dot3(a, b):
    a_hi, a_lo = _split_bf16(a)
    b_hi, b_lo = _split_bf16(b)
    return _dot(a_hi, b_hi) + _dot(a_lo, b_hi) + _dot(a_hi, b_lo)


def _sigmoid(x):
    return 1.0 / (1.0 + jnp.exp(-x))


def _resident(shape):
    nd = len(shape)
    return pl.BlockSpec(shape, lambda *_: (0,) * nd, pipeline_mode=pl.Buffered(1))


def _ada_body(a_ref, w_ref, b_ref, o_ref):
    a = a_ref[...]
    a = a * _sigmoid(a)
    o_ref[...] = _dot3(a, w_ref[...]) + b_ref[...]


def _ada(cc, w, b):
    rows, d = cc.shape
    n = w.shape[1]
    tn = 512
    return pl.pallas_call(
        _ada_body,
        grid=(n // tn,),
        in_specs=[pl.BlockSpec((rows, d), lambda j: (0, 0)),
                  pl.BlockSpec((d, tn), lambda j: (0, j)),
                  pl.BlockSpec((1, tn), lambda j: (0, j))],
        out_specs=pl.BlockSpec((rows, tn), lambda j: (0, j)),
        out_shape=jax.ShapeDtypeStruct((rows, n), F32),
        compiler_params=_cparams(("arbitrary",)),
        name="ada_mod",
    )(cc, w, b.reshape(1, n))


def _norm_mod(xv, gn, sc, sh):
    ms = jnp.mean(xv * xv, axis=-1, keepdims=True)
    y = xv * lax.rsqrt(ms + EPS) * gn
    return (y * (1.0 + sc) + sh).astype(BF16)


def _inproj_lat_body(x_ref, gn_ref, sh_ref, sc_ref, w_ref, sm_ref, wdw_ref, bdw_ref, gln_ref, bln_ref,
                     act_ref, q_ref, k_ref, v_ref, sg_ref, gc_ref, gg_ref, a_ref,
                     y_ref):
    h = _norm_mod(x_ref[0], gn_ref[...], sc_ref[0], sh_ref[0])
    ca = _dot(h, w_ref[:, _C_CONV_A:_C_CONV_B])
    cb = _dot(h, w_ref[:, _C_CONV_B:_C_Q])
    u = (ca * _sigmoid(cb)).astype(BF16)
    q_ref[0] = _dot(h, w_ref[:, _C_Q:_C_K]).astype(BF16)
    k_ref[0] = _dot(h, w_ref[:, _C_K:_C_V]).astype(BF16)
    v_ref[0] = _dot(h, w_ref[:, _C_V:_C_G]).astype(BF16)
    g = _dot(h, w_ref[:, _C_G:_C_GC])
    sg_ref[0] = (g * _sigmoid(g)).astype(BF16)
    gc_ref[0] = _sigmoid(_dot(h, w_ref[:, _C_GC:_C_GG])).astype(BF16)
    gg_ref[0] = _sigmoid(_dot(h, w_ref[:, _C_GG:_C_A])).astype(BF16)
    a_ref[0] = _dot(h, w_ref[:, _C_A:_C_END])
    act_ref[0] = _conv_ln_swish(u, sm_ref, wdw_ref, bdw_ref, gln_ref, bln_ref, y_ref)


def _inproj_ctx_body(x_ref, gn_ref, sh_ref, sc_ref, w_ref, k_ref, v_ref, a_ref):
    h = _norm_mod(x_ref[0], gn_ref[...], sc_ref[...], sh_ref[...])
    k_ref[0] = _dot(h, w_ref[:, _C_K:_C_V]).astype(BF16)
    v_ref[0] = _dot(h, w_ref[:, _C_V:_C_G]).astype(BF16)
    a_ref[0] = _dot(h, w_ref[:, _C_A:_C_END])


def _inproj_lat(x, gn, sh, sc, w, w_dw, b_dw, g_ln, b_ln, tm):
    b, s, d = x.shape
    row = lambda n: pl.BlockSpec((1, tm, n), lambda bi, i: (bi, i, 0))
    vec = pl.BlockSpec((1, 1, d), lambda bi, i: (bi, 0, 0))
    const = lambda shape: pl.BlockSpec(shape, lambda bi, i: (0,) * len(shape))
    shp = lambda n, dt: jax.ShapeDtypeStruct((b, s, n), dt)
    sm = _conv_shift_matrix()
    taps = (CONV_K + SUBLANES - 1) // SUBLANES * SUBLANES
    wpad = jnp.zeros((taps, CONV_W), F32).at[:CONV_K].set(w_dw)
    return pl.pallas_call(
        _inproj_lat_body,
        grid=(b, s // tm),
        in_specs=[row(d), const((1, d)), vec, vec, _resident(w.shape),
                  const(sm.shape), const(wpad.shape), const((1, CONV_W)), const((1, CONV_W)),
                  const((1, CONV_W))],
        out_specs=[row(CONV_W), row(GLA_KD), row(GLA_KD), row(GLA_VD), row(GLA_VD),
                   row(d), row(d), row(LANES)],
        out_shape=[shp(CONV_W, BF16), shp(GLA_KD, BF16), shp(GLA_KD, BF16), shp(GLA_VD, BF16),
                   shp(GLA_VD, BF16), shp(d, BF16), shp(d, BF16), shp(LANES, F32)],
        scratch_shapes=[pltpu.VMEM((tm, CONV_W), F32)],
        compiler_params=_cparams(("arbitrary", "arbitrary")),
        name="inproj_lat",
    )(x, gn, sh, sc, w, sm, wpad, b_dw.reshape(1, CONV_W), g_ln.reshape(1, CONV_W),
      b_ln.reshape(1, CONV_W))


def _inproj_ctx(ctx, gn, sh, sc, w, tm):
    b, l, d = ctx.shape
    row = lambda n: pl.BlockSpec((1, tm, n), lambda bi, i: (bi, i, 0))
    vec = pl.BlockSpec((1, d), lambda bi, i: (0, 0))
    shp = lambda n, dt: jax.ShapeDtypeStruct((b, l, n), dt)
    return pl.pallas_call(
        _inproj_ctx_body,
        grid=(b, l // tm),
        in_specs=[row(d), vec, vec, vec, _resident(w.shape)],
        out_specs=[row(GLA_KD), row(GLA_VD), row(LANES)],
        out_shape=[shp(GLA_KD, BF16), shp(GLA_VD, BF16), shp(LANES, F32)],
        compiler_params=_cparams(("arbitrary", "arbitrary")),
        name="inproj_ctx",
    )(ctx, gn, sh, sc, w)


_CONV_SUB = SUBLANES
_CONV_SPAN = 96
_CONV_LANES = 2 * LANES


def _conv_shift_matrix():
    row = lax.broadcasted_iota(I32, (_CONV_SUB, _CONV_SPAN, GRID_W), 1)
    shift = lax.broadcasted_iota(I32, (_CONV_SUB, _CONV_SPAN, GRID_W), 0)
    col = lax.broadcasted_iota(I32, (_CONV_SUB, _CONV_SPAN, GRID_W), 2)
    m = jnp.where(col == row + shift - CONV_K // 2, 1.0, 0.0)
    return m.reshape(_CONV_SUB * _CONV_SPAN, GRID_W).astype(BF16)


def _conv_ln_swish(u, sm_ref, w_ref, bdw_ref, gln_ref, bln_ref, y_ref):
    for r in range(u.shape[0] // GRID_W):
        ur = u[r * GRID_W:(r + 1) * GRID_W, :]
        for cq in range(CONV_W // _CONV_LANES):
            ls = slice(cq * _CONV_LANES, (cq + 1) * _CONV_LANES)
            acc = None
            for s in range(_CONV_SUB):
                win = _dot(sm_ref[s * _CONV_SPAN:(s + 1) * _CONV_SPAN, :], ur[:, ls])
                for a in range((CONV_K - 1 - s) // _CONV_SUB + 1):
                    k = _CONV_SUB * a + s
                    term = win[_CONV_SUB * a:_CONV_SUB * a + GRID_W, :] * w_ref[k:k + 1, ls]
                    acc = term if acc is None else acc + term
            y_ref[r * GRID_W:(r + 1) * GRID_W, ls] = acc + bdw_ref[:, ls]
    y = y_ref[...]
    mu = jnp.mean(y, axis=-1, keepdims=True)
    yc = y - mu
    var = jnp.mean(yc * yc, axis=-1, keepdims=True)
    yn = yc * lax.rsqrt(var + EPS) * gln_ref[...] + bln_ref[...]
    return (yn * _sigmoid(yn)).astype(BF16)


def _log_sigmoid(z):
    return jnp.minimum(z, 0.0) - jnp.log(1.0 + jnp.exp(-jnp.abs(z)))


def _gla_keep_mask(t, reverse):
    row = lax.broadcasted_iota(I32, (t, t), 0)
    col = lax.broadcasted_iota(I32, (t, t), 1)
    log2c = GLA_CHUNK.bit_length() - 1
    same_chunk = lax.shift_right_logical(row, log2c) == lax.shift_right_logical(col, log2c)
    return jnp.logical_and(same_chunk, (col >= row) if reverse else (col <= row))


def _gla_decay(a, wh_ref, wl_ref, ba, keep):
    a_hi, a_lo = _split_bf16(a)
    z = _dot(jnp.concatenate([a_hi, a_lo], axis=1), wh_ref[...]) + _dot(a_hi, wl_ref[...]) + ba
    loga = _log_sigmoid(z) * (1.0 / GLA_TAU)
    l_hi, l_lo = _split_bf16(loga)
    tri = jnp.where(keep, 1.0, 0.0).astype(BF16)
    return _dot(tri, l_hi) + _dot(tri, l_lo)


def _gla_step(k, v, bcum, keep, states, *, reverse, q=None):
    t = k.shape[0]
    c = GLA_CHUNK
    nch = t // c
    tots = [bcum[n * c:n * c + 1, :] if reverse else bcum[(n + 1) * c - 1:(n + 1) * c, :]
            for n in range(nch)]
    totb = jnp.concatenate([jnp.broadcast_to(tt, (c, GLA_KD)) for tt in tots], axis=0)
    k32 = k.astype(F32)
    k_end = (k32 * jnp.exp(totb - bcum)).astype(BF16)
    decs = [jnp.exp(tt) for tt in tots]
    if q is not None:
        q_dec = (q.astype(F32) * jnp.exp(bcum) * (GLA_DK ** -0.5)).astype(BF16)
        k_inv = (k32 * jnp.exp(-bcum)).astype(BF16)
    outs, new_states = [], []
    for h in range(GLA_H):
        ks = slice(h * GLA_DK, (h + 1) * GLA_DK)
        vh = v[:, h * GLA_DV:(h + 1) * GLA_DV]
        st = states[h]
        if q is not None:
            scores = lax.dot_general(q_dec[:, ks], k_inv[:, ks], (((1,), (1,)), ((), ())),
                                     preferred_element_type=F32)
            o_h = _dot(jnp.where(keep, scores, 0.0).astype(BF16), vh)
            inter = []
        for n in range(nch):
            rs = slice(n * c, (n + 1) * c)
            if q is not None:
                inter.append(_dot(q_dec[rs, ks], st.astype(BF16)))
            kv = lax.dot_general(k_end[rs, ks], vh[rs, :], (((0,), (0,)), ((), ())),
                                 preferred_element_type=F32)
            dt = jnp.transpose(jnp.broadcast_to(decs[n][:, ks], (GLA_DK, GLA_DK)))
            st = st * jnp.concatenate([dt] * (GLA_DV // GLA_DK), axis=1) + kv
        new_states.append(st)
        if q is not None:
            outs.append(o_h + jnp.concatenate(inter, axis=0))
    return new_states, (jnp.concatenate(outs, axis=1) if q is not None else None)


def _scan_order(x, reverse):
    if not reverse:
        return x
    nch = x.shape[0] // GLA_CHUNK
    return jnp.concatenate(
        [x[n * GLA_CHUNK:(n + 1) * GLA_CHUNK] for n in range(nch - 1, -1, -1)], axis=0)


def _gla_ctx_body(k_ref, v_ref, a_ref, wh_ref, wl_ref, ba_ref, st_ref, *, reverse):
    j = pl.program_id(1)

    @pl.when(j == 0)
    def _():
        st_ref[...] = jnp.zeros_like(st_ref)

    keep = _gla_keep_mask(k_ref.shape[1], reverse)
    scan = functools.partial(_scan_order, reverse=reverse)
    bcum = _gla_decay(scan(a_ref[0]), wh_ref, wl_ref, ba_ref[...], keep)
    states, _ = _gla_step(scan(k_ref[0]), scan(v_ref[0]), bcum, keep,
                          [st_ref[0, h] for h in range(GLA_H)], reverse=reverse)
    for h in range(GLA_H):
        st_ref[0, h] = states[h]


def _gla_lat_body(*refs, reverse, final):
    if final:
        (k_ref, v_ref, a_ref, an_ref, q_ref, wh_ref, wl_ref, ba_ref, s0_ref, sg_ref, op_ref,
         gn_ref, o_ref, st_ref, bc_ref) = refs
    else:
        (k_ref, v_ref, a_ref, an_ref, q_ref, wh_ref, wl_ref, ba_ref, s0_ref,
         o_ref, st_ref, bc_ref) = refs
    j = pl.program_id(1)
    nb, t = k_ref.shape[0], k_ref.shape[1]
    keep = _gla_keep_mask(t, reverse)
    scan = functools.partial(_scan_order, reverse=reverse)

    @pl.when(j == 0)
    def _():
        st_ref[...] = s0_ref[...]
        for bb in range(nb):
            bc_ref[bb] = _gla_decay(scan(a_ref[bb]), wh_ref, wl_ref, ba_ref[...], keep)

    for bb in range(nb):
        states, o = _gla_step(scan(k_ref[bb]), scan(v_ref[bb]), bc_ref[bb], keep,
                              [st_ref[bb, h] for h in range(GLA_H)], reverse=reverse,
                              q=scan(q_ref[bb]))
        bc_ref[bb] = _gla_decay(scan(an_ref[bb]), wh_ref, wl_ref, ba_ref[...], keep)
        for h in range(GLA_H):
            st_ref[bb, h] = states[h]
        if not final:
            o_ref[bb] = scan(o.astype(BF16))
            continue
        o = o + scan(op_ref[bb]).astype(F32)
        parts = []
        for h in range(GLA_H):
            oh = o[:, h * GLA_DV:(h + 1) * GLA_DV]
            ms = jnp.mean(oh * oh, axis=-1, keepdims=True)
            parts.append(oh * lax.rsqrt(ms + EPS) * gn_ref[...])
        o_ref[bb] = scan((jnp.concatenate(parts, axis=1)
                          * scan(sg_ref[bb]).astype(F32)).astype(BF16))


def _gla_decay_weights(w_alpha_d, reverse):
    a0 = GLA_RANK if reverse else 0
    w = jnp.zeros((LANES, GLA_KD), F32).at[a0:a0 + GLA_RANK].set(w_alpha_d)
    hi = w.astype(BF16)
    lo = (w - hi.astype(F32)).astype(BF16)
    return jnp.concatenate([hi, hi], axis=0), lo


def _gla_ctx(k, v, a, w_alpha_d, ba, reverse, cps):
    b, l, _ = k.shape
    tm = cps * GLA_CHUNK
    nj = l // tm
    jmap = (lambda j: nj - 1 - j) if reverse else (lambda j: j)
    row = lambda n: pl.BlockSpec((1, tm, n), lambda bi, j: (bi, jmap(j), 0))
    wh, wl = _gla_decay_weights(w_alpha_d, reverse)
    return pl.pallas_call(
        functools.partial(_gla_ctx_body, reverse=reverse),
        grid=(b, nj),
        in_specs=[row(GLA_KD), row(GLA_VD), row(LANES),
                  pl.BlockSpec(wh.shape, lambda bi, j: (0, 0)),
                  pl.BlockSpec(wl.shape, lambda bi, j: (0, 0)),
                  pl.BlockSpec((1, GLA_KD), lambda bi, j: (0, 0))],
        out_specs=pl.BlockSpec((1, GLA_H, GLA_DK, GLA_DV), lambda bi, j: (bi, 0, 0, 0)),
        out_shape=jax.ShapeDtypeStruct((b, GLA_H, GLA_DK, GLA_DV), F32),
        compiler_params=_cparams(("arbitrary", "arbitrary")),
        name="gla_ctx_bwd" if reverse else "gla_ctx_fwd",
    )(k, v, a, wh, wl, ba.reshape(1, GLA_KD))


def _gla_lat(k, v, a, q, w_alpha_d, ba, s0, reverse, cps, sg=None, o_prev=None, g_norm=None):
    b, s, _ = k.shape
    final = sg is not None
    tm = cps * GLA_CHUNK
    nj = s // tm
    jmap = (lambda j: nj - 1 - j) if reverse else (lambda j: j)
    nb = 8 if b % 8 == 0 else 2 if b % 2 == 0 else 1
    row = lambda n: pl.BlockSpec((nb, tm, n), lambda bi, j: (bi, jmap(j), 0))
    wh, wl = _gla_decay_weights(w_alpha_d, reverse)
    a_next = pl.BlockSpec((nb, tm, LANES), lambda bi, j: (bi, jmap(jnp.minimum(j + 1, nj - 1)), 0))
    in_specs = [row(GLA_KD), row(GLA_VD), row(LANES), a_next, row(GLA_KD),
                pl.BlockSpec(wh.shape, lambda bi, j: (0, 0)),
                pl.BlockSpec(wl.shape, lambda bi, j: (0, 0)),
                pl.BlockSpec((1, GLA_KD), lambda bi, j: (0, 0)),
                pl.BlockSpec((nb, GLA_H, GLA_DK, GLA_DV), lambda bi, j: (bi, 0, 0, 0))]
    args = [k, v, a, a, q, wh, wl, ba.reshape(1, GLA_KD), s0]
    if final:
        in_specs += [row(GLA_VD), row(GLA_VD), pl.BlockSpec((1, GLA_DV), lambda bi, j: (0, 0))]
        args += [sg, o_prev, g_norm.reshape(1, GLA_DV)]
    return pl.pallas_call(
        functools.partial(_gla_lat_body, reverse=reverse, final=final),
        grid=(b // nb, nj),
        in_specs=in_specs,
        out_specs=row(GLA_VD),
        out_shape=jax.ShapeDtypeStruct((b, s, GLA_VD), BF16),
        scratch_shapes=[pltpu.VMEM((nb, GLA_H, GLA_DK, GLA_DV), F32),
                        pltpu.VMEM((nb, tm, GLA_KD), F32)],
        compiler_params=_cparams(("arbitrary", "arbitrary")),
        name="gla_lat_bwd" if reverse else "gla_lat_fwd",
    )(*args)


def _pack_pair(lo, hi):
    lo_b = pltpu.bitcast(lo.astype(BF16).astype(F32), I32)
    hi_b = pltpu.bitcast(hi.astype(BF16).astype(F32), I32)
    return lax.shift_right_logical(lo_b, 16) | (hi_b & jnp.int32(-65536))


def _unpack_pair(w):
    lo = pltpu.bitcast(lax.shift_left(w, 16), F32)
    hi = pltpu.bitcast(w & jnp.int32(-65536), F32)
    return lo, hi


def _merge_body(act_ref, og_ref, gc_ref, gg_ref, x_ref, gt_ref, sh_ref, sc_ref, gn_ref,
                wc_ref, wg_ref, wo_ref, wr_ref, br_ref,
                x1_ref, h2_ref, ti_ref, tp_ref, cnt_ref):
    yc = _dot(act_ref[...], wc_ref[...])
    yg = _dot(og_ref[...], wg_ref[...])
    m = gc_ref[...].astype(F32) * yc + gg_ref[...].astype(F32) * yg
    x1 = x_ref[...] + gt_ref[0] * _dot(m.astype(BF16), wo_ref[...])
    x1_ref[...] = x1
    ms = jnp.mean(x1 * x1, axis=-1, keepdims=True)
    h2 = (x1 * lax.rsqrt(ms + EPS) * gn_ref[...]) * (1.0 + sc_ref[0]) + sh_ref[0]
    half = D_MODEL // 2
    h2_ref[...] = _pack_pair(h2[:, :half], h2[:, half:])
    logits = _dot3(h2, wr_ref[...]) + br_ref[...]
    lane = lax.broadcasted_iota(I32, logits.shape, 1).astype(F32)
    neg = jnp.float32(-jnp.inf)
    work = jnp.where(lane < N_EXPERTS, logits, neg)
    vals, idxs = [], []
    for _ in range(TOP_K):
        mx = jnp.max(work, axis=-1, keepdims=True)
        ix = jnp.min(jnp.where(work == mx, lane, float(LANES)), axis=-1, keepdims=True)
        vals.append(mx)
        idxs.append(ix)
        work = jnp.where(lane == ix, neg, work)
    es = [jnp.exp(v - vals[0]) for v in vals]
    den = es[0] + es[1] + es[2] + es[3]
    ti = jnp.zeros(logits.shape, F32)
    tp = jnp.zeros(logits.shape, F32)
    onehot = jnp.zeros(logits.shape, F32)
    for kk in range(TOP_K):
        ti = jnp.where(lane == kk, idxs[kk], ti)
        tp = jnp.where(lane == kk, es[kk] / den, tp)
        onehot = onehot + jnp.where(lane == idxs[kk], 1.0, 0.0)
    @pl.when(pl.program_id(0) == 0)
    def _():
        cnt_ref[...] = jnp.zeros_like(cnt_ref)

    tm = logits.shape[0]
    earlier = (lax.broadcasted_iota(I32, (tm, tm), 1) < lax.broadcasted_iota(I32, (tm, tm), 0))
    before = _dot(jnp.where(earlier, 1.0, 0.0).astype(BF16), onehot.astype(BF16)) + cnt_ref[0:1, :]
    for kk in range(TOP_K):
        rank = jnp.sum(jnp.where(lane == idxs[kk], before, 0.0), axis=-1, keepdims=True)
        ti = jnp.where(lane == TOP_K + kk, rank, ti)
    cnt_ref[...] = cnt_ref[...] + jnp.sum(onehot, axis=0, keepdims=True)
    ti_ref[...] = ti.astype(I32)
    tp_ref[...] = tp


def _merge(act, og, gc, gg, x2d, gt1, sh2, sc2, gn, wc, wg, wo, wr, br, tm, rows_per_batch,
           first_row, n):
    d = x2d.shape[1]
    per_b = rows_per_batch // tm
    off = first_row // tm
    row_in = lambda w: pl.BlockSpec((tm, w), lambda i: (i + off, 0))
    row = lambda w: pl.BlockSpec((tm, w), lambda i: (i, 0))
    vec = pl.BlockSpec((1, 1, d), lambda i: ((i + off) // per_b, 0, 0))
    const = lambda shape: pl.BlockSpec(shape, lambda i: (0,) * len(shape))
    return pl.pallas_call(
        _merge_body,
        grid=(n // tm,),
        in_specs=[row_in(d), row_in(d), row_in(d), row_in(d), row_in(d), vec, vec, vec,
                  const((1, d)),
                  _resident(wc.shape), _resident(wg.shape), _resident(wo.shape),
                  const(wr.shape), const((1, LANES))],
        out_specs=[row(d), row(d // 2), row(LANES), row(LANES), const((SUBLANES, LANES))],
        out_shape=[jax.ShapeDtypeStruct((n, d), F32), jax.ShapeDtypeStruct((n, d // 2), I32),
                   jax.ShapeDtypeStruct((n, LANES), I32), jax.ShapeDtypeStruct((n, LANES), F32),
                   jax.ShapeDtypeStruct((SUBLANES, LANES), F32)],
        compiler_params=_cparams(("arbitrary",)),
        name="merge_router",
    )(act, og, gc, gg, x2d, gt1, sh2, sc2, gn, wc, wg, wo, wr, br)


_GLU_GROUP = 2 * LANES


def _deinterleave_matrix():
    src = lax.broadcasted_iota(I32, (_GLU_GROUP, _GLU_GROUP), 0)
    dst = lax.broadcasted_iota(I32, (_GLU_GROUP, _GLU_GROUP), 1)
    want = jnp.where(dst < LANES, 2 * dst, 2 * (dst - LANES) + 1)
    return jnp.where(src == want, 1.0, 0.0).astype(BF16)


def _expert_body(be_ref, nreal_ref, xp_ref, w1_ref, b1_ref, w2_ref, b2_ref, yp_ref,
                 w1s_ref, w2s_ref):
    i = pl.program_id(0)
    new_expert = jnp.logical_or(i == 0, be_ref[i] != be_ref[jnp.maximum(i - 1, 0)])

    @pl.when(jnp.logical_and(new_expert, i < nreal_ref[0]))
    def _():
        perm = _deinterleave_matrix()
        for g in range(2 * D_FF // _GLU_GROUP):
            cs = slice(g * _GLU_GROUP, (g + 1) * _GLU_GROUP)
            w1s_ref[:, cs] = _dot(w1_ref[0, :, cs].astype(BF16), perm).astype(BF16)
        w2s_ref[...] = w2_ref[0].astype(BF16)

    @pl.when(i < nreal_ref[0])
    def _():
        lo, hi = _unpack_pair(xp_ref[...])
        x = jnp.concatenate([lo, hi], axis=1).astype(BF16)
        hid = _dot(x, w1s_ref[...]) + b1_ref[0]
        ngrp = 2 * D_FF // _GLU_GROUP
        hg = jnp.concatenate(
            [hid[:, g * _GLU_GROUP:g * _GLU_GROUP + LANES] for g in range(ngrp)], axis=1)
        hl = jnp.concatenate(
            [hid[:, g * _GLU_GROUP + LANES:(g + 1) * _GLU_GROUP] for g in range(ngrp)], axis=1)
        xg = jnp.minimum(hg, SWIGLU_LIMIT)
        xl = jnp.clip(hl, -SWIGLU_LIMIT, SWIGLU_LIMIT)
        act = xg * _sigmoid(SWIGLU_ALPHA * xg) * (xl + 1.0)
        y = _dot(act.astype(BF16), w2s_ref[...]) + b2_ref[0]
        half = D_MODEL // 2
        yp_ref[...] = _pack_pair(y[:, :half], y[:, half:])

    @pl.when(i >= nreal_ref[0])
    def _():
        yp_ref[...] = jnp.zeros_like(yp_ref)


def _experts(blk_expert, n_real, xp, w1, b1, w2, b2, bm):
    p, w = xp.shape
    grid_spec = pltpu.PrefetchScalarGridSpec(
        num_scalar_prefetch=2,
        grid=(p // bm,),
        in_specs=[pl.BlockSpec((bm, w), lambda i, be, nr: (i, 0)),
                  pl.BlockSpec((1, D_MODEL, 2 * D_FF), lambda i, be, nr: (be[i], 0, 0)),
                  pl.BlockSpec((1, 1, 2 * D_FF), lambda i, be, nr: (be[i], 0, 0)),
                  pl.BlockSpec((1, D_FF, D_MODEL), lambda i, be, nr: (be[i], 0, 0)),
                  pl.BlockSpec((1, 1, D_MODEL), lambda i, be, nr: (be[i], 0, 0))],
        out_specs=pl.BlockSpec((bm, w), lambda i, be, nr: (i, 0)),
        scratch_shapes=[pltpu.VMEM((D_MODEL, 2 * D_FF), BF16), pltpu.VMEM((D_FF, D_MODEL), BF16)],
    )
    return pl.pallas_call(
        _expert_body,
        grid_spec=grid_spec,
        out_shape=jax.ShapeDtypeStruct((p, w), I32),
        compiler_params=_cparams(("arbitrary",)),
        name="moe_experts",
    )(blk_expert, n_real, xp, w1, b1, w2, b2)


_SC_CORES = 2
_SC_SUBCORES = 16
_SC_CHUNK = 64


def _sc_gather_rows(table, idx):
    rows, width = idx.shape[0], table.shape[1]
    workers = _SC_CORES * _SC_SUBCORES
    per_worker = rows // workers
    assert rows % (workers * _SC_CHUNK) == 0
    mesh = plsc.VectorSubcoreMesh(core_axis_name="c", subcore_axis_name="s")

    n_chunks = per_worker // _SC_CHUNK
    assert n_chunks % 2 == 0 and n_chunks >= 4

    def body(table_hbm, idx_hbm, out_hbm, idx0, idx1, rows0, rows1, gsem0, gsem1, wsem0, wsem1):
        wid = lax.axis_index("s") * _SC_CORES + lax.axis_index("c")
        base = wid * per_worker
        bufs = ((idx0, rows0, gsem0, wsem0), (idx1, rows1, gsem1, wsem1))

        def out_rows(c):
            return out_hbm.at[pl.ds(base + c * _SC_CHUNK, _SC_CHUNK)]

        def gather_start(c, b):
            idx_v, rows_v, gsem, _ = bufs[b]
            pltpu.sync_copy(idx_hbm.at[pl.ds(base + c * _SC_CHUNK, _SC_CHUNK)], idx_v)
            pltpu.make_async_copy(table_hbm.at[idx_v], rows_v, gsem).start()

        def gather_wait(b):
            idx_v, rows_v, gsem, _ = bufs[b]
            pltpu.make_async_copy(table_hbm.at[idx_v], rows_v, gsem).wait()

        def write_start(c, b):
            _, rows_v, _, wsem = bufs[b]
            pltpu.make_async_copy(rows_v, out_rows(c), wsem).start()

        def write_wait(c, b):
            _, rows_v, _, wsem = bufs[b]
            pltpu.make_async_copy(rows_v, out_rows(c), wsem).wait()

        gather_start(0, 0)
        gather_wait(0)
        write_start(0, 0)
        gather_start(1, 1)

        @pl.loop(1, n_chunks - 1, step=2)
        def _(c):
            gather_wait(1)
            write_start(c, 1)
            write_wait(c - 1, 0)
            gather_start(c + 1, 0)
            gather_wait(0)
            write_start(c + 1, 0)
            write_wait(c, 1)
            gather_start(c + 2, 1)

        gather_wait(1)
        write_start(n_chunks - 1, 1)
        write_wait(n_chunks - 2, 0)
        write_wait(n_chunks - 1, 1)

    chunk = lambda: pltpu.VMEM((_SC_CHUNK, width), table.dtype)
    return pl.kernel(
        body,
        out_type=jax.ShapeDtypeStruct((rows, width), table.dtype),
        mesh=mesh,
        scratch_types=[pltpu.VMEM((_SC_CHUNK,), I32), pltpu.VMEM((_SC_CHUNK,), I32), chunk(), chunk(),
                       pltpu.SemaphoreType.DMA, pltpu.SemaphoreType.DMA,
                       pltpu.SemaphoreType.DMA, pltpu.SemaphoreType.DMA],
        name="sc_gather_rows",
    )(table, idx)


def _sc_scatter_rows(rows, idx_slots, out_rows):
    n, width = rows.shape
    workers = _SC_CORES * _SC_SUBCORES
    per_worker = n // workers
    assert n % (workers * _SC_CHUNK) == 0
    mesh = plsc.VectorSubcoreMesh(core_axis_name="c", subcore_axis_name="s")

    def body(rows_hbm, idx_hbm, out_hbm, idx_v, rows_v):
        wid = lax.axis_index("s") * _SC_CORES + lax.axis_index("c")
        base = wid * per_worker

        @pl.loop(0, per_worker // _SC_CHUNK)
        def _(i):
            t0 = base + i * _SC_CHUNK
            pltpu.sync_copy(rows_hbm.at[pl.ds(t0, _SC_CHUNK)], rows_v)
            for kk in range(TOP_K):
                pltpu.sync_copy(idx_hbm.at[pl.ds(kk * n + t0, _SC_CHUNK)], idx_v)
                pltpu.sync_copy(rows_v, out_hbm.at[idx_v])

    return pl.kernel(
        body,
        out_type=jax.ShapeDtypeStruct((out_rows, width), rows.dtype),
        mesh=mesh,
        scratch_types=[pltpu.VMEM((_SC_CHUNK,), I32), pltpu.VMEM((_SC_CHUNK, width), rows.dtype)],
        name="sc_scatter_rows",
    )(rows, idx_slots)


def _combine_dense_body(y4_ref, tp_ref, x1_ref, gt_ref, gf_ref, *out_refs):
    o_ref = out_refs[-1]
    half = D_MODEL // 2
    tp = tp_ref[...]
    y_lo = y_hi = None
    for kk in range(TOP_K):
        lo, hi = _unpack_pair(y4_ref[kk])
        pk = tp[:, kk:kk + 1]
        y_lo = pk * lo if y_lo is None else y_lo + pk * lo
        y_hi = pk * hi if y_hi is None else y_hi + pk * hi
    x2 = x1_ref[...] + gt_ref[0] * jnp.concatenate([y_lo, y_hi], axis=1)
    ms = jnp.mean(x2 * x2, axis=-1, keepdims=True)
    o_ref[...] = x2 * lax.rsqrt(ms + EPS) * gf_ref[...]


def _combine_dense(y4, tp, x1, gt2, gf, tokens, rows_per_batch, first_row, n_total, out_prev):
    n, d = x1.shape
    per_b = rows_per_batch // tokens
    off = first_row // tokens
    in_specs = [pl.BlockSpec((TOP_K, tokens, d // 2), lambda i: (0, i, 0)),
                pl.BlockSpec((tokens, LANES), lambda i: (i, 0)),
                pl.BlockSpec((tokens, d), lambda i: (i, 0)),
                pl.BlockSpec((1, 1, d), lambda i: ((i + off) // per_b, 0, 0)),
                pl.BlockSpec((1, d), lambda i: (0, 0))]
    args = [y4, tp, x1, gt2, gf]
    aliases = {}
    if out_prev is not None:
        in_specs.append(pl.BlockSpec(memory_space=pl.ANY))
        args.append(out_prev)
        aliases = {len(args) - 1: 0}
    return pl.pallas_call(
        _combine_dense_body,
        grid=(n // tokens,),
        in_specs=in_specs,
        out_specs=pl.BlockSpec((tokens, d), lambda i: (i + off, 0)),
        out_shape=jax.ShapeDtypeStruct((n_total, d), F32),
        input_output_aliases=aliases,
        compiler_params=_cparams(("arbitrary",)),
        name="moe_combine_dense",
    )(*args)


def _routing_tables(top_idx, rank, counts, bm):
    n = top_idx.shape[0]
    nk = n * TOP_K
    padded = (counts + bm - 1) // bm * bm
    pad_end = jnp.cumsum(padded)
    pad_start = pad_end - padded
    dest = (pad_start[top_idx] + rank).astype(I32)
    n_blocks = (nk + N_EXPERTS * (bm - 1) + bm - 1) // bm
    starts = jnp.arange(n_blocks, dtype=I32) * bm
    blk_expert = jnp.minimum(jnp.sum((pad_end[None, :] <= starts[:, None]).astype(I32), axis=1),
                             N_EXPERTS - 1).astype(I32)
    n_real = (pad_end[-1] // bm).astype(I32).reshape(1)
    return dest, blk_expert, n_real, n_blocks


def _layer(x, c, ctx, c_ctx, w_ada, b_ada, g_mix_norm, w_in, w_dw, b_dw, g_conv_ln, b_conv_ln,
           w_conv_out, w_alpha, b_alpha, g_gla_norm, w_gla_out, w_out, g_ffn_norm, w_router,
           b_router, w_exp_in, b_exp_in, w_exp_out, b_exp_out, g_final, *, cfg):
    b, s, d = x.shape
    n = b * s

    rows = (b + 1 + SUBLANES - 1) // SUBLANES * SUBLANES
    cc = jnp.zeros((rows, d), F32).at[:b].set(c).at[b].set(c_ctx)
    mod = _ada(cc, w_ada, b_ada)
    sh1, sc1, gt1, sh2, sc2, gt2 = [mod[:b, i * d:(i + 1) * d].reshape(b, 1, d) for i in range(6)]
    csh1 = mod[b:b + 1, 0:d]
    csc1 = mod[b:b + 1, d:2 * d]

    a0 = 2 * CONV_W + 2 * GLA_KD + 2 * GLA_VD
    w_in_r = jnp.concatenate(
        [w_in[:, :a0], w_in[:, a0 + 2 * GLA_RANK:], w_in[:, a0:a0 + 2 * GLA_RANK],
         jnp.zeros((d, LANES - 2 * GLA_RANK), F32)], axis=1).astype(BF16)
    gmn = g_mix_norm.reshape(1, d)

    act, q, k, v, sg, gc, gg, a = _inproj_lat(x, gmn, sh1, sc1, w_in_r, w_dw, b_dw, g_conv_ln,
                                              b_conv_ln, cfg["tm_in"])
    act = act.reshape(n, CONV_W)
    kc, vc, ac = _inproj_ctx(ctx, gmn, csh1, csc1, w_in_r, cfg["tm_ctx"])

    cps, cps_ctx = cfg["gla_cps"], cfg["gla_cps_ctx"]
    st_f = _gla_ctx(kc, vc, ac, w_alpha[0], b_alpha[0], False, cps_ctx)
    o_f = _gla_lat(k, v, a, q, w_alpha[0], b_alpha[0], st_f, False, cps)
    st_b = _gla_ctx(kc, vc, ac, w_alpha[1], b_alpha[1], True, cps_ctx)
    og = _gla_lat(k, v, a, q, w_alpha[1], b_alpha[1], st_b, True, cps,
                  sg=sg, o_prev=o_f, g_norm=g_gla_norm)

    wr = jnp.zeros((d, LANES), F32).at[:, :N_EXPERTS].set(w_router)
    br = jnp.zeros((1, LANES), F32).at[0, :N_EXPERTS].set(b_router)
    b1 = b_exp_in.reshape(N_EXPERTS, 2 * D_FF // _GLU_GROUP, LANES, 2).transpose(0, 1, 3, 2)
    b1 = b1.reshape(N_EXPERTS, 1, 2 * D_FF)
    bm = cfg["moe_block"]

    groups = cfg["moe_groups"]
    ng = n // groups
    staged = []
    for g in range(groups):
        x1, h2p, ti, tp, cnt = _merge(
            act, og.reshape(n, d), gc.reshape(n, d), gg.reshape(n, d), x.reshape(n, d),
            gt1, sh2, sc2, g_ffn_norm.reshape(1, d),
            w_conv_out.astype(BF16), w_gla_out.astype(BF16), w_out.astype(BF16), wr, br,
            cfg["tm_merge"], s, g * ng, ng)
        dest, blk_expert, n_real, n_blocks = _routing_tables(
            ti[:, :TOP_K], ti[:, TOP_K:2 * TOP_K], cnt[0, :N_EXPERTS].astype(I32), bm)
        dest_slots = dest.T.reshape(ng * TOP_K)
        xp = _sc_scatter_rows(h2p, dest_slots, n_blocks * bm)
        staged.append((x1, tp, dest_slots, blk_expert, n_real, xp))
    gathered = []
    for x1, tp, dest_slots, blk_expert, n_real, xp in staged:
        yp = _experts(blk_expert, n_real, xp, w_exp_in, b1, w_exp_out,
                      b_exp_out.reshape(N_EXPERTS, 1, d), bm)
        gathered.append(_sc_gather_rows(yp, dest_slots).reshape(TOP_K, ng, d // 2))
    out = None
    for g, (x1, tp, *_) in enumerate(staged):
        out = _combine_dense(gathered[g], tp, x1, gt2, g_final.reshape(1, d), cfg["tm_combine"], s,
                             g * ng, n, out)
    return out.reshape(b, s, d)


def _config(s, l):
    return dict(tm_in=min(512, s), tm_ctx=min(256, l), gla_cps=2, gla_cps_ctx=4,
                tm_merge=min(512, s), moe_block=512, tm_combine=min(256, s), moe_groups=2)


def kernel(x, c, ctx, c_ctx, w_ada, b_ada, g_mix_norm, w_in, w_dw, b_dw, g_conv_ln, b_conv_ln,
           w_conv_out, w_alpha, b_alpha, g_gla_norm, w_gla_out, w_out, g_ffn_norm, w_router,
           b_router, w_exp_in, b_exp_in, w_exp_out, b_exp_out, g_final):
    depth = w_ada.shape[0]
    assert depth == 1, "single-layer block: the context stream is only consumed by the GLA scan"
    cfg = _config(x.shape[1], ctx.shape[1])
    return _layer(x, c, ctx, c_ctx, w_ada[0], b_ada[0], g_mix_norm[0], w_in[0], w_dw[0], b_dw[0],
                  g_conv_ln[0], b_conv_ln[0], w_conv_out[0], w_alpha[0], b_alpha[0],
                  g_gla_norm[0], w_gla_out[0], w_out[0], g_ffn_norm[0], w_router[0], b_router[0],
                  w_exp_in[0], b_exp_in[0], w_exp_out[0], b_exp_out[0], g_final, cfg=cfg)
```

```python
import functools

import jax
import jax.numpy as jnp
from jax import lax
from jax.experimental import pallas as pl
from jax.experimental.pallas import tpu as pltpu
from jax.experimental.pallas import tpu_sc as plsc

F32 = jnp.float32
BF16 = jnp.bfloat16
I32 = jnp.int32

D_MODEL = 1024
GRID_W = 64
EPS = 1e-6
CONV_W = 1024
CONV_K = 31
GLA_H = 4
GLA_DK = 128
GLA_DV = 256
GLA_KD = GLA_H * GLA_DK
GLA_VD = GLA_H * GLA_DV
GLA_RANK = 16
GLA_TAU = 16.0
GLA_CHUNK = 64
N_EXPERTS = 32
TOP_K = 4
D_FF = 1024
SWIGLU_ALPHA = 1.702
SWIGLU_LIMIT = 7.0

LANES = 128
SUBLANES = 8
VMEM_LIMIT = 56 * 1024 * 1024

_C_CONV_A = 0
_C_CONV_B = _C_CONV_A + CONV_W
_C_Q = _C_CONV_B + CONV_W
_C_K = _C_Q + GLA_KD
_C_V = _C_K + GLA_KD
_C_G = _C_V + GLA_VD
_C_GC = _C_G + GLA_VD
_C_GG = _C_GC + D_MODEL
_C_A = _C_GG + D_MODEL
_C_END = _C_A + LANES


def _cparams(sem):
    return pltpu.CompilerParams(dimension_semantics=sem, vmem_limit_bytes=VMEM_LIMIT)


def _dot(a, b):
    return jnp.dot(a, b, preferred_element_type=F32)


def _split_bf16(x):
    hi = x.astype(BF16)
    lo = (x - hi.astype(F32)).astype(BF16)
    return hi, lo


def _dot3(a, b):
    a_hi, a_lo = _split_bf16(a)
    b_hi, b_lo = _split_bf16(b)
    return _dot(a_hi, b_hi) + _dot(a_lo, b_hi) + _dot(a_hi, b_lo)


def _sigmoid(x):
    return 1.0 / (1.0 + jnp.exp(-x))


def _resident(shape):
    nd = len(shape)
    return pl.BlockSpec(shape, lambda *_: (0,) * nd, pipeline_mode=pl.Buffered(1))


def _ada_body(a_ref, w_ref, b_ref, o_ref):
    a = a_ref[...]
    a = a * _sigmoid(a)
    o_ref[...] = _dot3(a, w_ref[...]) + b_ref[...]


def _ada(cc, w, b):
    rows, d = cc.shape
    n = w.shape[1]
    tn = 512
    return pl.pallas_call(
        _ada_body,
        grid=(n // tn,),
        in_specs=[pl.BlockSpec((rows, d), lambda j: (0, 0)),
                  pl.BlockSpec((d, tn), lambda j: (0, j)),
                  pl.BlockSpec((1, tn), lambda j: (0, j))],
        out_specs=pl.BlockSpec((rows, tn), lambda j: (0, j)),
        out_shape=jax.ShapeDtypeStruct((rows, n), F32),
        compiler_params=_cparams(("arbitrary",)),
        name="ada_mod",
    )(cc, w, b.reshape(1, n))


def _norm_mod(xv, gn, sc, sh):
    ms = jnp.mean(xv * xv, axis=-1, keepdims=True)
    y = xv * lax.rsqrt(ms + EPS) * gn
    return (y * (1.0 + sc) + sh).astype(BF16)


def _inproj_lat_body(x_ref, gn_ref, sh_ref, sc_ref, w_ref, sm_ref, wdw_ref, bdw_ref, gln_ref, bln_ref,
                     act_ref, q_ref, k_ref, v_ref, sg_ref, gc_ref, gg_ref, a_ref,
                     y_ref):
    h = _norm_mod(x_ref[0], gn_ref[...], sc_ref[0], sh_ref[0])
    ca = _dot(h, w_ref[:, _C_CONV_A:_C_CONV_B])
    cb = _dot(h, w_ref[:, _C_CONV_B:_C_Q])
    u = (ca * _sigmoid(cb)).astype(BF16)
    q_ref[0] = _dot(h, w_ref[:, _C_Q:_C_K]).astype(BF16)
    k_ref[0] = _dot(h, w_ref[:, _C_K:_C_V]).astype(BF16)
    v_ref[0] = _dot(h, w_ref[:, _C_V:_C_G]).astype(BF16)
    g = _dot(h, w_ref[:, _C_G:_C_GC])
    sg_ref[0] = (g * _sigmoid(g)).astype(BF16)
    gc_ref[0] = _sigmoid(_dot(h, w_ref[:, _C_GC:_C_GG])).astype(BF16)
    gg_ref[0] = _sigmoid(_dot(h, w_ref[:, _C_GG:_C_A])).astype(BF16)
    a_ref[0] = _dot(h, w_ref[:, _C_A:_C_END])
    act_ref[0] = _conv_ln_swish(u, sm_ref, wdw_ref, bdw_ref, gln_ref, bln_ref, y_ref)


def _inproj_ctx_body(x_ref, gn_ref, sh_ref, sc_ref, w_ref, k_ref, v_ref, a_ref):
    h = _norm_mod(x_ref[0], gn_ref[...], sc_ref[...], sh_ref[...])
    k_ref[0] = _dot(h, w_ref[:, _C_K:_C_V]).astype(BF16)
    v_ref[0] = _dot(h, w_ref[:, _C_V:_C_G]).astype(BF16)
    a_ref[0] = _dot(h, w_ref[:, _C_A:_C_END])


def _inproj_lat(x, gn, sh, sc, w, w_dw, b_dw, g_ln, b_ln, tm):
    b, s, d = x.shape
    row = lambda n: pl.BlockSpec((1, tm, n), lambda bi, i: (bi, i, 0))
    vec = pl.BlockSpec((1, 1, d), lambda bi, i: (bi, 0, 0))
    const = lambda shape: pl.BlockSpec(shape, lambda bi, i: (0,) * len(shape))
    shp = lambda n, dt: jax.ShapeDtypeStruct((b, s, n), dt)
    sm = _conv_shift_matrix()
    taps = (CONV_K + SUBLANES - 1) // SUBLANES * SUBLANES
    wpad = jnp.zeros((taps, CONV_W), F32).at[:CONV_K].set(w_dw)
    return pl.pallas_call(
        _inproj_lat_body,
        grid=(b, s // tm),
        in_specs=[row(d), const((1, d)), vec, vec, _resident(w.shape),
                  const(sm.shape), const(wpad.shape), const((1, CONV_W)), const((1, CONV_W)),
                  const((1, CONV_W))],
        out_specs=[row(CONV_W), row(GLA_KD), row(GLA_KD), row(GLA_VD), row(GLA_VD),
                   row(d), row(d), row(LANES)],
        out_shape=[shp(CONV_W, BF16), shp(GLA_KD, BF16), shp(GLA_KD, BF16), shp(GLA_VD, BF16),
                   shp(GLA_VD, BF16), shp(d, BF16), shp(d, BF16), shp(LANES, F32)],
        scratch_shapes=[pltpu.VMEM((tm, CONV_W), F32)],
        compiler_params=_cparams(("arbitrary", "arbitrary")),
        name="inproj_lat",
    )(x, gn, sh, sc, w, sm, wpad, b_dw.reshape(1, CONV_W), g_ln.reshape(1, CONV_W),
      b_ln.reshape(1, CONV_W))


def _inproj_ctx(ctx, gn, sh, sc, w, tm):
    b, l, d = ctx.shape
    row = lambda n: pl.BlockSpec((1, tm, n), lambda bi, i: (bi, i, 0))
    vec = pl.BlockSpec((1, d), lambda bi, i: (0, 0))
    shp = lambda n, dt: jax.ShapeDtypeStruct((b, l, n), dt)
    return pl.pallas_call(
        _inproj_ctx_body,
        grid=(b, l // tm),
        in_specs=[row(d), vec, vec, vec, _resident(w.shape)],
        out_specs=[row(GLA_KD), row(GLA_VD), row(LANES)],
        out_shape=[shp(GLA_KD, BF16), shp(GLA_VD, BF16), shp(LANES, F32)],
        compiler_params=_cparams(("arbitrary", "arbitrary")),
        name="inproj_ctx",
    )(ctx, gn, sh, sc, w)


_CONV_SUB = SUBLANES
_CONV_USED = GRID_W + _CONV_SUB * ((CONV_K - 1) // _CONV_SUB)
_CONV_SPAN = -(-_CONV_USED // (2 * SUBLANES)) * (2 * SUBLANES)
_CONV_LANES = 2 * LANES


def _conv_shift_matrix():
    row = lax.broadcasted_iota(I32, (_CONV_SUB, _CONV_SPAN, GRID_W), 1)
    shift = lax.broadcasted_iota(I32, (_CONV_SUB, _CONV_SPAN, GRID_W), 0)
    col = lax.broadcasted_iota(I32, (_CONV_SUB, _CONV_SPAN, GRID_W), 2)
    m = jnp.where(col == row + shift - CONV_K // 2, 1.0, 0.0)
    return m.reshape(_CONV_SUB * _CONV_SPAN, GRID_W).astype(BF16)


def _conv_ln_swish(u, sm_ref, w_ref, bdw_ref, gln_ref, bln_ref, y_ref):
    for r in range(u.shape[0] // GRID_W):
        ur = u[r * GRID_W:(r + 1) * GRID_W, :]
        for cq in range(CONV_W // _CONV_LANES):
            ls = slice(cq * _CONV_LANES, (cq + 1) * _CONV_LANES)
            acc = None
            for s in range(_CONV_SUB):
                win = _dot(sm_ref[s * _CONV_SPAN:(s + 1) * _CONV_SPAN, :], ur[:, ls])
                for a in range((CONV_K - 1 - s) // _CONV_SUB + 1):
                    k = _CONV_SUB * a + s
                    term = win[_CONV_SUB * a:_CONV_SUB * a + GRID_W, :] * w_ref[k:k + 1, ls]
                    acc = term if acc is None else acc + term
            y_ref[r * GRID_W:(r + 1) * GRID_W, ls] = acc + bdw_ref[:, ls]
    y = y_ref[...]
    mu = jnp.mean(y, axis=-1, keepdims=True)
    yc = y - mu
    var = jnp.mean(yc * yc, axis=-1, keepdims=True)
    yn = yc * lax.rsqrt(var + EPS) * gln_ref[...] + bln_ref[...]
    return (yn * _sigmoid(yn)).astype(BF16)


def _log_sigmoid(z):
    return jnp.minimum(z, 0.0) - jnp.log(1.0 + jnp.exp(-jnp.abs(z)))


def _gla_keep_mask(t, reverse):
    row = lax.broadcasted_iota(I32, (t, t), 0)
    col = lax.broadcasted_iota(I32, (t, t), 1)
    log2c = GLA_CHUNK.bit_length() - 1
    same_chunk = lax.shift_right_logical(row, log2c) == lax.shift_right_logical(col, log2c)
    return jnp.logical_and(same_chunk, (col >= row) if reverse else (col <= row))


def _gla_decay(a, wh_ref, wl_ref, ba, keep):
    a_hi, a_lo = _split_bf16(a)
    z = _dot(jnp.concatenate([a_hi, a_lo], axis=1), wh_ref[...]) + _dot(a_hi, wl_ref[...]) + ba
    loga = _log_sigmoid(z) * (1.0 / GLA_TAU)
    l_hi, l_lo = _split_bf16(loga)
    tri = jnp.where(keep, 1.0, 0.0).astype(BF16)
    return _dot(tri, l_hi) + _dot(tri, l_lo)


def _gla_step(k, v, bcum, keep, states, *, reverse, q=None):
    t = k.shape[0]
    c = GLA_CHUNK
    nch = t // c
    tots = [bcum[n * c:n * c + 1, :] if reverse else bcum[(n + 1) * c - 1:(n + 1) * c, :]
            for n in range(nch)]
    totb = jnp.concatenate([jnp.broadcast_to(tt, (c, GLA_KD)) for tt in tots], axis=0)
    k32 = k.astype(F32)
    k_end = (k32 * jnp.exp(totb - bcum)).astype(BF16)
    decs = [jnp.exp(tt) for tt in tots]
    if q is not None:
        q_dec = (q.astype(F32) * jnp.exp(bcum) * (GLA_DK ** -0.5)).astype(BF16)
        k_inv = (k32 * jnp.exp(-bcum)).astype(BF16)
    outs, new_states = [], []
    for h in range(GLA_H):
        ks = slice(h * GLA_DK, (h + 1) * GLA_DK)
        vh = v[:, h * GLA_DV:(h + 1) * GLA_DV]
        st = states[h]
        if q is not None:
            scores = lax.dot_general(q_dec[:, ks], k_inv[:, ks], (((1,), (1,)), ((), ())),
                                     preferred_element_type=F32)
            o_h = _dot(jnp.where(keep, scores, 0.0).astype(BF16), vh)
            inter = []
        for n in range(nch):
            rs = slice(n * c, (n + 1) * c)
            if q is not None:
                inter.append(_dot(q_dec[rs, ks], st.astype(BF16)))
            kv = lax.dot_general(k_end[rs, ks], vh[rs, :], (((0,), (0,)), ((), ())),
                                 preferred_element_type=F32)
            dt = jnp.transpose(jnp.broadcast_to(decs[n][:, ks], (GLA_DK, GLA_DK)))
            st = st * jnp.concatenate([dt] * (GLA_DV // GLA_DK), axis=1) + kv
        new_states.append(st)
        if q is not None:
            outs.append(o_h + jnp.concatenate(inter, axis=0))
    return new_states, (jnp.concatenate(outs, axis=1) if q is not None else None)


def _scan_order(x, reverse):
    if not reverse:
        return x
    nch = x.shape[0] // GLA_CHUNK
    return jnp.concatenate(
        [x[n * GLA_CHUNK:(n + 1) * GLA_CHUNK] for n in range(nch - 1, -1, -1)], axis=0)


def _gla_ctx_body(k_ref, v_ref, a_ref, wh_ref, wl_ref, ba_ref, st_ref, *, reverse):
    j = pl.program_id(1)

    @pl.when(j == 0)
    def _():
        st_ref[...] = jnp.zeros_like(st_ref)

    keep = _gla_keep_mask(k_ref.shape[1], reverse)
    scan = functools.partial(_scan_order, reverse=reverse)
    bcum = _gla_decay(scan(a_ref[0]), wh_ref, wl_ref, ba_ref[...], keep)
    states, _ = _gla_step(scan(k_ref[0]), scan(v_ref[0]), bcum, keep,
                          [st_ref[0, h] for h in range(GLA_H)], reverse=reverse)
    for h in range(GLA_H):
        st_ref[0, h] = states[h]


def _gla_lat_body(*refs, reverse, final):
    if final:
        (k_ref, v_ref, a_ref, an_ref, q_ref, wh_ref, wl_ref, ba_ref, s0_ref, sg_ref, op_ref,
         gn_ref, o_ref, st_ref, bc_ref) = refs
    else:
        (k_ref, v_ref, a_ref, an_ref, q_ref, wh_ref, wl_ref, ba_ref, s0_ref,
         o_ref, st_ref, bc_ref) = refs
    j = pl.program_id(1)
    nb, t = k_ref.shape[0], k_ref.shape[1]
    keep = _gla_keep_mask(t, reverse)
    scan = functools.partial(_scan_order, reverse=reverse)

    @pl.when(j == 0)
    def _():
        st_ref[...] = s0_ref[...]
        for bb in range(nb):
            bc_ref[bb] = _gla_decay(scan(a_ref[bb]), wh_ref, wl_ref, ba_ref[...], keep)

    for bb in range(nb):
        states, o = _gla_step(scan(k_ref[bb]), scan(v_ref[bb]), bc_ref[bb], keep,
                              [st_ref[bb, h] for h in range(GLA_H)], reverse=reverse,
                              q=scan(q_ref[bb]))
        bc_ref[bb] = _gla_decay(scan(an_ref[bb]), wh_ref, wl_ref, ba_ref[...], keep)
        for h in range(GLA_H):
            st_ref[bb, h] = states[h]
        if not final:
            o_ref[bb] = scan(o.astype(BF16))
            continue
        o = o + scan(op_ref[bb]).astype(F32)
        parts = []
        for h in range(GLA_H):
            oh = o[:, h * GLA_DV:(h + 1) * GLA_DV]
            ms = jnp.mean(oh * oh, axis=-1, keepdims=True)
            parts.append(oh * lax.rsqrt(ms + EPS) * gn_ref[...])
        o_ref[bb] = scan((jnp.concatenate(parts, axis=1)
                          * scan(sg_ref[bb]).astype(F32)).astype(BF16))


def _gla_decay_weights(w_alpha_d, reverse):
    a0 = GLA_RANK if reverse else 0
    w = jnp.zeros((LANES, GLA_KD), F32).at[a0:a0 + GLA_RANK].set(w_alpha_d)
    hi = w.astype(BF16)
    lo = (w - hi.astype(F32)).astype(BF16)
    return jnp.concatenate([hi, hi], axis=0), lo


def _gla_ctx(k, v, a, w_alpha_d, ba, reverse, cps):
    b, l, _ = k.shape
    tm = cps * GLA_CHUNK
    nj = l // tm
    jmap = (lambda j: nj - 1 - j) if reverse else (lambda j: j)
    row = lambda n: pl.BlockSpec((1, tm, n), lambda bi, j: (bi, jmap(j), 0))
    wh, wl = _gla_decay_weights(w_alpha_d, reverse)
    return pl.pallas_call(
        functools.partial(_gla_ctx_body, reverse=reverse),
        grid=(b, nj),
        in_specs=[row(GLA_KD), row(GLA_VD), row(LANES),
                  pl.BlockSpec(wh.shape, lambda bi, j: (0, 0)),
                  pl.BlockSpec(wl.shape, lambda bi, j: (0, 0)),
                  pl.BlockSpec((1, GLA_KD), lambda bi, j: (0, 0))],
        out_specs=pl.BlockSpec((1, GLA_H, GLA_DK, GLA_DV), lambda bi, j: (bi, 0, 0, 0)),
        out_shape=jax.ShapeDtypeStruct((b, GLA_H, GLA_DK, GLA_DV), F32),
        compiler_params=_cparams(("arbitrary", "arbitrary")),
        name="gla_ctx_bwd" if reverse else "gla_ctx_fwd",
    )(k, v, a, wh, wl, ba.reshape(1, GLA_KD))


def _gla_lat(k, v, a, q, w_alpha_d, ba, s0, reverse, cps, sg=None, o_prev=None, g_norm=None):
    b, s, _ = k.shape
    final = sg is not None
    tm = cps * GLA_CHUNK
    nj = s // tm
    jmap = (lambda j: nj - 1 - j) if reverse else (lambda j: j)
    nb = 8 if b % 8 == 0 else 2 if b % 2 == 0 else 1
    row = lambda n: pl.BlockSpec((nb, tm, n), lambda bi, j: (bi, jmap(j), 0))
    wh, wl = _gla_decay_weights(w_alpha_d, reverse)
    a_next = pl.BlockSpec((nb, tm, LANES), lambda bi, j: (bi, jmap(jnp.minimum(j + 1, nj - 1)), 0))
    in_specs = [row(GLA_KD), row(GLA_VD), row(LANES), a_next, row(GLA_KD),
                pl.BlockSpec(wh.shape, lambda bi, j: (0, 0)),
                pl.BlockSpec(wl.shape, lambda bi, j: (0, 0)),
                pl.BlockSpec((1, GLA_KD), lambda bi, j: (0, 0)),
                pl.BlockSpec((nb, GLA_H, GLA_DK, GLA_DV), lambda bi, j: (bi, 0, 0, 0))]
    args = [k, v, a, a, q, wh, wl, ba.reshape(1, GLA_KD), s0]
    if final:
        in_specs += [row(GLA_VD), row(GLA_VD), pl.BlockSpec((1, GLA_DV), lambda bi, j: (0, 0))]
        args += [sg, o_prev, g_norm.reshape(1, GLA_DV)]
    return pl.pallas_call(
        functools.partial(_gla_lat_body, reverse=reverse, final=final),
        grid=(b // nb, nj),
        in_specs=in_specs,
        out_specs=row(GLA_VD),
        out_shape=jax.ShapeDtypeStruct((b, s, GLA_VD), BF16),
        scratch_shapes=[pltpu.VMEM((nb, GLA_H, GLA_DK, GLA_DV), F32),
                        pltpu.VMEM((nb, tm, GLA_KD), F32)],
        compiler_params=_cparams(("arbitrary", "arbitrary")),
        name="gla_lat_bwd" if reverse else "gla_lat_fwd",
    )(*args)


def _pack_pair(lo, hi):
    lo_b = pltpu.bitcast(lo.astype(BF16).astype(F32), I32)
    hi_b = pltpu.bitcast(hi.astype(BF16).astype(F32), I32)
    return lax.shift_right_logical(lo_b, 16) | (hi_b & jnp.int32(-65536))


def _unpack_pair(w):
    lo = pltpu.bitcast(lax.shift_left(w, 16), F32)
    hi = pltpu.bitcast(w & jnp.int32(-65536), F32)
    return lo, hi


def _merge_body(act_ref, og_ref, gc_ref, gg_ref, x_ref, gt_ref, sh_ref, sc_ref, gn_ref,
                wc_ref, wg_ref, wo_ref, wr_ref, br_ref,
                x1_ref, h2_ref, ti_ref, tp_ref, cnt_ref):
    yc = _dot(act_ref[...], wc_ref[...])
    yg = _dot(og_ref[...], wg_ref[...])
    m = gc_ref[...].astype(F32) * yc + gg_ref[...].astype(F32) * yg
    x1 = x_ref[...] + gt_ref[0] * _dot(m.astype(BF16), wo_ref[...])
    x1_ref[...] = x1
    ms = jnp.mean(x1 * x1, axis=-1, keepdims=True)
    h2 = (x1 * lax.rsqrt(ms + EPS) * gn_ref[...]) * (1.0 + sc_ref[0]) + sh_ref[0]
    half = D_MODEL // 2
    h2_ref[...] = _pack_pair(h2[:, :half], h2[:, half:])
    logits = _dot3(h2, wr_ref[...]) + br_ref[...]
    lane = lax.broadcasted_iota(I32, logits.shape, 1).astype(F32)
    neg = jnp.float32(-jnp.inf)
    work = jnp.where(lane < N_EXPERTS, logits, neg)
    vals, idxs = [], []
    for _ in range(TOP_K):
        mx = jnp.max(work, axis=-1, keepdims=True)
        ix = jnp.min(jnp.where(work == mx, lane, float(LANES)), axis=-1, keepdims=True)
        vals.append(mx)
        idxs.append(ix)
        work = jnp.where(lane == ix, neg, work)
    es = [jnp.exp(v - vals[0]) for v in vals]
    den = es[0] + es[1] + es[2] + es[3]
    ti = jnp.zeros(logits.shape, F32)
    tp = jnp.zeros(logits.shape, F32)
    onehot = jnp.zeros(logits.shape, F32)
    for kk in range(TOP_K):
        ti = jnp.where(lane == kk, idxs[kk], ti)
        tp = jnp.where(lane == kk, es[kk] / den, tp)
        onehot = onehot + jnp.where(lane == idxs[kk], 1.0, 0.0)
    @pl.when(pl.program_id(0) == 0)
    def _():
        cnt_ref[...] = jnp.zeros_like(cnt_ref)

    tm = logits.shape[0]
    earlier = (lax.broadcasted_iota(I32, (tm, tm), 1) < lax.broadcasted_iota(I32, (tm, tm), 0))
    before = _dot(jnp.where(earlier, 1.0, 0.0).astype(BF16), onehot.astype(BF16)) + cnt_ref[0:1, :]
    for kk in range(TOP_K):
        rank = jnp.sum(jnp.where(lane == idxs[kk], before, 0.0), axis=-1, keepdims=True)
        ti = jnp.where(lane == TOP_K + kk, rank, ti)
    cnt_ref[...] = cnt_ref[...] + jnp.sum(onehot, axis=0, keepdims=True)
    ti_ref[...] = ti.astype(I32)
    tp_ref[...] = tp


def _merge(act, og, gc, gg, x2d, gt1, sh2, sc2, gn, wc, wg, wo, wr, br, tm, rows_per_batch,
           first_row, n):
    d = x2d.shape[1]
    per_b = rows_per_batch // tm
    off = first_row // tm
    row_in = lambda w: pl.BlockSpec((tm, w), lambda i: (i + off, 0))
    row = lambda w: pl.BlockSpec((tm, w), lambda i: (i, 0))
    vec = pl.BlockSpec((1, 1, d), lambda i: ((i + off) // per_b, 0, 0))
    const = lambda shape: pl.BlockSpec(shape, lambda i: (0,) * len(shape))
    return pl.pallas_call(
        _merge_body,
        grid=(n // tm,),
        in_specs=[row_in(d), row_in(d), row_in(d), row_in(d), row_in(d), vec, vec, vec,
                  const((1, d)),
                  _resident(wc.shape), _resident(wg.shape), _resident(wo.shape),
                  const(wr.shape), const((1, LANES))],
        out_specs=[row(d), row(d // 2), row(LANES), row(LANES), const((SUBLANES, LANES))],
        out_shape=[jax.ShapeDtypeStruct((n, d), F32), jax.ShapeDtypeStruct((n, d // 2), I32),
                   jax.ShapeDtypeStruct((n, LANES), I32), jax.ShapeDtypeStruct((n, LANES), F32),
                   jax.ShapeDtypeStruct((SUBLANES, LANES), F32)],
        compiler_params=_cparams(("arbitrary",)),
        name="merge_router",
    )(act, og, gc, gg, x2d, gt1, sh2, sc2, gn, wc, wg, wo, wr, br)


_GLU_GROUP = 2 * LANES


def _deinterleave_matrix():
    src = lax.broadcasted_iota(I32, (_GLU_GROUP, _GLU_GROUP), 0)
    dst = lax.broadcasted_iota(I32, (_GLU_GROUP, _GLU_GROUP), 1)
    want = jnp.where(dst < LANES, 2 * dst, 2 * (dst - LANES) + 1)
    return jnp.where(src == want, 1.0, 0.0).astype(BF16)


def _expert_body(be_ref, nreal_ref, xp_ref, w1_ref, b1_ref, w2_ref, b2_ref, yp_ref,
                 w1s_ref, w2s_ref):
    i = pl.program_id(0)
    new_expert = jnp.logical_or(i == 0, be_ref[i] != be_ref[jnp.maximum(i - 1, 0)])

    @pl.when(jnp.logical_and(new_expert, i < nreal_ref[0]))
    def _():
        perm = _deinterleave_matrix()
        for g in range(2 * D_FF // _GLU_GROUP):
            cs = slice(g * _GLU_GROUP, (g + 1) * _GLU_GROUP)
            w1s_ref[:, cs] = _dot(w1_ref[0, :, cs].astype(BF16), perm).astype(BF16)
        w2s_ref[...] = w2_ref[0].astype(BF16)

    @pl.when(i < nreal_ref[0])
    def _():
        lo, hi = _unpack_pair(xp_ref[...])
        x = jnp.concatenate([lo, hi], axis=1).astype(BF16)
        hid = _dot(x, w1s_ref[...]) + b1_ref[0]
        ngrp = 2 * D_FF // _GLU_GROUP
        hg = jnp.concatenate(
            [hid[:, g * _GLU_GROUP:g * _GLU_GROUP + LANES] for g in range(ngrp)], axis=1)
        hl = jnp.concatenate(
            [hid[:, g * _GLU_GROUP + LANES:(g + 1) * _GLU_GROUP] for g in range(ngrp)], axis=1)
        xg = jnp.minimum(hg, SWIGLU_LIMIT)
        xl = jnp.clip(hl, -SWIGLU_LIMIT, SWIGLU_LIMIT)
        act = xg * _sigmoid(SWIGLU_ALPHA * xg) * (xl + 1.0)
        y = _dot(act.astype(BF16), w2s_ref[...]) + b2_ref[0]
        half = D_MODEL // 2
        yp_ref[...] = _pack_pair(y[:, :half], y[:, half:])

    @pl.when(i >= nreal_ref[0])
    def _():
        yp_ref[...] = jnp.zeros_like(yp_ref)


def _experts(blk_expert, n_real, xp, w1, b1, w2, b2, bm):
    p, w = xp.shape
    grid_spec = pltpu.PrefetchScalarGridSpec(
        num_scalar_prefetch=2,
        grid=(p // bm,),
        in_specs=[pl.BlockSpec((bm, w), lambda i, be, nr: (i, 0)),
                  pl.BlockSpec((1, D_MODEL, 2 * D_FF), lambda i, be, nr: (be[i], 0, 0)),
                  pl.BlockSpec((1, 1, 2 * D_FF), lambda i, be, nr: (be[i], 0, 0)),
                  pl.BlockSpec((1, D_FF, D_MODEL), lambda i, be, nr: (be[i], 0, 0)),
                  pl.BlockSpec((1, 1, D_MODEL), lambda i, be, nr: (be[i], 0, 0))],
        out_specs=pl.BlockSpec((bm, w), lambda i, be, nr: (i, 0)),
        scratch_shapes=[pltpu.VMEM((D_MODEL, 2 * D_FF), BF16), pltpu.VMEM((D_FF, D_MODEL), BF16)],
    )
    return pl.pallas_call(
        _expert_body,
        grid_spec=grid_spec,
        out_shape=jax.ShapeDtypeStruct((p, w), I32),
        compiler_params=_cparams(("arbitrary",)),
        name="moe_experts",
    )(blk_expert, n_real, xp, w1, b1, w2, b2)


_SC_CORES = 2
_SC_SUBCORES = 16
_SC_CHUNK = 64


def _sc_gather_rows(table, idx):
    rows, width = idx.shape[0], table.shape[1]
    workers = _SC_CORES * _SC_SUBCORES
    per_worker = rows // workers
    assert rows % (workers * _SC_CHUNK) == 0
    mesh = plsc.VectorSubcoreMesh(core_axis_name="c", subcore_axis_name="s")

    n_chunks = per_worker // _SC_CHUNK
    assert n_chunks % 2 == 0 and n_chunks >= 4

    def body(table_hbm, idx_hbm, out_hbm, idx0, idx1, rows0, rows1, gsem0, gsem1, wsem0, wsem1):
        wid = lax.axis_index("s") * _SC_CORES + lax.axis_index("c")
        base = wid * per_worker
        bufs = ((idx0, rows0, gsem0, wsem0), (idx1, rows1, gsem1, wsem1))

        def out_rows(c):
            return out_hbm.at[pl.ds(base + c * _SC_CHUNK, _SC_CHUNK)]

        def gather_start(c, b):
            idx_v, rows_v, gsem, _ = bufs[b]
            pltpu.sync_copy(idx_hbm.at[pl.ds(base + c * _SC_CHUNK, _SC_CHUNK)], idx_v)
            pltpu.make_async_copy(table_hbm.at[idx_v], rows_v, gsem).start()

        def gather_wait(b):
            idx_v, rows_v, gsem, _ = bufs[b]
            pltpu.make_async_copy(table_hbm.at[idx_v], rows_v, gsem).wait()

        def write_start(c, b):
            _, rows_v, _, wsem = bufs[b]
            pltpu.make_async_copy(rows_v, out_rows(c), wsem).start()

        def write_wait(c, b):
            _, rows_v, _, wsem = bufs[b]
            pltpu.make_async_copy(rows_v, out_rows(c), wsem).wait()

        gather_start(0, 0)
        gather_wait(0)
        write_start(0, 0)
        gather_start(1, 1)

        @pl.loop(1, n_chunks - 1, step=2)
        def _(c):
            gather_wait(1)
            write_start(c, 1)
            write_wait(c - 1, 0)
            gather_start(c + 1, 0)
            gather_wait(0)
            write_start(c + 1, 0)
            write_wait(c, 1)
            gather_start(c + 2, 1)

        gather_wait(1)
        write_start(n_chunks - 1, 1)
        write_wait(n_chunks - 2, 0)
        write_wait(n_chunks - 1, 1)

    chunk = lambda: pltpu.VMEM((_SC_CHUNK, width), table.dtype)
    return pl.kernel(
        body,
        out_type=jax.ShapeDtypeStruct((rows, width), table.dtype),
        mesh=mesh,
        scratch_types=[pltpu.VMEM((_SC_CHUNK,), I32), pltpu.VMEM((_SC_CHUNK,), I32), chunk(), chunk(),
                       pltpu.SemaphoreType.DMA, pltpu.SemaphoreType.DMA,
                       pltpu.SemaphoreType.DMA, pltpu.SemaphoreType.DMA],
        name="sc_gather_rows",
    )(table, idx)


def _sc_scatter_rows(rows, idx_slots, out_rows):
    n, width = rows.shape
    workers = _SC_CORES * _SC_SUBCORES
    per_worker = n // workers
    assert n % (workers * _SC_CHUNK) == 0
    mesh = plsc.VectorSubcoreMesh(core_axis_name="c", subcore_axis_name="s")

    def body(rows_hbm, idx_hbm, out_hbm, idx_v, rows_v):
        wid = lax.axis_index("s") * _SC_CORES + lax.axis_index("c")
        base = wid * per_worker

        @pl.loop(0, per_worker // _SC_CHUNK)
        def _(i):
            t0 = base + i * _SC_CHUNK
            pltpu.sync_copy(rows_hbm.at[pl.ds(t0, _SC_CHUNK)], rows_v)
            for kk in range(TOP_K):
                pltpu.sync_copy(idx_hbm.at[pl.ds(kk * n + t0, _SC_CHUNK)], idx_v)
                pltpu.sync_copy(rows_v, out_hbm.at[idx_v])

    return pl.kernel(
        body,
        out_type=jax.ShapeDtypeStruct((out_rows, width), rows.dtype),
        mesh=mesh,
        scratch_types=[pltpu.VMEM((_SC_CHUNK,), I32), pltpu.VMEM((_SC_CHUNK, width), rows.dtype)],
        name="sc_scatter_rows",
    )(rows, idx_slots)


def _combine_dense_body(y4_ref, tp_ref, x1_ref, gt_ref, gf_ref, *out_refs):
    o_ref = out_refs[-1]
    half = D_MODEL // 2
    tp = tp_ref[...]
    y_lo = y_hi = None
    for kk in range(TOP_K):
        lo, hi = _unpack_pair(y4_ref[kk])
        pk = tp[:, kk:kk + 1]
        y_lo = pk * lo if y_lo is None else y_lo + pk * lo
        y_hi = pk * hi if y_hi is None else y_hi + pk * hi
    x2 = x1_ref[...] + gt_ref[0] * jnp.concatenate([y_lo, y_hi], axis=1)
    ms = jnp.mean(x2 * x2, axis=-1, keepdims=True)
    o_ref[...] = x2 * lax.rsqrt(ms + EPS) * gf_ref[...]


def _combine_dense(y4, tp, x1, gt2, gf, tokens, rows_per_batch, first_row, n_total, out_prev):
    n, d = x1.shape
    per_b = rows_per_batch // tokens
    off = first_row // tokens
    in_specs = [pl.BlockSpec((TOP_K, tokens, d // 2), lambda i: (0, i, 0)),
                pl.BlockSpec((tokens, LANES), lambda i: (i, 0)),
                pl.BlockSpec((tokens, d), lambda i: (i, 0)),
                pl.BlockSpec((1, 1, d), lambda i: ((i + off) // per_b, 0, 0)),
                pl.BlockSpec((1, d), lambda i: (0, 0))]
    args = [y4, tp, x1, gt2, gf]
    aliases = {}
    if out_prev is not None:
        in_specs.append(pl.BlockSpec(memory_space=pl.ANY))
        args.append(out_prev)
        aliases = {len(args) - 1: 0}
    return pl.pallas_call(
        _combine_dense_body,
        grid=(n // tokens,),
        in_specs=in_specs,
        out_specs=pl.BlockSpec((tokens, d), lambda i: (i + off, 0)),
        out_shape=jax.ShapeDtypeStruct((n_total, d), F32),
        input_output_aliases=aliases,
        compiler_params=_cparams(("arbitrary",)),
        name="moe_combine_dense",
    )(*args)


def _routing_tables(top_idx, rank, counts, bm):
    n = top_idx.shape[0]
    nk = n * TOP_K
    padded = (counts + bm - 1) // bm * bm
    pad_end = jnp.cumsum(padded)
    pad_start = pad_end - padded
    dest = (pad_start[top_idx] + rank).astype(I32)
    n_blocks = (nk + N_EXPERTS * (bm - 1) + bm - 1) // bm
    starts = jnp.arange(n_blocks, dtype=I32) * bm
    blk_expert = jnp.minimum(jnp.sum((pad_end[None, :] <= starts[:, None]).astype(I32), axis=1),
                             N_EXPERTS - 1).astype(I32)
    n_real = (pad_end[-1] // bm).astype(I32).reshape(1)
    return dest, blk_expert, n_real, n_blocks


def _layer(x, c, ctx, c_ctx, w_ada, b_ada, g_mix_norm, w_in, w_dw, b_dw, g_conv_ln, b_conv_ln,
           w_conv_out, w_alpha, b_alpha, g_gla_norm, w_gla_out, w_out, g_ffn_norm, w_router,
           b_router, w_exp_in, b_exp_in, w_exp_out, b_exp_out, g_final, *, cfg):
    b, s, d = x.shape
    n = b * s

    rows = (b + 1 + SUBLANES - 1) // SUBLANES * SUBLANES
    cc = jnp.zeros((rows, d), F32).at[:b].set(c).at[b].set(c_ctx)
    mod = _ada(cc, w_ada, b_ada)
    sh1, sc1, gt1, sh2, sc2, gt2 = [mod[:b, i * d:(i + 1) * d].reshape(b, 1, d) for i in range(6)]
    csh1 = mod[b:b + 1, 0:d]
    csc1 = mod[b:b + 1, d:2 * d]

    a0 = 2 * CONV_W + 2 * GLA_KD + 2 * GLA_VD
    w_in_r = jnp.concatenate(
        [w_in[:, :a0], w_in[:, a0 + 2 * GLA_RANK:], w_in[:, a0:a0 + 2 * GLA_RANK],
         jnp.zeros((d, LANES - 2 * GLA_RANK), F32)], axis=1).astype(BF16)
    gmn = g_mix_norm.reshape(1, d)

    act, q, k, v, sg, gc, gg, a = _inproj_lat(x, gmn, sh1, sc1, w_in_r, w_dw, b_dw, g_conv_ln,
                                              b_conv_ln, cfg["tm_in"])
    act = act.reshape(n, CONV_W)
    kc, vc, ac = _inproj_ctx(ctx, gmn, csh1, csc1, w_in_r, cfg["tm_ctx"])

    cps, cps_ctx = cfg["gla_cps"], cfg["gla_cps_ctx"]
    st_f = _gla_ctx(kc, vc, ac, w_alpha[0], b_alpha[0], False, cps_ctx)
    o_f = _gla_lat(k, v, a, q, w_alpha[0], b_alpha[0], st_f, False, cps)
    st_b = _gla_ctx(kc, vc, ac, w_alpha[1], b_alpha[1], True, cps_ctx)
    og = _gla_lat(k, v, a, q, w_alpha[1], b_alpha[1], st_b, True, cps,
                  sg=sg, o_prev=o_f, g_norm=g_gla_norm)

    wr = jnp.zeros((d, LANES), F32).at[:, :N_EXPERTS].set(w_router)
    br = jnp.zeros((1, LANES), F32).at[0, :N_EXPERTS].set(b_router)
    b1 = b_exp_in.reshape(N_EXPERTS, 2 * D_FF // _GLU_GROUP, LANES, 2).transpose(0, 1, 3, 2)
    b1 = b1.reshape(N_EXPERTS, 1, 2 * D_FF)
    bm = cfg["moe_block"]

    groups = cfg["moe_groups"]
    ng = n // groups
    staged = []
    for g in range(groups):
        x1, h2p, ti, tp, cnt = _merge(
            act, og.reshape(n, d), gc.reshape(n, d), gg.reshape(n, d), x.reshape(n, d),
            gt1, sh2, sc2, g_ffn_norm.reshape(1, d),
            w_conv_out.astype(BF16), w_gla_out.astype(BF16), w_out.astype(BF16), wr, br,
            cfg["tm_merge"], s, g * ng, ng)
        dest, blk_expert, n_real, n_blocks = _routing_tables(
            ti[:, :TOP_K], ti[:, TOP_K:2 * TOP_K], cnt[0, :N_EXPERTS].astype(I32), bm)
        dest_slots = dest.T.reshape(ng * TOP_K)
        xp = _sc_scatter_rows(h2p, dest_slots, n_blocks * bm)
        staged.append((x1, tp, dest_slots, blk_expert, n_real, xp))
    gathered = []
    for x1, tp, dest_slots, blk_expert, n_real, xp in staged:
        yp = _experts(blk_expert, n_real, xp, w_exp_in, b1, w_exp_out,
                      b_exp_out.reshape(N_EXPERTS, 1, d), bm)
        gathered.append(_sc_gather_rows(yp, dest_slots).reshape(TOP_K, ng, d // 2))
    out = None
    for g, (x1, tp, *_) in enumerate(staged):
        out = _combine_dense(gathered[g], tp, x1, gt2, g_final.reshape(1, d), cfg["tm_combine"], s,
                             g * ng, n, out)
    return out.reshape(b, s, d)


def _config(s, l):
    return dict(tm_in=min(512, s), tm_ctx=min(256, l), gla_cps=2, gla_cps_ctx=4,
                tm_merge=min(512, s), moe_block=512, tm_combine=min(256, s), moe_groups=2)


def kernel(x, c, ctx, c_ctx, w_ada, b_ada, g_mix_norm, w_in, w_dw, b_dw, g_conv_ln, b_conv_ln,
           w_conv_out, w_alpha, b_alpha, g_gla_norm, w_gla_out, w_out, g_ffn_norm, w_router,
           b_router, w_exp_in, b_exp_in, w_exp_out, b_exp_out, g_final):
    depth = w_ada.shape[0]
    assert depth == 1, "single-layer block: the context stream is only consumed by the GLA scan"
    cfg = _config(x.shape[1], ctx.shape[1])
    return _layer(x, c, ctx, c_ctx, w_ada[0], b_ada[0], g_mix_norm[0], w_in[0], w_dw[0], b_dw[0],
                  g_conv_ln[0], b_conv_ln[0], w_conv_out[0], w_alpha[0], b_alpha[0],
                  g_gla_norm[0], w_gla_out[0], w_out[0], g_ffn_norm[0], w_router[0], b_router[0],
                  w_exp_in[0], b_exp_in[0], w_exp_out[0], b_exp_out[0], g_final, cfg=cfg)
```

```python
import functools

import jax
import jax.numpy as jnp
from jax import lax
from jax.experimental import pallas as pl
from jax.experimental.pallas import tpu as pltpu
from jax.experimental.pallas import tpu_sc as plsc

F32 = jnp.float32
BF16 = jnp.bfloat16
I32 = jnp.int32

D_MODEL = 1024
GRID_W = 64
EPS = 1e-6
CONV_W = 1024
CONV_K = 31
GLA_H = 4
GLA_DK = 128
GLA_DV = 256
GLA_KD = GLA_H * GLA_DK
GLA_VD = GLA_H * GLA_DV
GLA_RANK = 16
GLA_TAU = 16.0
GLA_CHUNK = 64
N_EXPERTS = 32
TOP_K = 4
D_FF = 1024
SWIGLU_ALPHA = 1.702
SWIGLU_LIMIT = 7.0

LANES = 128
SUBLANES = 8
VMEM_LIMIT = 56 * 1024 * 1024

_C_CONV_A = 0
_C_CONV_B = _C_CONV_A + CONV_W
_C_Q = _C_CONV_B + CONV_W
_C_K = _C_Q + GLA_KD
_C_V = _C_K + GLA_KD
_C_G = _C_V + GLA_VD
_C_GC = _C_G + GLA_VD
_C_GG = _C_GC + D_MODEL
_C_A = _C_GG + D_MODEL
_C_END = _C_A + LANES


def _cparams(sem):
    return pltpu.CompilerParams(dimension_semantics=sem, vmem_limit_bytes=VMEM_LIMIT)


def _dot(a, b):
    return jnp.dot(a, b, preferred_element_type=F32)


def _split_bf16(x):
    hi = x.astype(BF16)
    lo = (x - hi.astype(F32)).astype(BF16)
    return hi, lo


def _dot3(a, b):
    a_hi, a_lo = _split_bf16(a)
    b_hi, b_lo = _split_bf16(b)
    return _dot(a_hi, b_hi) + _dot(a_lo, b_hi) + _dot(a_hi, b_lo)


def _sigmoid(x):
    return 1.0 / (1.0 + jnp.exp(-x))


def _resident(shape):
    nd = len(shape)
    return pl.BlockSpec(shape, lambda *_: (0,) * nd, pipeline_mode=pl.Buffered(1))


def _ada_body(a_ref, w_ref, b_ref, o_ref):
    a = a_ref[...]
    a = a * _sigmoid(a)
    o_ref[...] = _dot3(a, w_ref[...]) + b_ref[...]


def _ada(cc, w, b):
    rows, d = cc.shape
    n = w.shape[1]
    tn = 512
    return pl.pallas_call(
        _ada_body,
        grid=(n // tn,),
        in_specs=[pl.BlockSpec((rows, d), lambda j: (0, 0)),
                  pl.BlockSpec((d, tn), lambda j: (0, j)),
                  pl.BlockSpec((1, tn), lambda j: (0, j))],
        out_specs=pl.BlockSpec((rows, tn), lambda j: (0, j)),
        out_shape=jax.ShapeDtypeStruct((rows, n), F32),
        compiler_params=_cparams(("arbitrary",)),
        name="ada_mod",
    )(cc, w, b.reshape(1, n))


def _norm_mod(xv, gn, sc, sh):
    ms = jnp.mean(xv * xv, axis=-1, keepdims=True)
    y = xv * lax.rsqrt(ms + EPS) * gn
    return (y * (1.0 + sc) + sh).astype(BF16)


def _inproj_lat_body(x_ref, gn_ref, sh_ref, sc_ref, w_ref, sm_ref, wdw_ref, bdw_ref, gln_ref, bln_ref,
                     act_ref, q_ref, k_ref, v_ref, sg_ref, gc_ref, gg_ref, a_ref,
                     y_ref):
    h = _norm_mod(x_ref[0], gn_ref[...], sc_ref[0], sh_ref[0])
    ca = _dot(h, w_ref[:, _C_CONV_A:_C_CONV_B])
    cb = _dot(h, w_ref[:, _C_CONV_B:_C_Q])
    u = (ca * _sigmoid(cb)).astype(BF16)
    q_ref[0] = _dot(h, w_ref[:, _C_Q:_C_K]).astype(BF16)
    k_ref[0] = _dot(h, w_ref[:, _C_K:_C_V]).astype(BF16)
    v_ref[0] = _dot(h, w_ref[:, _C_V:_C_G]).astype(BF16)
    g = _dot(h, w_ref[:, _C_G:_C_GC])
    sg_ref[0] = (g * _sigmoid(g)).astype(BF16)
    gc_ref[0] = _sigmoid(_dot(h, w_ref[:, _C_GC:_C_GG])).astype(BF16)
    gg_ref[0] = _sigmoid(_dot(h, w_ref[:, _C_GG:_C_A])).astype(BF16)
    a_ref[0] = _dot(h, w_ref[:, _C_A:_C_END])
    act_ref[0] = _conv_ln_swish(u, sm_ref, wdw_ref, bdw_ref, gln_ref, bln_ref, y_ref)


def _inproj_ctx_body(x_ref, gn_ref, sh_ref, sc_ref, w_ref, k_ref, v_ref, a_ref):
    h = _norm_mod(x_ref[0], gn_ref[...], sc_ref[...], sh_ref[...])
    k_ref[0] = _dot(h, w_ref[:, _C_K:_C_V]).astype(BF16)
    v_ref[0] = _dot(h, w_ref[:, _C_V:_C_G]).astype(BF16)
    a_ref[0] = _dot(h, w_ref[:, _C_A:_C_END])


def _inproj_lat(x, gn, sh, sc, w, w_dw, b_dw, g_ln, b_ln, tm):
    b, s, d = x.shape
    row = lambda n: pl.BlockSpec((1, tm, n), lambda bi, i: (bi, i, 0))
    vec = pl.BlockSpec((1, 1, d), lambda bi, i: (bi, 0, 0))
    const = lambda shape: pl.BlockSpec(shape, lambda bi, i: (0,) * len(shape))
    shp = lambda n, dt: jax.ShapeDtypeStruct((b, s, n), dt)
    sm = _conv_shift_matrix()
    taps = (CONV_K + SUBLANES - 1) // SUBLANES * SUBLANES
    wpad = jnp.zeros((taps, CONV_W), F32).at[:CONV_K].set(w_dw)
    return pl.pallas_call(
        _inproj_lat_body,
        grid=(b, s // tm),
        in_specs=[row(d), const((1, d)), vec, vec, _resident(w.shape),
                  const(sm.shape), const(wpad.shape), const((1, CONV_W)), const((1, CONV_W)),
                  const((1, CONV_W))],
        out_specs=[row(CONV_W), row(GLA_KD), row(GLA_KD), row(GLA_VD), row(GLA_VD),
                   row(d), row(d), row(LANES)],
        out_shape=[shp(CONV_W, BF16), shp(GLA_KD, BF16), shp(GLA_KD, BF16), shp(GLA_VD, BF16),
                   shp(GLA_VD, BF16), shp(d, BF16), shp(d, BF16), shp(LANES, F32)],
        scratch_shapes=[pltpu.VMEM((tm, CONV_W), F32)],
        compiler_params=_cparams(("arbitrary", "arbitrary")),
        name="inproj_lat",
    )(x, gn, sh, sc, w, sm, wpad, b_dw.reshape(1, CONV_W), g_ln.reshape(1, CONV_W),
      b_ln.reshape(1, CONV_W))


def _inproj_ctx(ctx, gn, sh, sc, w, tm):
    b, l, d = ctx.shape
    row = lambda n: pl.BlockSpec((1, tm, n), lambda bi, i: (bi, i, 0))
    vec = pl.BlockSpec((1, d), lambda bi, i: (0, 0))
    shp = lambda n, dt: jax.ShapeDtypeStruct((b, l, n), dt)
    return pl.pallas_call(
        _inproj_ctx_body,
        grid=(b, l // tm),
        in_specs=[row(d), vec, vec, vec, _resident(w.shape)],
        out_specs=[row(GLA_KD), row(GLA_VD), row(LANES)],
        out_shape=[shp(GLA_KD, BF16), shp(GLA_VD, BF16), shp(LANES, F32)],
        compiler_params=_cparams(("arbitrary", "arbitrary")),
        name="inproj_ctx",
    )(ctx, gn, sh, sc, w)


_CONV_SUB = SUBLANES
_CONV_USED = GRID_W + _CONV_SUB * ((CONV_K - 1) // _CONV_SUB)
_CONV_SPAN = -(-_CONV_USED // (2 * SUBLANES)) * (2 * SUBLANES)
_CONV_LANES = 2 * LANES


def _conv_shift_matrix():
    row = lax.broadcasted_iota(I32, (_CONV_SUB, _CONV_SPAN, GRID_W), 1)
    shift = lax.broadcasted_iota(I32, (_CONV_SUB, _CONV_SPAN, GRID_W), 0)
    col = lax.broadcasted_iota(I32, (_CONV_SUB, _CONV_SPAN, GRID_W), 2)
    m = jnp.where(col == row + shift - CONV_K // 2, 1.0, 0.0)
    return m.reshape(_CONV_SUB * _CONV_SPAN, GRID_W).astype(BF16)


def _conv_ln_swish(u, sm_ref, w_ref, bdw_ref, gln_ref, bln_ref, y_ref):
    for r in range(u.shape[0] // GRID_W):
        ur = u[r * GRID_W:(r + 1) * GRID_W, :]
        for cq in range(CONV_W // _CONV_LANES):
            ls = slice(cq * _CONV_LANES, (cq + 1) * _CONV_LANES)
            acc = None
            for s in range(_CONV_SUB):
                win = _dot(sm_ref[s * _CONV_SPAN:(s + 1) * _CONV_SPAN, :], ur[:, ls])
                for a in range((CONV_K - 1 - s) // _CONV_SUB + 1):
                    k = _CONV_SUB * a + s
                    term = win[_CONV_SUB * a:_CONV_SUB * a + GRID_W, :] * w_ref[k:k + 1, ls]
                    acc = term if acc is None else acc + term
            y_ref[r * GRID_W:(r + 1) * GRID_W, ls] = acc + bdw_ref[:, ls]
    y = y_ref[...]
    mu = jnp.mean(y, axis=-1, keepdims=True)
    yc = y - mu
    var = jnp.mean(yc * yc, axis=-1, keepdims=True)
    yn = yc * lax.rsqrt(var + EPS) * gln_ref[...] + bln_ref[...]
    return (yn * _sigmoid(yn)).astype(BF16)


def _log_sigmoid(z):
    return jnp.minimum(z, 0.0) - jnp.log(1.0 + jnp.exp(-jnp.abs(z)))


def _gla_keep_mask(t, reverse):
    row = lax.broadcasted_iota(I32, (t, t), 0)
    col = lax.broadcasted_iota(I32, (t, t), 1)
    log2c = GLA_CHUNK.bit_length() - 1
    same_chunk = lax.shift_right_logical(row, log2c) == lax.shift_right_logical(col, log2c)
    return jnp.logical_and(same_chunk, (col >= row) if reverse else (col <= row))


def _gla_decay(a, wh_ref, wl_ref, ba, keep):
    a_hi, a_lo = _split_bf16(a)
    z = _dot(jnp.concatenate([a_hi, a_lo], axis=1), wh_ref[...]) + _dot(a_hi, wl_ref[...]) + ba
    loga = _log_sigmoid(z) * (1.0 / GLA_TAU)
    l_hi, l_lo = _split_bf16(loga)
    tri = jnp.where(keep, 1.0, 0.0).astype(BF16)
    return _dot(tri, l_hi) + _dot(tri, l_lo)


def _gla_step(k, v, bcum, keep, states, *, reverse, q=None):
    t = k.shape[0]
    c = GLA_CHUNK
    nch = t // c
    tots = [bcum[n * c:n * c + 1, :] if reverse else bcum[(n + 1) * c - 1:(n + 1) * c, :]
            for n in range(nch)]
    totb = jnp.concatenate([jnp.broadcast_to(tt, (c, GLA_KD)) for tt in tots], axis=0)
    k32 = k.astype(F32)
    k_end = (k32 * jnp.exp(totb - bcum)).astype(BF16)
    decs = [jnp.exp(tt) for tt in tots]
    if q is not None:
        q_dec = (q.astype(F32) * jnp.exp(bcum) * (GLA_DK ** -0.5)).astype(BF16)
        k_inv = (k32 * jnp.exp(-bcum)).astype(BF16)
    outs, new_states = [], []
    for h in range(GLA_H):
        ks = slice(h * GLA_DK, (h + 1) * GLA_DK)
        vh = v[:, h * GLA_DV:(h + 1) * GLA_DV]
        st = states[h]
        if q is not None:
            scores = lax.dot_general(q_dec[:, ks], k_inv[:, ks], (((1,), (1,)), ((), ())),
                                     preferred_element_type=F32)
            o_h = _dot(jnp.where(keep, scores, 0.0).astype(BF16), vh)
            inter = []
        for n in range(nch):
            rs = slice(n * c, (n + 1) * c)
            if q is not None:
                inter.append(_dot(q_dec[rs, ks], st.astype(BF16)))
            kv = lax.dot_general(k_end[rs, ks], vh[rs, :], (((0,), (0,)), ((), ())),
                                 preferred_element_type=F32)
            dt = jnp.transpose(jnp.broadcast_to(decs[n][:, ks], (GLA_DK, GLA_DK)))
            st = st * jnp.concatenate([dt] * (GLA_DV // GLA_DK), axis=1) + kv
        new_states.append(st)
        if q is not None:
            outs.append(o_h + jnp.concatenate(inter, axis=0))
    return new_states, (jnp.concatenate(outs, axis=1) if q is not None else None)


def _scan_order(x, reverse):
    if not reverse:
        return x
    nch = x.shape[0] // GLA_CHUNK
    return jnp.concatenate(
        [x[n * GLA_CHUNK:(n + 1) * GLA_CHUNK] for n in range(nch - 1, -1, -1)], axis=0)


def _gla_ctx_body(k_ref, v_ref, a_ref, wh_ref, wl_ref, ba_ref, st_ref, *, reverse):
    j = pl.program_id(1)

    @pl.when(j == 0)
    def _():
        st_ref[...] = jnp.zeros_like(st_ref)

    keep = _gla_keep_mask(k_ref.shape[1], reverse)
    scan = functools.partial(_scan_order, reverse=reverse)
    bcum = _gla_decay(scan(a_ref[0]), wh_ref, wl_ref, ba_ref[...], keep)
    states, _ = _gla_step(scan(k_ref[0]), scan(v_ref[0]), bcum, keep,
                          [st_ref[0, h] for h in range(GLA_H)], reverse=reverse)
    for h in range(GLA_H):
        st_ref[0, h] = states[h]


def _gla_lat_body(*refs, reverse, final):
    if final:
        (k_ref, v_ref, a_ref, an_ref, q_ref, wh_ref, wl_ref, ba_ref, s0_ref, sg_ref, op_ref,
         gn_ref, o_ref, st_ref, bc_ref) = refs
    else:
        (k_ref, v_ref, a_ref, an_ref, q_ref, wh_ref, wl_ref, ba_ref, s0_ref,
         o_ref, st_ref, bc_ref) = refs
    j = pl.program_id(1)
    nb, t = k_ref.shape[0], k_ref.shape[1]
    keep = _gla_keep_mask(t, reverse)
    scan = functools.partial(_scan_order, reverse=reverse)

    @pl.when(j == 0)
    def _():
        st_ref[...] = s0_ref[...]
        for bb in range(nb):
            bc_ref[bb] = _gla_decay(scan(a_ref[bb]), wh_ref, wl_ref, ba_ref[...], keep)

    for bb in range(nb):
        states, o = _gla_step(scan(k_ref[bb]), scan(v_ref[bb]), bc_ref[bb], keep,
                              [st_ref[bb, h] for h in range(GLA_H)], reverse=reverse,
                              q=scan(q_ref[bb]))
        bc_ref[bb] = _gla_decay(scan(an_ref[bb]), wh_ref, wl_ref, ba_ref[...], keep)
        for h in range(GLA_H):
            st_ref[bb, h] = states[h]
        if not final:
            o_ref[bb] = scan(o.astype(BF16))
            continue
        o = o + scan(op_ref[bb]).astype(F32)
        parts = []
        for h in range(GLA_H):
            oh = o[:, h * GLA_DV:(h + 1) * GLA_DV]
            ms = jnp.mean(oh * oh, axis=-1, keepdims=True)
            parts.append(oh * lax.rsqrt(ms + EPS) * gn_ref[...])
        o_ref[bb] = scan((jnp.concatenate(parts, axis=1)
                          * scan(sg_ref[bb]).astype(F32)).astype(BF16))


def _gla_decay_weights(w_alpha_d, reverse):
    a0 = GLA_RANK if reverse else 0
    w = jnp.zeros((LANES, GLA_KD), F32).at[a0:a0 + GLA_RANK].set(w_alpha_d)
    hi = w.astype(BF16)
    lo = (w - hi.astype(F32)).astype(BF16)
    return jnp.concatenate([hi, hi], axis=0), lo


def _gla_ctx(k, v, a, w_alpha_d, ba, reverse, cps):
    b, l, _ = k.shape
    tm = cps * GLA_CHUNK
    nj = l // tm
    jmap = (lambda j: nj - 1 - j) if reverse else (lambda j: j)
    row = lambda n: pl.BlockSpec((1, tm, n), lambda bi, j: (bi, jmap(j), 0))
    wh, wl = _gla_decay_weights(w_alpha_d, reverse)
    return pl.pallas_call(
        functools.partial(_gla_ctx_body, reverse=reverse),
        grid=(b, nj),
        in_specs=[row(GLA_KD), row(GLA_VD), row(LANES),
                  pl.BlockSpec(wh.shape, lambda bi, j: (0, 0)),
                  pl.BlockSpec(wl.shape, lambda bi, j: (0, 0)),
                  pl.BlockSpec((1, GLA_KD), lambda bi, j: (0, 0))],
        out_specs=pl.BlockSpec((1, GLA_H, GLA_DK, GLA_DV), lambda bi, j: (bi, 0, 0, 0)),
        out_shape=jax.ShapeDtypeStruct((b, GLA_H, GLA_DK, GLA_DV), F32),
        compiler_params=_cparams(("arbitrary", "arbitrary")),
        name="gla_ctx_bwd" if reverse else "gla_ctx_fwd",
    )(k, v, a, wh, wl, ba.reshape(1, GLA_KD))


def _gla_lat(k, v, a, q, w_alpha_d, ba, s0, reverse, cps, sg=None, o_prev=None, g_norm=None):
    b, s, _ = k.shape
    final = sg is not None
    tm = cps * GLA_CHUNK
    nj = s // tm
    jmap = (lambda j: nj - 1 - j) if reverse else (lambda j: j)
    nb = 8 if b % 8 == 0 else 2 if b % 2 == 0 else 1
    row = lambda n: pl.BlockSpec((nb, tm, n), lambda bi, j: (bi, jmap(j), 0))
    wh, wl = _gla_decay_weights(w_alpha_d, reverse)
    a_next = pl.BlockSpec((nb, tm, LANES), lambda bi, j: (bi, jmap(jnp.minimum(j + 1, nj - 1)), 0))
    in_specs = [row(GLA_KD), row(GLA_VD), row(LANES), a_next, row(GLA_KD),
                pl.BlockSpec(wh.shape, lambda bi, j: (0, 0)),
                pl.BlockSpec(wl.shape, lambda bi, j: (0, 0)),
                pl.BlockSpec((1, GLA_KD), lambda bi, j: (0, 0)),
                pl.BlockSpec((nb, GLA_H, GLA_DK, GLA_DV), lambda bi, j: (bi, 0, 0, 0))]
    args = [k, v, a, a, q, wh, wl, ba.reshape(1, GLA_KD), s0]
    if final:
        in_specs += [row(GLA_VD), row(GLA_VD), pl.BlockSpec((1, GLA_DV), lambda bi, j: (0, 0))]
        args += [sg, o_prev, g_norm.reshape(1, GLA_DV)]
    return pl.pallas_call(
        functools.partial(_gla_lat_body, reverse=reverse, final=final),
        grid=(b // nb, nj),
        in_specs=in_specs,
        out_specs=row(GLA_VD),
        out_shape=jax.ShapeDtypeStruct((b, s, GLA_VD), BF16),
        scratch_shapes=[pltpu.VMEM((nb, GLA_H, GLA_DK, GLA_DV), F32),
                        pltpu.VMEM((nb, tm, GLA_KD), F32)],
        compiler_params=_cparams(("arbitrary", "arbitrary")),
        name="gla_lat_bwd" if reverse else "gla_lat_fwd",
    )(*args)


def _pack_pair(lo, hi):
    lo_b = pltpu.bitcast(lo.astype(BF16).astype(F32), I32)
    hi_b = pltpu.bitcast(hi.astype(BF16).astype(F32), I32)
    return lax.shift_right_logical(lo_b, 16) | (hi_b & jnp.int32(-65536))


def _unpack_pair(w):
    lo = pltpu.bitcast(lax.shift_left(w, 16), F32)
    hi = pltpu.bitcast(w & jnp.int32(-65536), F32)
    return lo, hi


def _merge_body(act_ref, og_ref, gc_ref, gg_ref, x_ref, gt_ref, sh_ref, sc_ref, gn_ref,
                wc_ref, wg_ref, wo_ref, wr_ref, br_ref,
                x1_ref, h2_ref, ti_ref, tp_ref, cnt_ref):
    yc = _dot(act_ref[...], wc_ref[...])
    yg = _dot(og_ref[...], wg_ref[...])
    m = gc_ref[...].astype(F32) * yc + gg_ref[...].astype(F32) * yg
    x1 = x_ref[...] + gt_ref[0] * _dot(m.astype(BF16), wo_ref[...])
    x1_ref[...] = x1
    ms = jnp.mean(x1 * x1, axis=-1, keepdims=True)
    h2 = (x1 * lax.rsqrt(ms + EPS) * gn_ref[...]) * (1.0 + sc_ref[0]) + sh_ref[0]
    half = D_MODEL // 2
    h2_ref[...] = _pack_pair(h2[:, :half], h2[:, half:])
    logits = _dot3(h2, wr_ref[...]) + br_ref[...]
    lane = lax.broadcasted_iota(I32, logits.shape, 1).astype(F32)
    neg = jnp.float32(-jnp.inf)
    work = jnp.where(lane < N_EXPERTS, logits, neg)
    vals, idxs = [], []
    for _ in range(TOP_K):
        mx = jnp.max(work, axis=-1, keepdims=True)
        ix = jnp.min(jnp.where(work == mx, lane, float(LANES)), axis=-1, keepdims=True)
        vals.append(mx)
        idxs.append(ix)
        work = jnp.where(lane == ix, neg, work)
    es = [jnp.exp(v - vals[0]) for v in vals]
    den = es[0] + es[1] + es[2] + es[3]
    ti = jnp.zeros(logits.shape, F32)
    tp = jnp.zeros(logits.shape, F32)
    onehot = jnp.zeros(logits.shape, F32)
    for kk in range(TOP_K):
        ti = jnp.where(lane == kk, idxs[kk], ti)
        tp = jnp.where(lane == kk, es[kk] / den, tp)
        onehot = onehot + jnp.where(lane == idxs[kk], 1.0, 0.0)
    @pl.when(pl.program_id(0) == 0)
    def _():
        cnt_ref[...] = jnp.zeros_like(cnt_ref)

    tm = logits.shape[0]
    earlier = (lax.broadcasted_iota(I32, (tm, tm), 1) < lax.broadcasted_iota(I32, (tm, tm), 0))
    before = _dot(jnp.where(earlier, 1.0, 0.0).astype(BF16), onehot.astype(BF16)) + cnt_ref[0:1, :]
    for kk in range(TOP_K):
        rank = jnp.sum(jnp.where(lane == idxs[kk], before, 0.0), axis=-1, keepdims=True)
        ti = jnp.where(lane == TOP_K + kk, rank, ti)
    cnt_ref[...] = cnt_ref[...] + jnp.sum(onehot, axis=0, keepdims=True)
    ti_ref[...] = ti.astype(I32)
    tp_ref[...] = tp


def _merge(act, og, gc, gg, x2d, gt1, sh2, sc2, gn, wc, wg, wo, wr, br, tm, rows_per_batch,
           first_row, n):
    d = x2d.shape[1]
    per_b = rows_per_batch // tm
    off = first_row // tm
    row_in = lambda w: pl.BlockSpec((tm, w), lambda i: (i + off, 0))
    row = lambda w: pl.BlockSpec((tm, w), lambda i: (i, 0))
    vec = pl.BlockSpec((1, 1, d), lambda i: ((i + off) // per_b, 0, 0))
    const = lambda shape: pl.BlockSpec(shape, lambda i: (0,) * len(shape))
    return pl.pallas_call(
        _merge_body,
        grid=(n // tm,),
        in_specs=[row_in(d), row_in(d), row_in(d), row_in(d), row_in(d), vec, vec, vec,
                  const((1, d)),
                  _resident(wc.shape), _resident(wg.shape), _resident(wo.shape),
                  const(wr.shape), const((1, LANES))],
        out_specs=[row(d), row(d // 2), row(LANES), row(LANES), const((SUBLANES, LANES))],
        out_shape=[jax.ShapeDtypeStruct((n, d), F32), jax.ShapeDtypeStruct((n, d // 2), I32),
                   jax.ShapeDtypeStruct((n, LANES), I32), jax.ShapeDtypeStruct((n, LANES), F32),
                   jax.ShapeDtypeStruct((SUBLANES, LANES), F32)],
        compiler_params=_cparams(("arbitrary",)),
        name="merge_router",
    )(act, og, gc, gg, x2d, gt1, sh2, sc2, gn, wc, wg, wo, wr, br)


_GLU_GROUP = 2 * LANES


def _deinterleave_matrix():
    src = lax.broadcasted_iota(I32, (_GLU_GROUP, _GLU_GROUP), 0)
    dst = lax.broadcasted_iota(I32, (_GLU_GROUP, _GLU_GROUP), 1)
    want = jnp.where(dst < LANES, 2 * dst, 2 * (dst - LANES) + 1)
    return jnp.where(src == want, 1.0, 0.0).astype(BF16)


def _expert_body(be_ref, nreal_ref, xp_ref, w1_ref, b1_ref, w2_ref, b2_ref, yp_ref,
                 w1s_ref, w2s_ref):
    i = pl.program_id(0)
    new_expert = jnp.logical_or(i == 0, be_ref[i] != be_ref[jnp.maximum(i - 1, 0)])

    @pl.when(jnp.logical_and(new_expert, i < nreal_ref[0]))
    def _():
        perm = _deinterleave_matrix()
        for g in range(2 * D_FF // _GLU_GROUP):
            cs = slice(g * _GLU_GROUP, (g + 1) * _GLU_GROUP)
            w1s_ref[:, cs] = _dot(w1_ref[0, :, cs].astype(BF16), perm).astype(BF16)
        w2s_ref[...] = w2_ref[0].astype(BF16)

    @pl.when(i < nreal_ref[0])
    def _():
        lo, hi = _unpack_pair(xp_ref[...])
        x = jnp.concatenate([lo, hi], axis=1).astype(BF16)
        hid = _dot(x, w1s_ref[...]) + b1_ref[0]
        ngrp = 2 * D_FF // _GLU_GROUP
        hg = jnp.concatenate(
            [hid[:, g * _GLU_GROUP:g * _GLU_GROUP + LANES] for g in range(ngrp)], axis=1)
        hl = jnp.concatenate(
            [hid[:, g * _GLU_GROUP + LANES:(g + 1) * _GLU_GROUP] for g in range(ngrp)], axis=1)
        xg = jnp.minimum(hg, SWIGLU_LIMIT)
        xl = jnp.clip(hl, -SWIGLU_LIMIT, SWIGLU_LIMIT)
        act = xg * _sigmoid(SWIGLU_ALPHA * xg) * (xl + 1.0)
        y = _dot(act.astype(BF16), w2s_ref[...]) + b2_ref[0]
        half = D_MODEL // 2
        yp_ref[...] = _pack_pair(y[:, :half], y[:, half:])

    @pl.when(i >= nreal_ref[0])
    def _():
        yp_ref[...] = jnp.zeros_like(yp_ref)


def _experts(blk_expert, n_real, xp, w1, b1, w2, b2, bm):
    p, w = xp.shape
    grid_spec = pltpu.PrefetchScalarGridSpec(
        num_scalar_prefetch=2,
        grid=(p // bm,),
        in_specs=[pl.BlockSpec((bm, w), lambda i, be, nr: (i, 0)),
                  pl.BlockSpec((1, D_MODEL, 2 * D_FF), lambda i, be, nr: (be[i], 0, 0)),
                  pl.BlockSpec((1, 1, 2 * D_FF), lambda i, be, nr: (be[i], 0, 0)),
                  pl.BlockSpec((1, D_FF, D_MODEL), lambda i, be, nr: (be[i], 0, 0)),
                  pl.BlockSpec((1, 1, D_MODEL), lambda i, be, nr: (be[i], 0, 0))],
        out_specs=pl.BlockSpec((bm, w), lambda i, be, nr: (i, 0)),
        scratch_shapes=[pltpu.VMEM((D_MODEL, 2 * D_FF), BF16), pltpu.VMEM((D_FF, D_MODEL), BF16)],
    )
    return pl.pallas_call(
        _expert_body,
        grid_spec=grid_spec,
        out_shape=jax.ShapeDtypeStruct((p, w), I32),
        compiler_params=_cparams(("arbitrary",)),
        name="moe_experts",
    )(blk_expert, n_real, xp, w1, b1, w2, b2)


_SC_CORES = 2
_SC_SUBCORES = 16
_SC_CHUNK = 64


def _sc_gather_rows(table, idx):
    rows, width = idx.shape[0], table.shape[1]
    workers = _SC_CORES * _SC_SUBCORES
    per_worker = rows // workers
    assert rows % (workers * _SC_CHUNK) == 0
    mesh = plsc.VectorSubcoreMesh(core_axis_name="c", subcore_axis_name="s")

    n_chunks = per_worker // _SC_CHUNK
    assert n_chunks % 2 == 0 and n_chunks >= 4

    def body(table_hbm, idx_hbm, out_hbm, idx0, idx1, rows0, rows1, gsem0, gsem1, wsem0, wsem1):
        wid = lax.axis_index("s") * _SC_CORES + lax.axis_index("c")
        base = wid * per_worker
        bufs = ((idx0, rows0, gsem0, wsem0), (idx1, rows1, gsem1, wsem1))

        def out_rows(c):
            return out_hbm.at[pl.ds(base + c * _SC_CHUNK, _SC_CHUNK)]

        def gather_start(c, b):
            idx_v, rows_v, gsem, _ = bufs[b]
            pltpu.sync_copy(idx_hbm.at[pl.ds(base + c * _SC_CHUNK, _SC_CHUNK)], idx_v)
            pltpu.make_async_copy(table_hbm.at[idx_v], rows_v, gsem).start()

        def gather_wait(b):
            idx_v, rows_v, gsem, _ = bufs[b]
            pltpu.make_async_copy(table_hbm.at[idx_v], rows_v, gsem).wait()

        def write_start(c, b):
            _, rows_v, _, wsem = bufs[b]
            pltpu.make_async_copy(rows_v, out_rows(c), wsem).start()

        def write_wait(c, b):
            _, rows_v, _, wsem = bufs[b]
            pltpu.make_async_copy(rows_v, out_rows(c), wsem).wait()

        gather_start(0, 0)
        gather_wait(0)
        write_start(0, 0)
        gather_start(1, 1)

        @pl.loop(1, n_chunks - 1, step=2)
        def _(c):
            gather_wait(1)
            write_start(c, 1)
            write_wait(c - 1, 0)
            gather_start(c + 1, 0)
            gather_wait(0)
            write_start(c + 1, 0)
            write_wait(c, 1)
            gather_start(c + 2, 1)

        gather_wait(1)
        write_start(n_chunks - 1, 1)
        write_wait(n_chunks - 2, 0)
        write_wait(n_chunks - 1, 1)

    chunk = lambda: pltpu.VMEM((_SC_CHUNK, width), table.dtype)
    return pl.kernel(
        body,
        out_type=jax.ShapeDtypeStruct((rows, width), table.dtype),
        mesh=mesh,
        scratch_types=[pltpu.VMEM((_SC_CHUNK,), I32), pltpu.VMEM((_SC_CHUNK,), I32), chunk(), chunk(),
                       pltpu.SemaphoreType.DMA, pltpu.SemaphoreType.DMA,
                       pltpu.SemaphoreType.DMA, pltpu.SemaphoreType.DMA],
        name="sc_gather_rows",
    )(table, idx)


def _sc_scatter_rows(rows, idx_slots, out_rows):
    n, width = rows.shape
    workers = _SC_CORES * _SC_SUBCORES
    per_worker = n // workers
    assert n % (workers * _SC_CHUNK) == 0
    mesh = plsc.VectorSubcoreMesh(core_axis_name="c", subcore_axis_name="s")

    def body(rows_hbm, idx_hbm, out_hbm, idx_v, rows_v):
        wid = lax.axis_index("s") * _SC_CORES + lax.axis_index("c")
        base = wid * per_worker

        @pl.loop(0, per_worker // _SC_CHUNK)
        def _(i):
            t0 = base + i * _SC_CHUNK
            pltpu.sync_copy(rows_hbm.at[pl.ds(t0, _SC_CHUNK)], rows_v)
            for kk in range(TOP_K):
                pltpu.sync_copy(idx_hbm.at[pl.ds(kk * n + t0, _SC_CHUNK)], idx_v)
                pltpu.sync_copy(rows_v, out_hbm.at[idx_v])

    return pl.kernel(
        body,
        out_type=jax.ShapeDtypeStruct((out_rows, width), rows.dtype),
        mesh=mesh,
        scratch_types=[pltpu.VMEM((_SC_CHUNK,), I32), pltpu.VMEM((_SC_CHUNK, width), rows.dtype)],
        name="sc_scatter_rows",
    )(rows, idx_slots)


def _combine_dense_body(y4_ref, tp_ref, x1_ref, gt_ref, gf_ref, *out_refs):
    o_ref = out_refs[-1]
    half = D_MODEL // 2
    tp = tp_ref[...]
    y_lo = y_hi = None
    for kk in range(TOP_K):
        lo, hi = _unpack_pair(y4_ref[kk])
        pk = tp[:, kk:kk + 1]
        y_lo = pk * lo if y_lo is None else y_lo + pk * lo
        y_hi = pk * hi if y_hi is None else y_hi + pk * hi
    x2 = x1_ref[...] + gt_ref[0] * jnp.concatenate([y_lo, y_hi], axis=1)
    ms = jnp.mean(x2 * x2, axis=-1, keepdims=True)
    o_ref[...] = x2 * lax.rsqrt(ms + EPS) * gf_ref[...]


def _combine_dense(y4, tp, x1, gt2, gf, tokens, rows_per_batch, first_row, n_total, out_prev):
    n, d = x1.shape
    per_b = rows_per_batch // tokens
    off = first_row // tokens
    in_specs = [pl.BlockSpec((TOP_K, tokens, d // 2), lambda i: (0, i, 0)),
                pl.BlockSpec((tokens, LANES), lambda i: (i, 0)),
                pl.BlockSpec((tokens, d), lambda i: (i, 0)),
                pl.BlockSpec((1, 1, d), lambda i: ((i + off) // per_b, 0, 0)),
                pl.BlockSpec((1, d), lambda i: (0, 0))]
    args = [y4, tp, x1, gt2, gf]
    aliases = {}
    if out_prev is not None:
        in_specs.append(pl.BlockSpec(memory_space=pl.ANY))
        args.append(out_prev)
        aliases = {len(args) - 1: 0}
    return pl.pallas_call(
        _combine_dense_body,
        grid=(n // tokens,),
        in_specs=in_specs,
        out_specs=pl.BlockSpec((tokens, d), lambda i: (i + off, 0)),
        out_shape=jax.ShapeDtypeStruct((n_total, d), F32),
        input_output_aliases=aliases,
        compiler_params=_cparams(("arbitrary",)),
        name="moe_combine_dense",
    )(*args)


def _routing_tables(top_idx, rank, counts, bm):
    n = top_idx.shape[0]
    nk = n * TOP_K
    padded = (counts + bm - 1) // bm * bm
    pad_end = jnp.cumsum(padded)
    pad_start = pad_end - padded
    dest = (pad_start[top_idx] + rank).astype(I32)
    n_blocks = (nk + N_EXPERTS * (bm - 1) + bm - 1) // bm
    starts = jnp.arange(n_blocks, dtype=I32) * bm
    blk_expert = jnp.minimum(jnp.sum((pad_end[None, :] <= starts[:, None]).astype(I32), axis=1),
                             N_EXPERTS - 1).astype(I32)
    n_real = (pad_end[-1] // bm).astype(I32).reshape(1)
    return dest, blk_expert, n_real, n_blocks


def _layer(x, c, ctx, c_ctx, w_ada, b_ada, g_mix_norm, w_in, w_dw, b_dw, g_conv_ln, b_conv_ln,
           w_conv_out, w_alpha, b_alpha, g_gla_norm, w_gla_out, w_out, g_ffn_norm, w_router,
           b_router, w_exp_in, b_exp_in, w_exp_out, b_exp_out, g_final, *, cfg):
    b, s, d = x.shape
    n = b * s

    rows = (b + 1 + SUBLANES - 1) // SUBLANES * SUBLANES
    cc = jnp.zeros((rows, d), F32).at[:b].set(c).at[b].set(c_ctx)
    mod = _ada(cc, w_ada, b_ada)
    sh1, sc1, gt1, sh2, sc2, gt2 = [mod[:b, i * d:(i + 1) * d].reshape(b, 1, d) for i in range(6)]
    csh1 = mod[b:b + 1, 0:d]
    csc1 = mod[b:b + 1, d:2 * d]

    a0 = 2 * CONV_W + 2 * GLA_KD + 2 * GLA_VD
    w_in_r = jnp.concatenate(
        [w_in[:, :a0], w_in[:, a0 + 2 * GLA_RANK:], w_in[:, a0:a0 + 2 * GLA_RANK],
         jnp.zeros((d, LANES - 2 * GLA_RANK), F32)], axis=1).astype(BF16)
    gmn = g_mix_norm.reshape(1, d)

    act, q, k, v, sg, gc, gg, a = _inproj_lat(x, gmn, sh1, sc1, w_in_r, w_dw, b_dw, g_conv_ln,
                                              b_conv_ln, cfg["tm_in"])
    act = act.reshape(n, CONV_W)
    kc, vc, ac = _inproj_ctx(ctx, gmn, csh1, csc1, w_in_r, cfg["tm_ctx"])

    cps, cps_ctx = cfg["gla_cps"], cfg["gla_cps_ctx"]
    st_f = _gla_ctx(kc, vc, ac, w_alpha[0], b_alpha[0], False, cps_ctx)
    o_f = _gla_lat(k, v, a, q, w_alpha[0], b_alpha[0], st_f, False, cps)
    st_b = _gla_ctx(kc, vc, ac, w_alpha[1], b_alpha[1], True, cps_ctx)
    og = _gla_lat(k, v, a, q, w_alpha[1], b_alpha[1], st_b, True, cps,
                  sg=sg, o_prev=o_f, g_norm=g_gla_norm)

    wr = jnp.zeros((d, LANES), F32).at[:, :N_EXPERTS].set(w_router)
    br = jnp.zeros((1, LANES), F32).at[0, :N_EXPERTS].set(b_router)
    b1 = b_exp_in.reshape(N_EXPERTS, 2 * D_FF // _GLU_GROUP, LANES, 2).transpose(0, 1, 3, 2)
    b1 = b1.reshape(N_EXPERTS, 1, 2 * D_FF)
    bm = cfg["moe_block"]

    groups = cfg["moe_groups"]
    ng = n // groups
    staged = []
    for g in range(groups):
        x1, h2p, ti, tp, cnt = _merge(
            act, og.reshape(n, d), gc.reshape(n, d), gg.reshape(n, d), x.reshape(n, d),
            gt1, sh2, sc2, g_ffn_norm.reshape(1, d),
            w_conv_out.astype(BF16), w_gla_out.astype(BF16), w_out.astype(BF16), wr, br,
            cfg["tm_merge"], s, g * ng, ng)
        dest, blk_expert, n_real, n_blocks = _routing_tables(
            ti[:, :TOP_K], ti[:, TOP_K:2 * TOP_K], cnt[0, :N_EXPERTS].astype(I32), bm)
        dest_slots = dest.T.reshape(ng * TOP_K)
        xp = _sc_scatter_rows(h2p, dest_slots, n_blocks * bm)
        staged.append((x1, tp, dest_slots, blk_expert, n_real, xp))
    gathered = []
    for x1, tp, dest_slots, blk_expert, n_real, xp in staged:
        yp = _experts(blk_expert, n_real, xp, w_exp_in, b1, w_exp_out,
                      b_exp_out.reshape(N_EXPERTS, 1, d), bm)
        gathered.append(_sc_gather_rows(yp, dest_slots).reshape(TOP_K, ng, d // 2))
    out = None
    for g, (x1, tp, *_) in enumerate(staged):
        out = _combine_dense(gathered[g], tp, x1, gt2, g_final.reshape(1, d), cfg["tm_combine"], s,
                             g * ng, n, out)
    return out.reshape(b, s, d)


def _config(s, l):
    return dict(tm_in=min(512, s), tm_ctx=min(256, l), gla_cps=2, gla_cps_ctx=4,
                tm_merge=min(512, s), moe_block=512, tm_combine=min(1024, s), moe_groups=2)


def kernel(x, c, ctx, c_ctx, w_ada, b_ada, g_mix_norm, w_in, w_dw, b_dw, g_conv_ln, b_conv_ln,
           w_conv_out, w_alpha, b_alpha, g_gla_norm, w_gla_out, w_out, g_ffn_norm, w_router,
           b_router, w_exp_in, b_exp_in, w_exp_out, b_exp_out, g_final):
    depth = w_ada.shape[0]
    assert depth == 1, "single-layer block: the context stream is only consumed by the GLA scan"
    cfg = _config(x.shape[1], ctx.shape[1])
    return _layer(x, c, ctx, c_ctx, w_ada[0], b_ada[0], g_mix_norm[0], w_in[0], w_dw[0], b_dw[0],
                  g_conv_ln[0], b_conv_ln[0], w_conv_out[0], w_alpha[0], b_alpha[0],
                  g_gla_norm[0], w_gla_out[0], w_out[0], g_ffn_norm[0], w_router[0], b_router[0],
                  w_exp_in[0], b_exp_in[0], w_exp_out[0], b_exp_out[0], g_final, cfg=cfg)
```

```python
import functools

import jax
import jax.numpy as jnp
from jax import lax
from jax.experimental import pallas as pl
from jax.experimental.pallas import tpu as pltpu
from jax.experimental.pallas import tpu_sc as plsc

F32 = jnp.float32
BF16 = jnp.bfloat16
I32 = jnp.int32

D_MODEL = 1024
GRID_W = 64
EPS = 1e-6
CONV_W = 1024
CONV_K = 31
GLA_H = 4
GLA_DK = 128
GLA_DV = 256
GLA_KD = GLA_H * GLA_DK
GLA_VD = GLA_H * GLA_DV
GLA_RANK = 16
GLA_TAU = 16.0
GLA_CHUNK = 64
N_EXPERTS = 32
TOP_K = 4
D_FF = 1024
SWIGLU_ALPHA = 1.702
SWIGLU_LIMIT = 7.0

LANES = 128
SUBLANES = 8
VMEM_LIMIT = 56 * 1024 * 1024

_C_CONV_A = 0
_C_CONV_B = _C_CONV_A + CONV_W
_C_Q = _C_CONV_B + CONV_W
_C_K = _C_Q + GLA_KD
_C_V = _C_K + GLA_KD
_C_G = _C_V + GLA_VD
_C_GC = _C_G + GLA_VD
_C_GG = _C_GC + D_MODEL
_C_A = _C_GG + D_MODEL
_C_END = _C_A + LANES


def _cparams(sem):
    return pltpu.CompilerParams(dimension_semantics=sem, vmem_limit_bytes=VMEM_LIMIT)


def _dot(a, b):
    return jnp.dot(a, b, preferred_element_type=F32)


def _split_bf16(x):
    hi = x.astype(BF16)
    lo = (x - hi.astype(F32)).astype(BF16)
    return hi, lo


def _dot3(a, b):
    a_hi, a_lo = _split_bf16(a)
    b_hi, b_lo = _split_bf16(b)
    return _dot(a_hi, b_hi) + _dot(a_lo, b_hi) + _dot(a_hi, b_lo)


def _sigmoid(x):
    return 1.0 / (1.0 + jnp.exp(-x))


def _resident(shape):
    nd = len(shape)
    return pl.BlockSpec(shape, lambda *_: (0,) * nd, pipeline_mode=pl.Buffered(1))


def _ada_body(a_ref, w_ref, b_ref, o_ref):
    a = a_ref[...]
    a = a * _sigmoid(a)
    o_ref[...] = _dot3(a, w_ref[...]) + b_ref[...]


def _ada(cc, w, b):
    rows, d = cc.shape
    n = w.shape[1]
    tn = 512
    return pl.pallas_call(
        _ada_body,
        grid=(n // tn,),
        in_specs=[pl.BlockSpec((rows, d), lambda j: (0, 0)),
                  pl.BlockSpec((d, tn), lambda j: (0, j)),
                  pl.BlockSpec((1, tn), lambda j: (0, j))],
        out_specs=pl.BlockSpec((rows, tn), lambda j: (0, j)),
        out_shape=jax.ShapeDtypeStruct((rows, n), F32),
        compiler_params=_cparams(("arbitrary",)),
        name="ada_mod",
    )(cc, w, b.reshape(1, n))


def _norm_mod(xv, gn, sc, sh):
    ms = jnp.mean(xv * xv, axis=-1, keepdims=True)
    y = xv * lax.rsqrt(ms + EPS) * gn
    return (y * (1.0 + sc) + sh).astype(BF16)


def _inproj_lat_body(x_ref, gn_ref, sh_ref, sc_ref, w_ref, sm_ref, wdw_ref, bdw_ref, gln_ref, bln_ref,
                     act_ref, q_ref, k_ref, v_ref, sg_ref, gc_ref, gg_ref, a_ref,
                     y_ref):
    h = _norm_mod(x_ref[0], gn_ref[...], sc_ref[0], sh_ref[0])
    ca = _dot(h, w_ref[:, _C_CONV_A:_C_CONV_B])
    cb = _dot(h, w_ref[:, _C_CONV_B:_C_Q])
    u = (ca * _sigmoid(cb)).astype(BF16)
    q_ref[0] = _dot(h, w_ref[:, _C_Q:_C_K]).astype(BF16)
    k_ref[0] = _dot(h, w_ref[:, _C_K:_C_V]).astype(BF16)
    v_ref[0] = _dot(h, w_ref[:, _C_V:_C_G]).astype(BF16)
    g = _dot(h, w_ref[:, _C_G:_C_GC])
    sg_ref[0] = (g * _sigmoid(g)).astype(BF16)
    gc_ref[0] = _sigmoid(_dot(h, w_ref[:, _C_GC:_C_GG])).astype(BF16)
    gg_ref[0] = _sigmoid(_dot(h, w_ref[:, _C_GG:_C_A])).astype(BF16)
    a_ref[0] = _dot(h, w_ref[:, _C_A:_C_END])
    act_ref[0] = _conv_ln_swish(u, sm_ref, wdw_ref, bdw_ref, gln_ref, bln_ref, y_ref)


def _inproj_ctx_body(x_ref, gn_ref, sh_ref, sc_ref, w_ref, k_ref, v_ref, a_ref):
    h = _norm_mod(x_ref[0], gn_ref[...], sc_ref[...], sh_ref[...])
    k_ref[0] = _dot(h, w_ref[:, _C_K:_C_V]).astype(BF16)
    v_ref[0] = _dot(h, w_ref[:, _C_V:_C_G]).astype(BF16)
    a_ref[0] = _dot(h, w_ref[:, _C_A:_C_END])


def _inproj_lat(x, gn, sh, sc, w, w_dw, b_dw, g_ln, b_ln, tm):
    b, s, d = x.shape
    row = lambda n: pl.BlockSpec((1, tm, n), lambda bi, i: (bi, i, 0))
    vec = pl.BlockSpec((1, 1, d), lambda bi, i: (bi, 0, 0))
    const = lambda shape: pl.BlockSpec(shape, lambda bi, i: (0,) * len(shape))
    shp = lambda n, dt: jax.ShapeDtypeStruct((b, s, n), dt)
    sm = _conv_shift_matrix()
    taps = (CONV_K + SUBLANES - 1) // SUBLANES * SUBLANES
    wpad = jnp.zeros((taps, CONV_W), F32).at[:CONV_K].set(w_dw)
    return pl.pallas_call(
        _inproj_lat_body,
        grid=(b, s // tm),
        in_specs=[row(d), const((1, d)), vec, vec, _resident(w.shape),
                  const(sm.shape), const(wpad.shape), const((1, CONV_W)), const((1, CONV_W)),
                  const((1, CONV_W))],
        out_specs=[row(CONV_W), row(GLA_KD), row(GLA_KD), row(GLA_VD), row(GLA_VD),
                   row(d), row(d), row(LANES)],
        out_shape=[shp(CONV_W, BF16), shp(GLA_KD, BF16), shp(GLA_KD, BF16), shp(GLA_VD, BF16),
                   shp(GLA_VD, BF16), shp(d, BF16), shp(d, BF16), shp(LANES, F32)],
        scratch_shapes=[pltpu.VMEM((tm, CONV_W), F32)],
        compiler_params=_cparams(("arbitrary", "arbitrary")),
        name="inproj_lat",
    )(x, gn, sh, sc, w, sm, wpad, b_dw.reshape(1, CONV_W), g_ln.reshape(1, CONV_W),
      b_ln.reshape(1, CONV_W))


def _inproj_ctx(ctx, gn, sh, sc, w, tm):
    b, l, d = ctx.shape
    row = lambda n: pl.BlockSpec((1, tm, n), lambda bi, i: (bi, i, 0))
    vec = pl.BlockSpec((1, d), lambda bi, i: (0, 0))
    shp = lambda n, dt: jax.ShapeDtypeStruct((b, l, n), dt)
    return pl.pallas_call(
        _inproj_ctx_body,
        grid=(b, l // tm),
        in_specs=[row(d), vec, vec, vec, _resident(w.shape)],
        out_specs=[row(GLA_KD), row(GLA_VD), row(LANES)],
        out_shape=[shp(GLA_KD, BF16), shp(GLA_VD, BF16), shp(LANES, F32)],
        compiler_params=_cparams(("arbitrary", "arbitrary")),
        name="inproj_ctx",
    )(ctx, gn, sh, sc, w)


_CONV_SUB = SUBLANES
_CONV_USED = GRID_W + _CONV_SUB * ((CONV_K - 1) // _CONV_SUB)
_CONV_SPAN = -(-_CONV_USED // (2 * SUBLANES)) * (2 * SUBLANES)
_CONV_LANES = 2 * LANES


def _conv_shift_matrix():
    row = lax.broadcasted_iota(I32, (_CONV_SUB, _CONV_SPAN, GRID_W), 1)
    shift = lax.broadcasted_iota(I32, (_CONV_SUB, _CONV_SPAN, GRID_W), 0)
    col = lax.broadcasted_iota(I32, (_CONV_SUB, _CONV_SPAN, GRID_W), 2)
    m = jnp.where(col == row + shift - CONV_K // 2, 1.0, 0.0)
    return m.reshape(_CONV_SUB * _CONV_SPAN, GRID_W).astype(BF16)


def _conv_ln_swish(u, sm_ref, w_ref, bdw_ref, gln_ref, bln_ref, y_ref):
    for r in range(u.shape[0] // GRID_W):
        ur = u[r * GRID_W:(r + 1) * GRID_W, :]
        for cq in range(CONV_W // _CONV_LANES):
            ls = slice(cq * _CONV_LANES, (cq + 1) * _CONV_LANES)
            acc = None
            for s in range(_CONV_SUB):
                win = _dot(sm_ref[s * _CONV_SPAN:(s + 1) * _CONV_SPAN, :], ur[:, ls])
                for a in range((CONV_K - 1 - s) // _CONV_SUB + 1):
                    k = _CONV_SUB * a + s
                    term = win[_CONV_SUB * a:_CONV_SUB * a + GRID_W, :] * w_ref[k:k + 1, ls]
                    acc = term if acc is None else acc + term
            y_ref[r * GRID_W:(r + 1) * GRID_W, ls] = acc + bdw_ref[:, ls]
    y = y_ref[...]
    mu = jnp.mean(y, axis=-1, keepdims=True)
    yc = y - mu
    var = jnp.mean(yc * yc, axis=-1, keepdims=True)
    yn = yc * lax.rsqrt(var + EPS) * gln_ref[...] + bln_ref[...]
    return (yn * _sigmoid(yn)).astype(BF16)


def _log_sigmoid(z):
    return jnp.minimum(z, 0.0) - jnp.log(1.0 + jnp.exp(-jnp.abs(z)))


def _gla_keep_mask(t, reverse):
    row = lax.broadcasted_iota(I32, (t, t), 0)
    col = lax.broadcasted_iota(I32, (t, t), 1)
    log2c = GLA_CHUNK.bit_length() - 1
    same_chunk = lax.shift_right_logical(row, log2c) == lax.shift_right_logical(col, log2c)
    return jnp.logical_and(same_chunk, (col >= row) if reverse else (col <= row))


def _gla_decay(a, wh_ref, wl_ref, ba, keep):
    a_hi, a_lo = _split_bf16(a)
    z = _dot(jnp.concatenate([a_hi, a_lo], axis=1), wh_ref[...]) + _dot(a_hi, wl_ref[...]) + ba
    loga = _log_sigmoid(z) * (1.0 / GLA_TAU)
    l_hi, l_lo = _split_bf16(loga)
    tri = jnp.where(keep, 1.0, 0.0).astype(BF16)
    return _dot(tri, l_hi) + _dot(tri, l_lo)


def _gla_step(k, v, bcum, keep, states, *, reverse, q=None):
    t = k.shape[0]
    c = GLA_CHUNK
    nch = t // c
    tots = [bcum[n * c:n * c + 1, :] if reverse else bcum[(n + 1) * c - 1:(n + 1) * c, :]
            for n in range(nch)]
    totb = jnp.concatenate([jnp.broadcast_to(tt, (c, GLA_KD)) for tt in tots], axis=0)
    k32 = k.astype(F32)
    k_end = (k32 * jnp.exp(totb - bcum)).astype(BF16)
    decs = [jnp.exp(tt) for tt in tots]
    if q is not None:
        q_dec = (q.astype(F32) * jnp.exp(bcum) * (GLA_DK ** -0.5)).astype(BF16)
        k_inv = (k32 * jnp.exp(-bcum)).astype(BF16)
    outs, new_states = [], []
    for h in range(GLA_H):
        ks = slice(h * GLA_DK, (h + 1) * GLA_DK)
        vh = v[:, h * GLA_DV:(h + 1) * GLA_DV]
        st = states[h]
        if q is not None:
            scores = lax.dot_general(q_dec[:, ks], k_inv[:, ks], (((1,), (1,)), ((), ())),
                                     preferred_element_type=F32)
            o_h = _dot(jnp.where(keep, scores, 0.0).astype(BF16), vh)
            inter = []
        for n in range(nch):
            rs = slice(n * c, (n + 1) * c)
            if q is not None:
                inter.append(_dot(q_dec[rs, ks], st.astype(BF16)))
            kv = lax.dot_general(k_end[rs, ks], vh[rs, :], (((0,), (0,)), ((), ())),
                                 preferred_element_type=F32)
            dt = jnp.transpose(jnp.broadcast_to(decs[n][:, ks], (GLA_DK, GLA_DK)))
            st = st * jnp.concatenate([dt] * (GLA_DV // GLA_DK), axis=1) + kv
        new_states.append(st)
        if q is not None:
            outs.append(o_h + jnp.concatenate(inter, axis=0))
    return new_states, (jnp.concatenate(outs, axis=1) if q is not None else None)


def _scan_order(x, reverse):
    if not reverse:
        return x
    nch = x.shape[0] // GLA_CHUNK
    return jnp.concatenate(
        [x[n * GLA_CHUNK:(n + 1) * GLA_CHUNK] for n in range(nch - 1, -1, -1)], axis=0)


def _gla_ctx_body(k_ref, v_ref, a_ref, wh_ref, wl_ref, ba_ref, st_ref, *, reverse):
    j = pl.program_id(1)

    @pl.when(j == 0)
    def _():
        st_ref[...] = jnp.zeros_like(st_ref)

    keep = _gla_keep_mask(k_ref.shape[1], reverse)
    scan = functools.partial(_scan_order, reverse=reverse)
    bcum = _gla_decay(scan(a_ref[0]), wh_ref, wl_ref, ba_ref[...], keep)
    states, _ = _gla_step(scan(k_ref[0]), scan(v_ref[0]), bcum, keep,
                          [st_ref[0, h] for h in range(GLA_H)], reverse=reverse)
    for h in range(GLA_H):
        st_ref[0, h] = states[h]


def _gla_lat_body(*refs, reverse, final):
    if final:
        (k_ref, v_ref, a_ref, an_ref, q_ref, wh_ref, wl_ref, ba_ref, s0_ref, sg_ref, op_ref,
         gn_ref, o_ref, st_ref, bc_ref) = refs
    else:
        (k_ref, v_ref, a_ref, an_ref, q_ref, wh_ref, wl_ref, ba_ref, s0_ref,
         o_ref, st_ref, bc_ref) = refs
    j = pl.program_id(1)
    nb, t = k_ref.shape[0], k_ref.shape[1]
    keep = _gla_keep_mask(t, reverse)
    scan = functools.partial(_scan_order, reverse=reverse)

    @pl.when(j == 0)
    def _():
        st_ref[...] = s0_ref[...]
        for bb in range(nb):
            bc_ref[bb] = _gla_decay(scan(a_ref[bb]), wh_ref, wl_ref, ba_ref[...], keep)

    for bb in range(nb):
        states, o = _gla_step(scan(k_ref[bb]), scan(v_ref[bb]), bc_ref[bb], keep,
                              [st_ref[bb, h] for h in range(GLA_H)], reverse=reverse,
                              q=scan(q_ref[bb]))
        bc_ref[bb] = _gla_decay(scan(an_ref[bb]), wh_ref, wl_ref, ba_ref[...], keep)
        for h in range(GLA_H):
            st_ref[bb, h] = states[h]
        if not final:
            o_ref[bb] = scan(o.astype(BF16))
            continue
        o = o + scan(op_ref[bb]).astype(F32)
        parts = []
        for h in range(GLA_H):
            oh = o[:, h * GLA_DV:(h + 1) * GLA_DV]
            ms = jnp.mean(oh * oh, axis=-1, keepdims=True)
            parts.append(oh * lax.rsqrt(ms + EPS) * gn_ref[...])
        o_ref[bb] = scan((jnp.concatenate(parts, axis=1)
                          * scan(sg_ref[bb]).astype(F32)).astype(BF16))


def _gla_decay_weights(w_alpha_d, reverse):
    a0 = GLA_RANK if reverse else 0
    w = jnp.zeros((LANES, GLA_KD), F32).at[a0:a0 + GLA_RANK].set(w_alpha_d)
    hi = w.astype(BF16)
    lo = (w - hi.astype(F32)).astype(BF16)
    return jnp.concatenate([hi, hi], axis=0), lo


def _gla_ctx(k, v, a, w_alpha_d, ba, reverse, cps):
    b, l, _ = k.shape
    tm = cps * GLA_CHUNK
    nj = l // tm
    jmap = (lambda j: nj - 1 - j) if reverse else (lambda j: j)
    row = lambda n: pl.BlockSpec((1, tm, n), lambda bi, j: (bi, jmap(j), 0))
    wh, wl = _gla_decay_weights(w_alpha_d, reverse)
    return pl.pallas_call(
        functools.partial(_gla_ctx_body, reverse=reverse),
        grid=(b, nj),
        in_specs=[row(GLA_KD), row(GLA_VD), row(LANES),
                  pl.BlockSpec(wh.shape, lambda bi, j: (0, 0)),
                  pl.BlockSpec(wl.shape, lambda bi, j: (0, 0)),
                  pl.BlockSpec((1, GLA_KD), lambda bi, j: (0, 0))],
        out_specs=pl.BlockSpec((1, GLA_H, GLA_DK, GLA_DV), lambda bi, j: (bi, 0, 0, 0)),
        out_shape=jax.ShapeDtypeStruct((b, GLA_H, GLA_DK, GLA_DV), F32),
        compiler_params=_cparams(("arbitrary", "arbitrary")),
        name="gla_ctx_bwd" if reverse else "gla_ctx_fwd",
    )(k, v, a, wh, wl, ba.reshape(1, GLA_KD))


def _gla_lat(k, v, a, q, w_alpha_d, ba, s0, reverse, cps, sg=None, o_prev=None, g_norm=None):
    b, s, _ = k.shape
    final = sg is not None
    tm = cps * GLA_CHUNK
    nj = s // tm
    jmap = (lambda j: nj - 1 - j) if reverse else (lambda j: j)
    nb = 8 if b % 8 == 0 else 2 if b % 2 == 0 else 1
    row = lambda n: pl.BlockSpec((nb, tm, n), lambda bi, j: (bi, jmap(j), 0))
    wh, wl = _gla_decay_weights(w_alpha_d, reverse)
    a_next = pl.BlockSpec((nb, tm, LANES), lambda bi, j: (bi, jmap(jnp.minimum(j + 1, nj - 1)), 0))
    in_specs = [row(GLA_KD), row(GLA_VD), row(LANES), a_next, row(GLA_KD),
                pl.BlockSpec(wh.shape, lambda bi, j: (0, 0)),
                pl.BlockSpec(wl.shape, lambda bi, j: (0, 0)),
                pl.BlockSpec((1, GLA_KD), lambda bi, j: (0, 0)),
                pl.BlockSpec((nb, GLA_H, GLA_DK, GLA_DV), lambda bi, j: (bi, 0, 0, 0))]
    args = [k, v, a, a, q, wh, wl, ba.reshape(1, GLA_KD), s0]
    if final:
        in_specs += [row(GLA_VD), row(GLA_VD), pl.BlockSpec((1, GLA_DV), lambda bi, j: (0, 0))]
        args += [sg, o_prev, g_norm.reshape(1, GLA_DV)]
    return pl.pallas_call(
        functools.partial(_gla_lat_body, reverse=reverse, final=final),
        grid=(b // nb, nj),
        in_specs=in_specs,
        out_specs=row(GLA_VD),
        out_shape=jax.ShapeDtypeStruct((b, s, GLA_VD), BF16),
        scratch_shapes=[pltpu.VMEM((nb, GLA_H, GLA_DK, GLA_DV), F32),
                        pltpu.VMEM((nb, tm, GLA_KD), F32)],
        compiler_params=_cparams(("arbitrary", "arbitrary")),
        name="gla_lat_bwd" if reverse else "gla_lat_fwd",
    )(*args)


def _pack_pair(lo, hi):
    lo_b = pltpu.bitcast(lo.astype(BF16).astype(F32), I32)
    hi_b = pltpu.bitcast(hi.astype(BF16).astype(F32), I32)
    return lax.shift_right_logical(lo_b, 16) | (hi_b & jnp.int32(-65536))


def _unpack_pair(w):
    lo = pltpu.bitcast(lax.shift_left(w, 16), F32)
    hi = pltpu.bitcast(w & jnp.int32(-65536), F32)
    return lo, hi


def _merge_body(act_ref, og_ref, gc_ref, gg_ref, x_ref, gt_ref, sh_ref, sc_ref, gn_ref,
                wc_ref, wg_ref, wo_ref, wr_ref, br_ref,
                x1_ref, h2_ref, ti_ref, tp_ref, cnt_ref):
    yc = _dot(act_ref[...], wc_ref[...])
    yg = _dot(og_ref[...], wg_ref[...])
    m = gc_ref[...].astype(F32) * yc + gg_ref[...].astype(F32) * yg
    x1 = x_ref[...] + gt_ref[0] * _dot(m.astype(BF16), wo_ref[...])
    x1_ref[...] = x1
    ms = jnp.mean(x1 * x1, axis=-1, keepdims=True)
    h2 = (x1 * lax.rsqrt(ms + EPS) * gn_ref[...]) * (1.0 + sc_ref[0]) + sh_ref[0]
    half = D_MODEL // 2
    h2_ref[...] = _pack_pair(h2[:, :half], h2[:, half:])
    logits = _dot3(h2, wr_ref[...]) + br_ref[...]
    lane = lax.broadcasted_iota(I32, logits.shape, 1).astype(F32)
    neg = jnp.float32(-jnp.inf)
    work = jnp.where(lane < N_EXPERTS, logits, neg)
    vals, idxs = [], []
    for _ in range(TOP_K):
        mx = jnp.max(work, axis=-1, keepdims=True)
        ix = jnp.min(jnp.where(work == mx, lane, float(LANES)), axis=-1, keepdims=True)
        vals.append(mx)
        idxs.append(ix)
        work = jnp.where(lane == ix, neg, work)
    es = [jnp.exp(v - vals[0]) for v in vals]
    den = es[0] + es[1] + es[2] + es[3]
    ti = jnp.zeros(logits.shape, F32)
    tp = jnp.zeros(logits.shape, F32)
    onehot = jnp.zeros(logits.shape, F32)
    for kk in range(TOP_K):
        ti = jnp.where(lane == kk, idxs[kk], ti)
        tp = jnp.where(lane == kk, es[kk] / den, tp)
        onehot = onehot + jnp.where(lane == idxs[kk], 1.0, 0.0)
    @pl.when(pl.program_id(0) == 0)
    def _():
        cnt_ref[...] = jnp.zeros_like(cnt_ref)

    tm = logits.shape[0]
    earlier = (lax.broadcasted_iota(I32, (tm, tm), 1) < lax.broadcasted_iota(I32, (tm, tm), 0))
    before = _dot(jnp.where(earlier, 1.0, 0.0).astype(BF16), onehot.astype(BF16)) + cnt_ref[0:1, :]
    for kk in range(TOP_K):
        rank = jnp.sum(jnp.where(lane == idxs[kk], before, 0.0), axis=-1, keepdims=True)
        ti = jnp.where(lane == TOP_K + kk, rank, ti)
    cnt_ref[...] = cnt_ref[...] + jnp.sum(onehot, axis=0, keepdims=True)
    ti_ref[...] = ti.astype(I32)
    tp_ref[...] = tp


def _merge(act, og, gc, gg, x2d, gt1, sh2, sc2, gn, wc, wg, wo, wr, br, tm, rows_per_batch,
           first_row, n):
    d = x2d.shape[1]
    per_b = rows_per_batch // tm
    off = first_row // tm
    row_in = lambda w: pl.BlockSpec((tm, w), lambda i: (i + off, 0))
    row = lambda w: pl.BlockSpec((tm, w), lambda i: (i, 0))
    vec = pl.BlockSpec((1, 1, d), lambda i: ((i + off) // per_b, 0, 0))
    const = lambda shape: pl.BlockSpec(shape, lambda i: (0,) * len(shape))
    return pl.pallas_call(
        _merge_body,
        grid=(n // tm,),
        in_specs=[row_in(d), row_in(d), row_in(d), row_in(d), row_in(d), vec, vec, vec,
                  const((1, d)),
                  _resident(wc.shape), _resident(wg.shape), _resident(wo.shape),
                  const(wr.shape), const((1, LANES))],
        out_specs=[row(d), row(d // 2), row(LANES), row(LANES), const((SUBLANES, LANES))],
        out_shape=[jax.ShapeDtypeStruct((n, d), F32), jax.ShapeDtypeStruct((n, d // 2), I32),
                   jax.ShapeDtypeStruct((n, LANES), I32), jax.ShapeDtypeStruct((n, LANES), F32),
                   jax.ShapeDtypeStruct((SUBLANES, LANES), F32)],
        compiler_params=_cparams(("arbitrary",)),
        name="merge_router",
    )(act, og, gc, gg, x2d, gt1, sh2, sc2, gn, wc, wg, wo, wr, br)


_GLU_GROUP = 2 * LANES


def _deinterleave_matrix():
    src = lax.broadcasted_iota(I32, (_GLU_GROUP, _GLU_GROUP), 0)
    dst = lax.broadcasted_iota(I32, (_GLU_GROUP, _GLU_GROUP), 1)
    want = jnp.where(dst < LANES, 2 * dst, 2 * (dst - LANES) + 1)
    return jnp.where(src == want, 1.0, 0.0).astype(BF16)


def _expert_body(be_ref, nreal_ref, nxt_ref, xp_ref, w1_hbm, b1_ref, w2_hbm, b2_ref, yp_ref,
                 w1f_ref, w2f_ref, w1s_ref, w2s_ref, sem, nchg_ref):
    i = pl.program_id(0)
    new_expert = jnp.logical_or(i == 0, be_ref[i] != be_ref[jnp.maximum(i - 1, 0)])

    def weight_copies(e, slot):
        return (pltpu.make_async_copy(w1_hbm.at[e], w1f_ref.at[slot], sem.at[slot]),
                pltpu.make_async_copy(w2_hbm.at[e], w2f_ref.at[slot], sem.at[slot]))

    @pl.when(i == 0)
    def _():
        nchg_ref[0] = 0
        for cp in weight_copies(be_ref[0], 0):
            cp.start()

    @pl.when(jnp.logical_and(new_expert, i < nreal_ref[0]))
    def _():
        slot = nchg_ref[0] % 2
        for cp in weight_copies(be_ref[i], slot):
            cp.wait()

        @pl.when(nxt_ref[i] >= 0)
        def _():
            for cp in weight_copies(nxt_ref[i], 1 - slot):
                cp.start()

        nchg_ref[0] = nchg_ref[0] + 1
        perm = _deinterleave_matrix()
        for g in range(2 * D_FF // _GLU_GROUP):
            cs = slice(g * _GLU_GROUP, (g + 1) * _GLU_GROUP)
            w1s_ref[:, cs] = _dot(w1f_ref[slot, :, cs].astype(BF16), perm).astype(BF16)
        w2s_ref[...] = w2f_ref[slot].astype(BF16)

    @pl.when(i < nreal_ref[0])
    def _():
        lo, hi = _unpack_pair(xp_ref[...])
        x = jnp.concatenate([lo, hi], axis=1).astype(BF16)
        hid = _dot(x, w1s_ref[...]) + b1_ref[0]
        ngrp = 2 * D_FF // _GLU_GROUP
        hg = jnp.concatenate(
            [hid[:, g * _GLU_GROUP:g * _GLU_GROUP + LANES] for g in range(ngrp)], axis=1)
        hl = jnp.concatenate(
            [hid[:, g * _GLU_GROUP + LANES:(g + 1) * _GLU_GROUP] for g in range(ngrp)], axis=1)
        xg = jnp.minimum(hg, SWIGLU_LIMIT)
        xl = jnp.clip(hl, -SWIGLU_LIMIT, SWIGLU_LIMIT)
        act = xg * _sigmoid(SWIGLU_ALPHA * xg) * (xl + 1.0)
        y = _dot(act.astype(BF16), w2s_ref[...]) + b2_ref[0]
        half = D_MODEL // 2
        yp_ref[...] = _pack_pair(y[:, :half], y[:, half:])

    @pl.when(i >= nreal_ref[0])
    def _():
        yp_ref[...] = jnp.zeros_like(yp_ref)


def _experts(blk_expert, n_real, next_expert, xp, w1, b1, w2, b2, bm):
    p, w = xp.shape
    grid_spec = pltpu.PrefetchScalarGridSpec(
        num_scalar_prefetch=3,
        grid=(p // bm,),
        in_specs=[pl.BlockSpec((bm, w), lambda i, be, nr, nx: (i, 0)),
                  pl.BlockSpec(memory_space=pl.ANY),
                  pl.BlockSpec((1, 1, 2 * D_FF), lambda i, be, nr, nx: (be[i], 0, 0)),
                  pl.BlockSpec(memory_space=pl.ANY),
                  pl.BlockSpec((1, 1, D_MODEL), lambda i, be, nr, nx: (be[i], 0, 0))],
        out_specs=pl.BlockSpec((bm, w), lambda i, be, nr, nx: (i, 0)),
        scratch_shapes=[pltpu.VMEM((2, D_MODEL, 2 * D_FF), F32), pltpu.VMEM((2, D_FF, D_MODEL), F32),
                        pltpu.VMEM((D_MODEL, 2 * D_FF), BF16), pltpu.VMEM((D_FF, D_MODEL), BF16),
                        pltpu.SemaphoreType.DMA((2,)), pltpu.SMEM((1,), I32)],
    )
    return pl.pallas_call(
        _expert_body,
        grid_spec=grid_spec,
        out_shape=jax.ShapeDtypeStruct((p, w), I32),
        compiler_params=_cparams(("arbitrary",)),
        name="moe_experts",
    )(blk_expert, n_real, next_expert, xp, w1, b1, w2, b2)


_SC_CORES = 2
_SC_SUBCORES = 16
_SC_CHUNK = 64


def _sc_gather_rows(table, idx):
    rows, width = idx.shape[0], table.shape[1]
    workers = _SC_CORES * _SC_SUBCORES
    per_worker = rows // workers
    assert rows % (workers * _SC_CHUNK) == 0
    mesh = plsc.VectorSubcoreMesh(core_axis_name="c", subcore_axis_name="s")

    n_chunks = per_worker // _SC_CHUNK
    assert n_chunks % 2 == 0 and n_chunks >= 4

    def body(table_hbm, idx_hbm, out_hbm, idx0, idx1, rows0, rows1, gsem0, gsem1, wsem0, wsem1):
        wid = lax.axis_index("s") * _SC_CORES + lax.axis_index("c")
        base = wid * per_worker
        bufs = ((idx0, rows0, gsem0, wsem0), (idx1, rows1, gsem1, wsem1))

        def out_rows(c):
            return out_hbm.at[pl.ds(base + c * _SC_CHUNK, _SC_CHUNK)]

        def gather_start(c, b):
            idx_v, rows_v, gsem, _ = bufs[b]
            pltpu.sync_copy(idx_hbm.at[pl.ds(base + c * _SC_CHUNK, _SC_CHUNK)], idx_v)
            pltpu.make_async_copy(table_hbm.at[idx_v], rows_v, gsem).start()

        def gather_wait(b):
            idx_v, rows_v, gsem, _ = bufs[b]
            pltpu.make_async_copy(table_hbm.at[idx_v], rows_v, gsem).wait()

        def write_start(c, b):
            _, rows_v, _, wsem = bufs[b]
            pltpu.make_async_copy(rows_v, out_rows(c), wsem).start()

        def write_wait(c, b):
            _, rows_v, _, wsem = bufs[b]
            pltpu.make_async_copy(rows_v, out_rows(c), wsem).wait()

        gather_start(0, 0)
        gather_wait(0)
        write_start(0, 0)
        gather_start(1, 1)

        @pl.loop(1, n_chunks - 1, step=2)
        def _(c):
            gather_wait(1)
            write_start(c, 1)
            write_wait(c - 1, 0)
            gather_start(c + 1, 0)
            gather_wait(0)
            write_start(c + 1, 0)
            write_wait(c, 1)
            gather_start(c + 2, 1)

        gather_wait(1)
        write_start(n_chunks - 1, 1)
        write_wait(n_chunks - 2, 0)
        write_wait(n_chunks - 1, 1)

    chunk = lambda: pltpu.VMEM((_SC_CHUNK, width), table.dtype)
    return pl.kernel(
        body,
        out_type=jax.ShapeDtypeStruct((rows, width), table.dtype),
        mesh=mesh,
        scratch_types=[pltpu.VMEM((_SC_CHUNK,), I32), pltpu.VMEM((_SC_CHUNK,), I32), chunk(), chunk(),
                       pltpu.SemaphoreType.DMA, pltpu.SemaphoreType.DMA,
                       pltpu.SemaphoreType.DMA, pltpu.SemaphoreType.DMA],
        name="sc_gather_rows",
    )(table, idx)


def _sc_scatter_rows(rows, idx_slots, out_rows):
    n, width = rows.shape
    workers = _SC_CORES * _SC_SUBCORES
    per_worker = n // workers
    assert n % (workers * _SC_CHUNK) == 0
    mesh = plsc.VectorSubcoreMesh(core_axis_name="c", subcore_axis_name="s")

    def body(rows_hbm, idx_hbm, out_hbm, idx_v, rows_v):
        wid = lax.axis_index("s") * _SC_CORES + lax.axis_index("c")
        base = wid * per_worker

        @pl.loop(0, per_worker // _SC_CHUNK)
        def _(i):
            t0 = base + i * _SC_CHUNK
            pltpu.sync_copy(rows_hbm.at[pl.ds(t0, _SC_CHUNK)], rows_v)
            for kk in range(TOP_K):
                pltpu.sync_copy(idx_hbm.at[pl.ds(kk * n + t0, _SC_CHUNK)], idx_v)
                pltpu.sync_copy(rows_v, out_hbm.at[idx_v])

    return pl.kernel(
        body,
        out_type=jax.ShapeDtypeStruct((out_rows, width), rows.dtype),
        mesh=mesh,
        scratch_types=[pltpu.VMEM((_SC_CHUNK,), I32), pltpu.VMEM((_SC_CHUNK, width), rows.dtype)],
        name="sc_scatter_rows",
    )(rows, idx_slots)


def _combine_dense_body(y4_ref, tp_ref, x1_ref, gt_ref, gf_ref, *out_refs):
    o_ref = out_refs[-1]
    half = D_MODEL // 2
    tp = tp_ref[...]
    y_lo = y_hi = None
    for kk in range(TOP_K):
        lo, hi = _unpack_pair(y4_ref[kk])
        pk = tp[:, kk:kk + 1]
        y_lo = pk * lo if y_lo is None else y_lo + pk * lo
        y_hi = pk * hi if y_hi is None else y_hi + pk * hi
    x2 = x1_ref[...] + gt_ref[0] * jnp.concatenate([y_lo, y_hi], axis=1)
    ms = jnp.mean(x2 * x2, axis=-1, keepdims=True)
    o_ref[...] = x2 * lax.rsqrt(ms + EPS) * gf_ref[...]


def _combine_dense(y4, tp, x1, gt2, gf, tokens, rows_per_batch, first_row, n_total, out_prev):
    n, d = x1.shape
    per_b = rows_per_batch // tokens
    off = first_row // tokens
    in_specs = [pl.BlockSpec((TOP_K, tokens, d // 2), lambda i: (0, i, 0)),
                pl.BlockSpec((tokens, LANES), lambda i: (i, 0)),
                pl.BlockSpec((tokens, d), lambda i: (i, 0)),
                pl.BlockSpec((1, 1, d), lambda i: ((i + off) // per_b, 0, 0)),
                pl.BlockSpec((1, d), lambda i: (0, 0))]
    args = [y4, tp, x1, gt2, gf]
    aliases = {}
    if out_prev is not None:
        in_specs.append(pl.BlockSpec(memory_space=pl.ANY))
        args.append(out_prev)
        aliases = {len(args) - 1: 0}
    return pl.pallas_call(
        _combine_dense_body,
        grid=(n // tokens,),
        in_specs=in_specs,
        out_specs=pl.BlockSpec((tokens, d), lambda i: (i + off, 0)),
        out_shape=jax.ShapeDtypeStruct((n_total, d), F32),
        input_output_aliases=aliases,
        compiler_params=_cparams(("arbitrary",)),
        name="moe_combine_dense",
    )(*args)


def _routing_tables(top_idx, rank, counts, bm):
    n = top_idx.shape[0]
    nk = n * TOP_K
    padded = (counts + bm - 1) // bm * bm
    pad_end = jnp.cumsum(padded)
    pad_start = pad_end - padded
    dest = (pad_start[top_idx] + rank).astype(I32)
    n_blocks = (nk + N_EXPERTS * (bm - 1) + bm - 1) // bm
    starts = jnp.arange(n_blocks, dtype=I32) * bm
    blk_expert = jnp.minimum(jnp.sum((pad_end[None, :] <= starts[:, None]).astype(I32), axis=1),
                             N_EXPERTS - 1).astype(I32)
    n_real = (pad_end[-1] // bm).astype(I32).reshape(1)
    ids = jnp.arange(N_EXPERTS, dtype=I32)
    later = jnp.logical_and(counts[None, :] > 0, ids[None, :] > ids[:, None])
    nxt = jnp.min(jnp.where(later, ids[None, :], N_EXPERTS), axis=1)
    next_expert = jnp.where(nxt < N_EXPERTS, nxt, -1).astype(I32)[blk_expert]
    return dest, blk_expert, n_real, next_expert, n_blocks


def _layer(x, c, ctx, c_ctx, w_ada, b_ada, g_mix_norm, w_in, w_dw, b_dw, g_conv_ln, b_conv_ln,
           w_conv_out, w_alpha, b_alpha, g_gla_norm, w_gla_out, w_out, g_ffn_norm, w_router,
           b_router, w_exp_in, b_exp_in, w_exp_out, b_exp_out, g_final, *, cfg):
    b, s, d = x.shape
    n = b * s

    rows = (b + 1 + SUBLANES - 1) // SUBLANES * SUBLANES
    cc = jnp.zeros((rows, d), F32).at[:b].set(c).at[b].set(c_ctx)
    mod = _ada(cc, w_ada, b_ada)
    sh1, sc1, gt1, sh2, sc2, gt2 = [mod[:b, i * d:(i + 1) * d].reshape(b, 1, d) for i in range(6)]
    csh1 = mod[b:b + 1, 0:d]
    csc1 = mod[b:b + 1, d:2 * d]

    a0 = 2 * CONV_W + 2 * GLA_KD + 2 * GLA_VD
    w_in_r = jnp.concatenate(
        [w_in[:, :a0], w_in[:, a0 + 2 * GLA_RANK:], w_in[:, a0:a0 + 2 * GLA_RANK],
         jnp.zeros((d, LANES - 2 * GLA_RANK), F32)], axis=1).astype(BF16)
    gmn = g_mix_norm.reshape(1, d)

    act, q, k, v, sg, gc, gg, a = _inproj_lat(x, gmn, sh1, sc1, w_in_r, w_dw, b_dw, g_conv_ln,
                                              b_conv_ln, cfg["tm_in"])
    act = act.reshape(n, CONV_W)
    kc, vc, ac = _inproj_ctx(ctx, gmn, csh1, csc1, w_in_r, cfg["tm_ctx"])

    cps, cps_ctx = cfg["gla_cps"], cfg["gla_cps_ctx"]
    st_f = _gla_ctx(kc, vc, ac, w_alpha[0], b_alpha[0], False, cps_ctx)
    o_f = _gla_lat(k, v, a, q, w_alpha[0], b_alpha[0], st_f, False, cps)
    st_b = _gla_ctx(kc, vc, ac, w_alpha[1], b_alpha[1], True, cps_ctx)
    og = _gla_lat(k, v, a, q, w_alpha[1], b_alpha[1], st_b, True, cps,
                  sg=sg, o_prev=o_f, g_norm=g_gla_norm)

    wr = jnp.zeros((d, LANES), F32).at[:, :N_EXPERTS].set(w_router)
    br = jnp.zeros((1, LANES), F32).at[0, :N_EXPERTS].set(b_router)
    b1 = b_exp_in.reshape(N_EXPERTS, 2 * D_FF // _GLU_GROUP, LANES, 2).transpose(0, 1, 3, 2)
    b1 = b1.reshape(N_EXPERTS, 1, 2 * D_FF)
    bm = cfg["moe_block"]

    groups = cfg["moe_groups"]
    ng = n // groups
    staged = []
    for g in range(groups):
        x1, h2p, ti, tp, cnt = _merge(
            act, og.reshape(n, d), gc.reshape(n, d), gg.reshape(n, d), x.reshape(n, d),
            gt1, sh2, sc2, g_ffn_norm.reshape(1, d),
            w_conv_out.astype(BF16), w_gla_out.astype(BF16), w_out.astype(BF16), wr, br,
            cfg["tm_merge"], s, g * ng, ng)
        dest, blk_expert, n_real, next_expert, n_blocks = _routing_tables(
            ti[:, :TOP_K], ti[:, TOP_K:2 * TOP_K], cnt[0, :N_EXPERTS].astype(I32), bm)
        dest_slots = dest.T.reshape(ng * TOP_K)
        xp = _sc_scatter_rows(h2p, dest_slots, n_blocks * bm)
        staged.append((x1, tp, dest_slots, blk_expert, n_real, next_expert, xp))
    gathered = []
    for x1, tp, dest_slots, blk_expert, n_real, next_expert, xp in staged:
        yp = _experts(blk_expert, n_real, next_expert, xp, w_exp_in, b1, w_exp_out,
                      b_exp_out.reshape(N_EXPERTS, 1, d), bm)
        gathered.append(_sc_gather_rows(yp, dest_slots).reshape(TOP_K, ng, d // 2))
    out = None
    for g, (x1, tp, *_) in enumerate(staged):
        out = _combine_dense(gathered[g], tp, x1, gt2, g_final.reshape(1, d), cfg["tm_combine"], s,
                             g * ng, n, out)
    return out.reshape(b, s, d)


def _config(s, l):
    return dict(tm_in=min(512, s), tm_ctx=min(256, l), gla_cps=2, gla_cps_ctx=4,
                tm_merge=min(512, s), moe_block=512, tm_combine=min(1024, s), moe_groups=2)


def kernel(x, c, ctx, c_ctx, w_ada, b_ada, g_mix_norm, w_in, w_dw, b_dw, g_conv_ln, b_conv_ln,
           w_conv_out, w_alpha, b_alpha, g_gla_norm, w_gla_out, w_out, g_ffn_norm, w_router,
           b_router, w_exp_in, b_exp_in, w_exp_out, b_exp_out, g_final):
    depth = w_ada.shape[0]
    assert depth == 1, "single-layer block: the context stream is only consumed by the GLA scan"
    cfg = _config(x.shape[1], ctx.shape[1])
    return _layer(x, c, ctx, c_ctx, w_ada[0], b_ada[0], g_mix_norm[0], w_in[0], w_dw[0], b_dw[0],
                  g_conv_ln[0], b_conv_ln[0], w_conv_out[0], w_alpha[0], b_alpha[0],
                  g_gla_norm[0], w_gla_out[0], w_out[0], g_ffn_norm[0], w_router[0], b_router[0],
                  w_exp_in[0], b_exp_in[0], w_exp_out[0], b_exp_out[0], g_final, cfg=cfg)
```
